```python
import math
import jax
import jax.numpy as jnp
from jax import lax
import numpy as np

D_MODEL = 1024
BATCH = 8
SEQ = 4096
DEPTH = 4

GRID_W = 64
CTX_LEN = 256
EPS = 1e-6
FOURIER_GROUPS = 4
FOURIER_GROUP_W = 64
FOURIER_WIDTH = FOURIER_GROUPS * FOURIER_GROUP_W
HYENA_WIDTH = 256
HYENA_SHORT = 3
HYENA_BANDS = 16
HYENA_EMB = 1 + 2 * HYENA_BANDS
HYENA_FFN = 64
HYENA_MIN_DECAY = -math.log(1e-2) / 1.5
HYENA_MAX_DECAY = -math.log(1e-2) / 0.3
POOL_WINDOWS = (2, 4, 8, 16)
POOL_GROUP_W = 64
POOL_WIDTH = len(POOL_WINDOWS) * POOL_GROUP_W
SSM_GROUPS = 2
HEADS_PER_GROUP = 4
SSM_HEADS = SSM_GROUPS * HEADS_PER_GROUP
SSM_HEAD_DIM = 64
SSM_WIDTH = SSM_HEADS * SSM_HEAD_DIM
SSM_STATE = 128
SSM_CONV = 3
SSM_CHUNK = 128
SSM_CONV_CH = SSM_WIDTH + 2 * SSM_GROUPS * SSM_STATE
SSM_IN = SSM_WIDTH + SSM_CONV_CH + 2 * SSM_HEADS
IN_WIDTH = FOURIER_WIDTH + 3 * HYENA_WIDTH + POOL_WIDTH + SSM_IN
IN_SPLITS = (FOURIER_WIDTH, FOURIER_WIDTH + 3 * HYENA_WIDTH, FOURIER_WIDTH + 3 * HYENA_WIDTH + POOL_WIDTH)
N_BRANCHES = 4
BRANCH_WIDTH = FOURIER_WIDTH + HYENA_WIDTH + POOL_WIDTH + SSM_WIDTH
BRANCH_SPLITS = (FOURIER_WIDTH, FOURIER_WIDTH + HYENA_WIDTH, FOURIER_WIDTH + HYENA_WIDTH + POOL_WIDTH)
N_EXPERTS = 32
TOP_K = 4
EXPERT_FF = 1024
SWIGLU_LIMIT = 7.0
SWIGLU_ALPHA = 1.702
MOE_BLOCK = 128

kernel_name = 'hybrid_fourier_hyena_pool_ssd_moe_dit'


def rmsnorm(x, g):
    xf = x.astype(jnp.float32)
    y = xf * lax.rsqrt(jnp.mean(xf * xf, axis=-1, keepdims=True) + EPS)
    return y.astype(x.dtype) * g


def dwconv_centred(x, w, b):
    k_w = w.shape[0]
    n = x.shape[1]
    xp = jnp.pad(x, ((0, 0), ((k_w - 1) // 2, k_w // 2), (0, 0)))
    return sum(w[k] * xp[:, k:k + n] for k in range(k_w)) + b


def fourier_mix(u):
    b, n, _ = u.shape
    ug = u.reshape(b, n, FOURIER_GROUPS, FOURIER_GROUP_W).astype(jnp.float32)
    y = jnp.fft.fft2(ug, axes=(1, 3), norm='ortho').real
    return y.reshape(b, n, FOURIER_WIDTH).astype(u.dtype)


def implicit_filter(n, lp):
    pos = jnp.arange(n, dtype=jnp.float32)
    t = pos / (n - 1)
    ang = 2.0 * math.pi * pos / n
    freqs = jnp.linspace(1e-4, HYENA_BANDS - 1, HYENA_BANDS, dtype=jnp.float32)
    feats = jnp.concatenate([t[:, None], jnp.cos(ang[:, None] * freqs), -jnp.sin(ang[:, None] * freqs)], axis=-1)
    hdn = jnp.sin(feats @ lp['hy_ffn_w1'] + lp['hy_ffn_b1'])
    hdn = jnp.sin(hdn @ lp['hy_ffn_w2'] + lp['hy_ffn_b2'])
    k = (hdn @ lp['hy_ffn_w3']).astype(jnp.float32).reshape(n, 2, HYENA_WIDTH)
    deltas = jnp.linspace(HYENA_MIN_DECAY, HYENA_MAX_DECAY, HYENA_WIDTH, dtype=jnp.float32)
    k = k * jnp.exp(-t[:, None, None] * deltas)
    k_two = jnp.concatenate([k[:, 0], jnp.zeros((1, HYENA_WIDTH), jnp.float32), jnp.flip(k[1:, 1], axis=0)], axis=0)
    return k_two / jnp.sum(jnp.abs(k_two), axis=0, keepdims=True)


def hyena_mix(u, lp):
    b, n, _ = u.shape
    u = dwconv_centred(u, lp['hy_conv_w'], lp['hy_conv_b'])
    x1, x2, v = jnp.split(u, 3, axis=-1)
    k_two = implicit_filter(n, lp)
    vv = (x2 * v).astype(jnp.float32)
    n_fft = 2 * n
    y = jnp.fft.irfft(jnp.fft.rfft(vv, n=n_fft, axis=1) * jnp.fft.rfft(k_two, n=n_fft, axis=0)[None], n=n_fft, axis=1)[:, :n]
    y = y + vv * lp['hy_bias']
    return (x1 * y).astype(u.dtype)


def pool_mix(u, lp, n_rows):
    b, n, cw = u.shape
    row_len = n // n_rows
    ug = u.reshape(b, n_rows, row_len, len(POOL_WINDOWS), POOL_GROUP_W)
    cs = jnp.pad(jnp.cumsum(ug.astype(jnp.float32), axis=2), ((0, 0), (0, 0), (1, 0), (0, 0), (0, 0)))
    pos = jnp.arange(row_len)
    means = []
    for gi, win in enumerate(POOL_WINDOWS):
        lo = jnp.clip(pos - win // 2, 0, row_len)
        hi = jnp.clip(pos + win // 2, 0, row_len)
        csg = cs[:, :, :, gi]
        means.append((csg[:, :, hi] - csg[:, :, lo]) / (hi - lo).astype(jnp.float32)[:, None])
    pooled = jnp.stack(means, axis=3) - ug
    mixed = jnp.einsum('brwgc,gcd->brwgd', pooled, lp['pool_w'])
    return mixed.reshape(b, n, cw) * lp['pool_scale']


def ssd_scan(xs, dt, a, bm, cm, h0):
    b, n_tok, g, r, p = xs.shape
    n_st = bm.shape[-1]
    q = SSM_CHUNK
    nc = n_tok // q
    xs = xs.reshape(b, nc, q, g, r, p)
    dt = dt.reshape(b, nc, q, g, r)
    bm = bm.reshape(b, nc, q, g, n_st)
    cm = cm.reshape(b, nc, q, g, n_st)
    acs = jnp.cumsum(dt * a, axis=2)
    seg = acs[:, :, :, None] - acs[:, :, None, :]
    mask = jnp.tril(jnp.ones((q, q), bool))[:, :, None, None]
    decay_in = jnp.exp(jnp.where(mask, seg, -jnp.inf))
    scores = jnp.einsum('bcign,bcjgn->bcijg', cm, bm)
    xdt = xs * dt[..., None]
    y_diag = jnp.einsum('bcijgr,bcjgrp->bcigrp', scores[..., None] * decay_in, xdt)
    to_end = jnp.exp(acs[:, :, -1:] - acs)
    chunk_state = jnp.einsum('bcjgn,bcjgr,bcjgrp->bcgrpn', bm, to_end, xdt)
    chunk_decay = jnp.exp(acs[:, :, -1])

    def carry_step(h, inp):
        s, d = inp
        return h * d[..., None, None] + s, h

    h_last, h_enter = lax.scan(carry_step, h0, (jnp.moveaxis(chunk_state, 1, 0), jnp.moveaxis(chunk_decay, 1, 0)))
    h_enter = jnp.moveaxis(h_enter, 0, 1)
    y_off = jnp.einsum('bcign,bcgrpn,bcigr->bcigrp', cm, h_enter, jnp.exp(acs))
    return (y_diag + y_off).reshape(b, n_tok, g, r, p), h_last


def ssd_mix(u, lp, init_f, init_b):
    b, n, _ = u.shape
    z, xbc, dt_raw = jnp.split(u, [SSM_WIDTH, SSM_WIDTH + SSM_CONV_CH], axis=-1)
    xbc = jax.nn.silu(dwconv_centred(xbc, lp['ssm_conv_w'], lp['ssm_conv_b']))
    xs, bm, cm = jnp.split(xbc, [SSM_WIDTH, SSM_WIDTH + SSM_GROUPS * SSM_STATE], axis=-1)
    xs = xs.reshape(b, n, SSM_GROUPS, HEADS_PER_GROUP, SSM_HEAD_DIM)
    bm = bm.reshape(b, n, SSM_GROUPS, SSM_STATE)
    cm = cm.reshape(b, n, SSM_GROUPS, SSM_STATE)
    dt = jax.nn.softplus(dt_raw.astype(jnp.float32).reshape(b, n, 2, SSM_GROUPS, HEADS_PER_GROUP)
                         + lp['ssm_dt_bias'].astype(jnp.float32).reshape(2, SSM_GROUPS, HEADS_PER_GROUP))
    a = -jnp.exp(lp['ssm_a_log'].astype(jnp.float32)).reshape(2, SSM_GROUPS, HEADS_PER_GROUP)
    flip = lambda t: jnp.flip(t, axis=1)
    y_f, s_f = ssd_scan(xs, dt[:, :, 0], a[0], bm, cm, init_f)
    y_b, s_b = ssd_scan(flip(xs), flip(dt[:, :, 1]), a[1], flip(bm), flip(cm), init_b)
    y = y_f + flip(y_b) + xs * lp['ssm_d'].reshape(SSM_GROUPS, HEADS_PER_GROUP)[:, :, None]
    y = y.reshape(b, n, SSM_WIDTH) * jax.nn.silu(z)
    return rmsnorm(y, lp['ssm_norm']), (s_f, s_b)


def token_mixer(h, lp, n_rows, init_f, init_b):
    proj = h @ lp['w_in']
    f_in, hy_in, pool_in, ssm_in = jnp.split(proj, IN_SPLITS, axis=-1)
    y_ssm, states = ssd_mix(ssm_in, lp, init_f, init_b)
    ys = (fourier_mix(f_in), hyena_mix(hy_in, lp), pool_mix(pool_in, lp, n_rows), y_ssm)
    w_br = jnp.split(lp['w_branch'], BRANCH_SPLITS, axis=0)
    merged = 0.0
    for k in range(N_BRANCHES):
        gate = jax.nn.sigmoid(h @ lp['w_gate'][k])
        merged = merged + gate * (ys[k] @ w_br[k])
    return merged @ lp['w_out'], states


def moe_ffn(h, lp):
    lead = h.shape[:-1]
    h = h.reshape(-1, D_MODEL)
    n_tok = h.shape[0]
    n_pairs = n_tok * TOP_K
    logits = (h @ lp['router_w'] + lp['router_b']).astype(jnp.float32)
    top_logit, top_e = lax.top_k(logits, TOP_K)
    top_w = jax.nn.softmax(top_logit, axis=-1)
    flat_e = top_e.reshape(-1)
    order = jnp.argsort(flat_e)
    e_sorted = flat_e[order]
    tok_sorted = (order // TOP_K).astype(jnp.int32)
    w_sorted = top_w.reshape(-1)[order]
    counts = jnp.bincount(flat_e, length=N_EXPERTS)
    padded = (counts + MOE_BLOCK - 1) // MOE_BLOCK * MOE_BLOCK
    start = jnp.cumsum(counts) - counts
    pad_end = jnp.cumsum(padded)
    pad_start = pad_end - padded
    dest = pad_start[e_sorted] + jnp.arange(n_pairs) - start[e_sorted]
    n_blk = -(-(n_pairs + N_EXPERTS * (MOE_BLOCK - 1)) // MOE_BLOCK)
    n_rows = n_blk * MOE_BLOCK
    row_tok = jnp.full((n_rows,), n_tok, jnp.int32).at[dest].set(tok_sorted)
    row_w = jnp.zeros((n_rows,), jnp.float32).at[dest].set(w_sorted)
    blk_e = jnp.minimum(jnp.searchsorted(pad_end, jnp.arange(n_blk) * MOE_BLOCK, side='right'), N_EXPERTS - 1)
    h_pad = jnp.concatenate([h, jnp.zeros((1, D_MODEL), h.dtype)], axis=0)
    w_up, b_up, w_down, b_down = lp['exp_w_up'], lp['exp_b_up'], lp['exp_w_down'], lp['exp_b_down']

    def expert_block(args):
        rows, wts, e = args
        hu = h_pad[rows] @ w_up[e] + b_up[e]
        glu = jnp.minimum(hu[:, :EXPERT_FF], SWIGLU_LIMIT)
        lin = jnp.clip(hu[:, EXPERT_FF:], -SWIGLU_LIMIT, SWIGLU_LIMIT)
        act = glu * jax.nn.sigmoid(SWIGLU_ALPHA * glu) * (lin + 1.0)
        y = act @ w_down[e] + b_down[e]
        return y * wts[:, None].astype(y.dtype)

    y_rows = lax.map(expert_block, (row_tok.reshape(n_blk, MOE_BLOCK), row_w.reshape(n_blk, MOE_BLOCK), blk_e))
    y = jax.ops.segment_sum(y_rows.reshape(n_rows, D_MODEL), row_tok, num_segments=n_tok + 1)[:n_tok]
    return y.reshape(*lead, D_MODEL)


def setup_inputs(seed: int = 0) -> dict:
    key = jax.random.key(seed)
    keys = iter(jax.random.split(key, 48))

    def nrm(shape, scale):
        return scale * jax.random.normal(next(keys), shape, jnp.float32)

    def gain(shape):
        return 1.0 + nrm(shape, 0.05)

    dt0 = jnp.exp(jax.random.uniform(next(keys), (DEPTH, 2, SSM_HEADS), jnp.float32, math.log(1e-3), math.log(1e-1)))
    a0 = jax.random.uniform(next(keys), (DEPTH, 2, SSM_HEADS), jnp.float32, 1.0, 16.0)
    branch_scale = jnp.concatenate([jnp.full((BRANCH_WIDTH - SSM_WIDTH, 1), FOURIER_WIDTH ** -0.5, jnp.float32),
                                    jnp.full((SSM_WIDTH, 1), SSM_WIDTH ** -0.5, jnp.float32)], axis=0)
    return {
        'x': nrm((BATCH, SEQ, D_MODEL), 1.0),
        'c': nrm((BATCH, D_MODEL), 1.0),
        'ctx': nrm((BATCH, CTX_LEN, D_MODEL), 1.0),
        'c_ctx': nrm((D_MODEL,), 1.0),
        'w_mod': nrm((DEPTH, D_MODEL, 6 * D_MODEL), 0.3 * D_MODEL ** -0.5),
        'b_mod': nrm((DEPTH, 6 * D_MODEL), 0.02),
        'norm_mix': gain((DEPTH, D_MODEL)),
        'norm_ffn': gain((DEPTH, D_MODEL)),
        'w_in': nrm((DEPTH, D_MODEL, IN_WIDTH), D_MODEL ** -0.5),
        'hy_conv_w': nrm((DEPTH, HYENA_SHORT, 3 * HYENA_WIDTH), HYENA_SHORT ** -0.5),
        'hy_conv_b': nrm((DEPTH, 3 * HYENA_WIDTH), 0.02),
        'hy_ffn_w1': nrm((DEPTH, HYENA_EMB, HYENA_FFN), HYENA_EMB ** -0.5),
        'hy_ffn_b1': nrm((DEPTH, HYENA_FFN), 0.02),
        'hy_ffn_w2': nrm((DEPTH, HYENA_FFN, HYENA_FFN), HYENA_FFN ** -0.5),
        'hy_ffn_b2': nrm((DEPTH, HYENA_FFN), 0.02),
        'hy_ffn_w3': nrm((DEPTH, HYENA_FFN, 2 * HYENA_WIDTH), HYENA_FFN ** -0.5),
        'hy_bias': nrm((DEPTH, HYENA_WIDTH), 0.5),
        'pool_w': nrm((DEPTH, len(POOL_WINDOWS), POOL_GROUP_W, POOL_GROUP_W), POOL_GROUP_W ** -0.5),
        'pool_scale': gain((DEPTH, POOL_WIDTH)),
        'ssm_conv_w': nrm((DEPTH, SSM_CONV, SSM_CONV_CH), SSM_CONV ** -0.5),
        'ssm_conv_b': nrm((DEPTH, SSM_CONV_CH), 0.02),
        'ssm_dt_bias': dt0 + jnp.log(-jnp.expm1(-dt0)),
        'ssm_a_log': jnp.log(a0),
        'ssm_d': 1.0 + nrm((DEPTH, SSM_HEADS), 0.1),
        'ssm_norm': gain((DEPTH, SSM_WIDTH)),
        'w_branch': nrm((DEPTH, BRANCH_WIDTH, D_MODEL), 1.0) * branch_scale,
        'w_gate': nrm((DEPTH, N_BRANCHES, D_MODEL, D_MODEL), D_MODEL ** -0.5),
        'w_out': nrm((DEPTH, D_MODEL, D_MODEL), D_MODEL ** -0.5),
        'router_w': nrm((DEPTH, D_MODEL, N_EXPERTS), D_MODEL ** -0.5),
        'router_b': nrm((DEPTH, N_EXPERTS), 0.01),
        'exp_w_up': nrm((DEPTH, N_EXPERTS, D_MODEL, 2 * EXPERT_FF), D_MODEL ** -0.5),
        'exp_b_up': nrm((DEPTH, N_EXPERTS, 2 * EXPERT_FF), 0.02),
        'exp_w_down': nrm((DEPTH, N_EXPERTS, EXPERT_FF, D_MODEL), EXPERT_FF ** -0.5),
        'exp_b_down': nrm((DEPTH, N_EXPERTS, D_MODEL), 0.02),
        'norm_final': gain((D_MODEL,)),
    }


def reference(x, c, ctx, c_ctx, w_mod, b_mod, norm_mix, norm_ffn, w_in, hy_conv_w, hy_conv_b,
              hy_ffn_w1, hy_ffn_b1, hy_ffn_w2, hy_ffn_b2, hy_ffn_w3, hy_bias, pool_w, pool_scale,
              ssm_conv_w, ssm_conv_b, ssm_dt_bias, ssm_a_log, ssm_d, ssm_norm, w_branch, w_gate, w_out,
              router_w, router_b, exp_w_up, exp_b_up, exp_w_down, exp_b_down, norm_final):
    batch, n_lat = x.shape[0], x.shape[1]
    rows = n_lat // GRID_W
    zero_state = jnp.zeros((batch, SSM_GROUPS, HEADS_PER_GROUP, SSM_HEAD_DIM, SSM_STATE), jnp.float32)
    silu_c = jax.nn.silu(c)
    silu_cc = jax.nn.silu(c_ctx)
    for layer in range(DEPTH):
        lp = {
            'w_in': w_in[layer], 'hy_conv_w': hy_conv_w[layer], 'hy_conv_b': hy_conv_b[layer],
            'hy_ffn_w1': hy_ffn_w1[layer], 'hy_ffn_b1': hy_ffn_b1[layer], 'hy_ffn_w2': hy_ffn_w2[layer],
            'hy_ffn_b2': hy_ffn_b2[layer], 'hy_ffn_w3': hy_ffn_w3[layer], 'hy_bias': hy_bias[layer],
            'pool_w': pool_w[layer], 'pool_scale': pool_scale[layer],
            'ssm_conv_w': ssm_conv_w[layer], 'ssm_conv_b': ssm_conv_b[layer], 'ssm_dt_bias': ssm_dt_bias[layer],
            'ssm_a_log': ssm_a_log[layer], 'ssm_d': ssm_d[layer], 'ssm_norm': ssm_norm[layer],
            'w_branch': w_branch[layer], 'w_gate': w_gate[layer], 'w_out': w_out[layer],
            'router_w': router_w[layer], 'router_b': router_b[layer],
            'exp_w_up': exp_w_up[layer], 'exp_b_up': exp_b_up[layer],
            'exp_w_down': exp_w_down[layer], 'exp_b_down': exp_b_down[layer],
        }
        mod_lat = (silu_c @ w_mod[layer] + b_mod[layer])[:, None, :]
        mod_ctx = (silu_cc @ w_mod[layer] + b_mod[layer])[None, None, :]
        sh1, sc1, g1, sh2, sc2, g2 = jnp.split(mod_lat, 6, axis=-1)
        csh1, csc1, cg1, csh2, csc2, cg2 = jnp.split(mod_ctx, 6, axis=-1)
        h_ctx = rmsnorm(ctx, norm_mix[layer]) * (1.0 + csc1) + csh1
        y_ctx, ctx_states = token_mixer(h_ctx, lp, 1, zero_state, zero_state)
        h_lat = rmsnorm(x, norm_mix[layer]) * (1.0 + sc1) + sh1
        y_lat, _ = token_mixer(h_lat, lp, rows, ctx_states[0], ctx_states[1])
        x = x + g1 * y_lat
        h_lat = rmsnorm(x, norm_ffn[layer]) * (1.0 + sc2) + sh2
        x = x + g2 * moe_ffn(h_lat, lp)
        if layer < DEPTH - 1:
            ctx = ctx + cg1 * y_ctx
            h_ctx = rmsnorm(ctx, norm_ffn[layer]) * (1.0 + csc2) + csh2
            ctx = ctx + cg2 * moe_ffn(h_ctx, lp)
    return rmsnorm(x, norm_final)
```

```python
import functools
import math

import jax
import jax.numpy as jnp
from jax import lax
from jax.experimental import pallas as pl
from jax.experimental.pallas import tpu as pltpu

F32 = jnp.float32
BF16 = jnp.bfloat16

GRID_W = 64
EPS = 1e-6
FOURIER_GROUP_W = 64
HYENA_BANDS = 16
HYENA_MIN_DECAY = -math.log(1e-2) / 1.5
HYENA_MAX_DECAY = -math.log(1e-2) / 0.3
POOL_WINDOWS = (2, 4, 8, 16)
POOL_GROUP_W = 64
SSM_GROUPS = 2
HEADS_PER_GROUP = 4
SSM_HEADS = SSM_GROUPS * HEADS_PER_GROUP
SSM_HEAD_DIM = 64
SSM_STATE = 128
SSM_CHUNK = 128
TOP_K = 4
SWIGLU_LIMIT = 7.0
SWIGLU_ALPHA = 1.702

LANE = 128
BF16_SUBLANE = 16
TOKEN_TILE = 256
SEQ_TILE = 512
MOE_ROWS = 512
VMEM_LIMIT = 56 * 1024 * 1024


def _cparams(sem):
    return pltpu.CompilerParams(dimension_semantics=sem, vmem_limit_bytes=VMEM_LIMIT)


def _sigmoid(x):
    return 1.0 / (1.0 + jnp.exp(-x))


def _dot(a, b):
    return jnp.dot(a, b, preferred_element_type=F32)


def _dot_hi(a, b):
    return jnp.dot(a, b, preferred_element_type=F32, precision=lax.Precision.HIGHEST)


def _split3(x):
    x1 = x.astype(BF16)
    r1 = x - x1.astype(F32)
    x2 = r1.astype(BF16)
    x3 = (r1 - x2.astype(F32)).astype(BF16)
    return x1, x2, x3


def _dot_exact01(m01, x):
    x1, x2, x3 = _split3(x)
    return _dot(m01, x1) + _dot(m01, x2) + _dot(m01, x3)


def _mod_kernel(c_ref, w_ref, b_ref, o_ref):
    c = c_ref[...]
    s = c * _sigmoid(c)
    o_ref[...] = _dot_hi(s, w_ref[...]) + b_ref[...]


def _modulation(cc, w_mod, b_mod):
    depth, d, six_d = w_mod.shape
    rows = cc.shape[0]
    nj = six_d // d
    return pl.pallas_call(
        _mod_kernel,
        grid=(depth, nj),
        in_specs=[
            pl.BlockSpec((rows, d), lambda l, j: (0, 0)),
            pl.BlockSpec((None, d, d), lambda l, j: (l, 0, j)),
            pl.BlockSpec((None, 1, d), lambda l, j: (l, 0, j)),
        ],
        out_specs=pl.BlockSpec((None, rows, d), lambda l, j: (l, 0, j)),
        out_shape=jax.ShapeDtypeStruct((depth, rows, six_d), F32),
        compiler_params=_cparams(("arbitrary", "arbitrary")),
        name="modulation",
    )(cc, w_mod, b_mod.reshape(depth, 1, six_d))


class _Tok:
    def __init__(self, batch, n_lat, n_ctx):
        assert n_ctx == TOKEN_TILE and n_lat % TOKEN_TILE == 0
        self.batch, self.n_lat, self.n_ctx = batch, n_lat, n_ctx
        self.seq = n_lat + n_ctx
        self.tiles_per_b = self.seq // TOKEN_TILE
        self.lat_tiles = n_lat // TOKEN_TILE
        self.n_tiles = batch * self.tiles_per_b
        self.n_tok = batch * self.seq

    def b(self, t):
        return t // self.tiles_per_b

    def w(self, t):
        return t % self.tiles_per_b

    def is_ctx(self, t):
        return (t % self.tiles_per_b) >= self.lat_tiles

    def mod_row(self, t):
        return jnp.where(self.is_ctx(t), self.batch, self.b(t))

    def mod_spec(self, layer, part, d):
        return pl.BlockSpec((None, None, None, 1, d), lambda t: (layer, self.mod_row(t), part, 0, 0))

    def lat_spec(self, width):
        return pl.BlockSpec((None, TOKEN_TILE, width),
                            lambda t: (self.b(t), jnp.minimum(self.w(t), self.lat_tiles - 1), 0))

    def ctx_spec(self, width):
        return pl.BlockSpec((None, TOKEN_TILE, width), lambda t: (self.b(t), 0, 0))


def _norm_mod(x, gain, scale, shift):
    ms = jnp.mean(x * x, axis=-1, keepdims=True)
    return (x * lax.rsqrt(ms + EPS)) * gain * (1.0 + scale) + shift


def _inproj_kernel(x_ref, sc_ref, sh_ref, g_ref, win_ref, wc_ref, band_ref, invc_ref, pw_ref, ps_ref,
                   xcs_ref, hy_ref, pool_ref, z_ref, xbc_ref, dt_ref, *, widths):
    fw, hw, pw, zw, cw = widths
    hb = _norm_mod(x_ref[...], g_ref[...], sc_ref[...], sh_ref[...]).astype(BF16)
    o = 0
    f_in = _dot(hb, win_ref[:, o:o + fw])
    xcs_ref[...] = _dot(f_in.astype(BF16), wc_ref[...]).astype(BF16)
    o += fw
    hy_ref[...] = _dot(hb, win_ref[:, o:o + hw]).astype(BF16)
    o += hw
    u = _dot(hb, win_ref[:, o:o + pw])
    o += pw
    z_ref[...] = _dot(hb, win_ref[:, o:o + zw]).astype(BF16)
    o += zw
    xbc_ref[...] = _dot(hb, win_ref[:, o:o + cw]).astype(BF16)
    o += cw
    dt_ref[...] = _dot(hb, win_ref[:, o:o + LANE])
    gw = POOL_GROUP_W
    parts = []
    for gi in range(len(POOL_WINDOWS)):
        ug = u[:, gi * gw:(gi + 1) * gw]
        parts.append(_dot_exact01(band_ref[gi], ug))
    pooled = jnp.concatenate(parts, axis=1) * invc_ref[...] - u
    pool_ref[...] = (_dot(pooled.astype(BF16), pw_ref[...]) * ps_ref[...]).astype(BF16)


def _pool_tables(n_lat, n_ctx):
    tm = TOKEN_TILE
    p = jnp.arange(tm)
    bands, invs = [], []
    for row_len in (GRID_W, n_ctx):
        assert tm % row_len == 0
        pr, rr = p % row_len, p // row_len
        bg, ig = [], []
        for win in POOL_WINDOWS:
            lo = jnp.clip(pr - win // 2, 0, row_len)
            hi = jnp.clip(pr + win // 2, 0, row_len)
            q = pr[None, :]
            m = (rr[:, None] == rr[None, :]) & (q >= lo[:, None]) & (q < hi[:, None])
            bg.append(m.astype(BF16))
            ig.append(jnp.broadcast_to((1.0 / (hi - lo).astype(F32))[:, None], (tm, POOL_GROUP_W)))
        bands.append(jnp.stack(bg))
        invs.append(jnp.concatenate(ig, axis=1))
    return jnp.stack(bands), jnp.stack(invs)


def _block_diag(w):
    g, a, b = w.shape
    out = jnp.zeros((g * a, g * b), w.dtype)
    for i in range(g):
        out = out.at[i * a:(i + 1) * a, i * b:(i + 1) * b].set(w[i])
    return out


def _inproj(tok, layer, x_all, mod, norm_mix, w_in_p, wc, band, invc, pool_w_bd, pool_scale, widths):
    d = x_all.shape[-1]
    fw, hw, pw, zw, cw = widths
    tm = TOKEN_TILE
    n_tok = tok.n_tok
    kind = lambda t: jnp.where(tok.is_ctx(t), 1, 0)
    const2 = lambda t: (0, 0)
    outs = [
        jax.ShapeDtypeStruct((n_tok, 2 * fw), BF16),
        jax.ShapeDtypeStruct((n_tok, hw), BF16),
        jax.ShapeDtypeStruct((n_tok, pw), BF16),
        jax.ShapeDtypeStruct((n_tok, zw), BF16),
        jax.ShapeDtypeStruct((n_tok, cw), BF16),
        jax.ShapeDtypeStruct((n_tok, LANE), F32),
    ]
    return pl.pallas_call(
        functools.partial(_inproj_kernel, widths=widths),
        grid=(tok.n_tiles,),
        in_specs=[
            pl.BlockSpec((tm, d), lambda t: (t, 0)),
            tok.mod_spec(layer, 1, d),
            tok.mod_spec(layer, 0, d),
            pl.BlockSpec((None, 1, d), lambda t: (layer, 0, 0)),
            pl.BlockSpec((None,) + w_in_p.shape[1:], lambda t: (layer, 0, 0)),
            pl.BlockSpec(wc.shape, const2),
            pl.BlockSpec((None,) + band.shape[1:], lambda t: (kind(t), 0, 0, 0)),
            pl.BlockSpec((None,) + invc.shape[1:], lambda t: (kind(t), 0, 0)),
            pl.BlockSpec((None,) + pool_w_bd.shape[1:], lambda t: (layer, 0, 0)),
            pl.BlockSpec((None, 1, pw), lambda t: (layer, 0, 0)),
        ],
        out_specs=[pl.BlockSpec((tm, s.shape[1]), lambda t: (t, 0)) for s in outs],
        out_shape=outs,
        compiler_params=_cparams(("arbitrary",)),
        name="inproj",
    )(x_all, mod, mod, norm_mix, w_in_p, wc, band, invc, pool_w_bd, pool_scale)


def _trig_tables(n, period):
    k = jnp.arange(n, dtype=jnp.int32)

    def cs(j):
        m = (j[:, None] * k[None, :]) % period
        ang = m.astype(F32) * (2.0 * math.pi / period)
        return jnp.cos(ang), jnp.sin(ang)

    split = 64
    if n % split or n <= split:
        return cs(k)
    c1, s1 = cs(jnp.arange(n // split, dtype=jnp.int32) * split)
    c2, s2 = cs(jnp.arange(split, dtype=jnp.int32))
    c = c1[:, None, :] * c2[None, :, :] - s1[:, None, :] * s2[None, :, :]
    s = s1[:, None, :] * c2[None, :, :] + c1[:, None, :] * s2[None, :, :]
    return c.reshape(n, n), s.reshape(n, n)


def _hyena_dft_tables(n):
    c, s = _trig_tables(n, 2 * n)
    nyq = jnp.where(jnp.arange(n) % 2 == 0, 1.0, -1.0).astype(F32)
    s = (-s).at[0].set(nyq)
    fwd = jnp.concatenate([c, s], axis=0).astype(BF16)
    return fwd, fwd.T


def _fourier_kernel(c_ref, s_ref, xc_ref, xs_ref, o_ref, acc_ref, *, scale):
    k = pl.program_id(1)

    @pl.when(k == 0)
    def _():
        acc_ref[...] = jnp.zeros_like(acc_ref)

    nb = xc_ref.shape[0]
    rc = jnp.concatenate([xc_ref[b] for b in range(nb)], axis=1)
    rs = jnp.concatenate([xs_ref[b] for b in range(nb)], axis=1)
    acc_ref[...] += _dot(c_ref[...], rc) - _dot(s_ref[...], rs)

    @pl.when(k == pl.num_programs(1) - 1)
    def _():
        w = o_ref.shape[2]
        for b in range(nb):
            o_ref[b] = (acc_ref[:, b * w:(b + 1) * w] * scale).astype(o_ref.dtype)


def _fourier_seq(xcs3, cn, sn, n, row_off, fw):
    batch = xcs3.shape[0]
    tl = min(SEQ_TILE, n)
    nt = n // tl
    off = row_off // tl
    scale = 1.0 / math.sqrt(n * FOURIER_GROUP_W)
    return pl.pallas_call(
        functools.partial(_fourier_kernel, scale=scale),
        grid=(nt, nt),
        in_specs=[
            pl.BlockSpec((tl, tl), lambda i, k: (i, k)),
            pl.BlockSpec((tl, tl), lambda i, k: (i, k)),
            pl.BlockSpec((batch, tl, fw), lambda i, k: (0, off + k, 0)),
            pl.BlockSpec((batch, tl, fw), lambda i, k: (0, off + k, 1)),
        ],
        out_specs=pl.BlockSpec((batch, tl, fw), lambda i, k: (0, i, 0)),
        out_shape=jax.ShapeDtypeStruct((batch, n, fw), BF16),
        scratch_shapes=[pltpu.VMEM((tl, batch * fw), F32)],
        compiler_params=_cparams(("arbitrary", "arbitrary")),
        name="fourier_seq",
    )(cn, sn, xcs3, xcs3)


def _shift_rows(x, prev_row, next_row):
    n = x.shape[0]
    rows = lax.broadcasted_iota(jnp.int32, x.shape, 0)
    xm = jnp.where(rows == 0, prev_row, pltpu.roll(x, 1, 0))
    xp = jnp.where(rows == n - 1, next_row, pltpu.roll(x, n - 1, 0))
    return xm, xp


def _conv3(x, prev_row, next_row, w_ref, b_ref):
    xm, xp = _shift_rows(x, prev_row, next_row)
    return w_ref[0:1, :] * xm + w_ref[1:2, :] * x + w_ref[2:3, :] * xp + b_ref[...]


def _hy_pre_kernel(x_ref, xp_ref, xn_ref, w_ref, b_ref, vv_ref, x1_ref, *, hw):
    i = pl.program_id(1)
    last = pl.num_programs(1) - 1
    x = x_ref[...].astype(F32)
    hs = xp_ref.shape[0]
    prev_row = jnp.where(i == 0, 0.0, xp_ref[hs - 1:hs, :].astype(F32))
    next_row = jnp.where(i == last, 0.0, xn_ref[0:1, :].astype(F32))
    u = _conv3(x, prev_row, next_row, w_ref, b_ref)
    x1_ref[...] = u[:, :hw].astype(BF16)
    vv_ref[...] = (u[:, hw:2 * hw] * u[:, 2 * hw:]).astype(BF16)


def _halo_specs(width, tile, row_off, seq_rows):
    hs = BF16_SUBLANE
    per = tile // hs
    base = row_off // hs
    top = seq_rows // hs - 1
    prev = pl.BlockSpec((None, hs, width), lambda b, i: (b, jnp.maximum(base + i * per - 1, 0), 0))
    nxt = pl.BlockSpec((None, hs, width), lambda b, i: (b, jnp.minimum(base + (i + 1) * per, top), 0))
    return prev, nxt


def _hy_pre(hy3, conv_w, conv_b, layer, n, row_off, hw):
    batch, seq, w3 = hy3.shape
    tl = min(SEQ_TILE, n)
    off = row_off // tl
    prev, nxt = _halo_specs(w3, tl, row_off, seq)
    out = jax.ShapeDtypeStruct((batch, n, hw), BF16)
    return pl.pallas_call(
        functools.partial(_hy_pre_kernel, hw=hw),
        grid=(batch, n // tl),
        in_specs=[
            pl.BlockSpec((None, tl, w3), lambda b, i: (b, off + i, 0)),
            prev, nxt,
            pl.BlockSpec((None,) + conv_w.shape[1:], lambda b, i: (layer, 0, 0)),
            pl.BlockSpec((None, 1, w3), lambda b, i: (layer, 0, 0)),
        ],
        out_specs=[pl.BlockSpec((None, tl, hw), lambda b, i: (b, i, 0))] * 2,
        out_shape=[out, out],
        compiler_params=_cparams(("arbitrary", "arbitrary")),
        name="hyena_pre",
    )(hy3, hy3, hy3, conv_w, conv_b)


def _filter_mlp_kernel(fr_ref, dl_ref, w1_ref, b1_ref, w2_ref, b2_ref, w3_ref, g_ref, nrm_ref, *, n, hw):
    j = pl.program_id(1)
    rt = g_ref.shape[0]
    lane = lax.broadcasted_iota(jnp.int32, (rt, LANE), 1)
    pos = (lax.broadcasted_iota(jnp.int32, (rt, LANE), 0) + j * rt).astype(F32)
    t = pos / (n - 1)
    ang = 2.0 * math.pi * pos / n
    arg = ang * fr_ref[...]
    feats = jnp.where(lane == 0, t,
                      jnp.where(lane <= HYENA_BANDS, jnp.cos(arg),
                                jnp.where(lane <= 2 * HYENA_BANDS, -jnp.sin(arg), 0.0)))
    h1 = jnp.sin(_dot_hi(feats, w1_ref[...]) + b1_ref[...])
    h2 = jnp.sin(_dot_hi(h1, w2_ref[...]) + b2_ref[...])
    k = _dot_hi(h2, w3_ref[...])
    decay = jnp.exp(-t[:, 0:1] * dl_ref[...])
    kf = k[:, :hw] * decay
    kb = jnp.where(pos[:, 0:1] == 0.0, 0.0, k[:, hw:] * decay)
    g_ref[:, :hw] = (kf + kb).astype(BF16)
    g_ref[:, hw:] = (kf - kb).astype(BF16)
    part = jnp.sum(jnp.abs(kf) + jnp.abs(kb), axis=0, keepdims=True)

    @pl.when(j == 0)
    def _():
        nrm_ref[...] = part

    @pl.when(j > 0)
    def _():
        nrm_ref[...] += part


def _filter_mlp(n, freqs_row, deltas_row, w1, b1, w2, b2, w3):
    depth = w1.shape[0]
    hw = deltas_row.shape[1]
    rt = min(SEQ_TILE, n)
    per_layer = lambda l, j: (l, 0, 0)
    return pl.pallas_call(
        functools.partial(_filter_mlp_kernel, n=n, hw=hw),
        grid=(depth, n // rt),
        in_specs=[
            pl.BlockSpec((1, LANE), lambda l, j: (0, 0)),
            pl.BlockSpec((1, hw), lambda l, j: (0, 0)),
            pl.BlockSpec((None,) + w1.shape[1:], per_layer),
            pl.BlockSpec((None,) + b1.shape[1:], per_layer),
            pl.BlockSpec((None,) + w2.shape[1:], per_layer),
            pl.BlockSpec((None,) + b2.shape[1:], per_layer),
            pl.BlockSpec((None,) + w3.shape[1:], per_layer),
        ],
        out_specs=[
            pl.BlockSpec((rt, 2 * hw), lambda l, j: (j, l)),
            pl.BlockSpec((None, 1, hw), lambda l, j: (l, 0, 0)),
        ],
        out_shape=[
            jax.ShapeDtypeStruct((n, depth * 2 * hw), BF16),
            jax.ShapeDtypeStruct((depth, 1, hw), F32),
        ],
        compiler_params=_cparams(("arbitrary", "arbitrary")),
        name="hyena_filter_mlp",
    )(freqs_row, deltas_row, w1, b1, w2, b2, w3)


def _rdft_accumulate(fc_ref, fs_ref, rhs, accc_ref, accs_ref):
    k = pl.program_id(1)

    @pl.when(k == 0)
    def _():
        accc_ref[...] = jnp.zeros_like(accc_ref)
        accs_ref[...] = jnp.zeros_like(accs_ref)

    accc_ref[...] += _dot(fc_ref[...], rhs)
    accs_ref[...] += _dot(fs_ref[...], rhs)


def _filter_spec_kernel(fc_ref, fs_ref, g_ref, nrm_ref, kr_ref, ki_ref, accc_ref, accs_ref, *, hw):
    i = pl.program_id(0)
    _rdft_accumulate(fc_ref, fs_ref, g_ref[...], accc_ref, accs_ref)

    @pl.when(pl.program_id(1) == pl.num_programs(1) - 1)
    def _():
        tm = accc_ref.shape[0]
        row0 = (lax.broadcasted_iota(jnp.int32, (tm, hw), 0) + i * tm) == 0
        for l in range(kr_ref.shape[0]):
            inv = 1.0 / nrm_ref[l]
            o = l * 2 * hw
            kr_ref[l] = accc_ref[:, o:o + hw] * inv
            ki_ref[l] = jnp.where(row0, accs_ref[:, o:o + hw], accs_ref[:, o + hw:o + 2 * hw]) * inv


def _filter_spectrum(fwd, g_all, nrm, n, hw):
    depth = nrm.shape[0]
    tl = min(SEQ_TILE, n)
    nt = n // tl
    width = g_all.shape[1]
    out = jax.ShapeDtypeStruct((depth, n, hw), F32)
    return pl.pallas_call(
        functools.partial(_filter_spec_kernel, hw=hw),
        grid=(nt, nt),
        in_specs=[
            pl.BlockSpec((tl, tl), lambda i, k: (i, k)),
            pl.BlockSpec((tl, tl), lambda i, k: (nt + i, k)),
            pl.BlockSpec((tl, width), lambda i, k: (k, 0)),
            pl.BlockSpec(nrm.shape, lambda i, k: (0, 0, 0)),
        ],
        out_specs=[pl.BlockSpec((depth, tl, hw), lambda i, k: (0, i, 0))] * 2,
        out_shape=[out, out],
        scratch_shapes=[pltpu.VMEM((tl, width), F32)] * 2,
        compiler_params=_cparams(("arbitrary", "arbitrary")),
        name="hyena_filter_spectrum",
    )(fwd, fwd, g_all, nrm)


def _hy_fwd_kernel(fc_ref, fs_ref, vv_ref, kr_ref, ki_ref, y_ref, accc_ref, accs_ref):
    i = pl.program_id(0)
    nb = vv_ref.shape[0]
    rhs = jnp.concatenate([vv_ref[b] for b in range(nb)], axis=1)
    _rdft_accumulate(fc_ref, fs_ref, rhs, accc_ref, accs_ref)

    @pl.when(pl.program_id(1) == pl.num_programs(1) - 1)
    def _():
        tm, hw = kr_ref.shape
        row0 = (lax.broadcasted_iota(jnp.int32, (tm, hw), 0) + i * tm) == 0
        kr, ki = kr_ref[...], ki_ref[...]
        for b in range(nb):
            vr = accc_ref[:, b * hw:(b + 1) * hw]
            vi = accs_ref[:, b * hw:(b + 1) * hw]
            yr = jnp.where(row0, 0.5 * vr * kr, vr * kr - vi * ki)
            yi = jnp.where(row0, 0.5 * vi * ki, vr * ki + vi * kr)
            y_ref[0, :, b * hw:(b + 1) * hw] = yr.astype(BF16)
            y_ref[1, :, b * hw:(b + 1) * hw] = yi.astype(BF16)


def _hy_fwd(fwd, vv, kr, ki, layer, n, hw):
    batch = vv.shape[0]
    tl = min(SEQ_TILE, n)
    nt = n // tl
    return pl.pallas_call(
        _hy_fwd_kernel,
        grid=(nt, nt),
        in_specs=[
            pl.BlockSpec((tl, tl), lambda i, k: (i, k)),
            pl.BlockSpec((tl, tl), lambda i, k: (nt + i, k)),
            pl.BlockSpec((batch, tl, hw), lambda i, k: (0, k, 0)),
            pl.BlockSpec((None, tl, hw), lambda i, k: (layer, i, 0)),
            pl.BlockSpec((None, tl, hw), lambda i, k: (layer, i, 0)),
        ],
        out_specs=pl.BlockSpec((2, tl, batch * hw), lambda i, k: (0, i, 0)),
        out_shape=jax.ShapeDtypeStruct((2, n, batch * hw), BF16),
        scratch_shapes=[pltpu.VMEM((tl, batch * hw), F32)] * 2,
        compiler_params=_cparams(("arbitrary", "arbitrary")),
        name="hyena_fwd_dft",
    )(fwd, fwd, vv, kr, ki)


def _hy_inv_kernel(fi_ref, y_ref, x1_ref, vv_ref, bias_ref, o_ref, acc_ref, *, scale):
    k = pl.program_id(1)

    @pl.when(k == 0)
    def _():
        acc_ref[...] = jnp.zeros_like(acc_ref)

    acc_ref[...] += _dot(fi_ref[...], y_ref[...])

    @pl.when(k == pl.num_programs(1) - 1)
    def _():
        nb, _, hw = x1_ref.shape
        for b in range(nb):
            conv = acc_ref[:, b * hw:(b + 1) * hw] * scale
            vv = vv_ref[b].astype(F32)
            o_ref[b] = (x1_ref[b].astype(F32) * (conv + vv * bias_ref[...])).astype(BF16)


def _hy_inv(inv, y2, x1, vv, hy_bias, layer, n, hw):
    batch = vv.shape[0]
    tl = min(SEQ_TILE, n)
    nt = n // tl
    return pl.pallas_call(
        functools.partial(_hy_inv_kernel, scale=1.0 / n),
        grid=(nt, 2 * nt),
        in_specs=[
            pl.BlockSpec((tl, tl), lambda i, k: (i, k)),
            pl.BlockSpec((tl, batch * hw), lambda i, k: (k, 0)),
            pl.BlockSpec((batch, tl, hw), lambda i, k: (0, i, 0)),
            pl.BlockSpec((batch, tl, hw), lambda i, k: (0, i, 0)),
            pl.BlockSpec((None, 1, hw), lambda i, k: (layer, 0, 0)),
        ],
        out_specs=pl.BlockSpec((batch, tl, hw), lambda i, k: (0, i, 0)),
        out_shape=jax.ShapeDtypeStruct((batch, n, hw), BF16),
        scratch_shapes=[pltpu.VMEM((tl, batch * hw), F32)],
        compiler_params=_cparams(("arbitrary", "arbitrary")),
        name="hyena_inv_dft",
    )(inv, y2.reshape(2 * n, batch * hw), x1, vv, hy_bias)


def _softplus(x):
    return jnp.maximum(x, 0.0) + jnp.log1p(jnp.exp(-jnp.abs(x)))


def _ssd_kernel(*refs, reverse, final, nc, lane0):
    if final:
        (xbc_ref, xp_ref, xn_ref, dt_ref, h0_ref, tri_ref, cw_ref, cb_ref, dtb_ref, a_ref,
         z_ref, yf_ref, d_ref, nrm_ref, out_ref, hout_ref, h_scr) = refs
    else:
        (xbc_ref, xp_ref, xn_ref, dt_ref, h0_ref, tri_ref, cw_ref, cb_ref, dtb_ref, a_ref,
         out_ref, hout_ref, h_scr) = refs
    step = pl.program_id(1)
    cc = (nc - 1 - step) if reverse else step

    @pl.when(step == 0)
    def _():
        h_scr[...] = h0_ref[...]

    q = xbc_ref.shape[0]
    hs = xp_ref.shape[0]
    x = xbc_ref[...].astype(F32)
    prev_row = jnp.where(cc == 0, 0.0, xp_ref[hs - 1:hs, :].astype(F32))
    next_row = jnp.where(cc == nc - 1, 0.0, xn_ref[0:1, :].astype(F32))
    u = _conv3(x, prev_row, next_row, cw_ref, cb_ref)
    u = u * _sigmoid(u)
    width = SSM_HEADS * SSM_HEAD_DIM
    gs = SSM_STATE
    xs = u[:, :width]
    bm = u[:, width:width + SSM_GROUPS * gs]
    cm = u[:, width + SSM_GROUPS * gs:]

    dt = _softplus(dt_ref[...] + dtb_ref[...])
    acs = _dot_exact01(tri_ref[...], dt * a_ref[...])
    acs_t = acs.T
    tot = acs[0:1, :] if reverse else acs[q - 1:q, :]
    to_end = jnp.exp(tot - acs)
    frm = jnp.exp(acs)
    cdec = jnp.exp(tot)
    ri = lax.broadcasted_iota(jnp.int32, (q, q), 0)
    ci = lax.broadcasted_iota(jnp.int32, (q, q), 1)
    mask = (ri <= ci) if reverse else (ri >= ci)

    pd = SSM_HEAD_DIM
    ys = []
    for g in range(SSM_GROUPS):
        bg = bm[:, g * gs:(g + 1) * gs]
        cg = cm[:, g * gs:(g + 1) * gs].astype(BF16)
        scores = lax.dot_general(cg, bg.astype(BF16), (((1,), (1,)), ((), ())), preferred_element_type=F32)
        hg = h_scr[g]
        yoff = _dot(cg, hg.astype(BF16))
        wparts, decs = [], []
        for r in range(HEADS_PER_GROUP):
            hd = g * HEADS_PER_GROUP + r
            li = lane0 + hd
            lm = jnp.where(mask, jnp.exp(acs[:, li:li + 1] - acs_t[li:li + 1, :]), 0.0)
            m = (scores * lm).astype(BF16)
            xdt = xs[:, hd * pd:(hd + 1) * pd] * dt[:, li:li + 1]
            yd = _dot(m, xdt.astype(BF16))
            ys.append(yd + yoff[:, r * pd:(r + 1) * pd] * frm[:, li:li + 1])
            wparts.append(xdt * to_end[:, li:li + 1])
            decs.append(jnp.broadcast_to(cdec[:, li:li + 1], (1, pd)))
        wg = jnp.concatenate(wparts, axis=1).astype(BF16)
        st = _dot(bg.T.astype(BF16), wg)
        h_scr[g] = hg * jnp.concatenate(decs, axis=1) + st
    y = jnp.concatenate(ys, axis=1)

    @pl.when(step == nc - 1)
    def _():
        hout_ref[...] = h_scr[...]

    if final:
        y = y + yf_ref[...] + xs * d_ref[...]
        z = z_ref[...].astype(F32)
        y = y * (z * _sigmoid(z))
        ms = jnp.mean(y * y, axis=-1, keepdims=True)
        out_ref[...] = ((y * lax.rsqrt(ms + EPS)) * nrm_ref[...]).astype(out_ref.dtype)
    else:
        out_ref[...] = y


def _ssd_pass(xbc3, dt3, h0, tri, conv_w, conv_b, dt_bias, a_row, layer, n, row_off, *, reverse,
              fin=None):
    batch, seq, cw = xbc3.shape
    q = SSM_CHUNK
    nc = n // q
    off = row_off // q
    width = SSM_HEADS * SSM_HEAD_DIM

    def cidx(c):
        return off + ((nc - 1 - c) if reverse else c)

    hsz = BF16_SUBLANE
    per = q // hsz
    top = seq // hsz - 1
    per_layer = lambda b, c: (layer, 0, 0)
    in_specs = [
        pl.BlockSpec((None, q, cw), lambda b, c: (b, cidx(c), 0)),
        pl.BlockSpec((None, hsz, cw), lambda b, c: (b, jnp.maximum(cidx(c) * per - 1, 0), 0)),
        pl.BlockSpec((None, hsz, cw), lambda b, c: (b, jnp.minimum((cidx(c) + 1) * per, top), 0)),
        pl.BlockSpec((None, q, LANE), lambda b, c: (b, cidx(c), 0)),
        pl.BlockSpec((None,) + h0.shape[1:], lambda b, c: (b, 0, 0, 0)),
        pl.BlockSpec((q, q), lambda b, c: (0, 0)),
        pl.BlockSpec((None,) + conv_w.shape[1:], per_layer),
        pl.BlockSpec((None, 1, cw), per_layer),
        pl.BlockSpec((None, 1, LANE), per_layer),
        pl.BlockSpec((None, 1, LANE), per_layer),
    ]
    args = [xbc3, xbc3, xbc3, dt3, h0, tri, conv_w, conv_b, dt_bias, a_row]
    if fin is not None:
        z3, yf, d_row, nrm_row = fin
        in_specs += [
            pl.BlockSpec((None, q, width), lambda b, c: (b, cidx(c), 0)),
            pl.BlockSpec((None, q, width), lambda b, c: (b, cidx(c) - off, 0)),
            pl.BlockSpec((None, 1, width), per_layer),
            pl.BlockSpec((None, 1, width), per_layer),
        ]
        args += [z3, yf, d_row, nrm_row]
    out_dtype = BF16 if fin is not None else F32
    return pl.pallas_call(
        functools.partial(_ssd_kernel, reverse=reverse, final=fin is not None, nc=nc,
                          lane0=SSM_HEADS if reverse else 0),
        grid=(batch, nc),
        in_specs=in_specs,
        out_specs=[
            pl.BlockSpec((None, q, width), lambda b, c: (b, cidx(c) - off, 0)),
            pl.BlockSpec((None,) + h0.shape[1:], lambda b, c: (b, 0, 0, 0)),
        ],
        out_shape=[
            jax.ShapeDtypeStruct((batch, n, width), out_dtype),
            jax.ShapeDtypeStruct(h0.shape, F32),
        ],
        scratch_shapes=[pltpu.VMEM(h0.shape[1:], F32)],
        compiler_params=_cparams(("arbitrary", "arbitrary")),
        name="ssd_bwd_final" if reverse else "ssd_fwd",
    )(*args)


def _merge_kernel(x_ref, sc_ref, sh_ref, g1_ref, gn_ref, fl_ref, fc_ref, hl_ref, hc_ref, p_ref, sl_ref,
                  scx_ref, wg_ref, wb_ref, wo_ref, o_ref, *, tok, splits):
    ctx = tok.is_ctx(pl.program_id(0))
    x = x_ref[...]
    hb = _norm_mod(x, gn_ref[...], sc_ref[...], sh_ref[...]).astype(BF16)
    ys = (jnp.where(ctx, fc_ref[...], fl_ref[...]), jnp.where(ctx, hc_ref[...], hl_ref[...]),
          p_ref[...], jnp.where(ctx, scx_ref[...], sl_ref[...]))
    merged = None
    lo = 0
    for k, hi in enumerate(splits):
        gate = _sigmoid(_dot(hb, wg_ref[k]))
        term = gate * _dot(ys[k], wb_ref[lo:hi, :])
        merged = term if merged is None else merged + term
        lo = hi
    out = _dot(merged.astype(BF16), wo_ref[...])
    o_ref[...] = x + g1_ref[...] * out


def _merge(tok, layer, x_all, mod, norm_mix, f_lat, f_ctx, h_lat, h_ctx, pool, s_lat, s_ctx,
           w_gate, w_branch, w_out, splits):
    d = x_all.shape[-1]
    tm = TOKEN_TILE
    fw, hw, pw, sw = f_lat.shape[-1], h_lat.shape[-1], pool.shape[-1], s_lat.shape[-1]
    return pl.pallas_call(
        functools.partial(_merge_kernel, tok=tok, splits=splits),
        grid=(tok.n_tiles,),
        in_specs=[
            pl.BlockSpec((tm, d), lambda t: (t, 0)),
            tok.mod_spec(layer, 1, d),
            tok.mod_spec(layer, 0, d),
            tok.mod_spec(layer, 2, d),
            pl.BlockSpec((None, 1, d), lambda t: (layer, 0, 0)),
            tok.lat_spec(fw), tok.ctx_spec(fw),
            tok.lat_spec(hw), tok.ctx_spec(hw),
            pl.BlockSpec((tm, pw), lambda t: (t, 0)),
            tok.lat_spec(sw), tok.ctx_spec(sw),
            pl.BlockSpec((None,) + w_gate.shape[1:], lambda t: (layer, 0, 0, 0)),
            pl.BlockSpec((None,) + w_branch.shape[1:], lambda t: (layer, 0, 0)),
            pl.BlockSpec((None,) + w_out.shape[1:], lambda t: (layer, 0, 0)),
        ],
        out_specs=pl.BlockSpec((tm, d), lambda t: (t, 0)),
        out_shape=jax.ShapeDtypeStruct(x_all.shape, F32),
        compiler_params=_cparams(("arbitrary",)),
        name="merge",
    )(x_all, mod, mod, mod, norm_mix, f_lat, f_ctx, h_lat, h_ctx, pool, s_lat, s_ctx,
      w_gate, w_branch, w_out)


def _router_kernel(x_ref, sc_ref, sh_ref, gn_ref, rw_ref, rb_ref, ltri_ref,
                   h_ref, e_ref, w_ref, r_ref, cnt_ref, *, n_exp):
    t = pl.program_id(0)

    @pl.when(t == 0)
    def _():
        cnt_ref[...] = jnp.zeros_like(cnt_ref)

    h = _norm_mod(x_ref[...], gn_ref[...], sc_ref[...], sh_ref[...])
    h_ref[...] = h.astype(BF16)
    tm = h.shape[0]
    lane = lax.broadcasted_iota(jnp.int32, (tm, LANE), 1)
    logits = jnp.where(lane < n_exp, _dot_hi(h, rw_ref[...]) + rb_ref[...], -jnp.inf)
    carry = cnt_ref[0:1, :]
    tops, sels = [], []
    cur = logits
    for _ in range(TOP_K):
        m = jnp.max(cur, axis=-1, keepdims=True)
        idx = jnp.min(jnp.where(cur == m, lane, LANE), axis=-1, keepdims=True)
        sel = lane == idx
        tops.append((m, idx))
        sels.append(sel)
        cur = jnp.where(sel, -jnp.inf, cur)
    chosen = sels[0]
    for s in sels[1:]:
        chosen = chosen | s
    cnt = jnp.where(chosen, 1.0, 0.0)
    before = _dot(ltri_ref[...], cnt.astype(BF16)) + carry
    exps = [jnp.exp(m - tops[0][0]) for m, _ in tops]
    den = exps[0]
    for v in exps[1:]:
        den = den + v
    e_out = jnp.zeros((tm, LANE), jnp.int32)
    w_out = jnp.zeros((tm, LANE), F32)
    r_out = jnp.zeros((tm, LANE), jnp.int32)
    for k in range(TOP_K):
        rank = jnp.sum(jnp.where(sels[k], before, 0.0), axis=-1, keepdims=True)
        e_out = jnp.where(lane == k, tops[k][1], e_out)
        w_out = jnp.where(lane == k, exps[k] / den, w_out)
        r_out = jnp.where(lane == k, rank.astype(jnp.int32), r_out)
    e_ref[...] = e_out
    w_ref[...] = w_out
    r_ref[...] = r_out
    cnt_ref[...] += jnp.broadcast_to(jnp.sum(cnt, axis=0, keepdims=True), cnt_ref.shape)


def _router(tok, layer, x_all, mod, norm_ffn, router_w_p, router_b_p, ltri, n_exp):
    d = x_all.shape[-1]
    tm = TOKEN_TILE
    n_tok = tok.n_tok
    lane_i = jax.ShapeDtypeStruct((n_tok, LANE), jnp.int32)
    return pl.pallas_call(
        functools.partial(_router_kernel, n_exp=n_exp),
        grid=(tok.n_tiles,),
        in_specs=[
            pl.BlockSpec((tm, d), lambda t: (t, 0)),
            tok.mod_spec(layer, 4, d),
            tok.mod_spec(layer, 3, d),
            pl.BlockSpec((None, 1, d), lambda t: (layer, 0, 0)),
            pl.BlockSpec((None, d, LANE), lambda t: (layer, 0, 0)),
            pl.BlockSpec((None, 1, LANE), lambda t: (layer, 0, 0)),
            pl.BlockSpec((tm, tm), lambda t: (0, 0)),
        ],
        out_specs=[
            pl.BlockSpec((tm, d), lambda t: (t, 0)),
            pl.BlockSpec((tm, LANE), lambda t: (t, 0)),
            pl.BlockSpec((tm, LANE), lambda t: (t, 0)),
            pl.BlockSpec((tm, LANE), lambda t: (t, 0)),
            pl.BlockSpec((8, LANE), lambda t: (0, 0)),
        ],
        out_shape=[
            jax.ShapeDtypeStruct((n_tok, d), BF16),
            lane_i,
            jax.ShapeDtypeStruct((n_tok, LANE), F32),
            lane_i,
            jax.ShapeDtypeStruct((8, LANE), F32),
        ],
        compiler_params=_cparams(("arbitrary",)),
        name="router",
    )(x_all, mod, mod, norm_ffn, router_w_p, router_b_p, ltri)


def _moe_kernel(be_ref, na_ref, x_ref, wu_ref, bu_ref, wd_ref, bd_ref, o_ref, wu_scr, wd_scr, *, ff):
    i = pl.program_id(0)
    active = i < na_ref[0]
    fresh = jnp.logical_or(i == 0, be_ref[i] != be_ref[jnp.maximum(i - 1, 0)])

    @pl.when(jnp.logical_and(active, fresh))
    def _():
        wu_scr[...] = wu_ref[...].astype(BF16)
        wd_scr[...] = wd_ref[...].astype(BF16)

    @pl.when(active)
    def _():
        hu = _dot(x_ref[...], wu_scr[...]) + bu_ref[...]
        glu = jnp.minimum(hu[:, :ff], SWIGLU_LIMIT)
        lin = jnp.clip(hu[:, ff:], -SWIGLU_LIMIT, SWIGLU_LIMIT)
        act = glu * _sigmoid(SWIGLU_ALPHA * glu) * (lin + 1.0)
        o_ref[...] = (_dot(act.astype(BF16), wd_scr[...]) + bd_ref[...]).astype(o_ref.dtype)

    @pl.when(jnp.logical_not(active))
    def _():
        o_ref[...] = jnp.zeros_like(o_ref)


def _moe(layer, x_sorted, blk_e, n_active, w_up, b_up, w_down, b_down):
    n_rows, d = x_sorted.shape
    ff = w_down.shape[2]
    bm = MOE_ROWS
    grid_spec = pltpu.PrefetchScalarGridSpec(
        num_scalar_prefetch=2,
        grid=(n_rows // bm,),
        in_specs=[
            pl.BlockSpec((bm, d), lambda i, be, na: (i, 0)),
            pl.BlockSpec((None, None, d, 2 * ff), lambda i, be, na: (layer, be[i], 0, 0)),
            pl.BlockSpec((None, None, 1, 2 * ff), lambda i, be, na: (layer, be[i], 0, 0)),
            pl.BlockSpec((None, None, ff, d), lambda i, be, na: (layer, be[i], 0, 0)),
            pl.BlockSpec((None, None, 1, d), lambda i, be, na: (layer, be[i], 0, 0)),
        ],
        out_specs=pl.BlockSpec((bm, d), lambda i, be, na: (i, 0)),
        scratch_shapes=[pltpu.VMEM((d, 2 * ff), BF16), pltpu.VMEM((ff, d), BF16)],
    )
    return pl.pallas_call(
        functools.partial(_moe_kernel, ff=ff),
        grid_spec=grid_spec,
        out_shape=jax.ShapeDtypeStruct((n_rows, d), BF16),
        compiler_params=_cparams(("arbitrary",)),
        name="moe_experts",
    )(blk_e, n_active, x_sorted, w_up, b_up, w_down, b_down)


def _combine_kernel(x_ref, g2_ref, yg_ref, w_ref, o_ref):
    d = x_ref.shape[1]
    w = w_ref[...]
    y = None
    for k in range(TOP_K):
        term = w[:, k:k + 1] * yg_ref[:, k * d:(k + 1) * d].astype(F32)
        y = term if y is None else y + term
    o_ref[...] = x_ref[...] + g2_ref[...] * y


def _combine(tok, layer, x_all, mod, y_gath, top_w):
    d = x_all.shape[-1]
    tm = TOKEN_TILE
    return pl.pallas_call(
        _combine_kernel,
        grid=(tok.n_tiles,),
        in_specs=[
            pl.BlockSpec((tm, d), lambda t: (t, 0)),
            tok.mod_spec(layer, 5, d),
            pl.BlockSpec((tm, TOP_K * d), lambda t: (t, 0)),
            pl.BlockSpec((tm, LANE), lambda t: (t, 0)),
        ],
        out_specs=pl.BlockSpec((tm, d), lambda t: (t, 0)),
        out_shape=jax.ShapeDtypeStruct(x_all.shape, F32),
        compiler_params=_cparams(("arbitrary",)),
        name="moe_combine",
    )(x_all, mod, y_gath, top_w)


def _final_norm_kernel(x_ref, g_ref, o_ref):
    x = x_ref[...]
    ms = jnp.mean(x * x, axis=-1, keepdims=True)
    o_ref[...] = (x * lax.rsqrt(ms + EPS)) * g_ref[...]


def _final_norm(x3, gain, n_lat):
    batch, _, d = x3.shape
    tm = TOKEN_TILE
    return pl.pallas_call(
        _final_norm_kernel,
        grid=(batch, n_lat // tm),
        in_specs=[pl.BlockSpec((None, tm, d), lambda b, i: (b, i, 0)),
                  pl.BlockSpec((1, d), lambda b, i: (0, 0))],
        out_specs=pl.BlockSpec((None, tm, d), lambda b, i: (b, i, 0)),
        out_shape=jax.ShapeDtypeStruct((batch, n_lat, d), F32),
        compiler_params=_cparams(("arbitrary", "arbitrary")),
        name="final_norm",
    )(x3, gain)


def _dispatch_plan(top_e, rank, counts, n_tok):
    n_exp = counts.shape[0]
    bm = MOE_ROWS
    n_pairs = n_tok * TOP_K
    n_blk = -(-(n_pairs + n_exp * (bm - 1)) // bm)
    padded = (counts + bm - 1) // bm * bm
    pad_end = jnp.cumsum(padded)
    pad_start = pad_end - padded
    dest = pad_start[top_e] + rank
    n_active = (pad_end[-1] // bm).astype(jnp.int32).reshape(1)
    blk_e = jnp.minimum(jnp.searchsorted(pad_end, jnp.arange(n_blk) * bm, side='right'),
                        n_exp - 1).astype(jnp.int32)
    tok_ids = jnp.broadcast_to(jnp.arange(n_tok, dtype=jnp.int32)[:, None], dest.shape)
    row_tok = jnp.full((n_blk * bm,), n_tok, jnp.int32).at[dest.reshape(-1)].set(tok_ids.reshape(-1))
    return dest, row_tok, blk_e, n_active


def kernel(x, c, ctx, c_ctx, w_mod, b_mod, norm_mix, norm_ffn, w_in, hy_conv_w, hy_conv_b, hy_ffn_w1, hy_ffn_b1, hy_ffn_w2, hy_ffn_b2, hy_ffn_w3, hy_bias, pool_w, pool_scale, ssm_conv_w, ssm_conv_b, ssm_dt_bias, ssm_a_log, ssm_d, ssm_norm, w_branch, w_gate, w_out, router_w, router_b, exp_w_up, exp_b_up, exp_w_down, exp_b_down, norm_final):
    batch, n_lat, d = x.shape
    n_ctx = ctx.shape[1]
    depth = w_mod.shape[0]
    tok = _Tok(batch, n_lat, n_ctx)
    n_tok = tok.n_tok
    assert n_lat // GRID_W * GRID_W == n_lat and n_lat % n_ctx == 0

    hw = hy_bias.shape[1]
    pw = pool_scale.shape[1]
    sw = ssm_norm.shape[1]
    cw = ssm_conv_w.shape[2]
    fw = w_in.shape[2] - 3 * hw - pw - sw - cw - 2 * SSM_HEADS
    widths = (fw, 3 * hw, pw, sw, cw)
    splits = (fw, fw + hw, fw + hw + pw, fw + hw + pw + sw)
    n_exp = router_w.shape[2]

    w_in_p = jnp.pad(w_in, ((0, 0), (0, 0), (0, LANE - 2 * SSM_HEADS))).astype(BF16)
    w_gate_b = w_gate.astype(BF16)
    w_branch_b = w_branch.astype(BF16)
    w_out_b = w_out.astype(BF16)
    pool_w_bd = jax.vmap(_block_diag)(pool_w).astype(BF16)
    r3 = lambda a: a.reshape(a.shape[0], 1, a.shape[1])
    lane_pad = lambda a: jnp.pad(a, ((0, 0), (0, LANE - a.shape[1])))
    norm_mix3, norm_ffn3 = r3(norm_mix), r3(norm_ffn)
    hy_conv_b3, hy_bias3 = r3(hy_conv_b), r3(hy_bias)
    ssm_conv_b3 = r3(ssm_conv_b)
    pool_scale3 = r3(pool_scale)
    dt_bias3 = r3(lane_pad(ssm_dt_bias.reshape(depth, 2 * SSM_HEADS)))
    a_row3 = r3(lane_pad(-jnp.exp(ssm_a_log.astype(F32)).reshape(depth, 2 * SSM_HEADS)))
    d_row3 = r3(jnp.repeat(ssm_d, SSM_HEAD_DIM, axis=1))
    ssm_norm3 = r3(ssm_norm)
    router_w_p = jnp.pad(router_w, ((0, 0), (0, 0), (0, LANE - n_exp)))
    router_b3 = r3(lane_pad(router_b))
    exp_b_up4 = exp_b_up.reshape(depth, n_exp, 1, -1)
    exp_b_down4 = exp_b_down.reshape(depth, n_exp, 1, -1)
    emb = hy_ffn_w1.shape[1]
    ffn = hy_ffn_w1.shape[2]
    f_w1 = jnp.pad(hy_ffn_w1, ((0, 0), (0, LANE - emb), (0, LANE - ffn)))
    f_b1 = r3(lane_pad(hy_ffn_b1))
    f_w2 = jnp.pad(hy_ffn_w2, ((0, 0), (0, LANE - ffn), (0, LANE - ffn)))
    f_b2 = r3(lane_pad(hy_ffn_b2))
    f_w3 = jnp.pad(hy_ffn_w3, ((0, 0), (0, LANE - ffn), (0, 0)))

    freqs = jnp.linspace(1e-4, HYENA_BANDS - 1, HYENA_BANDS, dtype=F32)
    freqs_row = jnp.concatenate([jnp.zeros((1,), F32), freqs, freqs,
                                 jnp.zeros((LANE - 1 - 2 * HYENA_BANDS,), F32)])[None, :]
    deltas_row = jnp.linspace(HYENA_MIN_DECAY, HYENA_MAX_DECAY, hw, dtype=F32)[None, :]
    gi = jnp.arange(FOURIER_GROUP_W)
    ang = (gi[:, None] * gi[None, :] % FOURIER_GROUP_W).astype(F32) * (2.0 * math.pi / FOURIER_GROUP_W)
    n_fg = fw // FOURIER_GROUP_W
    eye = jnp.eye(n_fg, dtype=F32)
    wc = jnp.concatenate([jnp.kron(eye, jnp.cos(ang)), jnp.kron(eye, jnp.sin(ang))], axis=1).astype(BF16)
    band, invc = _pool_tables(n_lat, n_ctx)
    qi = jnp.arange(SSM_CHUNK)
    tri_f = (qi[:, None] >= qi[None, :]).astype(BF16)
    tri_b = (qi[:, None] <= qi[None, :]).astype(BF16)
    ti = jnp.arange(TOKEN_TILE)
    ltri = (ti[:, None] > ti[None, :]).astype(BF16)

    seqs = {}
    for name, n, row_off in (("ctx", n_ctx, n_lat), ("lat", n_lat, 0)):
        cn, sn = _trig_tables(n, n)
        fwd, inv = _hyena_dft_tables(n)
        g_all, nrm = _filter_mlp(n, freqs_row, deltas_row, f_w1, f_b1, f_w2, f_b2, f_w3)
        kr, ki = _filter_spectrum(fwd, g_all, nrm, n, hw)
        seqs[name] = dict(n=n, off=row_off, cn=cn.astype(BF16), sn=sn.astype(BF16), fwd=fwd, inv=inv,
                          kr=kr, ki=ki)

    cc = jnp.concatenate([c, c_ctx[None, :]], axis=0)
    rows = -(-cc.shape[0] // 8) * 8
    mod = _modulation(jnp.pad(cc, ((0, rows - cc.shape[0]), (0, 0))), w_mod, b_mod)
    mod = mod.reshape(depth, rows, 6, 1, d)

    x_all = jnp.concatenate([x, ctx], axis=1).reshape(n_tok, d)
    seq = tok.seq
    zero_state = jnp.zeros((batch, SSM_GROUPS, SSM_STATE, HEADS_PER_GROUP * SSM_HEAD_DIM), F32)

    for layer in range(depth):
        xcs, hy, pool, z, xbc, dtr = _inproj(tok, layer, x_all, mod, norm_mix3, w_in_p, wc, band, invc,
                                             pool_w_bd, pool_scale3, widths)
        xcs3 = xcs.reshape(batch, seq, 2 * fw)
        hy3 = hy.reshape(batch, seq, 3 * hw)
        z3 = z.reshape(batch, seq, sw)
        xbc3 = xbc.reshape(batch, seq, cw)
        dt3 = dtr.reshape(batch, seq, LANE)
        outs = {}
        states = (zero_state, zero_state)
        for name in ("ctx", "lat"):
            s = seqs[name]
            n, off = s["n"], s["off"]
            y_f = _fourier_seq(xcs3, s["cn"], s["sn"], n, off, fw)
            vv, x1 = _hy_pre(hy3, hy_conv_w, hy_conv_b3, layer, n, off, hw)
            y2 = _hy_fwd(s["fwd"], vv, s["kr"], s["ki"], layer, n, hw)
            y_h = _hy_inv(s["inv"], y2, x1, vv, hy_bias3, layer, n, hw)
            yf, st_f = _ssd_pass(xbc3, dt3, states[0], tri_f, ssm_conv_w, ssm_conv_b3, dt_bias3, a_row3,
                                 layer, n, off, reverse=False)
            y_s, st_b = _ssd_pass(xbc3, dt3, states[1], tri_b, ssm_conv_w, ssm_conv_b3, dt_bias3, a_row3,
                                  layer, n, off, reverse=True, fin=(z3, yf, d_row3, ssm_norm3))
            states = (st_f, st_b)
            outs[name] = (y_f, y_h, y_s)
        x_all = _merge(tok, layer, x_all, mod, norm_mix3, outs["lat"][0], outs["ctx"][0], outs["lat"][1],
                       outs["ctx"][1], pool, outs["lat"][2], outs["ctx"][2], w_gate_b, w_branch_b, w_out_b,
                       splits)
        h2, top_e, top_w, rank, counts = _router(tok, layer, x_all, mod, norm_ffn3, router_w_p, router_b3,
                                                 ltri, n_exp)
        dest, row_tok, blk_e, n_active = _dispatch_plan(top_e[:, :TOP_K], rank[:, :TOP_K],
                                                        counts[0, :n_exp].astype(jnp.int32), n_tok)
        h_pad = jnp.concatenate([h2, jnp.zeros((1, d), BF16)], axis=0)
        x_sorted = h_pad[row_tok]
        y_rows = _moe(layer, x_sorted, blk_e, n_active, exp_w_up, exp_b_up4, exp_w_down, exp_b_down4)
        y_gath = y_rows[dest.reshape(-1)].reshape(n_tok, TOP_K * d)
        x_all = _combine(tok, layer, x_all, mod, y_gath, top_w)

    return _final_norm(x_all.reshape(batch, seq, d), norm_final[None, :], n_lat)
```

```python
import functools
import math

import jax
import jax.numpy as jnp
from jax import lax
from jax.experimental import pallas as pl
from jax.experimental.pallas import tpu as pltpu

F32 = jnp.float32
BF16 = jnp.bfloat16

GRID_W = 64
EPS = 1e-6
FOURIER_GROUP_W = 64
HYENA_BANDS = 16
HYENA_MIN_DECAY = -math.log(1e-2) / 1.5
HYENA_MAX_DECAY = -math.log(1e-2) / 0.3
POOL_WINDOWS = (2, 4, 8, 16)
POOL_GROUP_W = 64
SSM_GROUPS = 2
HEADS_PER_GROUP = 4
SSM_HEADS = SSM_GROUPS * HEADS_PER_GROUP
SSM_HEAD_DIM = 64
SSM_STATE = 128
SSM_CHUNK = 128
TOP_K = 4
SWIGLU_LIMIT = 7.0
SWIGLU_ALPHA = 1.702

LANE = 128
BF16_SUBLANE = 16
TOKEN_TILE = 256
SEQ_TILE = 512
MOE_ROWS = 512
VMEM_LIMIT = 56 * 1024 * 1024


def _cparams(sem):
    return pltpu.CompilerParams(dimension_semantics=sem, vmem_limit_bytes=VMEM_LIMIT)


def _sigmoid(x):
    return 1.0 / (1.0 + jnp.exp(-x))


def _dot(a, b):
    return jnp.dot(a, b, preferred_element_type=F32)


def _dot_hi(a, b):
    return jnp.dot(a, b, preferred_element_type=F32, precision=lax.Precision.HIGHEST)


def _split3(x):
    x1 = x.astype(BF16)
    r1 = x - x1.astype(F32)
    x2 = r1.astype(BF16)
    x3 = (r1 - x2.astype(F32)).astype(BF16)
    return x1, x2, x3


def _dot_exact01(m01, x):
    x1, x2, x3 = _split3(x)
    return _dot(m01, x1) + _dot(m01, x2) + _dot(m01, x3)


def _mod_kernel(c_ref, w_ref, b_ref, o_ref):
    c = c_ref[...]
    s = c * _sigmoid(c)
    o_ref[...] = _dot_hi(s, w_ref[...]) + b_ref[...]


def _modulation(cc, w_mod, b_mod):
    depth, d, six_d = w_mod.shape
    rows = cc.shape[0]
    nj = six_d // d
    return pl.pallas_call(
        _mod_kernel,
        grid=(depth, nj),
        in_specs=[
            pl.BlockSpec((rows, d), lambda l, j: (0, 0)),
            pl.BlockSpec((None, d, d), lambda l, j: (l, 0, j)),
            pl.BlockSpec((None, 1, d), lambda l, j: (l, 0, j)),
        ],
        out_specs=pl.BlockSpec((None, rows, d), lambda l, j: (l, 0, j)),
        out_shape=jax.ShapeDtypeStruct((depth, rows, six_d), F32),
        compiler_params=_cparams(("arbitrary", "arbitrary")),
        name="modulation",
    )(cc, w_mod, b_mod.reshape(depth, 1, six_d))


class _Tok:
    def __init__(self, batch, n_lat, n_ctx):
        assert n_ctx == TOKEN_TILE and n_lat % TOKEN_TILE == 0
        self.batch, self.n_lat, self.n_ctx = batch, n_lat, n_ctx
        self.seq = n_lat + n_ctx
        self.tiles_per_b = self.seq // TOKEN_TILE
        self.lat_tiles = n_lat // TOKEN_TILE
        self.n_tiles = batch * self.tiles_per_b
        self.n_tok = batch * self.seq

    def b(self, t):
        return t // self.tiles_per_b

    def w(self, t):
        return t % self.tiles_per_b

    def is_ctx(self, t):
        return (t % self.tiles_per_b) >= self.lat_tiles

    def mod_row(self, t):
        return jnp.where(self.is_ctx(t), self.batch, self.b(t))

    def mod_spec(self, layer, part, d):
        return pl.BlockSpec((None, None, None, 1, d), lambda t: (layer, self.mod_row(t), part, 0, 0))

    def lat_spec(self, width):
        return pl.BlockSpec((None, TOKEN_TILE, width),
                            lambda t: (self.b(t), jnp.minimum(self.w(t), self.lat_tiles - 1), 0))

    def ctx_spec(self, width):
        return pl.BlockSpec((None, TOKEN_TILE, width), lambda t: (self.b(t), 0, 0))


def _norm_mod(x, gain, scale, shift):
    ms = jnp.mean(x * x, axis=-1, keepdims=True)
    return (x * lax.rsqrt(ms + EPS)) * gain * (1.0 + scale) + shift


def _inproj_kernel(x_ref, sc_ref, sh_ref, g_ref, win_ref, wc_ref, band_ref, invc_ref, pw_ref, ps_ref,
                   xcs_ref, hy_ref, pool_ref, z_ref, xbc_ref, dt_ref, *, widths):
    fw, hw, pw, zw, cw = widths
    hb = _norm_mod(x_ref[...], g_ref[...], sc_ref[...], sh_ref[...]).astype(BF16)
    o = 0
    f_in = _dot(hb, win_ref[:, o:o + fw])
    xcs_ref[...] = _dot(f_in.astype(BF16), wc_ref[...]).astype(BF16)
    o += fw
    hy_ref[...] = _dot(hb, win_ref[:, o:o + hw]).astype(BF16)
    o += hw
    u = _dot(hb, win_ref[:, o:o + pw])
    o += pw
    z_ref[...] = _dot(hb, win_ref[:, o:o + zw]).astype(BF16)
    o += zw
    xbc_ref[...] = _dot(hb, win_ref[:, o:o + cw]).astype(BF16)
    o += cw
    dt_ref[...] = _dot(hb, win_ref[:, o:o + LANE])
    gw = POOL_GROUP_W
    parts = []
    for gi in range(len(POOL_WINDOWS)):
        ug = u[:, gi * gw:(gi + 1) * gw]
        parts.append(_dot_exact01(band_ref[gi], ug))
    pooled = jnp.concatenate(parts, axis=1) * invc_ref[...] - u
    pool_ref[...] = (_dot(pooled.astype(BF16), pw_ref[...]) * ps_ref[...]).astype(BF16)


def _pool_tables(n_lat, n_ctx):
    tm = TOKEN_TILE
    p = jnp.arange(tm)
    bands, invs = [], []
    for row_len in (GRID_W, n_ctx):
        assert tm % row_len == 0
        pr, rr = p % row_len, p // row_len
        bg, ig = [], []
        for win in POOL_WINDOWS:
            lo = jnp.clip(pr - win // 2, 0, row_len)
            hi = jnp.clip(pr + win // 2, 0, row_len)
            q = pr[None, :]
            m = (rr[:, None] == rr[None, :]) & (q >= lo[:, None]) & (q < hi[:, None])
            bg.append(m.astype(BF16))
            ig.append(jnp.broadcast_to((1.0 / (hi - lo).astype(F32))[:, None], (tm, POOL_GROUP_W)))
        bands.append(jnp.stack(bg))
        invs.append(jnp.concatenate(ig, axis=1))
    return jnp.stack(bands), jnp.stack(invs)


def _block_diag(w):
    g, a, b = w.shape
    out = jnp.zeros((g * a, g * b), w.dtype)
    for i in range(g):
        out = out.at[i * a:(i + 1) * a, i * b:(i + 1) * b].set(w[i])
    return out


def _inproj(tok, layer, x_all, mod, norm_mix, w_in_p, wc, band, invc, pool_w_bd, pool_scale, widths):
    d = x_all.shape[-1]
    fw, hw, pw, zw, cw = widths
    tm = TOKEN_TILE
    n_tok = tok.n_tok
    kind = lambda t: jnp.where(tok.is_ctx(t), 1, 0)
    const2 = lambda t: (0, 0)
    outs = [
        jax.ShapeDtypeStruct((n_tok, 2 * fw), BF16),
        jax.ShapeDtypeStruct((n_tok, hw), BF16),
        jax.ShapeDtypeStruct((n_tok, pw), BF16),
        jax.ShapeDtypeStruct((n_tok, zw), BF16),
        jax.ShapeDtypeStruct((n_tok, cw), BF16),
        jax.ShapeDtypeStruct((n_tok, LANE), F32),
    ]
    return pl.pallas_call(
        functools.partial(_inproj_kernel, widths=widths),
        grid=(tok.n_tiles,),
        in_specs=[
            pl.BlockSpec((tm, d), lambda t: (t, 0)),
            tok.mod_spec(layer, 1, d),
            tok.mod_spec(layer, 0, d),
            pl.BlockSpec((None, 1, d), lambda t: (layer, 0, 0)),
            pl.BlockSpec((None,) + w_in_p.shape[1:], lambda t: (layer, 0, 0)),
            pl.BlockSpec(wc.shape, const2),
            pl.BlockSpec((None,) + band.shape[1:], lambda t: (kind(t), 0, 0, 0)),
            pl.BlockSpec((None,) + invc.shape[1:], lambda t: (kind(t), 0, 0)),
            pl.BlockSpec((None,) + pool_w_bd.shape[1:], lambda t: (layer, 0, 0)),
            pl.BlockSpec((None, 1, pw), lambda t: (layer, 0, 0)),
        ],
        out_specs=[pl.BlockSpec((tm, s.shape[1]), lambda t: (t, 0)) for s in outs],
        out_shape=outs,
        compiler_params=_cparams(("arbitrary",)),
        name="inproj",
    )(x_all, mod, mod, norm_mix, w_in_p, wc, band, invc, pool_w_bd, pool_scale)


def _trig_tables(n, period):
    k = jnp.arange(n, dtype=jnp.int32)

    def cs(j):
        m = (j[:, None] * k[None, :]) % period
        ang = m.astype(F32) * (2.0 * math.pi / period)
        return jnp.cos(ang), jnp.sin(ang)

    split = 64
    if n % split or n <= split:
        return cs(k)
    c1, s1 = cs(jnp.arange(n // split, dtype=jnp.int32) * split)
    c2, s2 = cs(jnp.arange(split, dtype=jnp.int32))
    c = c1[:, None, :] * c2[None, :, :] - s1[:, None, :] * s2[None, :, :]
    s = s1[:, None, :] * c2[None, :, :] + c1[:, None, :] * s2[None, :, :]
    return c.reshape(n, n), s.reshape(n, n)


def _hyena_dft_tables(n):
    c, s = _trig_tables(n, 2 * n)
    nyq = jnp.where(jnp.arange(n) % 2 == 0, 1.0, -1.0).astype(F32)
    s = (-s).at[0].set(nyq)
    fwd = jnp.concatenate([c, s], axis=0).astype(BF16)
    return fwd, fwd.T


def _fourier_kernel(c_ref, s_ref, xc_ref, xs_ref, o_ref, acc_ref, *, scale):
    k = pl.program_id(1)

    @pl.when(k == 0)
    def _():
        acc_ref[...] = jnp.zeros_like(acc_ref)

    nb = xc_ref.shape[0]
    rc = jnp.concatenate([xc_ref[b] for b in range(nb)], axis=1)
    rs = jnp.concatenate([xs_ref[b] for b in range(nb)], axis=1)
    acc_ref[...] += _dot(c_ref[...], rc) - _dot(s_ref[...], rs)

    @pl.when(k == pl.num_programs(1) - 1)
    def _():
        w = o_ref.shape[2]
        for b in range(nb):
            o_ref[b] = (acc_ref[:, b * w:(b + 1) * w] * scale).astype(o_ref.dtype)


def _fourier_seq(xcs3, cn, sn, n, row_off, fw):
    batch = xcs3.shape[0]
    tl = min(SEQ_TILE, n)
    nt = n // tl
    off = row_off // tl
    scale = 1.0 / math.sqrt(n * FOURIER_GROUP_W)
    return pl.pallas_call(
        functools.partial(_fourier_kernel, scale=scale),
        grid=(nt, nt),
        in_specs=[
            pl.BlockSpec((tl, tl), lambda i, k: (i, k)),
            pl.BlockSpec((tl, tl), lambda i, k: (i, k)),
            pl.BlockSpec((batch, tl, fw), lambda i, k: (0, off + k, 0)),
            pl.BlockSpec((batch, tl, fw), lambda i, k: (0, off + k, 1)),
        ],
        out_specs=pl.BlockSpec((batch, tl, fw), lambda i, k: (0, i, 0)),
        out_shape=jax.ShapeDtypeStruct((batch, n, fw), BF16),
        scratch_shapes=[pltpu.VMEM((tl, batch * fw), F32)],
        compiler_params=_cparams(("arbitrary", "arbitrary")),
        name="fourier_seq",
    )(cn, sn, xcs3, xcs3)


def _shift_rows(x, prev_row, next_row):
    n = x.shape[0]
    rows = lax.broadcasted_iota(jnp.int32, x.shape, 0)
    xm = jnp.where(rows == 0, prev_row, pltpu.roll(x, 1, 0))
    xp = jnp.where(rows == n - 1, next_row, pltpu.roll(x, n - 1, 0))
    return xm, xp


def _conv3(x, prev_row, next_row, w_ref, b_ref):
    xm, xp = _shift_rows(x, prev_row, next_row)
    return w_ref[0:1, :] * xm + w_ref[1:2, :] * x + w_ref[2:3, :] * xp + b_ref[...]


def _hy_pre_kernel(x_ref, xp_ref, xn_ref, w_ref, b_ref, vv_ref, x1_ref, *, hw):
    i = pl.program_id(1)
    last = pl.num_programs(1) - 1
    x = x_ref[...].astype(F32)
    hs = xp_ref.shape[0]
    prev_row = jnp.where(i == 0, 0.0, xp_ref[hs - 1:hs, :].astype(F32))
    next_row = jnp.where(i == last, 0.0, xn_ref[0:1, :].astype(F32))
    u = _conv3(x, prev_row, next_row, w_ref, b_ref)
    x1_ref[...] = u[:, :hw].astype(BF16)
    vv_ref[...] = (u[:, hw:2 * hw] * u[:, 2 * hw:]).astype(BF16)


def _halo_specs(width, tile, row_off, seq_rows):
    hs = BF16_SUBLANE
    per = tile // hs
    base = row_off // hs
    top = seq_rows // hs - 1
    prev = pl.BlockSpec((None, hs, width), lambda b, i: (b, jnp.maximum(base + i * per - 1, 0), 0))
    nxt = pl.BlockSpec((None, hs, width), lambda b, i: (b, jnp.minimum(base + (i + 1) * per, top), 0))
    return prev, nxt


def _hy_pre(hy3, conv_w, conv_b, layer, n, row_off, hw):
    batch, seq, w3 = hy3.shape
    tl = min(SEQ_TILE, n)
    off = row_off // tl
    prev, nxt = _halo_specs(w3, tl, row_off, seq)
    out = jax.ShapeDtypeStruct((batch, n, hw), BF16)
    return pl.pallas_call(
        functools.partial(_hy_pre_kernel, hw=hw),
        grid=(batch, n // tl),
        in_specs=[
            pl.BlockSpec((None, tl, w3), lambda b, i: (b, off + i, 0)),
            prev, nxt,
            pl.BlockSpec((None,) + conv_w.shape[1:], lambda b, i: (layer, 0, 0)),
            pl.BlockSpec((None, 1, w3), lambda b, i: (layer, 0, 0)),
        ],
        out_specs=[pl.BlockSpec((None, tl, hw), lambda b, i: (b, i, 0))] * 2,
        out_shape=[out, out],
        compiler_params=_cparams(("arbitrary", "arbitrary")),
        name="hyena_pre",
    )(hy3, hy3, hy3, conv_w, conv_b)


def _filter_mlp_kernel(fr_ref, dl_ref, w1_ref, b1_ref, w2_ref, b2_ref, w3_ref, g_ref, nrm_ref, *, n, hw):
    j = pl.program_id(1)
    rt = g_ref.shape[0]
    lane = lax.broadcasted_iota(jnp.int32, (rt, LANE), 1)
    pos = (lax.broadcasted_iota(jnp.int32, (rt, LANE), 0) + j * rt).astype(F32)
    t = pos / (n - 1)
    ang = 2.0 * math.pi * pos / n
    arg = ang * fr_ref[...]
    feats = jnp.where(lane == 0, t,
                      jnp.where(lane <= HYENA_BANDS, jnp.cos(arg),
                                jnp.where(lane <= 2 * HYENA_BANDS, -jnp.sin(arg), 0.0)))
    h1 = jnp.sin(_dot_hi(feats, w1_ref[...]) + b1_ref[...])
    h2 = jnp.sin(_dot_hi(h1, w2_ref[...]) + b2_ref[...])
    k = _dot_hi(h2, w3_ref[...])
    decay = jnp.exp(-t[:, 0:1] * dl_ref[...])
    kf = k[:, :hw] * decay
    kb = jnp.where(pos[:, 0:1] == 0.0, 0.0, k[:, hw:] * decay)
    g_ref[:, :hw] = (kf + kb).astype(BF16)
    g_ref[:, hw:] = (kf - kb).astype(BF16)
    part = jnp.sum(jnp.abs(kf) + jnp.abs(kb), axis=0, keepdims=True)

    @pl.when(j == 0)
    def _():
        nrm_ref[...] = part

    @pl.when(j > 0)
    def _():
        nrm_ref[...] += part


def _filter_mlp(n, freqs_row, deltas_row, w1, b1, w2, b2, w3):
    depth = w1.shape[0]
    hw = deltas_row.shape[1]
    rt = min(SEQ_TILE, n)
    per_layer = lambda l, j: (l, 0, 0)
    return pl.pallas_call(
        functools.partial(_filter_mlp_kernel, n=n, hw=hw),
        grid=(depth, n // rt),
        in_specs=[
            pl.BlockSpec((1, LANE), lambda l, j: (0, 0)),
            pl.BlockSpec((1, hw), lambda l, j: (0, 0)),
            pl.BlockSpec((None,) + w1.shape[1:], per_layer),
            pl.BlockSpec((None,) + b1.shape[1:], per_layer),
            pl.BlockSpec((None,) + w2.shape[1:], per_layer),
            pl.BlockSpec((None,) + b2.shape[1:], per_layer),
            pl.BlockSpec((None,) + w3.shape[1:], per_layer),
        ],
        out_specs=[
            pl.BlockSpec((rt, 2 * hw), lambda l, j: (j, l)),
            pl.BlockSpec((None, 1, hw), lambda l, j: (l, 0, 0)),
        ],
        out_shape=[
            jax.ShapeDtypeStruct((n, depth * 2 * hw), BF16),
            jax.ShapeDtypeStruct((depth, 1, hw), F32),
        ],
        compiler_params=_cparams(("arbitrary", "arbitrary")),
        name="hyena_filter_mlp",
    )(freqs_row, deltas_row, w1, b1, w2, b2, w3)


def _rdft_accumulate(fc_ref, fs_ref, rhs, accc_ref, accs_ref):
    k = pl.program_id(1)

    @pl.when(k == 0)
    def _():
        accc_ref[...] = jnp.zeros_like(accc_ref)
        accs_ref[...] = jnp.zeros_like(accs_ref)

    accc_ref[...] += _dot(fc_ref[...], rhs)
    accs_ref[...] += _dot(fs_ref[...], rhs)


def _filter_spec_kernel(fc_ref, fs_ref, g_ref, nrm_ref, kr_ref, ki_ref, accc_ref, accs_ref, *, hw):
    i = pl.program_id(0)
    _rdft_accumulate(fc_ref, fs_ref, g_ref[...], accc_ref, accs_ref)

    @pl.when(pl.program_id(1) == pl.num_programs(1) - 1)
    def _():
        tm = accc_ref.shape[0]
        row0 = (lax.broadcasted_iota(jnp.int32, (tm, hw), 0) + i * tm) == 0
        for l in range(kr_ref.shape[0]):
            inv = 1.0 / nrm_ref[l]
            o = l * 2 * hw
            kr_ref[l] = accc_ref[:, o:o + hw] * inv
            ki_ref[l] = jnp.where(row0, accs_ref[:, o:o + hw], accs_ref[:, o + hw:o + 2 * hw]) * inv


def _filter_spectrum(fwd, g_all, nrm, n, hw):
    depth = nrm.shape[0]
    tl = min(SEQ_TILE, n)
    nt = n // tl
    width = g_all.shape[1]
    out = jax.ShapeDtypeStruct((depth, n, hw), F32)
    return pl.pallas_call(
        functools.partial(_filter_spec_kernel, hw=hw),
        grid=(nt, nt),
        in_specs=[
            pl.BlockSpec((tl, tl), lambda i, k: (i, k)),
            pl.BlockSpec((tl, tl), lambda i, k: (nt + i, k)),
            pl.BlockSpec((tl, width), lambda i, k: (k, 0)),
            pl.BlockSpec(nrm.shape, lambda i, k: (0, 0, 0)),
        ],
        out_specs=[pl.BlockSpec((depth, tl, hw), lambda i, k: (0, i, 0))] * 2,
        out_shape=[out, out],
        scratch_shapes=[pltpu.VMEM((tl, width), F32)] * 2,
        compiler_params=_cparams(("arbitrary", "arbitrary")),
        name="hyena_filter_spectrum",
    )(fwd, fwd, g_all, nrm)


def _hy_fwd_kernel(fc_ref, fs_ref, vv_ref, kr_ref, ki_ref, y_ref, accc_ref, accs_ref):
    i = pl.program_id(0)
    nb = vv_ref.shape[0]
    rhs = jnp.concatenate([vv_ref[b] for b in range(nb)], axis=1)
    _rdft_accumulate(fc_ref, fs_ref, rhs, accc_ref, accs_ref)

    @pl.when(pl.program_id(1) == pl.num_programs(1) - 1)
    def _():
        tm, hw = kr_ref.shape
        row0 = (lax.broadcasted_iota(jnp.int32, (tm, hw), 0) + i * tm) == 0
        kr, ki = kr_ref[...], ki_ref[...]
        for b in range(nb):
            vr = accc_ref[:, b * hw:(b + 1) * hw]
            vi = accs_ref[:, b * hw:(b + 1) * hw]
            yr = jnp.where(row0, 0.5 * vr * kr, vr * kr - vi * ki)
            yi = jnp.where(row0, 0.5 * vi * ki, vr * ki + vi * kr)
            y_ref[0, :, b * hw:(b + 1) * hw] = yr.astype(BF16)
            y_ref[1, :, b * hw:(b + 1) * hw] = yi.astype(BF16)


def _hy_fwd(fwd, vv, kr, ki, layer, n, hw):
    batch = vv.shape[0]
    tl = min(SEQ_TILE, n)
    nt = n // tl
    return pl.pallas_call(
        _hy_fwd_kernel,
        grid=(nt, nt),
        in_specs=[
            pl.BlockSpec((tl, tl), lambda i, k: (i, k)),
            pl.BlockSpec((tl, tl), lambda i, k: (nt + i, k)),
            pl.BlockSpec((batch, tl, hw), lambda i, k: (0, k, 0)),
            pl.BlockSpec((None, tl, hw), lambda i, k: (layer, i, 0)),
            pl.BlockSpec((None, tl, hw), lambda i, k: (layer, i, 0)),
        ],
        out_specs=pl.BlockSpec((2, tl, batch * hw), lambda i, k: (0, i, 0)),
        out_shape=jax.ShapeDtypeStruct((2, n, batch * hw), BF16),
        scratch_shapes=[pltpu.VMEM((tl, batch * hw), F32)] * 2,
        compiler_params=_cparams(("arbitrary", "arbitrary")),
        name="hyena_fwd_dft",
    )(fwd, fwd, vv, kr, ki)


def _hy_inv_kernel(fi_ref, y_ref, x1_ref, vv_ref, bias_ref, o_ref, acc_ref, *, scale):
    k = pl.program_id(1)

    @pl.when(k == 0)
    def _():
        acc_ref[...] = jnp.zeros_like(acc_ref)

    acc_ref[...] += _dot(fi_ref[...], y_ref[...])

    @pl.when(k == pl.num_programs(1) - 1)
    def _():
        nb, _, hw = x1_ref.shape
        for b in range(nb):
            conv = acc_ref[:, b * hw:(b + 1) * hw] * scale
            vv = vv_ref[b].astype(F32)
            o_ref[b] = (x1_ref[b].astype(F32) * (conv + vv * bias_ref[...])).astype(BF16)


def _hy_inv(inv, y2, x1, vv, hy_bias, layer, n, hw):
    batch = vv.shape[0]
    tl = min(SEQ_TILE, n)
    nt = n // tl
    return pl.pallas_call(
        functools.partial(_hy_inv_kernel, scale=1.0 / n),
        grid=(nt, 2 * nt),
        in_specs=[
            pl.BlockSpec((tl, tl), lambda i, k: (i, k)),
            pl.BlockSpec((tl, batch * hw), lambda i, k: (k, 0)),
            pl.BlockSpec((batch, tl, hw), lambda i, k: (0, i, 0)),
            pl.BlockSpec((batch, tl, hw), lambda i, k: (0, i, 0)),
            pl.BlockSpec((None, 1, hw), lambda i, k: (layer, 0, 0)),
        ],
        out_specs=pl.BlockSpec((batch, tl, hw), lambda i, k: (0, i, 0)),
        out_shape=jax.ShapeDtypeStruct((batch, n, hw), BF16),
        scratch_shapes=[pltpu.VMEM((tl, batch * hw), F32)],
        compiler_params=_cparams(("arbitrary", "arbitrary")),
        name="hyena_inv_dft",
    )(inv, y2.reshape(2 * n, batch * hw), x1, vv, hy_bias)


def _softplus(x):
    return jnp.maximum(x, 0.0) + jnp.log1p(jnp.exp(-jnp.abs(x)))


def _ssd_kernel(*refs, reverse, final, nc, lane0):
    if final:
        (xbc_ref, xp_ref, xn_ref, dt_ref, h0_ref, tri_ref, cw_ref, cb_ref, dtb_ref, a_ref,
         z_ref, yf_ref, d_ref, nrm_ref, out_ref, hout_ref, h_scr) = refs
    else:
        (xbc_ref, xp_ref, xn_ref, dt_ref, h0_ref, tri_ref, cw_ref, cb_ref, dtb_ref, a_ref,
         out_ref, hout_ref, h_scr) = refs
    step = pl.program_id(1)
    cc = (nc - 1 - step) if reverse else step

    @pl.when(step == 0)
    def _():
        h_scr[...] = h0_ref[...]

    q = xbc_ref.shape[0]
    hs = xp_ref.shape[0]
    x = xbc_ref[...].astype(F32)
    prev_row = jnp.where(cc == 0, 0.0, xp_ref[hs - 1:hs, :].astype(F32))
    next_row = jnp.where(cc == nc - 1, 0.0, xn_ref[0:1, :].astype(F32))
    u = _conv3(x, prev_row, next_row, cw_ref, cb_ref)
    u = u * _sigmoid(u)
    width = SSM_HEADS * SSM_HEAD_DIM
    gs = SSM_STATE
    xs = u[:, :width]
    bm = u[:, width:width + SSM_GROUPS * gs]
    cm = u[:, width + SSM_GROUPS * gs:]

    dt = _softplus(dt_ref[...] + dtb_ref[...])
    acs = _dot_exact01(tri_ref[...], dt * a_ref[...])
    acs_t = acs.T
    tot = acs[0:1, :] if reverse else acs[q - 1:q, :]
    to_end = jnp.exp(tot - acs)
    frm = jnp.exp(acs)
    cdec = jnp.exp(tot)
    ri = lax.broadcasted_iota(jnp.int32, (q, q), 0)
    ci = lax.broadcasted_iota(jnp.int32, (q, q), 1)
    mask = (ri <= ci) if reverse else (ri >= ci)

    pd = SSM_HEAD_DIM
    ys = []
    for g in range(SSM_GROUPS):
        bg = bm[:, g * gs:(g + 1) * gs]
        cg = cm[:, g * gs:(g + 1) * gs].astype(BF16)
        scores = lax.dot_general(cg, bg.astype(BF16), (((1,), (1,)), ((), ())), preferred_element_type=F32)
        hg = h_scr[g]
        yoff = _dot(cg, hg.astype(BF16))
        wparts, decs = [], []
        for r in range(HEADS_PER_GROUP):
            hd = g * HEADS_PER_GROUP + r
            li = lane0 + hd
            lm = jnp.where(mask, jnp.exp(acs[:, li:li + 1] - acs_t[li:li + 1, :]), 0.0)
            m = (scores * lm).astype(BF16)
            xdt = xs[:, hd * pd:(hd + 1) * pd] * dt[:, li:li + 1]
            yd = _dot(m, xdt.astype(BF16))
            ys.append(yd + yoff[:, r * pd:(r + 1) * pd] * frm[:, li:li + 1])
            wparts.append(xdt * to_end[:, li:li + 1])
            decs.append(jnp.broadcast_to(cdec[:, li:li + 1], (1, pd)))
        wg = jnp.concatenate(wparts, axis=1).astype(BF16)
        st = _dot(bg.T.astype(BF16), wg)
        h_scr[g] = hg * jnp.concatenate(decs, axis=1) + st
    y = jnp.concatenate(ys, axis=1)

    @pl.when(step == nc - 1)
    def _():
        hout_ref[...] = h_scr[...]

    if final:
        y = y + yf_ref[...] + xs * d_ref[...]
        z = z_ref[...].astype(F32)
        y = y * (z * _sigmoid(z))
        ms = jnp.mean(y * y, axis=-1, keepdims=True)
        out_ref[...] = ((y * lax.rsqrt(ms + EPS)) * nrm_ref[...]).astype(out_ref.dtype)
    else:
        out_ref[...] = y


def _ssd_pass(xbc3, dt3, h0, tri, conv_w, conv_b, dt_bias, a_row, layer, n, row_off, *, reverse,
              fin=None):
    batch, seq, cw = xbc3.shape
    q = SSM_CHUNK
    nc = n // q
    off = row_off // q
    width = SSM_HEADS * SSM_HEAD_DIM

    def cidx(c):
        return off + ((nc - 1 - c) if reverse else c)

    hsz = BF16_SUBLANE
    per = q // hsz
    top = seq // hsz - 1
    per_layer = lambda b, c: (layer, 0, 0)
    in_specs = [
        pl.BlockSpec((None, q, cw), lambda b, c: (b, cidx(c), 0)),
        pl.BlockSpec((None, hsz, cw), lambda b, c: (b, jnp.maximum(cidx(c) * per - 1, 0), 0)),
        pl.BlockSpec((None, hsz, cw), lambda b, c: (b, jnp.minimum((cidx(c) + 1) * per, top), 0)),
        pl.BlockSpec((None, q, LANE), lambda b, c: (b, cidx(c), 0)),
        pl.BlockSpec((None,) + h0.shape[1:], lambda b, c: (b, 0, 0, 0)),
        pl.BlockSpec((q, q), lambda b, c: (0, 0)),
        pl.BlockSpec((None,) + conv_w.shape[1:], per_layer),
        pl.BlockSpec((None, 1, cw), per_layer),
        pl.BlockSpec((None, 1, LANE), per_layer),
        pl.BlockSpec((None, 1, LANE), per_layer),
    ]
    args = [xbc3, xbc3, xbc3, dt3, h0, tri, conv_w, conv_b, dt_bias, a_row]
    if fin is not None:
        z3, yf, d_row, nrm_row = fin
        in_specs += [
            pl.BlockSpec((None, q, width), lambda b, c: (b, cidx(c), 0)),
            pl.BlockSpec((None, q, width), lambda b, c: (b, cidx(c) - off, 0)),
            pl.BlockSpec((None, 1, width), per_layer),
            pl.BlockSpec((None, 1, width), per_layer),
        ]
        args += [z3, yf, d_row, nrm_row]
    out_dtype = BF16 if fin is not None else F32
    return pl.pallas_call(
        functools.partial(_ssd_kernel, reverse=reverse, final=fin is not None, nc=nc,
                          lane0=SSM_HEADS if reverse else 0),
        grid=(batch, nc),
        in_specs=in_specs,
        out_specs=[
            pl.BlockSpec((None, q, width), lambda b, c: (b, cidx(c) - off, 0)),
            pl.BlockSpec((None,) + h0.shape[1:], lambda b, c: (b, 0, 0, 0)),
        ],
        out_shape=[
            jax.ShapeDtypeStruct((batch, n, width), out_dtype),
            jax.ShapeDtypeStruct(h0.shape, F32),
        ],
        scratch_shapes=[pltpu.VMEM(h0.shape[1:], F32)],
        compiler_params=_cparams(("arbitrary", "arbitrary")),
        name="ssd_bwd_final" if reverse else "ssd_fwd",
    )(*args)


def _merge_kernel(x_ref, sc_ref, sh_ref, g1_ref, gn_ref, fl_ref, fc_ref, hl_ref, hc_ref, p_ref, sl_ref,
                  scx_ref, wg_ref, wb_ref, wo_ref, o_ref, *, tok, splits):
    ctx = tok.is_ctx(pl.program_id(0))
    x = x_ref[...]
    hb = _norm_mod(x, gn_ref[...], sc_ref[...], sh_ref[...]).astype(BF16)
    ys = (jnp.where(ctx, fc_ref[...], fl_ref[...]), jnp.where(ctx, hc_ref[...], hl_ref[...]),
          p_ref[...], jnp.where(ctx, scx_ref[...], sl_ref[...]))
    merged = None
    lo = 0
    for k, hi in enumerate(splits):
        gate = _sigmoid(_dot(hb, wg_ref[k]))
        term = gate * _dot(ys[k], wb_ref[lo:hi, :])
        merged = term if merged is None else merged + term
        lo = hi
    out = _dot(merged.astype(BF16), wo_ref[...])
    o_ref[...] = x + g1_ref[...] * out


def _merge(tok, layer, x_all, mod, norm_mix, f_lat, f_ctx, h_lat, h_ctx, pool, s_lat, s_ctx,
           w_gate, w_branch, w_out, splits):
    d = x_all.shape[-1]
    tm = TOKEN_TILE
    fw, hw, pw, sw = f_lat.shape[-1], h_lat.shape[-1], pool.shape[-1], s_lat.shape[-1]
    return pl.pallas_call(
        functools.partial(_merge_kernel, tok=tok, splits=splits),
        grid=(tok.n_tiles,),
        in_specs=[
            pl.BlockSpec((tm, d), lambda t: (t, 0)),
            tok.mod_spec(layer, 1, d),
            tok.mod_spec(layer, 0, d),
            tok.mod_spec(layer, 2, d),
            pl.BlockSpec((None, 1, d), lambda t: (layer, 0, 0)),
            tok.lat_spec(fw), tok.ctx_spec(fw),
            tok.lat_spec(hw), tok.ctx_spec(hw),
            pl.BlockSpec((tm, pw), lambda t: (t, 0)),
            tok.lat_spec(sw), tok.ctx_spec(sw),
            pl.BlockSpec((None,) + w_gate.shape[1:], lambda t: (layer, 0, 0, 0)),
            pl.BlockSpec((None,) + w_branch.shape[1:], lambda t: (layer, 0, 0)),
            pl.BlockSpec((None,) + w_out.shape[1:], lambda t: (layer, 0, 0)),
        ],
        out_specs=pl.BlockSpec((tm, d), lambda t: (t, 0)),
        out_shape=jax.ShapeDtypeStruct(x_all.shape, F32),
        compiler_params=_cparams(("arbitrary",)),
        name="merge",
    )(x_all, mod, mod, mod, norm_mix, f_lat, f_ctx, h_lat, h_ctx, pool, s_lat, s_ctx,
      w_gate, w_branch, w_out)


def _router_kernel(x_ref, sc_ref, sh_ref, gn_ref, rw_ref, rb_ref, ltri_ref,
                   h_ref, e_ref, w_ref, r_ref, cnt_ref, *, n_exp):
    t = pl.program_id(0)

    @pl.when(t == 0)
    def _():
        cnt_ref[...] = jnp.zeros_like(cnt_ref)

    h = _norm_mod(x_ref[...], gn_ref[...], sc_ref[...], sh_ref[...])
    h_ref[...] = h.astype(BF16)
    tm = h.shape[0]
    lane = lax.broadcasted_iota(jnp.int32, (tm, LANE), 1)
    logits = jnp.where(lane < n_exp, _dot_hi(h, rw_ref[...]) + rb_ref[...], -jnp.inf)
    carry = cnt_ref[0:1, :]
    tops, sels = [], []
    cur = logits
    for _ in range(TOP_K):
        m = jnp.max(cur, axis=-1, keepdims=True)
        idx = jnp.min(jnp.where(cur == m, lane, LANE), axis=-1, keepdims=True)
        sel = lane == idx
        tops.append((m, idx))
        sels.append(sel)
        cur = jnp.where(sel, -jnp.inf, cur)
    chosen = sels[0]
    for s in sels[1:]:
        chosen = chosen | s
    cnt = jnp.where(chosen, 1.0, 0.0)
    before = _dot(ltri_ref[...], cnt.astype(BF16)) + carry
    exps = [jnp.exp(m - tops[0][0]) for m, _ in tops]
    den = exps[0]
    for v in exps[1:]:
        den = den + v
    e_out = jnp.zeros((tm, LANE), jnp.int32)
    w_out = jnp.zeros((tm, LANE), F32)
    r_out = jnp.zeros((tm, LANE), jnp.int32)
    for k in range(TOP_K):
        rank = jnp.sum(jnp.where(sels[k], before, 0.0), axis=-1, keepdims=True)
        e_out = jnp.where(lane == k, tops[k][1], e_out)
        w_out = jnp.where(lane == k, exps[k] / den, w_out)
        r_out = jnp.where(lane == k, rank.astype(jnp.int32), r_out)
    e_ref[...] = e_out
    w_ref[...] = w_out
    r_ref[...] = r_out
    cnt_ref[...] += jnp.broadcast_to(jnp.sum(cnt, axis=0, keepdims=True), cnt_ref.shape)


def _router(tok, layer, x_all, mod, norm_ffn, router_w_p, router_b_p, ltri, n_exp):
    d = x_all.shape[-1]
    tm = TOKEN_TILE
    n_tok = tok.n_tok
    lane_i = jax.ShapeDtypeStruct((n_tok, LANE), jnp.int32)
    return pl.pallas_call(
        functools.partial(_router_kernel, n_exp=n_exp),
        grid=(tok.n_tiles,),
        in_specs=[
            pl.BlockSpec((tm, d), lambda t: (t, 0)),
            tok.mod_spec(layer, 4, d),
            tok.mod_spec(layer, 3, d),
            pl.BlockSpec((None, 1, d), lambda t: (layer, 0, 0)),
            pl.BlockSpec((None, d, LANE), lambda t: (layer, 0, 0)),
            pl.BlockSpec((None, 1, LANE), lambda t: (layer, 0, 0)),
            pl.BlockSpec((tm, tm), lambda t: (0, 0)),
        ],
        out_specs=[
            pl.BlockSpec((tm, d), lambda t: (t, 0)),
            pl.BlockSpec((tm, LANE), lambda t: (t, 0)),
            pl.BlockSpec((tm, LANE), lambda t: (t, 0)),
            pl.BlockSpec((tm, LANE), lambda t: (t, 0)),
            pl.BlockSpec((8, LANE), lambda t: (0, 0)),
        ],
        out_shape=[
            jax.ShapeDtypeStruct((n_tok, d), BF16),
            lane_i,
            jax.ShapeDtypeStruct((n_tok, LANE), F32),
            lane_i,
            jax.ShapeDtypeStruct((8, LANE), F32),
        ],
        compiler_params=_cparams(("arbitrary",)),
        name="router",
    )(x_all, mod, mod, norm_ffn, router_w_p, router_b_p, ltri)


def _moe_kernel(be_ref, na_ref, x_ref, wu_ref, bu_ref, wd_ref, bd_ref, o_ref, wu_scr, wd_scr, *, ff):
    i = pl.program_id(0)
    active = i < na_ref[0]
    fresh = jnp.logical_or(i == 0, be_ref[i] != be_ref[jnp.maximum(i - 1, 0)])

    @pl.when(jnp.logical_and(active, fresh))
    def _():
        wu_scr[...] = wu_ref[...].astype(BF16)
        wd_scr[...] = wd_ref[...].astype(BF16)

    @pl.when(active)
    def _():
        hu = _dot(x_ref[...], wu_scr[...]) + bu_ref[...]
        glu = jnp.minimum(hu[:, :ff], SWIGLU_LIMIT)
        lin = jnp.clip(hu[:, ff:], -SWIGLU_LIMIT, SWIGLU_LIMIT)
        act = glu * _sigmoid(SWIGLU_ALPHA * glu) * (lin + 1.0)
        o_ref[...] = (_dot(act.astype(BF16), wd_scr[...]) + bd_ref[...]).astype(o_ref.dtype)

    @pl.when(jnp.logical_not(active))
    def _():
        o_ref[...] = jnp.zeros_like(o_ref)


def _moe(layer, x_sorted, blk_e, n_active, w_up, b_up, w_down, b_down):
    n_rows, d = x_sorted.shape
    ff = w_down.shape[2]
    bm = MOE_ROWS
    grid_spec = pltpu.PrefetchScalarGridSpec(
        num_scalar_prefetch=2,
        grid=(n_rows // bm,),
        in_specs=[
            pl.BlockSpec((bm, d), lambda i, be, na: (i, 0)),
            pl.BlockSpec((None, None, d, 2 * ff), lambda i, be, na: (layer, be[i], 0, 0)),
            pl.BlockSpec((None, None, 1, 2 * ff), lambda i, be, na: (layer, be[i], 0, 0)),
            pl.BlockSpec((None, None, ff, d), lambda i, be, na: (layer, be[i], 0, 0)),
            pl.BlockSpec((None, None, 1, d), lambda i, be, na: (layer, be[i], 0, 0)),
        ],
        out_specs=pl.BlockSpec((bm, d), lambda i, be, na: (i, 0)),
        scratch_shapes=[pltpu.VMEM((d, 2 * ff), BF16), pltpu.VMEM((ff, d), BF16)],
    )
    return pl.pallas_call(
        functools.partial(_moe_kernel, ff=ff),
        grid_spec=grid_spec,
        out_shape=jax.ShapeDtypeStruct((n_rows, d), BF16),
        compiler_params=_cparams(("arbitrary",)),
        name="moe_experts",
    )(blk_e, n_active, x_sorted, w_up, b_up, w_down, b_down)


def _combine_kernel(x_ref, g2_ref, yg_ref, w_ref, o_ref):
    w = w_ref[...]
    y = None
    for k in range(TOP_K):
        term = w[:, k:k + 1] * yg_ref[k].astype(F32)
        y = term if y is None else y + term
    o_ref[...] = x_ref[...] + g2_ref[...] * y


def _combine(tok, layer, x_all, mod, y_gath, top_w):
    d = x_all.shape[-1]
    tm = TOKEN_TILE
    return pl.pallas_call(
        _combine_kernel,
        grid=(tok.n_tiles,),
        in_specs=[
            pl.BlockSpec((tm, d), lambda t: (t, 0)),
            tok.mod_spec(layer, 5, d),
            pl.BlockSpec((TOP_K, tm, d), lambda t: (0, t, 0)),
            pl.BlockSpec((tm, LANE), lambda t: (t, 0)),
        ],
        out_specs=pl.BlockSpec((tm, d), lambda t: (t, 0)),
        out_shape=jax.ShapeDtypeStruct(x_all.shape, F32),
        compiler_params=_cparams(("arbitrary",)),
        name="moe_combine",
    )(x_all, mod, y_gath, top_w)


def _final_norm_kernel(x_ref, g_ref, o_ref):
    x = x_ref[...]
    ms = jnp.mean(x * x, axis=-1, keepdims=True)
    o_ref[...] = (x * lax.rsqrt(ms + EPS)) * g_ref[...]


def _final_norm(x3, gain, n_lat):
    batch, _, d = x3.shape
    tm = TOKEN_TILE
    return pl.pallas_call(
        _final_norm_kernel,
        grid=(batch, n_lat // tm),
        in_specs=[pl.BlockSpec((None, tm, d), lambda b, i: (b, i, 0)),
                  pl.BlockSpec((1, d), lambda b, i: (0, 0))],
        out_specs=pl.BlockSpec((None, tm, d), lambda b, i: (b, i, 0)),
        out_shape=jax.ShapeDtypeStruct((batch, n_lat, d), F32),
        compiler_params=_cparams(("arbitrary", "arbitrary")),
        name="final_norm",
    )(x3, gain)


def _dispatch_plan(top_e, rank, counts, n_tok):
    n_exp = counts.shape[0]
    bm = MOE_ROWS
    n_pairs = n_tok * TOP_K
    n_blk = -(-(n_pairs + n_exp * (bm - 1)) // bm)
    padded = (counts + bm - 1) // bm * bm
    pad_end = jnp.cumsum(padded)
    pad_start = pad_end - padded
    onehot = top_e[:, :, None] == jnp.arange(n_exp, dtype=jnp.int32)[None, None, :]
    dest = jnp.sum(jnp.where(onehot, pad_start[None, None, :], 0), axis=-1) + rank
    n_active = (pad_end[-1] // bm).astype(jnp.int32).reshape(1)
    blk_start = jnp.arange(n_blk, dtype=jnp.int32) * bm
    blk_e = jnp.minimum(jnp.sum(pad_end[None, :] <= blk_start[:, None], axis=1), n_exp - 1).astype(jnp.int32)
    tok_ids = jnp.broadcast_to(jnp.arange(n_tok, dtype=jnp.int32)[:, None], dest.shape)
    row_tok = (jnp.arange(n_blk * bm, dtype=jnp.int32) % n_tok).at[dest.reshape(-1)].set(
        tok_ids.reshape(-1), unique_indices=True, mode='promise_in_bounds')
    return dest, row_tok, blk_e, n_active


def kernel(x, c, ctx, c_ctx, w_mod, b_mod, norm_mix, norm_ffn, w_in, hy_conv_w, hy_conv_b, hy_ffn_w1, hy_ffn_b1, hy_ffn_w2, hy_ffn_b2, hy_ffn_w3, hy_bias, pool_w, pool_scale, ssm_conv_w, ssm_conv_b, ssm_dt_bias, ssm_a_log, ssm_d, ssm_norm, w_branch, w_gate, w_out, router_w, router_b, exp_w_up, exp_b_up, exp_w_down, exp_b_down, norm_final):
    batch, n_lat, d = x.shape
    n_ctx = ctx.shape[1]
    depth = w_mod.shape[0]
    tok = _Tok(batch, n_lat, n_ctx)
    n_tok = tok.n_tok
    assert n_lat // GRID_W * GRID_W == n_lat and n_lat % n_ctx == 0

    hw = hy_bias.shape[1]
    pw = pool_scale.shape[1]
    sw = ssm_norm.shape[1]
    cw = ssm_conv_w.shape[2]
    fw = w_in.shape[2] - 3 * hw - pw - sw - cw - 2 * SSM_HEADS
    widths = (fw, 3 * hw, pw, sw, cw)
    splits = (fw, fw + hw, fw + hw + pw, fw + hw + pw + sw)
    n_exp = router_w.shape[2]

    w_in_p = jnp.pad(w_in, ((0, 0), (0, 0), (0, LANE - 2 * SSM_HEADS))).astype(BF16)
    w_gate_b = w_gate.astype(BF16)
    w_branch_b = w_branch.astype(BF16)
    w_out_b = w_out.astype(BF16)
    pool_w_bd = jax.vmap(_block_diag)(pool_w).astype(BF16)
    r3 = lambda a: a.reshape(a.shape[0], 1, a.shape[1])
    lane_pad = lambda a: jnp.pad(a, ((0, 0), (0, LANE - a.shape[1])))
    norm_mix3, norm_ffn3 = r3(norm_mix), r3(norm_ffn)
    hy_conv_b3, hy_bias3 = r3(hy_conv_b), r3(hy_bias)
    ssm_conv_b3 = r3(ssm_conv_b)
    pool_scale3 = r3(pool_scale)
    dt_bias3 = r3(lane_pad(ssm_dt_bias.reshape(depth, 2 * SSM_HEADS)))
    a_row3 = r3(lane_pad(-jnp.exp(ssm_a_log.astype(F32)).reshape(depth, 2 * SSM_HEADS)))
    d_row3 = r3(jnp.repeat(ssm_d, SSM_HEAD_DIM, axis=1))
    ssm_norm3 = r3(ssm_norm)
    router_w_p = jnp.pad(router_w, ((0, 0), (0, 0), (0, LANE - n_exp)))
    router_b3 = r3(lane_pad(router_b))
    exp_b_up4 = exp_b_up.reshape(depth, n_exp, 1, -1)
    exp_b_down4 = exp_b_down.reshape(depth, n_exp, 1, -1)
    emb = hy_ffn_w1.shape[1]
    ffn = hy_ffn_w1.shape[2]
    f_w1 = jnp.pad(hy_ffn_w1, ((0, 0), (0, LANE - emb), (0, LANE - ffn)))
    f_b1 = r3(lane_pad(hy_ffn_b1))
    f_w2 = jnp.pad(hy_ffn_w2, ((0, 0), (0, LANE - ffn), (0, LANE - ffn)))
    f_b2 = r3(lane_pad(hy_ffn_b2))
    f_w3 = jnp.pad(hy_ffn_w3, ((0, 0), (0, LANE - ffn), (0, 0)))

    freqs = jnp.linspace(1e-4, HYENA_BANDS - 1, HYENA_BANDS, dtype=F32)
    freqs_row = jnp.concatenate([jnp.zeros((1,), F32), freqs, freqs,
                                 jnp.zeros((LANE - 1 - 2 * HYENA_BANDS,), F32)])[None, :]
    deltas_row = jnp.linspace(HYENA_MIN_DECAY, HYENA_MAX_DECAY, hw, dtype=F32)[None, :]
    gi = jnp.arange(FOURIER_GROUP_W)
    ang = (gi[:, None] * gi[None, :] % FOURIER_GROUP_W).astype(F32) * (2.0 * math.pi / FOURIER_GROUP_W)
    n_fg = fw // FOURIER_GROUP_W
    eye = jnp.eye(n_fg, dtype=F32)
    wc = jnp.concatenate([jnp.kron(eye, jnp.cos(ang)), jnp.kron(eye, jnp.sin(ang))], axis=1).astype(BF16)
    band, invc = _pool_tables(n_lat, n_ctx)
    qi = jnp.arange(SSM_CHUNK)
    tri_f = (qi[:, None] >= qi[None, :]).astype(BF16)
    tri_b = (qi[:, None] <= qi[None, :]).astype(BF16)
    ti = jnp.arange(TOKEN_TILE)
    ltri = (ti[:, None] > ti[None, :]).astype(BF16)

    seqs = {}
    for name, n, row_off in (("ctx", n_ctx, n_lat), ("lat", n_lat, 0)):
        cn, sn = _trig_tables(n, n)
        fwd, inv = _hyena_dft_tables(n)
        g_all, nrm = _filter_mlp(n, freqs_row, deltas_row, f_w1, f_b1, f_w2, f_b2, f_w3)
        kr, ki = _filter_spectrum(fwd, g_all, nrm, n, hw)
        seqs[name] = dict(n=n, off=row_off, cn=cn.astype(BF16), sn=sn.astype(BF16), fwd=fwd, inv=inv,
                          kr=kr, ki=ki)

    cc = jnp.concatenate([c, c_ctx[None, :]], axis=0)
    rows = -(-cc.shape[0] // 8) * 8
    mod = _modulation(jnp.pad(cc, ((0, rows - cc.shape[0]), (0, 0))), w_mod, b_mod)
    mod = mod.reshape(depth, rows, 6, 1, d)

    x_all = jnp.concatenate([x, ctx], axis=1).reshape(n_tok, d)
    seq = tok.seq
    zero_state = jnp.zeros((batch, SSM_GROUPS, SSM_STATE, HEADS_PER_GROUP * SSM_HEAD_DIM), F32)

    for layer in range(depth):
        xcs, hy, pool, z, xbc, dtr = _inproj(tok, layer, x_all, mod, norm_mix3, w_in_p, wc, band, invc,
                                             pool_w_bd, pool_scale3, widths)
        xcs3 = xcs.reshape(batch, seq, 2 * fw)
        hy3 = hy.reshape(batch, seq, 3 * hw)
        z3 = z.reshape(batch, seq, sw)
        xbc3 = xbc.reshape(batch, seq, cw)
        dt3 = dtr.reshape(batch, seq, LANE)
        outs = {}
        states = (zero_state, zero_state)
        for name in ("ctx", "lat"):
            s = seqs[name]
            n, off = s["n"], s["off"]
            y_f = _fourier_seq(xcs3, s["cn"], s["sn"], n, off, fw)
            vv, x1 = _hy_pre(hy3, hy_conv_w, hy_conv_b3, layer, n, off, hw)
            y2 = _hy_fwd(s["fwd"], vv, s["kr"], s["ki"], layer, n, hw)
            y_h = _hy_inv(s["inv"], y2, x1, vv, hy_bias3, layer, n, hw)
            yf, st_f = _ssd_pass(xbc3, dt3, states[0], tri_f, ssm_conv_w, ssm_conv_b3, dt_bias3, a_row3,
                                 layer, n, off, reverse=False)
            y_s, st_b = _ssd_pass(xbc3, dt3, states[1], tri_b, ssm_conv_w, ssm_conv_b3, dt_bias3, a_row3,
                                  layer, n, off, reverse=True, fin=(z3, yf, d_row3, ssm_norm3))
            states = (st_f, st_b)
            outs[name] = (y_f, y_h, y_s)
        x_all = _merge(tok, layer, x_all, mod, norm_mix3, outs["lat"][0], outs["ctx"][0], outs["lat"][1],
                       outs["ctx"][1], pool, outs["lat"][2], outs["ctx"][2], w_gate_b, w_branch_b, w_out_b,
                       splits)
        h2, top_e, top_w, rank, counts = _router(tok, layer, x_all, mod, norm_ffn3, router_w_p, router_b3,
                                                 ltri, n_exp)
        dest, row_tok, blk_e, n_active = _dispatch_plan(top_e[:, :TOP_K], rank[:, :TOP_K],
                                                        counts[0, :n_exp].astype(jnp.int32), n_tok)
        x_sorted = h2[row_tok]
        y_rows = _moe(layer, x_sorted, blk_e, n_active, exp_w_up, exp_b_up4, exp_w_down, exp_b_down4)
        y_gath = y_rows[dest.T.reshape(-1)].reshape(TOP_K, n_tok, d)
        x_all = _combine(tok, layer, x_all, mod, y_gath, top_w)

    return _final_norm(x_all.reshape(batch, seq, d), norm_final[None, :], n_lat)
```

```python
import functools
import math

import jax
import jax.numpy as jnp
from jax import lax
from jax.experimental import pallas as pl
from jax.experimental.pallas import tpu as pltpu

F32 = jnp.float32
BF16 = jnp.bfloat16

GRID_W = 64
EPS = 1e-6
FOURIER_GROUP_W = 64
HYENA_BANDS = 16
HYENA_MIN_DECAY = -math.log(1e-2) / 1.5
HYENA_MAX_DECAY = -math.log(1e-2) / 0.3
POOL_WINDOWS = (2, 4, 8, 16)
POOL_GROUP_W = 64
SSM_GROUPS = 2
HEADS_PER_GROUP = 4
SSM_HEADS = SSM_GROUPS * HEADS_PER_GROUP
SSM_HEAD_DIM = 64
SSM_STATE = 128
SSM_CHUNK = 128
TOP_K = 4
SWIGLU_LIMIT = 7.0
SWIGLU_ALPHA = 1.702

LANE = 128
BF16_SUBLANE = 16
TOKEN_TILE = 256
SEQ_TILE = 512
MOE_ROWS = 512
VMEM_LIMIT = 56 * 1024 * 1024


def _cparams(sem):
    return pltpu.CompilerParams(dimension_semantics=sem, vmem_limit_bytes=VMEM_LIMIT)


def _sigmoid(x):
    return 1.0 / (1.0 + jnp.exp(-x))


def _dot(a, b):
    return jnp.dot(a, b, preferred_element_type=F32)


def _dot_hi(a, b):
    return jnp.dot(a, b, preferred_element_type=F32, precision=lax.Precision.HIGHEST)


def _split3(x):
    x1 = x.astype(BF16)
    r1 = x - x1.astype(F32)
    x2 = r1.astype(BF16)
    x3 = (r1 - x2.astype(F32)).astype(BF16)
    return x1, x2, x3


def _dot_exact01(m01, x):
    x1, x2, x3 = _split3(x)
    return _dot(m01, x1) + _dot(m01, x2) + _dot(m01, x3)


def _mod_kernel(c_ref, w_ref, b_ref, o_ref):
    c = c_ref[...]
    s = c * _sigmoid(c)
    o_ref[...] = _dot_hi(s, w_ref[...]) + b_ref[...]


def _modulation(cc, w_mod, b_mod):
    depth, d, six_d = w_mod.shape
    rows = cc.shape[0]
    nj = six_d // d
    return pl.pallas_call(
        _mod_kernel,
        grid=(depth, nj),
        in_specs=[
            pl.BlockSpec((rows, d), lambda l, j: (0, 0)),
            pl.BlockSpec((None, d, d), lambda l, j: (l, 0, j)),
            pl.BlockSpec((None, 1, d), lambda l, j: (l, 0, j)),
        ],
        out_specs=pl.BlockSpec((None, rows, d), lambda l, j: (l, 0, j)),
        out_shape=jax.ShapeDtypeStruct((depth, rows, six_d), F32),
        compiler_params=_cparams(("arbitrary", "arbitrary")),
        name="modulation",
    )(cc, w_mod, b_mod.reshape(depth, 1, six_d))


class _Tok:
    def __init__(self, batch, n_lat, n_ctx):
        assert n_ctx == TOKEN_TILE and n_lat % TOKEN_TILE == 0
        self.batch, self.n_lat, self.n_ctx = batch, n_lat, n_ctx
        self.seq = n_lat + n_ctx
        self.tiles_per_b = self.seq // TOKEN_TILE
        self.lat_tiles = n_lat // TOKEN_TILE
        self.n_tiles = batch * self.tiles_per_b
        self.n_tok = batch * self.seq

    def b(self, t):
        return t // self.tiles_per_b

    def w(self, t):
        return t % self.tiles_per_b

    def is_ctx(self, t):
        return (t % self.tiles_per_b) >= self.lat_tiles

    def mod_row(self, t):
        return jnp.where(self.is_ctx(t), self.batch, self.b(t))

    def mod_spec(self, layer, part, d):
        return pl.BlockSpec((None, None, None, 1, d), lambda t: (layer, self.mod_row(t), part, 0, 0))

    def lat_spec(self, width):
        return pl.BlockSpec((None, TOKEN_TILE, width),
                            lambda t: (self.b(t), jnp.minimum(self.w(t), self.lat_tiles - 1), 0))

    def ctx_spec(self, width):
        return pl.BlockSpec((None, TOKEN_TILE, width), lambda t: (self.b(t), 0, 0))


def _norm_mod(x, gain, scale, shift):
    ms = jnp.mean(x * x, axis=-1, keepdims=True)
    return (x * lax.rsqrt(ms + EPS)) * gain * (1.0 + scale) + shift


def _moe_residual(x, g2, yg_ref, w_ref):
    w = w_ref[...]
    y = None
    for k in range(TOP_K):
        term = w[:, k:k + 1] * yg_ref[k].astype(F32)
        y = term if y is None else y + term
    return x + g2 * y


def _inproj_kernel(*refs, widths, combine):
    if combine:
        (x_ref, g2_ref, yg_ref, w_ref, sc_ref, sh_ref, g_ref, win_ref, wc_ref, band_ref, invc_ref, pw_ref,
         ps_ref, xn_ref, xcs_ref, hy_ref, pool_ref, z_ref, xbc_ref, dt_ref) = refs
        x = _moe_residual(x_ref[...], g2_ref[...], yg_ref, w_ref)
        xn_ref[...] = x
    else:
        (x_ref, sc_ref, sh_ref, g_ref, win_ref, wc_ref, band_ref, invc_ref, pw_ref, ps_ref,
         xcs_ref, hy_ref, pool_ref, z_ref, xbc_ref, dt_ref) = refs
        x = x_ref[...]
    fw, hw, pw, zw, cw = widths
    hb = _norm_mod(x, g_ref[...], sc_ref[...], sh_ref[...]).astype(BF16)
    o = 0
    f_in = _dot(hb, win_ref[:, o:o + fw])
    xcs_ref[...] = _dot(f_in.astype(BF16), wc_ref[...]).astype(BF16)
    o += fw
    hy_ref[...] = _dot(hb, win_ref[:, o:o + hw]).astype(BF16)
    o += hw
    u = _dot(hb, win_ref[:, o:o + pw])
    o += pw
    z_ref[...] = _dot(hb, win_ref[:, o:o + zw]).astype(BF16)
    o += zw
    xbc_ref[...] = _dot(hb, win_ref[:, o:o + cw]).astype(BF16)
    o += cw
    dt_ref[...] = _dot(hb, win_ref[:, o:o + LANE])
    gw = POOL_GROUP_W
    parts = []
    for gi in range(len(POOL_WINDOWS)):
        ug = u[:, gi * gw:(gi + 1) * gw]
        parts.append(_dot_exact01(band_ref[gi], ug))
    pooled = jnp.concatenate(parts, axis=1) * invc_ref[...] - u
    pool_ref[...] = (_dot(pooled.astype(BF16), pw_ref[...]) * ps_ref[...]).astype(BF16)


def _pool_tables(n_lat, n_ctx):
    tm = TOKEN_TILE
    p = jnp.arange(tm)
    bands, invs = [], []
    for row_len in (GRID_W, n_ctx):
        assert tm % row_len == 0
        pr, rr = p % row_len, p // row_len
        bg, ig = [], []
        for win in POOL_WINDOWS:
            lo = jnp.clip(pr - win // 2, 0, row_len)
            hi = jnp.clip(pr + win // 2, 0, row_len)
            q = pr[None, :]
            m = (rr[:, None] == rr[None, :]) & (q >= lo[:, None]) & (q < hi[:, None])
            bg.append(m.astype(BF16))
            ig.append(jnp.broadcast_to((1.0 / (hi - lo).astype(F32))[:, None], (tm, POOL_GROUP_W)))
        bands.append(jnp.stack(bg))
        invs.append(jnp.concatenate(ig, axis=1))
    return jnp.stack(bands), jnp.stack(invs)


def _block_diag(w):
    g, a, b = w.shape
    out = jnp.zeros((g * a, g * b), w.dtype)
    for i in range(g):
        out = out.at[i * a:(i + 1) * a, i * b:(i + 1) * b].set(w[i])
    return out


def _inproj(tok, layer, x_all, mod, norm_mix, w_in_p, wc, band, invc, pool_w_bd, pool_scale, widths,
            moe=None):
    d = x_all.shape[-1]
    fw, hw, pw, zw, cw = widths
    tm = TOKEN_TILE
    n_tok = tok.n_tok
    kind = lambda t: jnp.where(tok.is_ctx(t), 1, 0)
    const2 = lambda t: (0, 0)
    pre_specs, pre_args, pre_outs = [], [], []
    if moe is not None:
        pre_specs = [tok.mod_spec(layer - 1, 5, d),
                     pl.BlockSpec((TOP_K, tm, d), lambda t: (0, t, 0)),
                     pl.BlockSpec((tm, LANE), lambda t: (t, 0))]
        pre_args = [mod, moe[0], moe[1]]
        pre_outs = [jax.ShapeDtypeStruct((n_tok, d), F32)]
    outs = pre_outs + [
        jax.ShapeDtypeStruct((n_tok, 2 * fw), BF16),
        jax.ShapeDtypeStruct((n_tok, hw), BF16),
        jax.ShapeDtypeStruct((n_tok, pw), BF16),
        jax.ShapeDtypeStruct((n_tok, zw), BF16),
        jax.ShapeDtypeStruct((n_tok, cw), BF16),
        jax.ShapeDtypeStruct((n_tok, LANE), F32),
    ]
    return pl.pallas_call(
        functools.partial(_inproj_kernel, widths=widths, combine=moe is not None),
        grid=(tok.n_tiles,),
        in_specs=[pl.BlockSpec((tm, d), lambda t: (t, 0))] + pre_specs + [
            tok.mod_spec(layer, 1, d),
            tok.mod_spec(layer, 0, d),
            pl.BlockSpec((None, 1, d), lambda t: (layer, 0, 0)),
            pl.BlockSpec((None,) + w_in_p.shape[1:], lambda t: (layer, 0, 0)),
            pl.BlockSpec(wc.shape, const2),
            pl.BlockSpec((None,) + band.shape[1:], lambda t: (kind(t), 0, 0, 0)),
            pl.BlockSpec((None,) + invc.shape[1:], lambda t: (kind(t), 0, 0)),
            pl.BlockSpec((None,) + pool_w_bd.shape[1:], lambda t: (layer, 0, 0)),
            pl.BlockSpec((None, 1, pw), lambda t: (layer, 0, 0)),
        ],
        out_specs=[pl.BlockSpec((tm, s.shape[1]), lambda t: (t, 0)) for s in outs],
        out_shape=outs,
        compiler_params=_cparams(("arbitrary",)),
        name="inproj",
    )(x_all, *pre_args, mod, mod, norm_mix, w_in_p, wc, band, invc, pool_w_bd, pool_scale)


def _trig_tables(n, period):
    k = jnp.arange(n, dtype=jnp.int32)

    def cs(j):
        m = (j[:, None] * k[None, :]) % period
        ang = m.astype(F32) * (2.0 * math.pi / period)
        return jnp.cos(ang), jnp.sin(ang)

    split = 64
    if n % split or n <= split:
        return cs(k)
    c1, s1 = cs(jnp.arange(n // split, dtype=jnp.int32) * split)
    c2, s2 = cs(jnp.arange(split, dtype=jnp.int32))
    c = c1[:, None, :] * c2[None, :, :] - s1[:, None, :] * s2[None, :, :]
    s = s1[:, None, :] * c2[None, :, :] + c1[:, None, :] * s2[None, :, :]
    return c.reshape(n, n), s.reshape(n, n)


def _hyena_dft_tables(n):
    c, s = _trig_tables(n, 2 * n)
    nyq = jnp.where(jnp.arange(n) % 2 == 0, 1.0, -1.0).astype(F32)
    s = (-s).at[0].set(nyq)
    fwd = jnp.concatenate([c, s], axis=0).astype(BF16)
    return fwd, fwd.T


def _fourier_kernel(c_ref, s_ref, xc_ref, xs_ref, o_ref, acc_ref, *, scale):
    k = pl.program_id(1)

    @pl.when(k == 0)
    def _():
        acc_ref[...] = jnp.zeros_like(acc_ref)

    nb = xc_ref.shape[0]
    rc = jnp.concatenate([xc_ref[b] for b in range(nb)], axis=1)
    rs = jnp.concatenate([xs_ref[b] for b in range(nb)], axis=1)
    acc_ref[...] += _dot(c_ref[...], rc) - _dot(s_ref[...], rs)

    @pl.when(k == pl.num_programs(1) - 1)
    def _():
        w = o_ref.shape[2]
        for b in range(nb):
            o_ref[b] = (acc_ref[:, b * w:(b + 1) * w] * scale).astype(o_ref.dtype)


def _fourier_seq(xcs3, cn, sn, n, row_off, fw):
    batch = xcs3.shape[0]
    tl = min(SEQ_TILE, n)
    nt = n // tl
    off = row_off // tl
    scale = 1.0 / math.sqrt(n * FOURIER_GROUP_W)
    return pl.pallas_call(
        functools.partial(_fourier_kernel, scale=scale),
        grid=(nt, nt),
        in_specs=[
            pl.BlockSpec((tl, tl), lambda i, k: (i, k)),
            pl.BlockSpec((tl, tl), lambda i, k: (i, k)),
            pl.BlockSpec((batch, tl, fw), lambda i, k: (0, off + k, 0)),
            pl.BlockSpec((batch, tl, fw), lambda i, k: (0, off + k, 1)),
        ],
        out_specs=pl.BlockSpec((batch, tl, fw), lambda i, k: (0, i, 0)),
        out_shape=jax.ShapeDtypeStruct((batch, n, fw), BF16),
        scratch_shapes=[pltpu.VMEM((tl, batch * fw), F32)],
        compiler_params=_cparams(("arbitrary", "arbitrary")),
        name="fourier_seq",
    )(cn, sn, xcs3, xcs3)


def _shift_rows(x, prev_row, next_row):
    n = x.shape[0]
    rows = lax.broadcasted_iota(jnp.int32, x.shape, 0)
    xm = jnp.where(rows == 0, prev_row, pltpu.roll(x, 1, 0))
    xp = jnp.where(rows == n - 1, next_row, pltpu.roll(x, n - 1, 0))
    return xm, xp


def _conv3(x, prev_row, next_row, w_ref, b_ref):
    xm, xp = _shift_rows(x, prev_row, next_row)
    return w_ref[0:1, :] * xm + w_ref[1:2, :] * x + w_ref[2:3, :] * xp + b_ref[...]


def _hy_pre_kernel(x_ref, xp_ref, xn_ref, w_ref, b_ref, vv_ref, x1_ref, *, hw):
    i = pl.program_id(1)
    last = pl.num_programs(1) - 1
    x = x_ref[...].astype(F32)
    hs = xp_ref.shape[0]
    prev_row = jnp.where(i == 0, 0.0, xp_ref[hs - 1:hs, :].astype(F32))
    next_row = jnp.where(i == last, 0.0, xn_ref[0:1, :].astype(F32))
    u = _conv3(x, prev_row, next_row, w_ref, b_ref)
    x1_ref[...] = u[:, :hw].astype(BF16)
    vv_ref[...] = (u[:, hw:2 * hw] * u[:, 2 * hw:]).astype(BF16)


def _halo_specs(width, tile, row_off, seq_rows):
    hs = BF16_SUBLANE
    per = tile // hs
    base = row_off // hs
    top = seq_rows // hs - 1
    prev = pl.BlockSpec((None, hs, width), lambda b, i: (b, jnp.maximum(base + i * per - 1, 0), 0))
    nxt = pl.BlockSpec((None, hs, width), lambda b, i: (b, jnp.minimum(base + (i + 1) * per, top), 0))
    return prev, nxt


def _hy_pre(hy3, conv_w, conv_b, layer, n, row_off, hw):
    batch, seq, w3 = hy3.shape
    tl = min(SEQ_TILE, n)
    off = row_off // tl
    prev, nxt = _halo_specs(w3, tl, row_off, seq)
    out = jax.ShapeDtypeStruct((batch, n, hw), BF16)
    return pl.pallas_call(
        functools.partial(_hy_pre_kernel, hw=hw),
        grid=(batch, n // tl),
        in_specs=[
            pl.BlockSpec((None, tl, w3), lambda b, i: (b, off + i, 0)),
            prev, nxt,
            pl.BlockSpec((None,) + conv_w.shape[1:], lambda b, i: (layer, 0, 0)),
            pl.BlockSpec((None, 1, w3), lambda b, i: (layer, 0, 0)),
        ],
        out_specs=[pl.BlockSpec((None, tl, hw), lambda b, i: (b, i, 0))] * 2,
        out_shape=[out, out],
        compiler_params=_cparams(("arbitrary", "arbitrary")),
        name="hyena_pre",
    )(hy3, hy3, hy3, conv_w, conv_b)


def _filter_mlp_kernel(fr_ref, dl_ref, w1_ref, b1_ref, w2_ref, b2_ref, w3_ref, g_ref, nrm_ref, *, n, hw):
    j = pl.program_id(1)
    rt = g_ref.shape[0]
    lane = lax.broadcasted_iota(jnp.int32, (rt, LANE), 1)
    pos = (lax.broadcasted_iota(jnp.int32, (rt, LANE), 0) + j * rt).astype(F32)
    t = pos / (n - 1)
    ang = 2.0 * math.pi * pos / n
    arg = ang * fr_ref[...]
    feats = jnp.where(lane == 0, t,
                      jnp.where(lane <= HYENA_BANDS, jnp.cos(arg),
                                jnp.where(lane <= 2 * HYENA_BANDS, -jnp.sin(arg), 0.0)))
    h1 = jnp.sin(_dot_hi(feats, w1_ref[...]) + b1_ref[...])
    h2 = jnp.sin(_dot_hi(h1, w2_ref[...]) + b2_ref[...])
    k = _dot_hi(h2, w3_ref[...])
    decay = jnp.exp(-t[:, 0:1] * dl_ref[...])
    kf = k[:, :hw] * decay
    kb = jnp.where(pos[:, 0:1] == 0.0, 0.0, k[:, hw:] * decay)
    g_ref[:, :hw] = (kf + kb).astype(BF16)
    g_ref[:, hw:] = (kf - kb).astype(BF16)
    part = jnp.sum(jnp.abs(kf) + jnp.abs(kb), axis=0, keepdims=True)

    @pl.when(j == 0)
    def _():
        nrm_ref[...] = part

    @pl.when(j > 0)
    def _():
        nrm_ref[...] += part


def _filter_mlp(n, freqs_row, deltas_row, w1, b1, w2, b2, w3):
    depth = w1.shape[0]
    hw = deltas_row.shape[1]
    rt = min(SEQ_TILE, n)
    per_layer = lambda l, j: (l, 0, 0)
    return pl.pallas_call(
        functools.partial(_filter_mlp_kernel, n=n, hw=hw),
        grid=(depth, n // rt),
        in_specs=[
            pl.BlockSpec((1, LANE), lambda l, j: (0, 0)),
            pl.BlockSpec((1, hw), lambda l, j: (0, 0)),
            pl.BlockSpec((None,) + w1.shape[1:], per_layer),
            pl.BlockSpec((None,) + b1.shape[1:], per_layer),
            pl.BlockSpec((None,) + w2.shape[1:], per_layer),
            pl.BlockSpec((None,) + b2.shape[1:], per_layer),
            pl.BlockSpec((None,) + w3.shape[1:], per_layer),
        ],
        out_specs=[
            pl.BlockSpec((rt, 2 * hw), lambda l, j: (j, l)),
            pl.BlockSpec((None, 1, hw), lambda l, j: (l, 0, 0)),
        ],
        out_shape=[
            jax.ShapeDtypeStruct((n, depth * 2 * hw), BF16),
            jax.ShapeDtypeStruct((depth, 1, hw), F32),
        ],
        compiler_params=_cparams(("arbitrary", "arbitrary")),
        name="hyena_filter_mlp",
    )(freqs_row, deltas_row, w1, b1, w2, b2, w3)


def _rdft_accumulate(fc_ref, fs_ref, rhs, accc_ref, accs_ref):
    k = pl.program_id(1)

    @pl.when(k == 0)
    def _():
        accc_ref[...] = jnp.zeros_like(accc_ref)
        accs_ref[...] = jnp.zeros_like(accs_ref)

    accc_ref[...] += _dot(fc_ref[...], rhs)
    accs_ref[...] += _dot(fs_ref[...], rhs)


def _filter_spec_kernel(fc_ref, fs_ref, g_ref, nrm_ref, kr_ref, ki_ref, accc_ref, accs_ref, *, hw):
    i = pl.program_id(0)
    _rdft_accumulate(fc_ref, fs_ref, g_ref[...], accc_ref, accs_ref)

    @pl.when(pl.program_id(1) == pl.num_programs(1) - 1)
    def _():
        tm = accc_ref.shape[0]
        row0 = (lax.broadcasted_iota(jnp.int32, (tm, hw), 0) + i * tm) == 0
        for l in range(kr_ref.shape[0]):
            inv = 1.0 / nrm_ref[l]
            o = l * 2 * hw
            kr_ref[l] = accc_ref[:, o:o + hw] * inv
            ki_ref[l] = jnp.where(row0, accs_ref[:, o:o + hw], accs_ref[:, o + hw:o + 2 * hw]) * inv


def _filter_spectrum(fwd, g_all, nrm, n, hw):
    depth = nrm.shape[0]
    tl = min(SEQ_TILE, n)
    nt = n // tl
    width = g_all.shape[1]
    out = jax.ShapeDtypeStruct((depth, n, hw), F32)
    return pl.pallas_call(
        functools.partial(_filter_spec_kernel, hw=hw),
        grid=(nt, nt),
        in_specs=[
            pl.BlockSpec((tl, tl), lambda i, k: (i, k)),
            pl.BlockSpec((tl, tl), lambda i, k: (nt + i, k)),
            pl.BlockSpec((tl, width), lambda i, k: (k, 0)),
            pl.BlockSpec(nrm.shape, lambda i, k: (0, 0, 0)),
        ],
        out_specs=[pl.BlockSpec((depth, tl, hw), lambda i, k: (0, i, 0))] * 2,
        out_shape=[out, out],
        scratch_shapes=[pltpu.VMEM((tl, width), F32)] * 2,
        compiler_params=_cparams(("arbitrary", "arbitrary")),
        name="hyena_filter_spectrum",
    )(fwd, fwd, g_all, nrm)


def _hy_fwd_kernel(fc_ref, fs_ref, vv_ref, kr_ref, ki_ref, y_ref, accc_ref, accs_ref):
    i = pl.program_id(0)
    nb = vv_ref.shape[0]
    rhs = jnp.concatenate([vv_ref[b] for b in range(nb)], axis=1)
    _rdft_accumulate(fc_ref, fs_ref, rhs, accc_ref, accs_ref)

    @pl.when(pl.program_id(1) == pl.num_programs(1) - 1)
    def _():
        tm, hw = kr_ref.shape
        row0 = (lax.broadcasted_iota(jnp.int32, (tm, hw), 0) + i * tm) == 0
        kr, ki = kr_ref[...], ki_ref[...]
        for b in range(nb):
            vr = accc_ref[:, b * hw:(b + 1) * hw]
            vi = accs_ref[:, b * hw:(b + 1) * hw]
            yr = jnp.where(row0, 0.5 * vr * kr, vr * kr - vi * ki)
            yi = jnp.where(row0, 0.5 * vi * ki, vr * ki + vi * kr)
            y_ref[0, :, b * hw:(b + 1) * hw] = yr.astype(BF16)
            y_ref[1, :, b * hw:(b + 1) * hw] = yi.astype(BF16)


def _hy_fwd(fwd, vv, kr, ki, layer, n, hw):
    batch = vv.shape[0]
    tl = min(SEQ_TILE, n)
    nt = n // tl
    return pl.pallas_call(
        _hy_fwd_kernel,
        grid=(nt, nt),
        in_specs=[
            pl.BlockSpec((tl, tl), lambda i, k: (i, k)),
            pl.BlockSpec((tl, tl), lambda i, k: (nt + i, k)),
            pl.BlockSpec((batch, tl, hw), lambda i, k: (0, k, 0)),
            pl.BlockSpec((None, tl, hw), lambda i, k: (layer, i, 0)),
            pl.BlockSpec((None, tl, hw), lambda i, k: (layer, i, 0)),
        ],
        out_specs=pl.BlockSpec((2, tl, batch * hw), lambda i, k: (0, i, 0)),
        out_shape=jax.ShapeDtypeStruct((2, n, batch * hw), BF16),
        scratch_shapes=[pltpu.VMEM((tl, batch * hw), F32)] * 2,
        compiler_params=_cparams(("arbitrary", "arbitrary")),
        name="hyena_fwd_dft",
    )(fwd, fwd, vv, kr, ki)


def _hy_inv_kernel(fi_ref, y_ref, x1_ref, vv_ref, bias_ref, o_ref, acc_ref, *, scale):
    k = pl.program_id(1)

    @pl.when(k == 0)
    def _():
        acc_ref[...] = jnp.zeros_like(acc_ref)

    acc_ref[...] += _dot(fi_ref[...], y_ref[...])

    @pl.when(k == pl.num_programs(1) - 1)
    def _():
        nb, _, hw = x1_ref.shape
        for b in range(nb):
            conv = acc_ref[:, b * hw:(b + 1) * hw] * scale
            vv = vv_ref[b].astype(F32)
            o_ref[b] = (x1_ref[b].astype(F32) * (conv + vv * bias_ref[...])).astype(BF16)


def _hy_inv(inv, y2, x1, vv, hy_bias, layer, n, hw):
    batch = vv.shape[0]
    tl = min(SEQ_TILE, n)
    nt = n // tl
    return pl.pallas_call(
        functools.partial(_hy_inv_kernel, scale=1.0 / n),
        grid=(nt, 2 * nt),
        in_specs=[
            pl.BlockSpec((tl, tl), lambda i, k: (i, k)),
            pl.BlockSpec((tl, batch * hw), lambda i, k: (k, 0)),
            pl.BlockSpec((batch, tl, hw), lambda i, k: (0, i, 0)),
            pl.BlockSpec((batch, tl, hw), lambda i, k: (0, i, 0)),
            pl.BlockSpec((None, 1, hw), lambda i, k: (layer, 0, 0)),
        ],
        out_specs=pl.BlockSpec((batch, tl, hw), lambda i, k: (0, i, 0)),
        out_shape=jax.ShapeDtypeStruct((batch, n, hw), BF16),
        scratch_shapes=[pltpu.VMEM((tl, batch * hw), F32)],
        compiler_params=_cparams(("arbitrary", "arbitrary")),
        name="hyena_inv_dft",
    )(inv, y2.reshape(2 * n, batch * hw), x1, vv, hy_bias)


def _softplus(x):
    return jnp.maximum(x, 0.0) + jnp.log1p(jnp.exp(-jnp.abs(x)))


def _ssd_kernel(*refs, reverse, final, nc, lane0):
    if final:
        (xbc_ref, xp_ref, xn_ref, dt_ref, h0_ref, tri_ref, cw_ref, cb_ref, dtb_ref, a_ref,
         z_ref, yf_ref, d_ref, nrm_ref, out_ref, hout_ref, h_scr) = refs
    else:
        (xbc_ref, xp_ref, xn_ref, dt_ref, h0_ref, tri_ref, cw_ref, cb_ref, dtb_ref, a_ref,
         out_ref, hout_ref, h_scr) = refs
    step = pl.program_id(1)
    cc = (nc - 1 - step) if reverse else step

    @pl.when(step == 0)
    def _():
        h_scr[...] = h0_ref[...]

    q = xbc_ref.shape[0]
    hs = xp_ref.shape[0]
    x = xbc_ref[...].astype(F32)
    prev_row = jnp.where(cc == 0, 0.0, xp_ref[hs - 1:hs, :].astype(F32))
    next_row = jnp.where(cc == nc - 1, 0.0, xn_ref[0:1, :].astype(F32))
    u = _conv3(x, prev_row, next_row, cw_ref, cb_ref)
    u = u * _sigmoid(u)
    width = SSM_HEADS * SSM_HEAD_DIM
    gs = SSM_STATE
    xs = u[:, :width]
    bm = u[:, width:width + SSM_GROUPS * gs]
    cm = u[:, width + SSM_GROUPS * gs:]

    dt = _softplus(dt_ref[...] + dtb_ref[...])
    acs = _dot_exact01(tri_ref[...], dt * a_ref[...])
    acs_t = acs.T
    tot = acs[0:1, :] if reverse else acs[q - 1:q, :]
    to_end = jnp.exp(tot - acs)
    frm = jnp.exp(acs)
    cdec = jnp.exp(tot)
    ri = lax.broadcasted_iota(jnp.int32, (q, q), 0)
    ci = lax.broadcasted_iota(jnp.int32, (q, q), 1)
    mask = (ri <= ci) if reverse else (ri >= ci)

    pd = SSM_HEAD_DIM
    ys = []
    for g in range(SSM_GROUPS):
        bg = bm[:, g * gs:(g + 1) * gs]
        cg = cm[:, g * gs:(g + 1) * gs].astype(BF16)
        scores = lax.dot_general(cg, bg.astype(BF16), (((1,), (1,)), ((), ())), preferred_element_type=F32)
        hg = h_scr[g]
        yoff = _dot(cg, hg.astype(BF16))
        wparts, decs = [], []
        for r in range(HEADS_PER_GROUP):
            hd = g * HEADS_PER_GROUP + r
            li = lane0 + hd
            lm = jnp.where(mask, jnp.exp(acs[:, li:li + 1] - acs_t[li:li + 1, :]), 0.0)
            m = (scores * lm).astype(BF16)
            xdt = xs[:, hd * pd:(hd + 1) * pd] * dt[:, li:li + 1]
            yd = _dot(m, xdt.astype(BF16))
            ys.append(yd + yoff[:, r * pd:(r + 1) * pd] * frm[:, li:li + 1])
            wparts.append(xdt * to_end[:, li:li + 1])
            decs.append(jnp.broadcast_to(cdec[:, li:li + 1], (1, pd)))
        wg = jnp.concatenate(wparts, axis=1).astype(BF16)
        st = _dot(bg.T.astype(BF16), wg)
        h_scr[g] = hg * jnp.concatenate(decs, axis=1) + st
    y = jnp.concatenate(ys, axis=1)

    @pl.when(step == nc - 1)
    def _():
        hout_ref[...] = h_scr[...]

    if final:
        y = y + yf_ref[...] + xs * d_ref[...]
        z = z_ref[...].astype(F32)
        y = y * (z * _sigmoid(z))
        ms = jnp.mean(y * y, axis=-1, keepdims=True)
        out_ref[...] = ((y * lax.rsqrt(ms + EPS)) * nrm_ref[...]).astype(out_ref.dtype)
    else:
        out_ref[...] = y


def _ssd_pass(xbc3, dt3, h0, tri, conv_w, conv_b, dt_bias, a_row, layer, n, row_off, *, reverse,
              fin=None):
    batch, seq, cw = xbc3.shape
    q = SSM_CHUNK
    nc = n // q
    off = row_off // q
    width = SSM_HEADS * SSM_HEAD_DIM

    def cidx(c):
        return off + ((nc - 1 - c) if reverse else c)

    hsz = BF16_SUBLANE
    per = q // hsz
    top = seq // hsz - 1
    per_layer = lambda b, c: (layer, 0, 0)
    in_specs = [
        pl.BlockSpec((None, q, cw), lambda b, c: (b, cidx(c), 0)),
        pl.BlockSpec((None, hsz, cw), lambda b, c: (b, jnp.maximum(cidx(c) * per - 1, 0), 0)),
        pl.BlockSpec((None, hsz, cw), lambda b, c: (b, jnp.minimum((cidx(c) + 1) * per, top), 0)),
        pl.BlockSpec((None, q, LANE), lambda b, c: (b, cidx(c), 0)),
        pl.BlockSpec((None,) + h0.shape[1:], lambda b, c: (b, 0, 0, 0)),
        pl.BlockSpec((q, q), lambda b, c: (0, 0)),
        pl.BlockSpec((None,) + conv_w.shape[1:], per_layer),
        pl.BlockSpec((None, 1, cw), per_layer),
        pl.BlockSpec((None, 1, LANE), per_layer),
        pl.BlockSpec((None, 1, LANE), per_layer),
    ]
    args = [xbc3, xbc3, xbc3, dt3, h0, tri, conv_w, conv_b, dt_bias, a_row]
    if fin is not None:
        z3, yf, d_row, nrm_row = fin
        in_specs += [
            pl.BlockSpec((None, q, width), lambda b, c: (b, cidx(c), 0)),
            pl.BlockSpec((None, q, width), lambda b, c: (b, cidx(c) - off, 0)),
            pl.BlockSpec((None, 1, width), per_layer),
            pl.BlockSpec((None, 1, width), per_layer),
        ]
        args += [z3, yf, d_row, nrm_row]
    out_dtype = BF16 if fin is not None else F32
    return pl.pallas_call(
        functools.partial(_ssd_kernel, reverse=reverse, final=fin is not None, nc=nc,
                          lane0=SSM_HEADS if reverse else 0),
        grid=(batch, nc),
        in_specs=in_specs,
        out_specs=[
            pl.BlockSpec((None, q, width), lambda b, c: (b, cidx(c) - off, 0)),
            pl.BlockSpec((None,) + h0.shape[1:], lambda b, c: (b, 0, 0, 0)),
        ],
        out_shape=[
            jax.ShapeDtypeStruct((batch, n, width), out_dtype),
            jax.ShapeDtypeStruct(h0.shape, F32),
        ],
        scratch_shapes=[pltpu.VMEM(h0.shape[1:], F32)],
        compiler_params=_cparams(("arbitrary", "arbitrary")),
        name="ssd_bwd_final" if reverse else "ssd_fwd",
    )(*args)


def _merge_kernel(x_ref, sc_ref, sh_ref, g1_ref, gn_ref, fl_ref, fc_ref, hl_ref, hc_ref, p_ref, sl_ref,
                  scx_ref, wg_ref, wb_ref, wo_ref, o_ref, *, tok, splits):
    ctx = tok.is_ctx(pl.program_id(0))
    x = x_ref[...]
    hb = _norm_mod(x, gn_ref[...], sc_ref[...], sh_ref[...]).astype(BF16)
    ys = (jnp.where(ctx, fc_ref[...], fl_ref[...]), jnp.where(ctx, hc_ref[...], hl_ref[...]),
          p_ref[...], jnp.where(ctx, scx_ref[...], sl_ref[...]))
    merged = None
    lo = 0
    for k, hi in enumerate(splits):
        gate = _sigmoid(_dot(hb, wg_ref[k]))
        term = gate * _dot(ys[k], wb_ref[lo:hi, :])
        merged = term if merged is None else merged + term
        lo = hi
    out = _dot(merged.astype(BF16), wo_ref[...])
    o_ref[...] = x + g1_ref[...] * out


def _merge(tok, layer, x_all, mod, norm_mix, f_lat, f_ctx, h_lat, h_ctx, pool, s_lat, s_ctx,
           w_gate, w_branch, w_out, splits):
    d = x_all.shape[-1]
    tm = TOKEN_TILE
    fw, hw, pw, sw = f_lat.shape[-1], h_lat.shape[-1], pool.shape[-1], s_lat.shape[-1]
    return pl.pallas_call(
        functools.partial(_merge_kernel, tok=tok, splits=splits),
        grid=(tok.n_tiles,),
        in_specs=[
            pl.BlockSpec((tm, d), lambda t: (t, 0)),
            tok.mod_spec(layer, 1, d),
            tok.mod_spec(layer, 0, d),
            tok.mod_spec(layer, 2, d),
            pl.BlockSpec((None, 1, d), lambda t: (layer, 0, 0)),
            tok.lat_spec(fw), tok.ctx_spec(fw),
            tok.lat_spec(hw), tok.ctx_spec(hw),
            pl.BlockSpec((tm, pw), lambda t: (t, 0)),
            tok.lat_spec(sw), tok.ctx_spec(sw),
            pl.BlockSpec((None,) + w_gate.shape[1:], lambda t: (layer, 0, 0, 0)),
            pl.BlockSpec((None,) + w_branch.shape[1:], lambda t: (layer, 0, 0)),
            pl.BlockSpec((None,) + w_out.shape[1:], lambda t: (layer, 0, 0)),
        ],
        out_specs=pl.BlockSpec((tm, d), lambda t: (t, 0)),
        out_shape=jax.ShapeDtypeStruct(x_all.shape, F32),
        compiler_params=_cparams(("arbitrary",)),
        name="merge",
    )(x_all, mod, mod, mod, norm_mix, f_lat, f_ctx, h_lat, h_ctx, pool, s_lat, s_ctx,
      w_gate, w_branch, w_out)


def _router_kernel(x_ref, sc_ref, sh_ref, gn_ref, rwh_ref, rwl_ref, rb_ref, utri_ref,
                   h_ref, e_ref, r_ref, w_ref, cnt_ref):
    t = pl.program_id(0)

    @pl.when(t == 0)
    def _():
        cnt_ref[...] = jnp.zeros_like(cnt_ref)

    h = _norm_mod(x_ref[...], gn_ref[...], sc_ref[...], sh_ref[...])
    hh = h.astype(BF16)
    h_ref[...] = hh
    hl = (h - hh.astype(F32)).astype(BF16)
    tm = h.shape[0]
    n_exp = rwh_ref.shape[0]
    nt = (((1,), (1,)), ((), ()))
    dg = lambda a, b: lax.dot_general(a, b, nt, preferred_element_type=F32)
    rwh = rwh_ref[...]
    logits = dg(rwh, hh) + dg(rwh, hl) + dg(rwl_ref[...], hh) + rb_ref[:, 0:1]
    eid = lax.broadcasted_iota(jnp.int32, (n_exp, tm), 0)
    carry = cnt_ref[:, 0:1]
    tops, sels = [], []
    cur = logits
    for _ in range(TOP_K):
        m = jnp.max(cur, axis=0, keepdims=True)
        idx = jnp.min(jnp.where(cur == m, eid, n_exp), axis=0, keepdims=True)
        sel = eid == idx
        tops.append((m, idx))
        sels.append(sel)
        cur = jnp.where(sel, -jnp.inf, cur)
    cnt = jnp.zeros((n_exp, tm), F32)
    for s in sels:
        cnt = jnp.where(s, 1.0, cnt)
    before = _dot(cnt.astype(BF16), utri_ref[...]) + carry
    exps = [jnp.exp(m - tops[0][0]) for m, _ in tops]
    den = exps[0]
    for v in exps[1:]:
        den = den + v
    row8 = lax.broadcasted_iota(jnp.int32, (8, tm), 0)
    rowl = lax.broadcasted_iota(jnp.int32, (LANE, tm), 0)
    e_out = jnp.zeros((8, tm), jnp.int32)
    r_out = jnp.zeros((8, tm), jnp.int32)
    w_t = jnp.zeros((LANE, tm), F32)
    for k in range(TOP_K):
        rank = jnp.sum(jnp.where(sels[k], before, 0.0), axis=0, keepdims=True)
        e_out = jnp.where(row8 == k, tops[k][1], e_out)
        r_out = jnp.where(row8 == k, rank.astype(jnp.int32), r_out)
        w_t = jnp.where(rowl == k, exps[k] / den, w_t)
    e_ref[...] = e_out
    r_ref[...] = r_out
    w_ref[...] = w_t.T
    cnt_ref[...] += jnp.broadcast_to(jnp.sum(cnt, axis=1, keepdims=True), cnt_ref.shape)


def _router(tok, layer, x_all, mod, norm_ffn, rw_hi, rw_lo, router_b_col, utri):
    d = x_all.shape[-1]
    tm = TOKEN_TILE
    n_tok = tok.n_tok
    n_exp = rw_hi.shape[1]
    row_i = jax.ShapeDtypeStruct((8, n_tok), jnp.int32)
    return pl.pallas_call(
        _router_kernel,
        grid=(tok.n_tiles,),
        in_specs=[
            pl.BlockSpec((tm, d), lambda t: (t, 0)),
            tok.mod_spec(layer, 4, d),
            tok.mod_spec(layer, 3, d),
            pl.BlockSpec((None, 1, d), lambda t: (layer, 0, 0)),
            pl.BlockSpec((None, n_exp, d), lambda t: (layer, 0, 0)),
            pl.BlockSpec((None, n_exp, d), lambda t: (layer, 0, 0)),
            pl.BlockSpec((None, n_exp, LANE), lambda t: (layer, 0, 0)),
            pl.BlockSpec((tm, tm), lambda t: (0, 0)),
        ],
        out_specs=[
            pl.BlockSpec((tm, d), lambda t: (t, 0)),
            pl.BlockSpec((8, tm), lambda t: (0, t)),
            pl.BlockSpec((8, tm), lambda t: (0, t)),
            pl.BlockSpec((tm, LANE), lambda t: (t, 0)),
            pl.BlockSpec((n_exp, LANE), lambda t: (0, 0)),
        ],
        out_shape=[
            jax.ShapeDtypeStruct((n_tok, d), BF16),
            row_i,
            row_i,
            jax.ShapeDtypeStruct((n_tok, LANE), F32),
            jax.ShapeDtypeStruct((n_exp, LANE), F32),
        ],
        compiler_params=_cparams(("arbitrary",)),
        name="router",
    )(x_all, mod, mod, norm_ffn, rw_hi, rw_lo, router_b_col, utri)


def _moe_kernel(be_ref, na_ref, x_ref, wu_ref, bu_ref, wd_ref, bd_ref, o_ref, wu_scr, wd_scr, *, ff):
    i = pl.program_id(0)
    active = i < na_ref[0]
    fresh = jnp.logical_or(i == 0, be_ref[i] != be_ref[jnp.maximum(i - 1, 0)])

    @pl.when(jnp.logical_and(active, fresh))
    def _():
        wu_scr[...] = wu_ref[...].astype(BF16)
        wd_scr[...] = wd_ref[...].astype(BF16)

    @pl.when(active)
    def _():
        hu = _dot(x_ref[...], wu_scr[...]) + bu_ref[...]
        glu = jnp.minimum(hu[:, :ff], SWIGLU_LIMIT)
        lin = jnp.clip(hu[:, ff:], -SWIGLU_LIMIT, SWIGLU_LIMIT)
        act = glu * _sigmoid(SWIGLU_ALPHA * glu) * (lin + 1.0)
        o_ref[...] = (_dot(act.astype(BF16), wd_scr[...]) + bd_ref[...]).astype(o_ref.dtype)

    @pl.when(jnp.logical_not(active))
    def _():
        o_ref[...] = jnp.zeros_like(o_ref)


def _moe(layer, x_sorted, blk_e, n_active, w_up, b_up, w_down, b_down):
    n_rows, d = x_sorted.shape
    ff = w_down.shape[2]
    bm = MOE_ROWS
    grid_spec = pltpu.PrefetchScalarGridSpec(
        num_scalar_prefetch=2,
        grid=(n_rows // bm,),
        in_specs=[
            pl.BlockSpec((bm, d), lambda i, be, na: (i, 0)),
            pl.BlockSpec((None, None, d, 2 * ff), lambda i, be, na: (layer, be[i], 0, 0)),
            pl.BlockSpec((None, None, 1, 2 * ff), lambda i, be, na: (layer, be[i], 0, 0)),
            pl.BlockSpec((None, None, ff, d), lambda i, be, na: (layer, be[i], 0, 0)),
            pl.BlockSpec((None, None, 1, d), lambda i, be, na: (layer, be[i], 0, 0)),
        ],
        out_specs=pl.BlockSpec((bm, d), lambda i, be, na: (i, 0)),
        scratch_shapes=[pltpu.VMEM((d, 2 * ff), BF16), pltpu.VMEM((ff, d), BF16)],
    )
    return pl.pallas_call(
        functools.partial(_moe_kernel, ff=ff),
        grid_spec=grid_spec,
        out_shape=jax.ShapeDtypeStruct((n_rows, d), BF16),
        compiler_params=_cparams(("arbitrary",)),
        name="moe_experts",
    )(blk_e, n_active, x_sorted, w_up, b_up, w_down, b_down)


def _final_kernel(x_ref, g2_ref, yg_ref, w_ref, g_ref, o_ref):
    x = _moe_residual(x_ref[...], g2_ref[...], yg_ref, w_ref)
    ms = jnp.mean(x * x, axis=-1, keepdims=True)
    o_ref[...] = (x * lax.rsqrt(ms + EPS)) * g_ref[...]


def _final(tok, layer, x_all, mod, y_gath, top_w, gain):
    batch, seq, n_lat = tok.batch, tok.seq, tok.n_lat
    d = x_all.shape[-1]
    tm = TOKEN_TILE
    return pl.pallas_call(
        _final_kernel,
        grid=(batch, n_lat // tm),
        in_specs=[
            pl.BlockSpec((None, tm, d), lambda b, i: (b, i, 0)),
            pl.BlockSpec((None, None, None, 1, d), lambda b, i: (layer, b, 5, 0, 0)),
            pl.BlockSpec((TOP_K, None, tm, d), lambda b, i: (0, b, i, 0)),
            pl.BlockSpec((None, tm, LANE), lambda b, i: (b, i, 0)),
            pl.BlockSpec((1, d), lambda b, i: (0, 0)),
        ],
        out_specs=pl.BlockSpec((None, tm, d), lambda b, i: (b, i, 0)),
        out_shape=jax.ShapeDtypeStruct((batch, n_lat, d), F32),
        compiler_params=_cparams(("arbitrary", "arbitrary")),
        name="final_combine_norm",
    )(x_all.reshape(batch, seq, d), mod, y_gath.reshape(TOP_K, batch, seq, d),
      top_w.reshape(batch, seq, LANE), gain)


def _dispatch_plan(top_e, rank, counts, n_tok):
    n_exp = counts.shape[0]
    bm = MOE_ROWS
    n_pairs = n_tok * TOP_K
    n_blk = -(-(n_pairs + n_exp * (bm - 1)) // bm)
    padded = (counts + bm - 1) // bm * bm
    pad_end = jnp.cumsum(padded)
    pad_start = pad_end - padded
    dest = rank
    for e in range(n_exp):
        dest = dest + jnp.where(top_e == e, pad_start[e], 0)
    n_active = (pad_end[-1] // bm).astype(jnp.int32).reshape(1)
    blk_start = jnp.arange(n_blk, dtype=jnp.int32) * bm
    blk_e = jnp.minimum(jnp.sum(pad_end[None, :] <= blk_start[:, None], axis=1), n_exp - 1).astype(jnp.int32)
    tok_ids = jnp.broadcast_to(jnp.arange(n_tok, dtype=jnp.int32)[None, :], dest.shape)
    row_tok = (jnp.arange(n_blk * bm, dtype=jnp.int32) % n_tok).at[dest.reshape(-1)].set(
        tok_ids.reshape(-1), unique_indices=True, mode='promise_in_bounds')
    return dest, row_tok, blk_e, n_active


def kernel(x, c, ctx, c_ctx, w_mod, b_mod, norm_mix, norm_ffn, w_in, hy_conv_w, hy_conv_b, hy_ffn_w1, hy_ffn_b1, hy_ffn_w2, hy_ffn_b2, hy_ffn_w3, hy_bias, pool_w, pool_scale, ssm_conv_w, ssm_conv_b, ssm_dt_bias, ssm_a_log, ssm_d, ssm_norm, w_branch, w_gate, w_out, router_w, router_b, exp_w_up, exp_b_up, exp_w_down, exp_b_down, norm_final):
    batch, n_lat, d = x.shape
    n_ctx = ctx.shape[1]
    depth = w_mod.shape[0]
    tok = _Tok(batch, n_lat, n_ctx)
    n_tok = tok.n_tok
    assert n_lat // GRID_W * GRID_W == n_lat and n_lat % n_ctx == 0

    hw = hy_bias.shape[1]
    pw = pool_scale.shape[1]
    sw = ssm_norm.shape[1]
    cw = ssm_conv_w.shape[2]
    fw = w_in.shape[2] - 3 * hw - pw - sw - cw - 2 * SSM_HEADS
    widths = (fw, 3 * hw, pw, sw, cw)
    splits = (fw, fw + hw, fw + hw + pw, fw + hw + pw + sw)
    n_exp = router_w.shape[2]

    w_in_p = jnp.pad(w_in, ((0, 0), (0, 0), (0, LANE - 2 * SSM_HEADS))).astype(BF16)
    w_gate_b = w_gate.astype(BF16)
    w_branch_b = w_branch.astype(BF16)
    w_out_b = w_out.astype(BF16)
    pool_w_bd = jax.vmap(_block_diag)(pool_w).astype(BF16)
    r3 = lambda a: a.reshape(a.shape[0], 1, a.shape[1])
    lane_pad = lambda a: jnp.pad(a, ((0, 0), (0, LANE - a.shape[1])))
    norm_mix3, norm_ffn3 = r3(norm_mix), r3(norm_ffn)
    hy_conv_b3, hy_bias3 = r3(hy_conv_b), r3(hy_bias)
    ssm_conv_b3 = r3(ssm_conv_b)
    pool_scale3 = r3(pool_scale)
    dt_bias3 = r3(lane_pad(ssm_dt_bias.reshape(depth, 2 * SSM_HEADS)))
    a_row3 = r3(lane_pad(-jnp.exp(ssm_a_log.astype(F32)).reshape(depth, 2 * SSM_HEADS)))
    d_row3 = r3(jnp.repeat(ssm_d, SSM_HEAD_DIM, axis=1))
    ssm_norm3 = r3(ssm_norm)
    rw_t = jnp.swapaxes(router_w, 1, 2)
    rw_hi = rw_t.astype(BF16)
    rw_lo = (rw_t - rw_hi.astype(F32)).astype(BF16)
    router_b_col = jnp.broadcast_to(router_b[:, :, None], (depth, n_exp, LANE))
    exp_b_up4 = exp_b_up.reshape(depth, n_exp, 1, -1)
    exp_b_down4 = exp_b_down.reshape(depth, n_exp, 1, -1)
    emb = hy_ffn_w1.shape[1]
    ffn = hy_ffn_w1.shape[2]
    f_w1 = jnp.pad(hy_ffn_w1, ((0, 0), (0, LANE - emb), (0, LANE - ffn)))
    f_b1 = r3(lane_pad(hy_ffn_b1))
    f_w2 = jnp.pad(hy_ffn_w2, ((0, 0), (0, LANE - ffn), (0, LANE - ffn)))
    f_b2 = r3(lane_pad(hy_ffn_b2))
    f_w3 = jnp.pad(hy_ffn_w3, ((0, 0), (0, LANE - ffn), (0, 0)))

    freqs = jnp.linspace(1e-4, HYENA_BANDS - 1, HYENA_BANDS, dtype=F32)
    freqs_row = jnp.concatenate([jnp.zeros((1,), F32), freqs, freqs,
                                 jnp.zeros((LANE - 1 - 2 * HYENA_BANDS,), F32)])[None, :]
    deltas_row = jnp.linspace(HYENA_MIN_DECAY, HYENA_MAX_DECAY, hw, dtype=F32)[None, :]
    gi = jnp.arange(FOURIER_GROUP_W)
    ang = (gi[:, None] * gi[None, :] % FOURIER_GROUP_W).astype(F32) * (2.0 * math.pi / FOURIER_GROUP_W)
    n_fg = fw // FOURIER_GROUP_W
    eye = jnp.eye(n_fg, dtype=F32)
    wc = jnp.concatenate([jnp.kron(eye, jnp.cos(ang)), jnp.kron(eye, jnp.sin(ang))], axis=1).astype(BF16)
    band, invc = _pool_tables(n_lat, n_ctx)
    qi = jnp.arange(SSM_CHUNK)
    tri_f = (qi[:, None] >= qi[None, :]).astype(BF16)
    tri_b = (qi[:, None] <= qi[None, :]).astype(BF16)
    ti = jnp.arange(TOKEN_TILE)
    utri = (ti[:, None] < ti[None, :]).astype(BF16)

    seqs = {}
    for name, n, row_off in (("ctx", n_ctx, n_lat), ("lat", n_lat, 0)):
        cn, sn = _trig_tables(n, n)
        fwd, inv = _hyena_dft_tables(n)
        g_all, nrm = _filter_mlp(n, freqs_row, deltas_row, f_w1, f_b1, f_w2, f_b2, f_w3)
        kr, ki = _filter_spectrum(fwd, g_all, nrm, n, hw)
        seqs[name] = dict(n=n, off=row_off, cn=cn.astype(BF16), sn=sn.astype(BF16), fwd=fwd, inv=inv,
                          kr=kr, ki=ki)

    cc = jnp.concatenate([c, c_ctx[None, :]], axis=0)
    rows = -(-cc.shape[0] // 8) * 8
    mod = _modulation(jnp.pad(cc, ((0, rows - cc.shape[0]), (0, 0))), w_mod, b_mod)
    mod = mod.reshape(depth, rows, 6, 1, d)

    x_all = jnp.concatenate([x, ctx], axis=1).reshape(n_tok, d)
    seq = tok.seq
    zero_state = jnp.zeros((batch, SSM_GROUPS, SSM_STATE, HEADS_PER_GROUP * SSM_HEAD_DIM), F32)

    moe = None
    for layer in range(depth):
        res = _inproj(tok, layer, x_all, mod, norm_mix3, w_in_p, wc, band, invc, pool_w_bd, pool_scale3,
                      widths, moe=moe)
        if moe is not None:
            x_all = res[0]
            res = res[1:]
        xcs, hy, pool, z, xbc, dtr = res
        xcs3 = xcs.reshape(batch, seq, 2 * fw)
        hy3 = hy.reshape(batch, seq, 3 * hw)
        z3 = z.reshape(batch, seq, sw)
        xbc3 = xbc.reshape(batch, seq, cw)
        dt3 = dtr.reshape(batch, seq, LANE)
        outs = {}
        states = (zero_state, zero_state)
        for name in ("ctx", "lat"):
            s = seqs[name]
            n, off = s["n"], s["off"]
            y_f = _fourier_seq(xcs3, s["cn"], s["sn"], n, off, fw)
            vv, x1 = _hy_pre(hy3, hy_conv_w, hy_conv_b3, layer, n, off, hw)
            y2 = _hy_fwd(s["fwd"], vv, s["kr"], s["ki"], layer, n, hw)
            y_h = _hy_inv(s["inv"], y2, x1, vv, hy_bias3, layer, n, hw)
            yf, st_f = _ssd_pass(xbc3, dt3, states[0], tri_f, ssm_conv_w, ssm_conv_b3, dt_bias3, a_row3,
                                 layer, n, off, reverse=False)
            y_s, st_b = _ssd_pass(xbc3, dt3, states[1], tri_b, ssm_conv_w, ssm_conv_b3, dt_bias3, a_row3,
                                  layer, n, off, reverse=True, fin=(z3, yf, d_row3, ssm_norm3))
            states = (st_f, st_b)
            outs[name] = (y_f, y_h, y_s)
        x_all = _merge(tok, layer, x_all, mod, norm_mix3, outs["lat"][0], outs["ctx"][0], outs["lat"][1],
                       outs["ctx"][1], pool, outs["lat"][2], outs["ctx"][2], w_gate_b, w_branch_b, w_out_b,
                       splits)
        h2, top_e, rank, top_w, counts = _router(tok, layer, x_all, mod, norm_ffn3, rw_hi, rw_lo, router_b_col,
                                                 utri)
        dest, row_tok, blk_e, n_active = _dispatch_plan(top_e[:TOP_K], rank[:TOP_K],
                                                        counts[:, 0].astype(jnp.int32), n_tok)
        x_sorted = h2[row_tok]
        y_rows = _moe(layer, x_sorted, blk_e, n_active, exp_w_up, exp_b_up4, exp_w_down, exp_b_down4)
        y_gath = y_rows[dest.reshape(-1)].reshape(TOP_K, n_tok, d)
        moe = (y_gath, top_w)

    return _final(tok, depth - 1, x_all, mod, moe[0], moe[1], norm_final[None, :])
```

```python
import functools
import math

import jax
import jax.numpy as jnp
from jax import lax
from jax.experimental import pallas as pl
from jax.experimental.pallas import tpu as pltpu

F32 = jnp.float32
BF16 = jnp.bfloat16

GRID_W = 64
EPS = 1e-6
FOURIER_GROUP_W = 64
HYENA_BANDS = 16
HYENA_MIN_DECAY = -math.log(1e-2) / 1.5
HYENA_MAX_DECAY = -math.log(1e-2) / 0.3
POOL_WINDOWS = (2, 4, 8, 16)
POOL_GROUP_W = 64
SSM_GROUPS = 2
HEADS_PER_GROUP = 4
SSM_HEADS = SSM_GROUPS * HEADS_PER_GROUP
SSM_HEAD_DIM = 64
SSM_STATE = 128
SSM_CHUNK = 128
TOP_K = 4
SWIGLU_LIMIT = 7.0
SWIGLU_ALPHA = 1.702

LANE = 128
BF16_SUBLANE = 16
TOKEN_TILE = 256
SEQ_TILE = 512
MOE_ROWS = 512
SAMPLE_GROUPS = 2
VMEM_LIMIT = 56 * 1024 * 1024


def _cparams(sem):
    return pltpu.CompilerParams(dimension_semantics=sem, vmem_limit_bytes=VMEM_LIMIT)


def _sigmoid(x):
    return 1.0 / (1.0 + jnp.exp(-x))


def _dot(a, b):
    return jnp.dot(a, b, preferred_element_type=F32)


def _dot_hi(a, b):
    return jnp.dot(a, b, preferred_element_type=F32, precision=lax.Precision.HIGHEST)


def _split3(x):
    x1 = x.astype(BF16)
    r1 = x - x1.astype(F32)
    x2 = r1.astype(BF16)
    x3 = (r1 - x2.astype(F32)).astype(BF16)
    return x1, x2, x3


def _dot_exact01(m01, x):
    x1, x2, x3 = _split3(x)
    return _dot(m01, x1) + _dot(m01, x2) + _dot(m01, x3)


def _mod_kernel(c_ref, w_ref, b_ref, o_ref):
    c = c_ref[...]
    s = c * _sigmoid(c)
    o_ref[...] = _dot_hi(s, w_ref[...]) + b_ref[...]


def _modulation(cc, w_mod, b_mod):
    depth, d, six_d = w_mod.shape
    rows = cc.shape[0]
    nj = six_d // d
    return pl.pallas_call(
        _mod_kernel,
        grid=(depth, nj),
        in_specs=[
            pl.BlockSpec((rows, d), lambda l, j: (0, 0)),
            pl.BlockSpec((None, d, d), lambda l, j: (l, 0, j)),
            pl.BlockSpec((None, 1, d), lambda l, j: (l, 0, j)),
        ],
        out_specs=pl.BlockSpec((None, rows, d), lambda l, j: (l, 0, j)),
        out_shape=jax.ShapeDtypeStruct((depth, rows, six_d), F32),
        compiler_params=_cparams(("arbitrary", "arbitrary")),
        name="modulation",
    )(cc, w_mod, b_mod.reshape(depth, 1, six_d))


class _Tok:
    def __init__(self, batch, n_lat, n_ctx, mod_off, mod_ctx):
        assert n_ctx == TOKEN_TILE and n_lat % TOKEN_TILE == 0
        self.batch, self.n_lat, self.n_ctx = batch, n_lat, n_ctx
        self.mod_off, self.mod_ctx = mod_off, mod_ctx
        self.seq = n_lat + n_ctx
        self.tiles_per_b = self.seq // TOKEN_TILE
        self.lat_tiles = n_lat // TOKEN_TILE
        self.n_tiles = batch * self.tiles_per_b
        self.n_tok = batch * self.seq

    def b(self, t):
        return t // self.tiles_per_b

    def w(self, t):
        return t % self.tiles_per_b

    def is_ctx(self, t):
        return (t % self.tiles_per_b) >= self.lat_tiles

    def mod_row(self, t):
        return jnp.where(self.is_ctx(t), self.mod_ctx, self.mod_off + self.b(t))

    def mod_spec(self, layer, part, d):
        return pl.BlockSpec((None, None, None, 1, d), lambda t: (layer, self.mod_row(t), part, 0, 0))

    def lat_spec(self, width):
        return pl.BlockSpec((None, TOKEN_TILE, width),
                            lambda t: (self.b(t), jnp.minimum(self.w(t), self.lat_tiles - 1), 0))

    def ctx_spec(self, width):
        return pl.BlockSpec((None, TOKEN_TILE, width), lambda t: (self.b(t), 0, 0))


def _norm_mod(x, gain, scale, shift):
    ms = jnp.mean(x * x, axis=-1, keepdims=True)
    return (x * lax.rsqrt(ms + EPS)) * gain * (1.0 + scale) + shift


def _moe_residual(x, g2, yg_ref, w_ref):
    w = w_ref[...]
    y = None
    for k in range(TOP_K):
        term = w[:, k:k + 1] * yg_ref[k].astype(F32)
        y = term if y is None else y + term
    return x + g2 * y


def _inproj_kernel(*refs, widths, combine):
    if combine:
        (x_ref, g2_ref, yg_ref, w_ref, sc_ref, sh_ref, g_ref, win_ref, wc_ref, band_ref, invc_ref, pw_ref,
         ps_ref, xn_ref, xcs_ref, hy_ref, pool_ref, z_ref, xbc_ref, dt_ref) = refs
        x = _moe_residual(x_ref[...], g2_ref[...], yg_ref, w_ref)
        xn_ref[...] = x
    else:
        (x_ref, sc_ref, sh_ref, g_ref, win_ref, wc_ref, band_ref, invc_ref, pw_ref, ps_ref,
         xcs_ref, hy_ref, pool_ref, z_ref, xbc_ref, dt_ref) = refs
        x = x_ref[...]
    fw, hw, pw, zw, cw = widths
    hb = _norm_mod(x, g_ref[...], sc_ref[...], sh_ref[...]).astype(BF16)
    o = 0
    f_in = _dot(hb, win_ref[:, o:o + fw])
    xcs_ref[...] = _dot(f_in.astype(BF16), wc_ref[...]).astype(BF16)
    o += fw
    hy_ref[...] = _dot(hb, win_ref[:, o:o + hw]).astype(BF16)
    o += hw
    u = _dot(hb, win_ref[:, o:o + pw])
    o += pw
    z_ref[...] = _dot(hb, win_ref[:, o:o + zw]).astype(BF16)
    o += zw
    xbc_ref[...] = _dot(hb, win_ref[:, o:o + cw]).astype(BF16)
    o += cw
    dt_ref[...] = _dot(hb, win_ref[:, o:o + LANE])
    gw = POOL_GROUP_W
    parts = []
    for gi in range(len(POOL_WINDOWS)):
        ug = u[:, gi * gw:(gi + 1) * gw]
        parts.append(_dot_exact01(band_ref[gi], ug))
    pooled = jnp.concatenate(parts, axis=1) * invc_ref[...] - u
    pool_ref[...] = (_dot(pooled.astype(BF16), pw_ref[...]) * ps_ref[...]).astype(BF16)


def _pool_tables(n_lat, n_ctx):
    tm = TOKEN_TILE
    p = jnp.arange(tm)
    bands, invs = [], []
    for row_len in (GRID_W, n_ctx):
        assert tm % row_len == 0
        pr, rr = p % row_len, p // row_len
        bg, ig = [], []
        for win in POOL_WINDOWS:
            lo = jnp.clip(pr - win // 2, 0, row_len)
            hi = jnp.clip(pr + win // 2, 0, row_len)
            q = pr[None, :]
            m = (rr[:, None] == rr[None, :]) & (q >= lo[:, None]) & (q < hi[:, None])
            bg.append(m.astype(BF16))
            ig.append(jnp.broadcast_to((1.0 / (hi - lo).astype(F32))[:, None], (tm, POOL_GROUP_W)))
        bands.append(jnp.stack(bg))
        invs.append(jnp.concatenate(ig, axis=1))
    return jnp.stack(bands), jnp.stack(invs)


def _block_diag(w):
    g, a, b = w.shape
    out = jnp.zeros((g * a, g * b), w.dtype)
    for i in range(g):
        out = out.at[i * a:(i + 1) * a, i * b:(i + 1) * b].set(w[i])
    return out


def _inproj(tok, layer, x_all, mod, norm_mix, w_in_p, wc, band, invc, pool_w_bd, pool_scale, widths,
            moe=None):
    d = x_all.shape[-1]
    fw, hw, pw, zw, cw = widths
    tm = TOKEN_TILE
    n_tok = tok.n_tok
    kind = lambda t: jnp.where(tok.is_ctx(t), 1, 0)
    const2 = lambda t: (0, 0)
    pre_specs, pre_args, pre_outs = [], [], []
    if moe is not None:
        pre_specs = [tok.mod_spec(layer - 1, 5, d),
                     pl.BlockSpec((TOP_K, tm, d), lambda t: (0, t, 0)),
                     pl.BlockSpec((tm, LANE), lambda t: (t, 0))]
        pre_args = [mod, moe[0], moe[1]]
        pre_outs = [jax.ShapeDtypeStruct((n_tok, d), F32)]
    outs = pre_outs + [
        jax.ShapeDtypeStruct((n_tok, 2 * fw), BF16),
        jax.ShapeDtypeStruct((n_tok, hw), BF16),
        jax.ShapeDtypeStruct((n_tok, pw), BF16),
        jax.ShapeDtypeStruct((n_tok, zw), BF16),
        jax.ShapeDtypeStruct((n_tok, cw), BF16),
        jax.ShapeDtypeStruct((n_tok, LANE), F32),
    ]
    return pl.pallas_call(
        functools.partial(_inproj_kernel, widths=widths, combine=moe is not None),
        grid=(tok.n_tiles,),
        in_specs=[pl.BlockSpec((tm, d), lambda t: (t, 0))] + pre_specs + [
            tok.mod_spec(layer, 1, d),
            tok.mod_spec(layer, 0, d),
            pl.BlockSpec((None, 1, d), lambda t: (layer, 0, 0)),
            pl.BlockSpec((None,) + w_in_p.shape[1:], lambda t: (layer, 0, 0)),
            pl.BlockSpec(wc.shape, const2),
            pl.BlockSpec((None,) + band.shape[1:], lambda t: (kind(t), 0, 0, 0)),
            pl.BlockSpec((None,) + invc.shape[1:], lambda t: (kind(t), 0, 0)),
            pl.BlockSpec((None,) + pool_w_bd.shape[1:], lambda t: (layer, 0, 0)),
            pl.BlockSpec((None, 1, pw), lambda t: (layer, 0, 0)),
        ],
        out_specs=[pl.BlockSpec((tm, s.shape[1]), lambda t: (t, 0)) for s in outs],
        out_shape=outs,
        compiler_params=_cparams(("arbitrary",)),
        name="inproj",
    )(x_all, *pre_args, mod, mod, norm_mix, w_in_p, wc, band, invc, pool_w_bd, pool_scale)


def _trig_tables(n, period):
    k = jnp.arange(n, dtype=jnp.int32)

    def cs(j):
        m = (j[:, None] * k[None, :]) % period
        ang = m.astype(F32) * (2.0 * math.pi / period)
        return jnp.cos(ang), jnp.sin(ang)

    split = 64
    if n % split or n <= split:
        return cs(k)
    c1, s1 = cs(jnp.arange(n // split, dtype=jnp.int32) * split)
    c2, s2 = cs(jnp.arange(split, dtype=jnp.int32))
    c = c1[:, None, :] * c2[None, :, :] - s1[:, None, :] * s2[None, :, :]
    s = s1[:, None, :] * c2[None, :, :] + c1[:, None, :] * s2[None, :, :]
    return c.reshape(n, n), s.reshape(n, n)


def _hyena_dft_tables(n):
    c, s = _trig_tables(n, 2 * n)
    nyq = jnp.where(jnp.arange(n) % 2 == 0, 1.0, -1.0).astype(F32)
    s = (-s).at[0].set(nyq)
    fwd = jnp.concatenate([c, s], axis=0).astype(BF16)
    return fwd, fwd.T


def _fourier_kernel(c_ref, s_ref, xc_ref, xs_ref, o_ref, acc_ref, *, scale):
    k = pl.program_id(1)

    @pl.when(k == 0)
    def _():
        acc_ref[...] = jnp.zeros_like(acc_ref)

    nb = xc_ref.shape[0]
    rc = jnp.concatenate([xc_ref[b] for b in range(nb)], axis=1)
    rs = jnp.concatenate([xs_ref[b] for b in range(nb)], axis=1)
    acc_ref[...] += _dot(c_ref[...], rc) - _dot(s_ref[...], rs)

    @pl.when(k == pl.num_programs(1) - 1)
    def _():
        w = o_ref.shape[2]
        for b in range(nb):
            o_ref[b] = (acc_ref[:, b * w:(b + 1) * w] * scale).astype(o_ref.dtype)


def _fourier_seq(xcs3, cn, sn, n, row_off, fw):
    batch = xcs3.shape[0]
    tl = min(SEQ_TILE, n)
    nt = n // tl
    off = row_off // tl
    scale = 1.0 / math.sqrt(n * FOURIER_GROUP_W)
    return pl.pallas_call(
        functools.partial(_fourier_kernel, scale=scale),
        grid=(nt, nt),
        in_specs=[
            pl.BlockSpec((tl, tl), lambda i, k: (i, k)),
            pl.BlockSpec((tl, tl), lambda i, k: (i, k)),
            pl.BlockSpec((batch, tl, fw), lambda i, k: (0, off + k, 0)),
            pl.BlockSpec((batch, tl, fw), lambda i, k: (0, off + k, 1)),
        ],
        out_specs=pl.BlockSpec((batch, tl, fw), lambda i, k: (0, i, 0)),
        out_shape=jax.ShapeDtypeStruct((batch, n, fw), BF16),
        scratch_shapes=[pltpu.VMEM((tl, batch * fw), F32)],
        compiler_params=_cparams(("arbitrary", "arbitrary")),
        name="fourier_seq",
    )(cn, sn, xcs3, xcs3)


def _shift_rows(x, prev_row, next_row):
    n = x.shape[0]
    rows = lax.broadcasted_iota(jnp.int32, x.shape, 0)
    xm = jnp.where(rows == 0, prev_row, pltpu.roll(x, 1, 0))
    xp = jnp.where(rows == n - 1, next_row, pltpu.roll(x, n - 1, 0))
    return xm, xp


def _conv3(x, prev_row, next_row, w_ref, b_ref):
    xm, xp = _shift_rows(x, prev_row, next_row)
    return w_ref[0:1, :] * xm + w_ref[1:2, :] * x + w_ref[2:3, :] * xp + b_ref[...]


def _hy_pre_kernel(x_ref, xp_ref, xn_ref, w_ref, b_ref, vv_ref, x1_ref, *, hw):
    i = pl.program_id(1)
    last = pl.num_programs(1) - 1
    x = x_ref[...].astype(F32)
    hs = xp_ref.shape[0]
    prev_row = jnp.where(i == 0, 0.0, xp_ref[hs - 1:hs, :].astype(F32))
    next_row = jnp.where(i == last, 0.0, xn_ref[0:1, :].astype(F32))
    u = _conv3(x, prev_row, next_row, w_ref, b_ref)
    x1_ref[...] = u[:, :hw].astype(BF16)
    vv_ref[...] = (u[:, hw:2 * hw] * u[:, 2 * hw:]).astype(BF16)


def _halo_specs(width, tile, row_off, seq_rows):
    hs = BF16_SUBLANE
    per = tile // hs
    base = row_off // hs
    top = seq_rows // hs - 1
    prev = pl.BlockSpec((None, hs, width), lambda b, i: (b, jnp.maximum(base + i * per - 1, 0), 0))
    nxt = pl.BlockSpec((None, hs, width), lambda b, i: (b, jnp.minimum(base + (i + 1) * per, top), 0))
    return prev, nxt


def _hy_pre(hy3, conv_w, conv_b, layer, n, row_off, hw):
    batch, seq, w3 = hy3.shape
    tl = min(SEQ_TILE, n)
    off = row_off // tl
    prev, nxt = _halo_specs(w3, tl, row_off, seq)
    out = jax.ShapeDtypeStruct((batch, n, hw), BF16)
    return pl.pallas_call(
        functools.partial(_hy_pre_kernel, hw=hw),
        grid=(batch, n // tl),
        in_specs=[
            pl.BlockSpec((None, tl, w3), lambda b, i: (b, off + i, 0)),
            prev, nxt,
            pl.BlockSpec((None,) + conv_w.shape[1:], lambda b, i: (layer, 0, 0)),
            pl.BlockSpec((None, 1, w3), lambda b, i: (layer, 0, 0)),
        ],
        out_specs=[pl.BlockSpec((None, tl, hw), lambda b, i: (b, i, 0))] * 2,
        out_shape=[out, out],
        compiler_params=_cparams(("arbitrary", "arbitrary")),
        name="hyena_pre",
    )(hy3, hy3, hy3, conv_w, conv_b)


def _filter_mlp_kernel(fr_ref, dl_ref, w1_ref, b1_ref, w2_ref, b2_ref, w3_ref, g_ref, nrm_ref, *, n, hw):
    j = pl.program_id(1)
    rt = g_ref.shape[0]
    lane = lax.broadcasted_iota(jnp.int32, (rt, LANE), 1)
    pos = (lax.broadcasted_iota(jnp.int32, (rt, LANE), 0) + j * rt).astype(F32)
    t = pos / (n - 1)
    ang = 2.0 * math.pi * pos / n
    arg = ang * fr_ref[...]
    feats = jnp.where(lane == 0, t,
                      jnp.where(lane <= HYENA_BANDS, jnp.cos(arg),
                                jnp.where(lane <= 2 * HYENA_BANDS, -jnp.sin(arg), 0.0)))
    h1 = jnp.sin(_dot_hi(feats, w1_ref[...]) + b1_ref[...])
    h2 = jnp.sin(_dot_hi(h1, w2_ref[...]) + b2_ref[...])
    k = _dot_hi(h2, w3_ref[...])
    decay = jnp.exp(-t[:, 0:1] * dl_ref[...])
    kf = k[:, :hw] * decay
    kb = jnp.where(pos[:, 0:1] == 0.0, 0.0, k[:, hw:] * decay)
    g_ref[:, :hw] = (kf + kb).astype(BF16)
    g_ref[:, hw:] = (kf - kb).astype(BF16)
    part = jnp.sum(jnp.abs(kf) + jnp.abs(kb), axis=0, keepdims=True)

    @pl.when(j == 0)
    def _():
        nrm_ref[...] = part

    @pl.when(j > 0)
    def _():
        nrm_ref[...] += part


def _filter_mlp(n, freqs_row, deltas_row, w1, b1, w2, b2, w3):
    depth = w1.shape[0]
    hw = deltas_row.shape[1]
    rt = min(SEQ_TILE, n)
    per_layer = lambda l, j: (l, 0, 0)
    return pl.pallas_call(
        functools.partial(_filter_mlp_kernel, n=n, hw=hw),
        grid=(depth, n // rt),
        in_specs=[
            pl.BlockSpec((1, LANE), lambda l, j: (0, 0)),
            pl.BlockSpec((1, hw), lambda l, j: (0, 0)),
            pl.BlockSpec((None,) + w1.shape[1:], per_layer),
            pl.BlockSpec((None,) + b1.shape[1:], per_layer),
            pl.BlockSpec((None,) + w2.shape[1:], per_layer),
            pl.BlockSpec((None,) + b2.shape[1:], per_layer),
            pl.BlockSpec((None,) + w3.shape[1:], per_layer),
        ],
        out_specs=[
            pl.BlockSpec((rt, 2 * hw), lambda l, j: (j, l)),
            pl.BlockSpec((None, 1, hw), lambda l, j: (l, 0, 0)),
        ],
        out_shape=[
            jax.ShapeDtypeStruct((n, depth * 2 * hw), BF16),
            jax.ShapeDtypeStruct((depth, 1, hw), F32),
        ],
        compiler_params=_cparams(("arbitrary", "arbitrary")),
        name="hyena_filter_mlp",
    )(freqs_row, deltas_row, w1, b1, w2, b2, w3)


def _rdft_accumulate(fc_ref, fs_ref, rhs, accc_ref, accs_ref):
    k = pl.program_id(1)

    @pl.when(k == 0)
    def _():
        accc_ref[...] = jnp.zeros_like(accc_ref)
        accs_ref[...] = jnp.zeros_like(accs_ref)

    accc_ref[...] += _dot(fc_ref[...], rhs)
    accs_ref[...] += _dot(fs_ref[...], rhs)


def _filter_spec_kernel(fc_ref, fs_ref, g_ref, nrm_ref, kr_ref, ki_ref, accc_ref, accs_ref, *, hw):
    i = pl.program_id(0)
    _rdft_accumulate(fc_ref, fs_ref, g_ref[...], accc_ref, accs_ref)

    @pl.when(pl.program_id(1) == pl.num_programs(1) - 1)
    def _():
        tm = accc_ref.shape[0]
        row0 = (lax.broadcasted_iota(jnp.int32, (tm, hw), 0) + i * tm) == 0
        for l in range(kr_ref.shape[0]):
            inv = 1.0 / nrm_ref[l]
            o = l * 2 * hw
            kr_ref[l] = accc_ref[:, o:o + hw] * inv
            ki_ref[l] = jnp.where(row0, accs_ref[:, o:o + hw], accs_ref[:, o + hw:o + 2 * hw]) * inv


def _filter_spectrum(fwd, g_all, nrm, n, hw):
    depth = nrm.shape[0]
    tl = min(SEQ_TILE, n)
    nt = n // tl
    width = g_all.shape[1]
    out = jax.ShapeDtypeStruct((depth, n, hw), F32)
    return pl.pallas_call(
        functools.partial(_filter_spec_kernel, hw=hw),
        grid=(nt, nt),
        in_specs=[
            pl.BlockSpec((tl, tl), lambda i, k: (i, k)),
            pl.BlockSpec((tl, tl), lambda i, k: (nt + i, k)),
            pl.BlockSpec((tl, width), lambda i, k: (k, 0)),
            pl.BlockSpec(nrm.shape, lambda i, k: (0, 0, 0)),
        ],
        out_specs=[pl.BlockSpec((depth, tl, hw), lambda i, k: (0, i, 0))] * 2,
        out_shape=[out, out],
        scratch_shapes=[pltpu.VMEM((tl, width), F32)] * 2,
        compiler_params=_cparams(("arbitrary", "arbitrary")),
        name="hyena_filter_spectrum",
    )(fwd, fwd, g_all, nrm)


def _hy_fwd_kernel(fc_ref, fs_ref, vv_ref, kr_ref, ki_ref, y_ref, accc_ref, accs_ref):
    i = pl.program_id(0)
    nb = vv_ref.shape[0]
    rhs = jnp.concatenate([vv_ref[b] for b in range(nb)], axis=1)
    _rdft_accumulate(fc_ref, fs_ref, rhs, accc_ref, accs_ref)

    @pl.when(pl.program_id(1) == pl.num_programs(1) - 1)
    def _():
        tm, hw = kr_ref.shape
        row0 = (lax.broadcasted_iota(jnp.int32, (tm, hw), 0) + i * tm) == 0
        kr, ki = kr_ref[...], ki_ref[...]
        for b in range(nb):
            vr = accc_ref[:, b * hw:(b + 1) * hw]
            vi = accs_ref[:, b * hw:(b + 1) * hw]
            yr = jnp.where(row0, 0.5 * vr * kr, vr * kr - vi * ki)
            yi = jnp.where(row0, 0.5 * vi * ki, vr * ki + vi * kr)
            y_ref[0, :, b * hw:(b + 1) * hw] = yr.astype(BF16)
            y_ref[1, :, b * hw:(b + 1) * hw] = yi.astype(BF16)


def _hy_fwd(fwd, vv, kr, ki, layer, n, hw):
    batch = vv.shape[0]
    tl = min(SEQ_TILE, n)
    nt = n // tl
    return pl.pallas_call(
        _hy_fwd_kernel,
        grid=(nt, nt),
        in_specs=[
            pl.BlockSpec((tl, tl), lambda i, k: (i, k)),
            pl.BlockSpec((tl, tl), lambda i, k: (nt + i, k)),
            pl.BlockSpec((batch, tl, hw), lambda i, k: (0, k, 0)),
            pl.BlockSpec((None, tl, hw), lambda i, k: (layer, i, 0)),
            pl.BlockSpec((None, tl, hw), lambda i, k: (layer, i, 0)),
        ],
        out_specs=pl.BlockSpec((2, tl, batch * hw), lambda i, k: (0, i, 0)),
        out_shape=jax.ShapeDtypeStruct((2, n, batch * hw), BF16),
        scratch_shapes=[pltpu.VMEM((tl, batch * hw), F32)] * 2,
        compiler_params=_cparams(("arbitrary", "arbitrary")),
        name="hyena_fwd_dft",
    )(fwd, fwd, vv, kr, ki)


def _hy_inv_kernel(fi_ref, y_ref, x1_ref, vv_ref, bias_ref, o_ref, acc_ref, *, scale):
    k = pl.program_id(1)

    @pl.when(k == 0)
    def _():
        acc_ref[...] = jnp.zeros_like(acc_ref)

    acc_ref[...] += _dot(fi_ref[...], y_ref[...])

    @pl.when(k == pl.num_programs(1) - 1)
    def _():
        nb, _, hw = x1_ref.shape
        for b in range(nb):
            conv = acc_ref[:, b * hw:(b + 1) * hw] * scale
            vv = vv_ref[b].astype(F32)
            o_ref[b] = (x1_ref[b].astype(F32) * (conv + vv * bias_ref[...])).astype(BF16)


def _hy_inv(inv, y2, x1, vv, hy_bias, layer, n, hw):
    batch = vv.shape[0]
    tl = min(SEQ_TILE, n)
    nt = n // tl
    return pl.pallas_call(
        functools.partial(_hy_inv_kernel, scale=1.0 / n),
        grid=(nt, 2 * nt),
        in_specs=[
            pl.BlockSpec((tl, tl), lambda i, k: (i, k)),
            pl.BlockSpec((tl, batch * hw), lambda i, k: (k, 0)),
            pl.BlockSpec((batch, tl, hw), lambda i, k: (0, i, 0)),
            pl.BlockSpec((batch, tl, hw), lambda i, k: (0, i, 0)),
            pl.BlockSpec((None, 1, hw), lambda i, k: (layer, 0, 0)),
        ],
        out_specs=pl.BlockSpec((batch, tl, hw), lambda i, k: (0, i, 0)),
        out_shape=jax.ShapeDtypeStruct((batch, n, hw), BF16),
        scratch_shapes=[pltpu.VMEM((tl, batch * hw), F32)],
        compiler_params=_cparams(("arbitrary", "arbitrary")),
        name="hyena_inv_dft",
    )(inv, y2.reshape(2 * n, batch * hw), x1, vv, hy_bias)


def _softplus(x):
    return jnp.maximum(x, 0.0) + jnp.log1p(jnp.exp(-jnp.abs(x)))


def _ssd_kernel(*refs, reverse, final, nc, lane0):
    if final:
        (xbc_ref, xp_ref, xn_ref, dt_ref, h0_ref, tri_ref, cw_ref, cb_ref, dtb_ref, a_ref,
         z_ref, yf_ref, d_ref, nrm_ref, out_ref, hout_ref, h_scr) = refs
    else:
        (xbc_ref, xp_ref, xn_ref, dt_ref, h0_ref, tri_ref, cw_ref, cb_ref, dtb_ref, a_ref,
         out_ref, hout_ref, h_scr) = refs
    step = pl.program_id(1)
    cc = (nc - 1 - step) if reverse else step

    @pl.when(step == 0)
    def _():
        h_scr[...] = h0_ref[...]

    q = xbc_ref.shape[0]
    hs = xp_ref.shape[0]
    x = xbc_ref[...].astype(F32)
    prev_row = jnp.where(cc == 0, 0.0, xp_ref[hs - 1:hs, :].astype(F32))
    next_row = jnp.where(cc == nc - 1, 0.0, xn_ref[0:1, :].astype(F32))
    u = _conv3(x, prev_row, next_row, cw_ref, cb_ref)
    u = u * _sigmoid(u)
    width = SSM_HEADS * SSM_HEAD_DIM
    gs = SSM_STATE
    xs = u[:, :width]
    bm = u[:, width:width + SSM_GROUPS * gs]
    cm = u[:, width + SSM_GROUPS * gs:]

    dt = _softplus(dt_ref[...] + dtb_ref[...])
    acs = _dot_exact01(tri_ref[...], dt * a_ref[...])
    acs_t = acs.T
    tot = acs[0:1, :] if reverse else acs[q - 1:q, :]
    to_end = jnp.exp(tot - acs)
    frm = jnp.exp(acs)
    cdec = jnp.exp(tot)
    ri = lax.broadcasted_iota(jnp.int32, (q, q), 0)
    ci = lax.broadcasted_iota(jnp.int32, (q, q), 1)
    mask = (ri <= ci) if reverse else (ri >= ci)

    pd = SSM_HEAD_DIM
    ys = []
    for g in range(SSM_GROUPS):
        bg = bm[:, g * gs:(g + 1) * gs]
        cg = cm[:, g * gs:(g + 1) * gs].astype(BF16)
        scores = lax.dot_general(cg, bg.astype(BF16), (((1,), (1,)), ((), ())), preferred_element_type=F32)
        hg = h_scr[g]
        yoff = _dot(cg, hg.astype(BF16))
        wparts, decs = [], []
        for r in range(HEADS_PER_GROUP):
            hd = g * HEADS_PER_GROUP + r
            li = lane0 + hd
            lm = jnp.where(mask, jnp.exp(acs[:, li:li + 1] - acs_t[li:li + 1, :]), 0.0)
            m = (scores * lm).astype(BF16)
            xdt = xs[:, hd * pd:(hd + 1) * pd] * dt[:, li:li + 1]
            yd = _dot(m, xdt.astype(BF16))
            ys.append(yd + yoff[:, r * pd:(r + 1) * pd] * frm[:, li:li + 1])
            wparts.append(xdt * to_end[:, li:li + 1])
            decs.append(jnp.broadcast_to(cdec[:, li:li + 1], (1, pd)))
        wg = jnp.concatenate(wparts, axis=1).astype(BF16)
        st = _dot(bg.T.astype(BF16), wg)
        h_scr[g] = hg * jnp.concatenate(decs, axis=1) + st
    y = jnp.concatenate(ys, axis=1)

    @pl.when(step == nc - 1)
    def _():
        hout_ref[...] = h_scr[...]

    if final:
        y = y + yf_ref[...] + xs * d_ref[...]
        z = z_ref[...].astype(F32)
        y = y * (z * _sigmoid(z))
        ms = jnp.mean(y * y, axis=-1, keepdims=True)
        out_ref[...] = ((y * lax.rsqrt(ms + EPS)) * nrm_ref[...]).astype(out_ref.dtype)
    else:
        out_ref[...] = y


def _ssd_pass(xbc3, dt3, h0, tri, conv_w, conv_b, dt_bias, a_row, layer, n, row_off, *, reverse,
              fin=None):
    batch, seq, cw = xbc3.shape
    q = SSM_CHUNK
    nc = n // q
    off = row_off // q
    width = SSM_HEADS * SSM_HEAD_DIM

    def cidx(c):
        return off + ((nc - 1 - c) if reverse else c)

    hsz = BF16_SUBLANE
    per = q // hsz
    top = seq // hsz - 1
    per_layer = lambda b, c: (layer, 0, 0)
    in_specs = [
        pl.BlockSpec((None, q, cw), lambda b, c: (b, cidx(c), 0)),
        pl.BlockSpec((None, hsz, cw), lambda b, c: (b, jnp.maximum(cidx(c) * per - 1, 0), 0)),
        pl.BlockSpec((None, hsz, cw), lambda b, c: (b, jnp.minimum((cidx(c) + 1) * per, top), 0)),
        pl.BlockSpec((None, q, LANE), lambda b, c: (b, cidx(c), 0)),
        pl.BlockSpec((None,) + h0.shape[1:], lambda b, c: (b, 0, 0, 0)),
        pl.BlockSpec((q, q), lambda b, c: (0, 0)),
        pl.BlockSpec((None,) + conv_w.shape[1:], per_layer),
        pl.BlockSpec((None, 1, cw), per_layer),
        pl.BlockSpec((None, 1, LANE), per_layer),
        pl.BlockSpec((None, 1, LANE), per_layer),
    ]
    args = [xbc3, xbc3, xbc3, dt3, h0, tri, conv_w, conv_b, dt_bias, a_row]
    if fin is not None:
        z3, yf, d_row, nrm_row = fin
        in_specs += [
            pl.BlockSpec((None, q, width), lambda b, c: (b, cidx(c), 0)),
            pl.BlockSpec((None, q, width), lambda b, c: (b, cidx(c) - off, 0)),
            pl.BlockSpec((None, 1, width), per_layer),
            pl.BlockSpec((None, 1, width), per_layer),
        ]
        args += [z3, yf, d_row, nrm_row]
    out_dtype = BF16 if fin is not None else F32
    return pl.pallas_call(
        functools.partial(_ssd_kernel, reverse=reverse, final=fin is not None, nc=nc,
                          lane0=SSM_HEADS if reverse else 0),
        grid=(batch, nc),
        in_specs=in_specs,
        out_specs=[
            pl.BlockSpec((None, q, width), lambda b, c: (b, cidx(c) - off, 0)),
            pl.BlockSpec((None,) + h0.shape[1:], lambda b, c: (b, 0, 0, 0)),
        ],
        out_shape=[
            jax.ShapeDtypeStruct((batch, n, width), out_dtype),
            jax.ShapeDtypeStruct(h0.shape, F32),
        ],
        scratch_shapes=[pltpu.VMEM(h0.shape[1:], F32)],
        compiler_params=_cparams(("arbitrary", "arbitrary")),
        name="ssd_bwd_final" if reverse else "ssd_fwd",
    )(*args)


def _merge_kernel(x_ref, sc_ref, sh_ref, g1_ref, gn_ref, fl_ref, fc_ref, hl_ref, hc_ref, p_ref, sl_ref,
                  scx_ref, wg_ref, wb_ref, wo_ref, o_ref, *, tok, splits):
    ctx = tok.is_ctx(pl.program_id(0))
    x = x_ref[...]
    hb = _norm_mod(x, gn_ref[...], sc_ref[...], sh_ref[...]).astype(BF16)
    ys = (jnp.where(ctx, fc_ref[...], fl_ref[...]), jnp.where(ctx, hc_ref[...], hl_ref[...]),
          p_ref[...], jnp.where(ctx, scx_ref[...], sl_ref[...]))
    merged = None
    lo = 0
    for k, hi in enumerate(splits):
        gate = _sigmoid(_dot(hb, wg_ref[k]))
        term = gate * _dot(ys[k], wb_ref[lo:hi, :])
        merged = term if merged is None else merged + term
        lo = hi
    out = _dot(merged.astype(BF16), wo_ref[...])
    o_ref[...] = x + g1_ref[...] * out


def _merge(tok, layer, x_all, mod, norm_mix, f_lat, f_ctx, h_lat, h_ctx, pool, s_lat, s_ctx,
           w_gate, w_branch, w_out, splits):
    d = x_all.shape[-1]
    tm = TOKEN_TILE
    fw, hw, pw, sw = f_lat.shape[-1], h_lat.shape[-1], pool.shape[-1], s_lat.shape[-1]
    return pl.pallas_call(
        functools.partial(_merge_kernel, tok=tok, splits=splits),
        grid=(tok.n_tiles,),
        in_specs=[
            pl.BlockSpec((tm, d), lambda t: (t, 0)),
            tok.mod_spec(layer, 1, d),
            tok.mod_spec(layer, 0, d),
            tok.mod_spec(layer, 2, d),
            pl.BlockSpec((None, 1, d), lambda t: (layer, 0, 0)),
            tok.lat_spec(fw), tok.ctx_spec(fw),
            tok.lat_spec(hw), tok.ctx_spec(hw),
            pl.BlockSpec((tm, pw), lambda t: (t, 0)),
            tok.lat_spec(sw), tok.ctx_spec(sw),
            pl.BlockSpec((None,) + w_gate.shape[1:], lambda t: (layer, 0, 0, 0)),
            pl.BlockSpec((None,) + w_branch.shape[1:], lambda t: (layer, 0, 0)),
            pl.BlockSpec((None,) + w_out.shape[1:], lambda t: (layer, 0, 0)),
        ],
        out_specs=pl.BlockSpec((tm, d), lambda t: (t, 0)),
        out_shape=jax.ShapeDtypeStruct(x_all.shape, F32),
        compiler_params=_cparams(("arbitrary",)),
        name="merge",
    )(x_all, mod, mod, mod, norm_mix, f_lat, f_ctx, h_lat, h_ctx, pool, s_lat, s_ctx,
      w_gate, w_branch, w_out)


def _router_kernel(x_ref, sc_ref, sh_ref, gn_ref, rwh_ref, rwl_ref, rb_ref, utri_ref,
                   h_ref, e_ref, r_ref, w_ref, cnt_ref):
    t = pl.program_id(0)

    @pl.when(t == 0)
    def _():
        cnt_ref[...] = jnp.zeros_like(cnt_ref)

    h = _norm_mod(x_ref[...], gn_ref[...], sc_ref[...], sh_ref[...])
    hh = h.astype(BF16)
    h_ref[...] = hh
    hl = (h - hh.astype(F32)).astype(BF16)
    tm = h.shape[0]
    n_exp = rwh_ref.shape[0]
    nt = (((1,), (1,)), ((), ()))
    dg = lambda a, b: lax.dot_general(a, b, nt, preferred_element_type=F32)
    rwh = rwh_ref[...]
    logits = dg(rwh, hh) + dg(rwh, hl) + dg(rwl_ref[...], hh) + rb_ref[:, 0:1]
    eid = lax.broadcasted_iota(jnp.int32, (n_exp, tm), 0)
    carry = cnt_ref[:, 0:1]
    tops, sels = [], []
    cur = logits
    for _ in range(TOP_K):
        m = jnp.max(cur, axis=0, keepdims=True)
        idx = jnp.min(jnp.where(cur == m, eid, n_exp), axis=0, keepdims=True)
        sel = eid == idx
        tops.append((m, idx))
        sels.append(sel)
        cur = jnp.where(sel, -jnp.inf, cur)
    cnt = jnp.zeros((n_exp, tm), F32)
    for s in sels:
        cnt = jnp.where(s, 1.0, cnt)
    before = _dot(cnt.astype(BF16), utri_ref[...]) + carry
    exps = [jnp.exp(m - tops[0][0]) for m, _ in tops]
    den = exps[0]
    for v in exps[1:]:
        den = den + v
    row8 = lax.broadcasted_iota(jnp.int32, (8, tm), 0)
    rowl = lax.broadcasted_iota(jnp.int32, (LANE, tm), 0)
    e_out = jnp.zeros((8, tm), jnp.int32)
    r_out = jnp.zeros((8, tm), jnp.int32)
    w_t = jnp.zeros((LANE, tm), F32)
    for k in range(TOP_K):
        rank = jnp.sum(jnp.where(sels[k], before, 0.0), axis=0, keepdims=True)
        e_out = jnp.where(row8 == k, tops[k][1], e_out)
        r_out = jnp.where(row8 == k, rank.astype(jnp.int32), r_out)
        w_t = jnp.where(rowl == k, exps[k] / den, w_t)
    e_ref[...] = e_out
    r_ref[...] = r_out
    w_ref[...] = w_t.T
    cnt_ref[...] += jnp.broadcast_to(jnp.sum(cnt, axis=1, keepdims=True), cnt_ref.shape)


def _router(tok, layer, x_all, mod, norm_ffn, rw_hi, rw_lo, router_b_col, utri):
    d = x_all.shape[-1]
    tm = TOKEN_TILE
    n_tok = tok.n_tok
    n_exp = rw_hi.shape[1]
    row_i = jax.ShapeDtypeStruct((8, n_tok), jnp.int32)
    return pl.pallas_call(
        _router_kernel,
        grid=(tok.n_tiles,),
        in_specs=[
            pl.BlockSpec((tm, d), lambda t: (t, 0)),
            tok.mod_spec(layer, 4, d),
            tok.mod_spec(layer, 3, d),
            pl.BlockSpec((None, 1, d), lambda t: (layer, 0, 0)),
            pl.BlockSpec((None, n_exp, d), lambda t: (layer, 0, 0)),
            pl.BlockSpec((None, n_exp, d), lambda t: (layer, 0, 0)),
            pl.BlockSpec((None, n_exp, LANE), lambda t: (layer, 0, 0)),
            pl.BlockSpec((tm, tm), lambda t: (0, 0)),
        ],
        out_specs=[
            pl.BlockSpec((tm, d), lambda t: (t, 0)),
            pl.BlockSpec((8, tm), lambda t: (0, t)),
            pl.BlockSpec((8, tm), lambda t: (0, t)),
            pl.BlockSpec((tm, LANE), lambda t: (t, 0)),
            pl.BlockSpec((n_exp, LANE), lambda t: (0, 0)),
        ],
        out_shape=[
            jax.ShapeDtypeStruct((n_tok, d), BF16),
            row_i,
            row_i,
            jax.ShapeDtypeStruct((n_tok, LANE), F32),
            jax.ShapeDtypeStruct((n_exp, LANE), F32),
        ],
        compiler_params=_cparams(("arbitrary",)),
        name="router",
    )(x_all, mod, mod, norm_ffn, rw_hi, rw_lo, router_b_col, utri)


def _moe_kernel(be_ref, na_ref, x_ref, wu_ref, bu_ref, wd_ref, bd_ref, o_ref, wu_scr, wd_scr, *, ff):
    i = pl.program_id(0)
    active = i < na_ref[0]
    fresh = jnp.logical_or(i == 0, be_ref[i] != be_ref[jnp.maximum(i - 1, 0)])

    @pl.when(jnp.logical_and(active, fresh))
    def _():
        wu_scr[...] = wu_ref[...].astype(BF16)
        wd_scr[...] = wd_ref[...].astype(BF16)

    @pl.when(active)
    def _():
        hu = _dot(x_ref[...], wu_scr[...]) + bu_ref[...]
        glu = jnp.minimum(hu[:, :ff], SWIGLU_LIMIT)
        lin = jnp.clip(hu[:, ff:], -SWIGLU_LIMIT, SWIGLU_LIMIT)
        act = glu * _sigmoid(SWIGLU_ALPHA * glu) * (lin + 1.0)
        o_ref[...] = (_dot(act.astype(BF16), wd_scr[...]) + bd_ref[...]).astype(o_ref.dtype)

    @pl.when(jnp.logical_not(active))
    def _():
        o_ref[...] = jnp.zeros_like(o_ref)


def _moe(layer, x_sorted, blk_e, n_active, w_up, b_up, w_down, b_down):
    n_rows, d = x_sorted.shape
    ff = w_down.shape[2]
    bm = MOE_ROWS
    grid_spec = pltpu.PrefetchScalarGridSpec(
        num_scalar_prefetch=2,
        grid=(n_rows // bm,),
        in_specs=[
            pl.BlockSpec((bm, d), lambda i, be, na: (i, 0)),
            pl.BlockSpec((None, None, d, 2 * ff), lambda i, be, na: (layer, be[i], 0, 0)),
            pl.BlockSpec((None, None, 1, 2 * ff), lambda i, be, na: (layer, be[i], 0, 0)),
            pl.BlockSpec((None, None, ff, d), lambda i, be, na: (layer, be[i], 0, 0)),
            pl.BlockSpec((None, None, 1, d), lambda i, be, na: (layer, be[i], 0, 0)),
        ],
        out_specs=pl.BlockSpec((bm, d), lambda i, be, na: (i, 0)),
        scratch_shapes=[pltpu.VMEM((d, 2 * ff), BF16), pltpu.VMEM((ff, d), BF16)],
    )
    return pl.pallas_call(
        functools.partial(_moe_kernel, ff=ff),
        grid_spec=grid_spec,
        out_shape=jax.ShapeDtypeStruct((n_rows, d), BF16),
        compiler_params=_cparams(("arbitrary",)),
        name="moe_experts",
    )(blk_e, n_active, x_sorted, w_up, b_up, w_down, b_down)


def _final_kernel(x_ref, g2_ref, yg_ref, w_ref, g_ref, o_ref):
    x = _moe_residual(x_ref[...], g2_ref[...], yg_ref, w_ref)
    ms = jnp.mean(x * x, axis=-1, keepdims=True)
    o_ref[...] = (x * lax.rsqrt(ms + EPS)) * g_ref[...]


def _final(tok, layer, x_all, mod, y_gath, top_w, gain):
    batch, seq, n_lat = tok.batch, tok.seq, tok.n_lat
    d = x_all.shape[-1]
    tm = TOKEN_TILE
    return pl.pallas_call(
        _final_kernel,
        grid=(batch, n_lat // tm),
        in_specs=[
            pl.BlockSpec((None, tm, d), lambda b, i: (b, i, 0)),
            pl.BlockSpec((None, None, None, 1, d), lambda b, i: (layer, tok.mod_off + b, 5, 0, 0)),
            pl.BlockSpec((TOP_K, None, tm, d), lambda b, i: (0, b, i, 0)),
            pl.BlockSpec((None, tm, LANE), lambda b, i: (b, i, 0)),
            pl.BlockSpec((1, d), lambda b, i: (0, 0)),
        ],
        out_specs=pl.BlockSpec((None, tm, d), lambda b, i: (b, i, 0)),
        out_shape=jax.ShapeDtypeStruct((batch, n_lat, d), F32),
        compiler_params=_cparams(("arbitrary", "arbitrary")),
        name="final_combine_norm",
    )(x_all.reshape(batch, seq, d), mod, y_gath.reshape(TOP_K, batch, seq, d),
      top_w.reshape(batch, seq, LANE), gain)


def _dispatch_plan(top_e, rank, counts, n_tok):
    n_exp = counts.shape[0]
    bm = MOE_ROWS
    n_pairs = n_tok * TOP_K
    n_blk = -(-(n_pairs + n_exp * (bm - 1)) // bm)
    padded = (counts + bm - 1) // bm * bm
    pad_end = jnp.cumsum(padded)
    pad_start = pad_end - padded
    dest = rank
    for e in range(n_exp):
        dest = dest + jnp.where(top_e == e, pad_start[e], 0)
    n_active = (pad_end[-1] // bm).astype(jnp.int32).reshape(1)
    blk_start = jnp.arange(n_blk, dtype=jnp.int32) * bm
    blk_e = jnp.minimum(jnp.sum(pad_end[None, :] <= blk_start[:, None], axis=1), n_exp - 1).astype(jnp.int32)
    tok_ids = jnp.broadcast_to(jnp.arange(n_tok, dtype=jnp.int32)[None, :], dest.shape)
    row_tok = (jnp.arange(n_blk * bm, dtype=jnp.int32) % n_tok).at[dest.reshape(-1)].set(
        tok_ids.reshape(-1), unique_indices=True, mode='promise_in_bounds')
    return dest, row_tok, blk_e, n_active


def kernel(x, c, ctx, c_ctx, w_mod, b_mod, norm_mix, norm_ffn, w_in, hy_conv_w, hy_conv_b, hy_ffn_w1, hy_ffn_b1, hy_ffn_w2, hy_ffn_b2, hy_ffn_w3, hy_bias, pool_w, pool_scale, ssm_conv_w, ssm_conv_b, ssm_dt_bias, ssm_a_log, ssm_d, ssm_norm, w_branch, w_gate, w_out, router_w, router_b, exp_w_up, exp_b_up, exp_w_down, exp_b_down, norm_final):
    batch, n_lat, d = x.shape
    n_ctx = ctx.shape[1]
    depth = w_mod.shape[0]
    assert n_lat // GRID_W * GRID_W == n_lat and n_lat % n_ctx == 0

    hw = hy_bias.shape[1]
    pw = pool_scale.shape[1]
    sw = ssm_norm.shape[1]
    cw = ssm_conv_w.shape[2]
    fw = w_in.shape[2] - 3 * hw - pw - sw - cw - 2 * SSM_HEADS
    widths = (fw, 3 * hw, pw, sw, cw)
    splits = (fw, fw + hw, fw + hw + pw, fw + hw + pw + sw)
    n_exp = router_w.shape[2]

    w_in_p = jnp.pad(w_in, ((0, 0), (0, 0), (0, LANE - 2 * SSM_HEADS))).astype(BF16)
    w_gate_b = w_gate.astype(BF16)
    w_branch_b = w_branch.astype(BF16)
    w_out_b = w_out.astype(BF16)
    pool_w_bd = jax.vmap(_block_diag)(pool_w).astype(BF16)
    r3 = lambda a: a.reshape(a.shape[0], 1, a.shape[1])
    lane_pad = lambda a: jnp.pad(a, ((0, 0), (0, LANE - a.shape[1])))
    norm_mix3, norm_ffn3 = r3(norm_mix), r3(norm_ffn)
    hy_conv_b3, hy_bias3 = r3(hy_conv_b), r3(hy_bias)
    ssm_conv_b3 = r3(ssm_conv_b)
    pool_scale3 = r3(pool_scale)
    dt_bias3 = r3(lane_pad(ssm_dt_bias.reshape(depth, 2 * SSM_HEADS)))
    a_row3 = r3(lane_pad(-jnp.exp(ssm_a_log.astype(F32)).reshape(depth, 2 * SSM_HEADS)))
    d_row3 = r3(jnp.repeat(ssm_d, SSM_HEAD_DIM, axis=1))
    ssm_norm3 = r3(ssm_norm)
    rw_t = jnp.swapaxes(router_w, 1, 2)
    rw_hi = rw_t.astype(BF16)
    rw_lo = (rw_t - rw_hi.astype(F32)).astype(BF16)
    router_b_col = jnp.broadcast_to(router_b[:, :, None], (depth, n_exp, LANE))
    exp_b_up4 = exp_b_up.reshape(depth, n_exp, 1, -1)
    exp_b_down4 = exp_b_down.reshape(depth, n_exp, 1, -1)
    emb = hy_ffn_w1.shape[1]
    ffn = hy_ffn_w1.shape[2]
    f_w1 = jnp.pad(hy_ffn_w1, ((0, 0), (0, LANE - emb), (0, LANE - ffn)))
    f_b1 = r3(lane_pad(hy_ffn_b1))
    f_w2 = jnp.pad(hy_ffn_w2, ((0, 0), (0, LANE - ffn), (0, LANE - ffn)))
    f_b2 = r3(lane_pad(hy_ffn_b2))
    f_w3 = jnp.pad(hy_ffn_w3, ((0, 0), (0, LANE - ffn), (0, 0)))

    freqs = jnp.linspace(1e-4, HYENA_BANDS - 1, HYENA_BANDS, dtype=F32)
    freqs_row = jnp.concatenate([jnp.zeros((1,), F32), freqs, freqs,
                                 jnp.zeros((LANE - 1 - 2 * HYENA_BANDS,), F32)])[None, :]
    deltas_row = jnp.linspace(HYENA_MIN_DECAY, HYENA_MAX_DECAY, hw, dtype=F32)[None, :]
    gi = jnp.arange(FOURIER_GROUP_W)
    ang = (gi[:, None] * gi[None, :] % FOURIER_GROUP_W).astype(F32) * (2.0 * math.pi / FOURIER_GROUP_W)
    n_fg = fw // FOURIER_GROUP_W
    eye = jnp.eye(n_fg, dtype=F32)
    wc = jnp.concatenate([jnp.kron(eye, jnp.cos(ang)), jnp.kron(eye, jnp.sin(ang))], axis=1).astype(BF16)
    band, invc = _pool_tables(n_lat, n_ctx)
    qi = jnp.arange(SSM_CHUNK)
    tri_f = (qi[:, None] >= qi[None, :]).astype(BF16)
    tri_b = (qi[:, None] <= qi[None, :]).astype(BF16)
    ti = jnp.arange(TOKEN_TILE)
    utri = (ti[:, None] < ti[None, :]).astype(BF16)

    seqs = {}
    for name, n, row_off in (("ctx", n_ctx, n_lat), ("lat", n_lat, 0)):
        cn, sn = _trig_tables(n, n)
        fwd, inv = _hyena_dft_tables(n)
        g_all, nrm = _filter_mlp(n, freqs_row, deltas_row, f_w1, f_b1, f_w2, f_b2, f_w3)
        kr, ki = _filter_spectrum(fwd, g_all, nrm, n, hw)
        seqs[name] = dict(n=n, off=row_off, cn=cn.astype(BF16), sn=sn.astype(BF16), fwd=fwd, inv=inv,
                          kr=kr, ki=ki)

    cc = jnp.concatenate([c, c_ctx[None, :]], axis=0)
    rows = -(-cc.shape[0] // 8) * 8
    mod = _modulation(jnp.pad(cc, ((0, rows - cc.shape[0]), (0, 0))), w_mod, b_mod)
    mod = mod.reshape(depth, rows, 6, 1, d)

    seq = n_lat + n_ctx

    def layer_step(tok, layer, x_all, moe):
        gb, n_tok = tok.batch, tok.n_tok
        res = _inproj(tok, layer, x_all, mod, norm_mix3, w_in_p, wc, band, invc, pool_w_bd, pool_scale3,
                      widths, moe=moe)
        if moe is not None:
            x_all = res[0]
            res = res[1:]
        xcs, hy, pool, z, xbc, dtr = res
        xcs3 = xcs.reshape(gb, seq, 2 * fw)
        hy3 = hy.reshape(gb, seq, 3 * hw)
        z3 = z.reshape(gb, seq, sw)
        xbc3 = xbc.reshape(gb, seq, cw)
        dt3 = dtr.reshape(gb, seq, LANE)
        outs = {}
        zero_state = jnp.zeros((gb, SSM_GROUPS, SSM_STATE, HEADS_PER_GROUP * SSM_HEAD_DIM), F32)
        states = (zero_state, zero_state)
        for name in ("ctx", "lat"):
            s = seqs[name]
            n, off = s["n"], s["off"]
            y_f = _fourier_seq(xcs3, s["cn"], s["sn"], n, off, fw)
            vv, x1 = _hy_pre(hy3, hy_conv_w, hy_conv_b3, layer, n, off, hw)
            y2 = _hy_fwd(s["fwd"], vv, s["kr"], s["ki"], layer, n, hw)
            y_h = _hy_inv(s["inv"], y2, x1, vv, hy_bias3, layer, n, hw)
            yf, st_f = _ssd_pass(xbc3, dt3, states[0], tri_f, ssm_conv_w, ssm_conv_b3, dt_bias3, a_row3,
                                 layer, n, off, reverse=False)
            y_s, st_b = _ssd_pass(xbc3, dt3, states[1], tri_b, ssm_conv_w, ssm_conv_b3, dt_bias3, a_row3,
                                  layer, n, off, reverse=True, fin=(z3, yf, d_row3, ssm_norm3))
            states = (st_f, st_b)
            outs[name] = (y_f, y_h, y_s)
        x_all = _merge(tok, layer, x_all, mod, norm_mix3, outs["lat"][0], outs["ctx"][0], outs["lat"][1],
                       outs["ctx"][1], pool, outs["lat"][2], outs["ctx"][2], w_gate_b, w_branch_b, w_out_b,
                       splits)
        h2, top_e, rank, top_w, counts = _router(tok, layer, x_all, mod, norm_ffn3, rw_hi, rw_lo, router_b_col,
                                                 utri)
        dest, row_tok, blk_e, n_active = _dispatch_plan(top_e[:TOP_K], rank[:TOP_K],
                                                        counts[:, 0].astype(jnp.int32), n_tok)
        x_sorted = h2[row_tok]
        y_rows = _moe(layer, x_sorted, blk_e, n_active, exp_w_up, exp_b_up4, exp_w_down, exp_b_down4)
        y_gath = y_rows[dest.reshape(-1)].reshape(TOP_K, n_tok, d)
        return x_all, (y_gath, top_w)

    n_groups = SAMPLE_GROUPS if batch % SAMPLE_GROUPS == 0 else 1
    gb = batch // n_groups
    groups = []
    for g in range(n_groups):
        tok = _Tok(gb, n_lat, n_ctx, mod_off=g * gb, mod_ctx=batch)
        x_all = jnp.concatenate([x[g * gb:(g + 1) * gb], ctx[g * gb:(g + 1) * gb]], axis=1)
        groups.append([tok, x_all.reshape(tok.n_tok, d), None])
    for layer in range(depth):
        for grp in groups:
            grp[1], grp[2] = layer_step(grp[0], layer, grp[1], grp[2])
    outs = [_final(tok, depth - 1, x_all, mod, moe[0], moe[1], norm_final[None, :])
            for tok, x_all, moe in groups]
    return outs[0] if n_groups == 1 else jnp.concatenate(outs, axis=0)
```

```python
import functools
import math

import jax
import jax.numpy as jnp
from jax import lax
from jax.experimental import pallas as pl
from jax.experimental.pallas import tpu as pltpu

F32 = jnp.float32
BF16 = jnp.bfloat16

GRID_W = 64
EPS = 1e-6
FOURIER_GROUP_W = 64
HYENA_BANDS = 16
HYENA_MIN_DECAY = -math.log(1e-2) / 1.5
HYENA_MAX_DECAY = -math.log(1e-2) / 0.3
POOL_WINDOWS = (2, 4, 8, 16)
POOL_GROUP_W = 64
SSM_GROUPS = 2
HEADS_PER_GROUP = 4
SSM_HEADS = SSM_GROUPS * HEADS_PER_GROUP
SSM_HEAD_DIM = 64
SSM_STATE = 128
SSM_CHUNK = 128
TOP_K = 4
SWIGLU_LIMIT = 7.0
SWIGLU_ALPHA = 1.702

LANE = 128
BF16_SUBLANE = 16
TOKEN_TILE = 256
SEQ_TILE = 512
MOE_ROWS = 512
SSD_SAMPLES = 2
SAMPLE_GROUPS = 1
VMEM_LIMIT = 56 * 1024 * 1024


def _cparams(sem):
    return pltpu.CompilerParams(dimension_semantics=sem, vmem_limit_bytes=VMEM_LIMIT)


def _sigmoid(x):
    return 1.0 / (1.0 + jnp.exp(-x))


def _dot(a, b):
    return jnp.dot(a, b, preferred_element_type=F32)


def _dot_hi(a, b):
    return jnp.dot(a, b, preferred_element_type=F32, precision=lax.Precision.HIGHEST)


def _split3(x):
    x1 = x.astype(BF16)
    r1 = x - x1.astype(F32)
    x2 = r1.astype(BF16)
    x3 = (r1 - x2.astype(F32)).astype(BF16)
    return x1, x2, x3


def _dot_exact01(m01, x):
    x1, x2, x3 = _split3(x)
    return _dot(m01, x1) + _dot(m01, x2) + _dot(m01, x3)


def _mod_kernel(c_ref, w_ref, b_ref, o_ref):
    c = c_ref[...]
    s = c * _sigmoid(c)
    o_ref[...] = _dot_hi(s, w_ref[...]) + b_ref[...]


def _modulation(cc, w_mod, b_mod):
    depth, d, six_d = w_mod.shape
    rows = cc.shape[0]
    nj = six_d // d
    return pl.pallas_call(
        _mod_kernel,
        grid=(depth, nj),
        in_specs=[
            pl.BlockSpec((rows, d), lambda l, j: (0, 0)),
            pl.BlockSpec((None, d, d), lambda l, j: (l, 0, j)),
            pl.BlockSpec((None, 1, d), lambda l, j: (l, 0, j)),
        ],
        out_specs=pl.BlockSpec((None, rows, d), lambda l, j: (l, 0, j)),
        out_shape=jax.ShapeDtypeStruct((depth, rows, six_d), F32),
        compiler_params=_cparams(("arbitrary", "arbitrary")),
        name="modulation",
    )(cc, w_mod, b_mod.reshape(depth, 1, six_d))


class _Tok:
    def __init__(self, batch, n_lat, n_ctx, mod_off, mod_ctx):
        assert n_ctx == TOKEN_TILE and n_lat % TOKEN_TILE == 0
        self.batch, self.n_lat, self.n_ctx = batch, n_lat, n_ctx
        self.mod_off, self.mod_ctx = mod_off, mod_ctx
        self.seq = n_lat + n_ctx
        self.tiles_per_b = self.seq // TOKEN_TILE
        self.lat_tiles = n_lat // TOKEN_TILE
        self.n_tiles = batch * self.tiles_per_b
        self.n_tok = batch * self.seq

    def b(self, t):
        return t // self.tiles_per_b

    def w(self, t):
        return t % self.tiles_per_b

    def is_ctx(self, t):
        return (t % self.tiles_per_b) >= self.lat_tiles

    def mod_row(self, t):
        return jnp.where(self.is_ctx(t), self.mod_ctx, self.mod_off + self.b(t))

    def mod_spec(self, layer, part, d):
        return pl.BlockSpec((None, None, None, 1, d), lambda t: (layer, self.mod_row(t), part, 0, 0))

    def lat_spec(self, width):
        return pl.BlockSpec((None, TOKEN_TILE, width),
                            lambda t: (self.b(t), jnp.minimum(self.w(t), self.lat_tiles - 1), 0))

    def ctx_spec(self, width):
        return pl.BlockSpec((None, TOKEN_TILE, width), lambda t: (self.b(t), 0, 0))


def _norm_mod(x, gain, scale, shift):
    ms = jnp.mean(x * x, axis=-1, keepdims=True)
    return (x * lax.rsqrt(ms + EPS)) * gain * (1.0 + scale) + shift


def _moe_residual(x, g2, yg_ref, w_ref):
    w = w_ref[...]
    y = None
    for k in range(TOP_K):
        term = w[:, k:k + 1] * yg_ref[k].astype(F32)
        y = term if y is None else y + term
    return x + g2 * y


def _inproj_kernel(*refs, widths, combine):
    if combine:
        (x_ref, g2_ref, yg_ref, w_ref, sc_ref, sh_ref, g_ref, win_ref, wc_ref, band_ref, invc_ref, pw_ref,
         ps_ref, xn_ref, xcs_ref, hy_ref, pool_ref, z_ref, xbc_ref, dt_ref) = refs
        x = _moe_residual(x_ref[...], g2_ref[...], yg_ref, w_ref)
        xn_ref[...] = x
    else:
        (x_ref, sc_ref, sh_ref, g_ref, win_ref, wc_ref, band_ref, invc_ref, pw_ref, ps_ref,
         xcs_ref, hy_ref, pool_ref, z_ref, xbc_ref, dt_ref) = refs
        x = x_ref[...]
    fw, hw, pw, zw, cw = widths
    hb = _norm_mod(x, g_ref[...], sc_ref[...], sh_ref[...]).astype(BF16)
    o = 0
    f_in = _dot(hb, win_ref[:, o:o + fw])
    xcs_ref[...] = _dot(f_in.astype(BF16), wc_ref[...]).astype(BF16)
    o += fw
    hy_ref[...] = _dot(hb, win_ref[:, o:o + hw]).astype(BF16)
    o += hw
    u = _dot(hb, win_ref[:, o:o + pw])
    o += pw
    z_ref[...] = _dot(hb, win_ref[:, o:o + zw]).astype(BF16)
    o += zw
    xbc_ref[...] = _dot(hb, win_ref[:, o:o + cw]).astype(BF16)
    o += cw
    dt_ref[...] = _dot(hb, win_ref[:, o:o + LANE])
    gw = POOL_GROUP_W
    parts = []
    for gi in range(len(POOL_WINDOWS)):
        ug = u[:, gi * gw:(gi + 1) * gw]
        parts.append(_dot_exact01(band_ref[gi], ug))
    pooled = jnp.concatenate(parts, axis=1) * invc_ref[...] - u
    pool_ref[...] = (_dot(pooled.astype(BF16), pw_ref[...]) * ps_ref[...]).astype(BF16)


def _pool_tables(n_lat, n_ctx):
    tm = TOKEN_TILE
    p = jnp.arange(tm)
    bands, invs = [], []
    for row_len in (GRID_W, n_ctx):
        assert tm % row_len == 0
        pr, rr = p % row_len, p // row_len
        bg, ig = [], []
        for win in POOL_WINDOWS:
            lo = jnp.clip(pr - win // 2, 0, row_len)
            hi = jnp.clip(pr + win // 2, 0, row_len)
            q = pr[None, :]
            m = (rr[:, None] == rr[None, :]) & (q >= lo[:, None]) & (q < hi[:, None])
            bg.append(m.astype(BF16))
            ig.append(jnp.broadcast_to((1.0 / (hi - lo).astype(F32))[:, None], (tm, POOL_GROUP_W)))
        bands.append(jnp.stack(bg))
        invs.append(jnp.concatenate(ig, axis=1))
    return jnp.stack(bands), jnp.stack(invs)


def _block_diag(w):
    g, a, b = w.shape
    out = jnp.zeros((g * a, g * b), w.dtype)
    for i in range(g):
        out = out.at[i * a:(i + 1) * a, i * b:(i + 1) * b].set(w[i])
    return out


def _inproj(tok, layer, x_all, mod, norm_mix, w_in_p, wc, band, invc, pool_w_bd, pool_scale, widths,
            moe=None):
    d = x_all.shape[-1]
    fw, hw, pw, zw, cw = widths
    tm = TOKEN_TILE
    n_tok = tok.n_tok
    kind = lambda t: jnp.where(tok.is_ctx(t), 1, 0)
    const2 = lambda t: (0, 0)
    pre_specs, pre_args, pre_outs = [], [], []
    if moe is not None:
        pre_specs = [tok.mod_spec(layer - 1, 5, d),
                     pl.BlockSpec((TOP_K, tm, d), lambda t: (0, t, 0)),
                     pl.BlockSpec((tm, LANE), lambda t: (t, 0))]
        pre_args = [mod, moe[0], moe[1]]
        pre_outs = [jax.ShapeDtypeStruct((n_tok, d), F32)]
    outs = pre_outs + [
        jax.ShapeDtypeStruct((n_tok, 2 * fw), BF16),
        jax.ShapeDtypeStruct((n_tok, hw), BF16),
        jax.ShapeDtypeStruct((n_tok, pw), BF16),
        jax.ShapeDtypeStruct((n_tok, zw), BF16),
        jax.ShapeDtypeStruct((n_tok, cw), BF16),
        jax.ShapeDtypeStruct((n_tok, LANE), F32),
    ]
    return pl.pallas_call(
        functools.partial(_inproj_kernel, widths=widths, combine=moe is not None),
        grid=(tok.n_tiles,),
        in_specs=[pl.BlockSpec((tm, d), lambda t: (t, 0))] + pre_specs + [
            tok.mod_spec(layer, 1, d),
            tok.mod_spec(layer, 0, d),
            pl.BlockSpec((None, 1, d), lambda t: (layer, 0, 0)),
            pl.BlockSpec((None,) + w_in_p.shape[1:], lambda t: (layer, 0, 0)),
            pl.BlockSpec(wc.shape, const2),
            pl.BlockSpec((None,) + band.shape[1:], lambda t: (kind(t), 0, 0, 0)),
            pl.BlockSpec((None,) + invc.shape[1:], lambda t: (kind(t), 0, 0)),
            pl.BlockSpec((None,) + pool_w_bd.shape[1:], lambda t: (layer, 0, 0)),
            pl.BlockSpec((None, 1, pw), lambda t: (layer, 0, 0)),
        ],
        out_specs=[pl.BlockSpec((tm, s.shape[1]), lambda t: (t, 0)) for s in outs],
        out_shape=outs,
        compiler_params=_cparams(("arbitrary",)),
        name="inproj",
    )(x_all, *pre_args, mod, mod, norm_mix, w_in_p, wc, band, invc, pool_w_bd, pool_scale)


def _trig_tables(n, period):
    k = jnp.arange(n, dtype=jnp.int32)

    def cs(j):
        m = (j[:, None] * k[None, :]) % period
        ang = m.astype(F32) * (2.0 * math.pi / period)
        return jnp.cos(ang), jnp.sin(ang)

    split = 64
    if n % split or n <= split:
        return cs(k)
    c1, s1 = cs(jnp.arange(n // split, dtype=jnp.int32) * split)
    c2, s2 = cs(jnp.arange(split, dtype=jnp.int32))
    c = c1[:, None, :] * c2[None, :, :] - s1[:, None, :] * s2[None, :, :]
    s = s1[:, None, :] * c2[None, :, :] + c1[:, None, :] * s2[None, :, :]
    return c.reshape(n, n), s.reshape(n, n)


def _hyena_dft_tables(n):
    c, s = _trig_tables(n, 2 * n)
    nyq = jnp.where(jnp.arange(n) % 2 == 0, 1.0, -1.0).astype(F32)
    s = (-s).at[0].set(nyq)
    fwd = jnp.concatenate([c, s], axis=0).astype(BF16)
    return fwd, fwd.T


def _fourier_kernel(c_ref, s_ref, xc_ref, xs_ref, o_ref, acc_ref, *, scale):
    k = pl.program_id(1)

    @pl.when(k == 0)
    def _():
        acc_ref[...] = jnp.zeros_like(acc_ref)

    nb = xc_ref.shape[0]
    rc = jnp.concatenate([xc_ref[b] for b in range(nb)], axis=1)
    rs = jnp.concatenate([xs_ref[b] for b in range(nb)], axis=1)
    acc_ref[...] += _dot(c_ref[...], rc) - _dot(s_ref[...], rs)

    @pl.when(k == pl.num_programs(1) - 1)
    def _():
        w = o_ref.shape[2]
        for b in range(nb):
            o_ref[b] = (acc_ref[:, b * w:(b + 1) * w] * scale).astype(o_ref.dtype)


def _fourier_seq(xcs3, cn, sn, n, row_off, fw):
    batch = xcs3.shape[0]
    tl = min(SEQ_TILE, n)
    nt = n // tl
    off = row_off // tl
    scale = 1.0 / math.sqrt(n * FOURIER_GROUP_W)
    return pl.pallas_call(
        functools.partial(_fourier_kernel, scale=scale),
        grid=(nt, nt),
        in_specs=[
            pl.BlockSpec((tl, tl), lambda i, k: (i, k)),
            pl.BlockSpec((tl, tl), lambda i, k: (i, k)),
            pl.BlockSpec((batch, tl, fw), lambda i, k: (0, off + k, 0)),
            pl.BlockSpec((batch, tl, fw), lambda i, k: (0, off + k, 1)),
        ],
        out_specs=pl.BlockSpec((batch, tl, fw), lambda i, k: (0, i, 0)),
        out_shape=jax.ShapeDtypeStruct((batch, n, fw), BF16),
        scratch_shapes=[pltpu.VMEM((tl, batch * fw), F32)],
        compiler_params=_cparams(("arbitrary", "arbitrary")),
        name="fourier_seq",
    )(cn, sn, xcs3, xcs3)


def _shift_rows(x, prev_row, next_row):
    n = x.shape[0]
    rows = lax.broadcasted_iota(jnp.int32, x.shape, 0)
    xm = jnp.where(rows == 0, prev_row, pltpu.roll(x, 1, 0))
    xp = jnp.where(rows == n - 1, next_row, pltpu.roll(x, n - 1, 0))
    return xm, xp


def _conv3(x, prev_row, next_row, w_ref, b_ref):
    xm, xp = _shift_rows(x, prev_row, next_row)
    return w_ref[0:1, :] * xm + w_ref[1:2, :] * x + w_ref[2:3, :] * xp + b_ref[...]


def _hy_pre_kernel(x_ref, xp_ref, xn_ref, w_ref, b_ref, vv_ref, x1_ref, *, hw):
    i = pl.program_id(1)
    last = pl.num_programs(1) - 1
    x = x_ref[...].astype(F32)
    hs = xp_ref.shape[0]
    prev_row = jnp.where(i == 0, 0.0, xp_ref[hs - 1:hs, :].astype(F32))
    next_row = jnp.where(i == last, 0.0, xn_ref[0:1, :].astype(F32))
    u = _conv3(x, prev_row, next_row, w_ref, b_ref)
    x1_ref[...] = u[:, :hw].astype(BF16)
    vv_ref[...] = (u[:, hw:2 * hw] * u[:, 2 * hw:]).astype(BF16)


def _halo_specs(width, tile, row_off, seq_rows):
    hs = BF16_SUBLANE
    per = tile // hs
    base = row_off // hs
    top = seq_rows // hs - 1
    prev = pl.BlockSpec((None, hs, width), lambda b, i: (b, jnp.maximum(base + i * per - 1, 0), 0))
    nxt = pl.BlockSpec((None, hs, width), lambda b, i: (b, jnp.minimum(base + (i + 1) * per, top), 0))
    return prev, nxt


def _hy_pre(hy3, conv_w, conv_b, layer, n, row_off, hw):
    batch, seq, w3 = hy3.shape
    tl = min(SEQ_TILE, n)
    off = row_off // tl
    prev, nxt = _halo_specs(w3, tl, row_off, seq)
    out = jax.ShapeDtypeStruct((batch, n, hw), BF16)
    return pl.pallas_call(
        functools.partial(_hy_pre_kernel, hw=hw),
        grid=(batch, n // tl),
        in_specs=[
            pl.BlockSpec((None, tl, w3), lambda b, i: (b, off + i, 0)),
            prev, nxt,
            pl.BlockSpec((None,) + conv_w.shape[1:], lambda b, i: (layer, 0, 0)),
            pl.BlockSpec((None, 1, w3), lambda b, i: (layer, 0, 0)),
        ],
        out_specs=[pl.BlockSpec((None, tl, hw), lambda b, i: (b, i, 0))] * 2,
        out_shape=[out, out],
        compiler_params=_cparams(("arbitrary", "arbitrary")),
        name="hyena_pre",
    )(hy3, hy3, hy3, conv_w, conv_b)


def _filter_mlp_kernel(fr_ref, dl_ref, w1_ref, b1_ref, w2_ref, b2_ref, w3_ref, g_ref, nrm_ref, *, n, hw):
    j = pl.program_id(1)
    rt = g_ref.shape[0]
    lane = lax.broadcasted_iota(jnp.int32, (rt, LANE), 1)
    pos = (lax.broadcasted_iota(jnp.int32, (rt, LANE), 0) + j * rt).astype(F32)
    t = pos / (n - 1)
    ang = 2.0 * math.pi * pos / n
    arg = ang * fr_ref[...]
    feats = jnp.where(lane == 0, t,
                      jnp.where(lane <= HYENA_BANDS, jnp.cos(arg),
                                jnp.where(lane <= 2 * HYENA_BANDS, -jnp.sin(arg), 0.0)))
    h1 = jnp.sin(_dot_hi(feats, w1_ref[...]) + b1_ref[...])
    h2 = jnp.sin(_dot_hi(h1, w2_ref[...]) + b2_ref[...])
    k = _dot_hi(h2, w3_ref[...])
    decay = jnp.exp(-t[:, 0:1] * dl_ref[...])
    kf = k[:, :hw] * decay
    kb = jnp.where(pos[:, 0:1] == 0.0, 0.0, k[:, hw:] * decay)
    g_ref[:, :hw] = (kf + kb).astype(BF16)
    g_ref[:, hw:] = (kf - kb).astype(BF16)
    part = jnp.sum(jnp.abs(kf) + jnp.abs(kb), axis=0, keepdims=True)

    @pl.when(j == 0)
    def _():
        nrm_ref[...] = part

    @pl.when(j > 0)
    def _():
        nrm_ref[...] += part


def _filter_mlp(n, freqs_row, deltas_row, w1, b1, w2, b2, w3):
    depth = w1.shape[0]
    hw = deltas_row.shape[1]
    rt = min(SEQ_TILE, n)
    per_layer = lambda l, j: (l, 0, 0)
    return pl.pallas_call(
        functools.partial(_filter_mlp_kernel, n=n, hw=hw),
        grid=(depth, n // rt),
        in_specs=[
            pl.BlockSpec((1, LANE), lambda l, j: (0, 0)),
            pl.BlockSpec((1, hw), lambda l, j: (0, 0)),
            pl.BlockSpec((None,) + w1.shape[1:], per_layer),
            pl.BlockSpec((None,) + b1.shape[1:], per_layer),
            pl.BlockSpec((None,) + w2.shape[1:], per_layer),
            pl.BlockSpec((None,) + b2.shape[1:], per_layer),
            pl.BlockSpec((None,) + w3.shape[1:], per_layer),
        ],
        out_specs=[
            pl.BlockSpec((rt, 2 * hw), lambda l, j: (j, l)),
            pl.BlockSpec((None, 1, hw), lambda l, j: (l, 0, 0)),
        ],
        out_shape=[
            jax.ShapeDtypeStruct((n, depth * 2 * hw), BF16),
            jax.ShapeDtypeStruct((depth, 1, hw), F32),
        ],
        compiler_params=_cparams(("arbitrary", "arbitrary")),
        name="hyena_filter_mlp",
    )(freqs_row, deltas_row, w1, b1, w2, b2, w3)


def _rdft_accumulate(fc_ref, fs_ref, rhs, accc_ref, accs_ref):
    k = pl.program_id(1)

    @pl.when(k == 0)
    def _():
        accc_ref[...] = jnp.zeros_like(accc_ref)
        accs_ref[...] = jnp.zeros_like(accs_ref)

    accc_ref[...] += _dot(fc_ref[...], rhs)
    accs_ref[...] += _dot(fs_ref[...], rhs)


def _filter_spec_kernel(fc_ref, fs_ref, g_ref, nrm_ref, kr_ref, ki_ref, accc_ref, accs_ref, *, hw):
    i = pl.program_id(0)
    _rdft_accumulate(fc_ref, fs_ref, g_ref[...], accc_ref, accs_ref)

    @pl.when(pl.program_id(1) == pl.num_programs(1) - 1)
    def _():
        tm = accc_ref.shape[0]
        row0 = (lax.broadcasted_iota(jnp.int32, (tm, hw), 0) + i * tm) == 0
        for l in range(kr_ref.shape[0]):
            inv = 1.0 / nrm_ref[l]
            o = l * 2 * hw
            kr_ref[l] = accc_ref[:, o:o + hw] * inv
            ki_ref[l] = jnp.where(row0, accs_ref[:, o:o + hw], accs_ref[:, o + hw:o + 2 * hw]) * inv


def _filter_spectrum(fwd, g_all, nrm, n, hw):
    depth = nrm.shape[0]
    tl = min(SEQ_TILE, n)
    nt = n // tl
    width = g_all.shape[1]
    out = jax.ShapeDtypeStruct((depth, n, hw), F32)
    return pl.pallas_call(
        functools.partial(_filter_spec_kernel, hw=hw),
        grid=(nt, nt),
        in_specs=[
            pl.BlockSpec((tl, tl), lambda i, k: (i, k)),
            pl.BlockSpec((tl, tl), lambda i, k: (nt + i, k)),
            pl.BlockSpec((tl, width), lambda i, k: (k, 0)),
            pl.BlockSpec(nrm.shape, lambda i, k: (0, 0, 0)),
        ],
        out_specs=[pl.BlockSpec((depth, tl, hw), lambda i, k: (0, i, 0))] * 2,
        out_shape=[out, out],
        scratch_shapes=[pltpu.VMEM((tl, width), F32)] * 2,
        compiler_params=_cparams(("arbitrary", "arbitrary")),
        name="hyena_filter_spectrum",
    )(fwd, fwd, g_all, nrm)


def _hy_fwd_kernel(fc_ref, fs_ref, vv_ref, kr_ref, ki_ref, y_ref, accc_ref, accs_ref):
    i = pl.program_id(0)
    nb = vv_ref.shape[0]
    rhs = jnp.concatenate([vv_ref[b] for b in range(nb)], axis=1)
    _rdft_accumulate(fc_ref, fs_ref, rhs, accc_ref, accs_ref)

    @pl.when(pl.program_id(1) == pl.num_programs(1) - 1)
    def _():
        tm, hw = kr_ref.shape
        row0 = (lax.broadcasted_iota(jnp.int32, (tm, hw), 0) + i * tm) == 0
        kr, ki = kr_ref[...], ki_ref[...]
        for b in range(nb):
            vr = accc_ref[:, b * hw:(b + 1) * hw]
            vi = accs_ref[:, b * hw:(b + 1) * hw]
            yr = jnp.where(row0, 0.5 * vr * kr, vr * kr - vi * ki)
            yi = jnp.where(row0, 0.5 * vi * ki, vr * ki + vi * kr)
            y_ref[0, :, b * hw:(b + 1) * hw] = yr.astype(BF16)
            y_ref[1, :, b * hw:(b + 1) * hw] = yi.astype(BF16)


def _hy_fwd(fwd, vv, kr, ki, layer, n, hw):
    batch = vv.shape[0]
    tl = min(SEQ_TILE, n)
    nt = n // tl
    return pl.pallas_call(
        _hy_fwd_kernel,
        grid=(nt, nt),
        in_specs=[
            pl.BlockSpec((tl, tl), lambda i, k: (i, k)),
            pl.BlockSpec((tl, tl), lambda i, k: (nt + i, k)),
            pl.BlockSpec((batch, tl, hw), lambda i, k: (0, k, 0)),
            pl.BlockSpec((None, tl, hw), lambda i, k: (layer, i, 0)),
            pl.BlockSpec((None, tl, hw), lambda i, k: (layer, i, 0)),
        ],
        out_specs=pl.BlockSpec((2, tl, batch * hw), lambda i, k: (0, i, 0)),
        out_shape=jax.ShapeDtypeStruct((2, n, batch * hw), BF16),
        scratch_shapes=[pltpu.VMEM((tl, batch * hw), F32)] * 2,
        compiler_params=_cparams(("arbitrary", "arbitrary")),
        name="hyena_fwd_dft",
    )(fwd, fwd, vv, kr, ki)


def _hy_inv_kernel(fi_ref, y_ref, x1_ref, vv_ref, bias_ref, o_ref, acc_ref, *, scale):
    k = pl.program_id(1)

    @pl.when(k == 0)
    def _():
        acc_ref[...] = jnp.zeros_like(acc_ref)

    acc_ref[...] += _dot(fi_ref[...], y_ref[...])

    @pl.when(k == pl.num_programs(1) - 1)
    def _():
        nb, _, hw = x1_ref.shape
        for b in range(nb):
            conv = acc_ref[:, b * hw:(b + 1) * hw] * scale
            vv = vv_ref[b].astype(F32)
            o_ref[b] = (x1_ref[b].astype(F32) * (conv + vv * bias_ref[...])).astype(BF16)


def _hy_inv(inv, y2, x1, vv, hy_bias, layer, n, hw):
    batch = vv.shape[0]
    tl = min(SEQ_TILE, n)
    nt = n // tl
    return pl.pallas_call(
        functools.partial(_hy_inv_kernel, scale=1.0 / n),
        grid=(nt, 2 * nt),
        in_specs=[
            pl.BlockSpec((tl, tl), lambda i, k: (i, k)),
            pl.BlockSpec((tl, batch * hw), lambda i, k: (k, 0)),
            pl.BlockSpec((batch, tl, hw), lambda i, k: (0, i, 0)),
            pl.BlockSpec((batch, tl, hw), lambda i, k: (0, i, 0)),
            pl.BlockSpec((None, 1, hw), lambda i, k: (layer, 0, 0)),
        ],
        out_specs=pl.BlockSpec((batch, tl, hw), lambda i, k: (0, i, 0)),
        out_shape=jax.ShapeDtypeStruct((batch, n, hw), BF16),
        scratch_shapes=[pltpu.VMEM((tl, batch * hw), F32)],
        compiler_params=_cparams(("arbitrary", "arbitrary")),
        name="hyena_inv_dft",
    )(inv, y2.reshape(2 * n, batch * hw), x1, vv, hy_bias)


def _softplus(x):
    return jnp.maximum(x, 0.0) + jnp.log1p(jnp.exp(-jnp.abs(x)))


def _ssd_kernel(*refs, reverse, final, nc, lane0):
    if final:
        (xbc_ref, xp_ref, xn_ref, dt_ref, h0_ref, tri_ref, exp_ref, cw_ref, cb_ref, dtb_ref, a_ref,
         z_ref, yf_ref, d_ref, nrm_ref, out_ref, hout_ref, h_scr) = refs
    else:
        (xbc_ref, xp_ref, xn_ref, dt_ref, h0_ref, tri_ref, exp_ref, cw_ref, cb_ref, dtb_ref, a_ref,
         out_ref, hout_ref, h_scr) = refs
    step = pl.program_id(1)
    cc = (nc - 1 - step) if reverse else step

    @pl.when(step == 0)
    def _():
        h_scr[...] = h0_ref[...]

    nb, q = xbc_ref.shape[0], xbc_ref.shape[1]
    hs = xp_ref.shape[1]
    width = SSM_HEADS * SSM_HEAD_DIM
    gs = SSM_STATE
    pd = SSM_HEAD_DIM
    ri = lax.broadcasted_iota(jnp.int32, (q, q), 0)
    ci = lax.broadcasted_iota(jnp.int32, (q, q), 1)
    mask = (ri <= ci) if reverse else (ri >= ci)

    for s in range(nb):
        x = xbc_ref[s].astype(F32)
        prev_row = jnp.where(cc == 0, 0.0, xp_ref[s, hs - 1:hs, :].astype(F32))
        next_row = jnp.where(cc == nc - 1, 0.0, xn_ref[s, 0:1, :].astype(F32))
        u = _conv3(x, prev_row, next_row, cw_ref, cb_ref)
        u = u * _sigmoid(u)
        xs = u[:, :width]
        bm = u[:, width:width + SSM_GROUPS * gs]
        cm = u[:, width + SSM_GROUPS * gs:]

        dt = _softplus(dt_ref[s] + dtb_ref[...])
        acs = _dot_exact01(tri_ref[...], dt * a_ref[...])
        acs_t = acs.T
        tot = acs[0:1, :] if reverse else acs[q - 1:q, :]
        to_end = jnp.exp(tot - acs)
        frm = jnp.exp(acs)
        cdec = jnp.exp(tot)

        stacked = jnp.concatenate([dt, to_end, frm, jnp.broadcast_to(cdec, (8, LANE))], axis=0)
        s1, s2, s3 = _split3(stacked)
        ex = exp_ref[...]
        rep = _dot(s1, ex) + _dot(s2, ex) + _dot(s3, ex)
        xdt = xs * rep[0:q]
        wts = (xdt * rep[q:2 * q]).astype(BF16)
        frm_rep = rep[2 * q:3 * q]
        cdec_rep = rep[3 * q:3 * q + 1]
        xdt_b = xdt.astype(BF16)
        half = lax.broadcasted_iota(jnp.int32, (q, 2 * pd), 1) < pd

        ys = []
        gw = HEADS_PER_GROUP * pd
        for g in range(SSM_GROUPS):
            bg = bm[:, g * gs:(g + 1) * gs]
            cg = cm[:, g * gs:(g + 1) * gs].astype(BF16)
            scores = lax.dot_general(cg, bg.astype(BF16), (((1,), (1,)), ((), ())),
                                     preferred_element_type=F32)
            hg = h_scr[s, g]
            yoff = _dot(cg, hg.astype(BF16)) * frm_rep[:, g * gw:(g + 1) * gw]
            for pr in range(HEADS_PER_GROUP // 2):
                pair = []
                c0 = g * gw + pr * 2 * pd
                for j in range(2):
                    li = lane0 + g * HEADS_PER_GROUP + 2 * pr + j
                    lm = jnp.where(mask, jnp.exp(acs[:, li:li + 1] - acs_t[li:li + 1, :]), 0.0)
                    m = (scores * lm).astype(BF16)
                    pair.append(_dot(m, xdt_b[:, c0:c0 + 2 * pd]))
                ys.append(jnp.where(half, pair[0], pair[1]) + yoff[:, pr * 2 * pd:(pr + 1) * 2 * pd])
            st = _dot(bg.T.astype(BF16), wts[:, g * gw:(g + 1) * gw])
            h_scr[s, g] = hg * cdec_rep[:, g * gw:(g + 1) * gw] + st
        y = jnp.concatenate(ys, axis=1)

        if final:
            y = y + yf_ref[s] + xs * d_ref[...]
            z = z_ref[s].astype(F32)
            y = y * (z * _sigmoid(z))
            ms = jnp.mean(y * y, axis=-1, keepdims=True)
            out_ref[s] = ((y * lax.rsqrt(ms + EPS)) * nrm_ref[...]).astype(out_ref.dtype)
        else:
            out_ref[s] = y

    @pl.when(step == nc - 1)
    def _():
        hout_ref[...] = h_scr[...]


def _ssd_pass(xbc3, dt3, h0, tri, conv_w, conv_b, dt_bias, a_row, layer, n, row_off, *, reverse,
              fin=None):
    batch, seq, cw = xbc3.shape
    q = SSM_CHUNK
    nc = n // q
    off = row_off // q
    width = SSM_HEADS * SSM_HEAD_DIM

    def cidx(c):
        return off + ((nc - 1 - c) if reverse else c)

    hsz = BF16_SUBLANE
    per = q // hsz
    top = seq // hsz - 1
    nb = SSD_SAMPLES if batch % SSD_SAMPLES == 0 else 1
    per_layer = lambda b, c: (layer, 0, 0)
    in_specs = [
        pl.BlockSpec((nb, q, cw), lambda b, c: (b, cidx(c), 0)),
        pl.BlockSpec((nb, hsz, cw), lambda b, c: (b, jnp.maximum(cidx(c) * per - 1, 0), 0)),
        pl.BlockSpec((nb, hsz, cw), lambda b, c: (b, jnp.minimum((cidx(c) + 1) * per, top), 0)),
        pl.BlockSpec((nb, q, LANE), lambda b, c: (b, cidx(c), 0)),
        pl.BlockSpec((nb,) + h0.shape[1:], lambda b, c: (b, 0, 0, 0)),
        pl.BlockSpec((q, q), lambda b, c: (0, 0)),
        pl.BlockSpec((LANE, width), lambda b, c: (0, 0)),
        pl.BlockSpec((None,) + conv_w.shape[1:], per_layer),
        pl.BlockSpec((None, 1, cw), per_layer),
        pl.BlockSpec((None, 1, LANE), per_layer),
        pl.BlockSpec((None, 1, LANE), per_layer),
    ]
    lane0 = SSM_HEADS if reverse else 0
    head_rep = (jnp.arange(LANE)[:, None] == lane0 + jnp.arange(width)[None, :] // SSM_HEAD_DIM).astype(BF16)
    args = [xbc3, xbc3, xbc3, dt3, h0, tri, head_rep, conv_w, conv_b, dt_bias, a_row]
    if fin is not None:
        z3, yf, d_row, nrm_row = fin
        in_specs += [
            pl.BlockSpec((nb, q, width), lambda b, c: (b, cidx(c), 0)),
            pl.BlockSpec((nb, q, width), lambda b, c: (b, cidx(c) - off, 0)),
            pl.BlockSpec((None, 1, width), per_layer),
            pl.BlockSpec((None, 1, width), per_layer),
        ]
        args += [z3, yf, d_row, nrm_row]
    out_dtype = BF16 if fin is not None else F32
    return pl.pallas_call(
        functools.partial(_ssd_kernel, reverse=reverse, final=fin is not None, nc=nc, lane0=lane0),
        grid=(batch // nb, nc),
        in_specs=in_specs,
        out_specs=[
            pl.BlockSpec((nb, q, width), lambda b, c: (b, cidx(c) - off, 0)),
            pl.BlockSpec((nb,) + h0.shape[1:], lambda b, c: (b, 0, 0, 0)),
        ],
        out_shape=[
            jax.ShapeDtypeStruct((batch, n, width), out_dtype),
            jax.ShapeDtypeStruct(h0.shape, F32),
        ],
        scratch_shapes=[pltpu.VMEM((nb,) + h0.shape[1:], F32)],
        compiler_params=_cparams(("arbitrary", "arbitrary")),
        name="ssd_bwd_final" if reverse else "ssd_fwd",
    )(*args)


def _merge_kernel(x_ref, sc_ref, sh_ref, g1_ref, gn_ref, fl_ref, fc_ref, hl_ref, hc_ref, p_ref, sl_ref,
                  scx_ref, wg_ref, wb_ref, wo_ref, o_ref, *, tok, splits):
    ctx = tok.is_ctx(pl.program_id(0))
    x = x_ref[...]
    hb = _norm_mod(x, gn_ref[...], sc_ref[...], sh_ref[...]).astype(BF16)
    ys = (jnp.where(ctx, fc_ref[...], fl_ref[...]), jnp.where(ctx, hc_ref[...], hl_ref[...]),
          p_ref[...], jnp.where(ctx, scx_ref[...], sl_ref[...]))
    merged = None
    lo = 0
    for k, hi in enumerate(splits):
        gate = _sigmoid(_dot(hb, wg_ref[k]))
        term = gate * _dot(ys[k], wb_ref[lo:hi, :])
        merged = term if merged is None else merged + term
        lo = hi
    out = _dot(merged.astype(BF16), wo_ref[...])
    o_ref[...] = x + g1_ref[...] * out


def _merge(tok, layer, x_all, mod, norm_mix, f_lat, f_ctx, h_lat, h_ctx, pool, s_lat, s_ctx,
           w_gate, w_branch, w_out, splits):
    d = x_all.shape[-1]
    tm = TOKEN_TILE
    fw, hw, pw, sw = f_lat.shape[-1], h_lat.shape[-1], pool.shape[-1], s_lat.shape[-1]
    return pl.pallas_call(
        functools.partial(_merge_kernel, tok=tok, splits=splits),
        grid=(tok.n_tiles,),
        in_specs=[
            pl.BlockSpec((tm, d), lambda t: (t, 0)),
            tok.mod_spec(layer, 1, d),
            tok.mod_spec(layer, 0, d),
            tok.mod_spec(layer, 2, d),
            pl.BlockSpec((None, 1, d), lambda t: (layer, 0, 0)),
            tok.lat_spec(fw), tok.ctx_spec(fw),
            tok.lat_spec(hw), tok.ctx_spec(hw),
            pl.BlockSpec((tm, pw), lambda t: (t, 0)),
            tok.lat_spec(sw), tok.ctx_spec(sw),
            pl.BlockSpec((None,) + w_gate.shape[1:], lambda t: (layer, 0, 0, 0)),
            pl.BlockSpec((None,) + w_branch.shape[1:], lambda t: (layer, 0, 0)),
            pl.BlockSpec((None,) + w_out.shape[1:], lambda t: (layer, 0, 0)),
        ],
        out_specs=pl.BlockSpec((tm, d), lambda t: (t, 0)),
        out_shape=jax.ShapeDtypeStruct(x_all.shape, F32),
        compiler_params=_cparams(("arbitrary",)),
        name="merge",
    )(x_all, mod, mod, mod, norm_mix, f_lat, f_ctx, h_lat, h_ctx, pool, s_lat, s_ctx,
      w_gate, w_branch, w_out)


def _router_kernel(x_ref, sc_ref, sh_ref, gn_ref, rwh_ref, rwl_ref, rb_ref, utri_ref,
                   h_ref, e_ref, r_ref, w_ref, cnt_ref):
    t = pl.program_id(0)

    @pl.when(t == 0)
    def _():
        cnt_ref[...] = jnp.zeros_like(cnt_ref)

    h = _norm_mod(x_ref[...], gn_ref[...], sc_ref[...], sh_ref[...])
    hh = h.astype(BF16)
    h_ref[...] = hh
    hl = (h - hh.astype(F32)).astype(BF16)
    tm = h.shape[0]
    n_exp = rwh_ref.shape[0]
    nt = (((1,), (1,)), ((), ()))
    dg = lambda a, b: lax.dot_general(a, b, nt, preferred_element_type=F32)
    rwh = rwh_ref[...]
    logits = dg(rwh, hh) + dg(rwh, hl) + dg(rwl_ref[...], hh) + rb_ref[:, 0:1]
    eid = lax.broadcasted_iota(jnp.int32, (n_exp, tm), 0)
    carry = cnt_ref[:, 0:1]
    tops, sels = [], []
    cur = logits
    for _ in range(TOP_K):
        m = jnp.max(cur, axis=0, keepdims=True)
        idx = jnp.min(jnp.where(cur == m, eid, n_exp), axis=0, keepdims=True)
        sel = eid == idx
        tops.append((m, idx))
        sels.append(sel)
        cur = jnp.where(sel, -jnp.inf, cur)
    cnt = jnp.zeros((n_exp, tm), F32)
    for s in sels:
        cnt = jnp.where(s, 1.0, cnt)
    before = _dot(cnt.astype(BF16), utri_ref[...]) + carry
    exps = [jnp.exp(m - tops[0][0]) for m, _ in tops]
    den = exps[0]
    for v in exps[1:]:
        den = den + v
    row8 = lax.broadcasted_iota(jnp.int32, (8, tm), 0)
    rowl = lax.broadcasted_iota(jnp.int32, (LANE, tm), 0)
    e_out = jnp.zeros((8, tm), jnp.int32)
    r_out = jnp.zeros((8, tm), jnp.int32)
    w_t = jnp.zeros((LANE, tm), F32)
    for k in range(TOP_K):
        rank = jnp.sum(jnp.where(sels[k], before, 0.0), axis=0, keepdims=True)
        e_out = jnp.where(row8 == k, tops[k][1], e_out)
        r_out = jnp.where(row8 == k, rank.astype(jnp.int32), r_out)
        w_t = jnp.where(rowl == k, exps[k] / den, w_t)
    e_ref[...] = e_out
    r_ref[...] = r_out
    w_ref[...] = w_t.T
    cnt_ref[...] += jnp.broadcast_to(jnp.sum(cnt, axis=1, keepdims=True), cnt_ref.shape)


def _router(tok, layer, x_all, mod, norm_ffn, rw_hi, rw_lo, router_b_col, utri):
    d = x_all.shape[-1]
    tm = TOKEN_TILE
    n_tok = tok.n_tok
    n_exp = rw_hi.shape[1]
    row_i = jax.ShapeDtypeStruct((8, n_tok), jnp.int32)
    return pl.pallas_call(
        _router_kernel,
        grid=(tok.n_tiles,),
        in_specs=[
            pl.BlockSpec((tm, d), lambda t: (t, 0)),
            tok.mod_spec(layer, 4, d),
            tok.mod_spec(layer, 3, d),
            pl.BlockSpec((None, 1, d), lambda t: (layer, 0, 0)),
            pl.BlockSpec((None, n_exp, d), lambda t: (layer, 0, 0)),
            pl.BlockSpec((None, n_exp, d), lambda t: (layer, 0, 0)),
            pl.BlockSpec((None, n_exp, LANE), lambda t: (layer, 0, 0)),
            pl.BlockSpec((tm, tm), lambda t: (0, 0)),
        ],
        out_specs=[
            pl.BlockSpec((tm, d), lambda t: (t, 0)),
            pl.BlockSpec((8, tm), lambda t: (0, t)),
            pl.BlockSpec((8, tm), lambda t: (0, t)),
            pl.BlockSpec((tm, LANE), lambda t: (t, 0)),
            pl.BlockSpec((n_exp, LANE), lambda t: (0, 0)),
        ],
        out_shape=[
            jax.ShapeDtypeStruct((n_tok, d), BF16),
            row_i,
            row_i,
            jax.ShapeDtypeStruct((n_tok, LANE), F32),
            jax.ShapeDtypeStruct((n_exp, LANE), F32),
        ],
        compiler_params=_cparams(("arbitrary",)),
        name="router",
    )(x_all, mod, mod, norm_ffn, rw_hi, rw_lo, router_b_col, utri)


def _moe_kernel(be_ref, na_ref, x_ref, wu_ref, bu_ref, wd_ref, bd_ref, o_ref, wu_scr, wd_scr, *, ff):
    i = pl.program_id(0)
    active = i < na_ref[0]
    fresh = jnp.logical_or(i == 0, be_ref[i] != be_ref[jnp.maximum(i - 1, 0)])

    @pl.when(jnp.logical_and(active, fresh))
    def _():
        wu_scr[...] = wu_ref[...].astype(BF16)
        wd_scr[...] = wd_ref[...].astype(BF16)

    @pl.when(active)
    def _():
        hu = _dot(x_ref[...], wu_scr[...]) + bu_ref[...]
        glu = jnp.minimum(hu[:, :ff], SWIGLU_LIMIT)
        lin = jnp.clip(hu[:, ff:], -SWIGLU_LIMIT, SWIGLU_LIMIT)
        act = glu * _sigmoid(SWIGLU_ALPHA * glu) * (lin + 1.0)
        o_ref[...] = (_dot(act.astype(BF16), wd_scr[...]) + bd_ref[...]).astype(o_ref.dtype)

    @pl.when(jnp.logical_not(active))
    def _():
        o_ref[...] = jnp.zeros_like(o_ref)


def _moe(layer, x_sorted, blk_e, n_active, w_up, b_up, w_down, b_down):
    n_rows, d = x_sorted.shape
    ff = w_down.shape[2]
    bm = MOE_ROWS
    grid_spec = pltpu.PrefetchScalarGridSpec(
        num_scalar_prefetch=2,
        grid=(n_rows // bm,),
        in_specs=[
            pl.BlockSpec((bm, d), lambda i, be, na: (i, 0)),
            pl.BlockSpec((None, None, d, 2 * ff), lambda i, be, na: (layer, be[i], 0, 0)),
            pl.BlockSpec((None, None, 1, 2 * ff), lambda i, be, na: (layer, be[i], 0, 0)),
            pl.BlockSpec((None, None, ff, d), lambda i, be, na: (layer, be[i], 0, 0)),
            pl.BlockSpec((None, None, 1, d), lambda i, be, na: (layer, be[i], 0, 0)),
        ],
        out_specs=pl.BlockSpec((bm, d), lambda i, be, na: (i, 0)),
        scratch_shapes=[pltpu.VMEM((d, 2 * ff), BF16), pltpu.VMEM((ff, d), BF16)],
    )
    return pl.pallas_call(
        functools.partial(_moe_kernel, ff=ff),
        grid_spec=grid_spec,
        out_shape=jax.ShapeDtypeStruct((n_rows, d), BF16),
        compiler_params=_cparams(("arbitrary",)),
        name="moe_experts",
    )(blk_e, n_active, x_sorted, w_up, b_up, w_down, b_down)


def _final_kernel(x_ref, g2_ref, yg_ref, w_ref, g_ref, o_ref):
    x = _moe_residual(x_ref[...], g2_ref[...], yg_ref, w_ref)
    ms = jnp.mean(x * x, axis=-1, keepdims=True)
    o_ref[...] = (x * lax.rsqrt(ms + EPS)) * g_ref[...]


def _final(tok, layer, x_all, mod, y_gath, top_w, gain):
    batch, seq, n_lat = tok.batch, tok.seq, tok.n_lat
    d = x_all.shape[-1]
    tm = TOKEN_TILE
    return pl.pallas_call(
        _final_kernel,
        grid=(batch, n_lat // tm),
        in_specs=[
            pl.BlockSpec((None, tm, d), lambda b, i: (b, i, 0)),
            pl.BlockSpec((None, None, None, 1, d), lambda b, i: (layer, tok.mod_off + b, 5, 0, 0)),
            pl.BlockSpec((TOP_K, None, tm, d), lambda b, i: (0, b, i, 0)),
            pl.BlockSpec((None, tm, LANE), lambda b, i: (b, i, 0)),
            pl.BlockSpec((1, d), lambda b, i: (0, 0)),
        ],
        out_specs=pl.BlockSpec((None, tm, d), lambda b, i: (b, i, 0)),
        out_shape=jax.ShapeDtypeStruct((batch, n_lat, d), F32),
        compiler_params=_cparams(("arbitrary", "arbitrary")),
        name="final_combine_norm",
    )(x_all.reshape(batch, seq, d), mod, y_gath.reshape(TOP_K, batch, seq, d),
      top_w.reshape(batch, seq, LANE), gain)


def _dispatch_plan(top_e, rank, counts, n_tok):
    n_exp = counts.shape[0]
    bm = MOE_ROWS
    n_pairs = n_tok * TOP_K
    n_blk = -(-(n_pairs + n_exp * (bm - 1)) // bm)
    padded = (counts + bm - 1) // bm * bm
    pad_end = jnp.cumsum(padded)
    pad_start = pad_end - padded
    dest = rank
    for e in range(n_exp):
        dest = dest + jnp.where(top_e == e, pad_start[e], 0)
    n_active = (pad_end[-1] // bm).astype(jnp.int32).reshape(1)
    blk_start = jnp.arange(n_blk, dtype=jnp.int32) * bm
    blk_e = jnp.minimum(jnp.sum(pad_end[None, :] <= blk_start[:, None], axis=1), n_exp - 1).astype(jnp.int32)
    tok_ids = jnp.broadcast_to(jnp.arange(n_tok, dtype=jnp.int32)[None, :], dest.shape)
    row_tok = (jnp.arange(n_blk * bm, dtype=jnp.int32) % n_tok).at[dest.reshape(-1)].set(
        tok_ids.reshape(-1), unique_indices=True, mode='promise_in_bounds')
    return dest, row_tok, blk_e, n_active


def kernel(x, c, ctx, c_ctx, w_mod, b_mod, norm_mix, norm_ffn, w_in, hy_conv_w, hy_conv_b, hy_ffn_w1, hy_ffn_b1, hy_ffn_w2, hy_ffn_b2, hy_ffn_w3, hy_bias, pool_w, pool_scale, ssm_conv_w, ssm_conv_b, ssm_dt_bias, ssm_a_log, ssm_d, ssm_norm, w_branch, w_gate, w_out, router_w, router_b, exp_w_up, exp_b_up, exp_w_down, exp_b_down, norm_final):
    batch, n_lat, d = x.shape
    n_ctx = ctx.shape[1]
    depth = w_mod.shape[0]
    assert n_lat // GRID_W * GRID_W == n_lat and n_lat % n_ctx == 0

    hw = hy_bias.shape[1]
    pw = pool_scale.shape[1]
    sw = ssm_norm.shape[1]
    cw = ssm_conv_w.shape[2]
    fw = w_in.shape[2] - 3 * hw - pw - sw - cw - 2 * SSM_HEADS
    widths = (fw, 3 * hw, pw, sw, cw)
    splits = (fw, fw + hw, fw + hw + pw, fw + hw + pw + sw)
    n_exp = router_w.shape[2]

    w_in_p = jnp.pad(w_in, ((0, 0), (0, 0), (0, LANE - 2 * SSM_HEADS))).astype(BF16)
    w_gate_b = w_gate.astype(BF16)
    w_branch_b = w_branch.astype(BF16)
    w_out_b = w_out.astype(BF16)
    pool_w_bd = jax.vmap(_block_diag)(pool_w).astype(BF16)
    r3 = lambda a: a.reshape(a.shape[0], 1, a.shape[1])
    lane_pad = lambda a: jnp.pad(a, ((0, 0), (0, LANE - a.shape[1])))
    norm_mix3, norm_ffn3 = r3(norm_mix), r3(norm_ffn)
    hy_conv_b3, hy_bias3 = r3(hy_conv_b), r3(hy_bias)
    ssm_conv_b3 = r3(ssm_conv_b)
    pool_scale3 = r3(pool_scale)
    dt_bias3 = r3(lane_pad(ssm_dt_bias.reshape(depth, 2 * SSM_HEADS)))
    a_row3 = r3(lane_pad(-jnp.exp(ssm_a_log.astype(F32)).reshape(depth, 2 * SSM_HEADS)))
    d_row3 = r3(jnp.repeat(ssm_d, SSM_HEAD_DIM, axis=1))
    ssm_norm3 = r3(ssm_norm)
    rw_t = jnp.swapaxes(router_w, 1, 2)
    rw_hi = rw_t.astype(BF16)
    rw_lo = (rw_t - rw_hi.astype(F32)).astype(BF16)
    router_b_col = jnp.broadcast_to(router_b[:, :, None], (depth, n_exp, LANE))
    exp_b_up4 = exp_b_up.reshape(depth, n_exp, 1, -1)
    exp_b_down4 = exp_b_down.reshape(depth, n_exp, 1, -1)
    emb = hy_ffn_w1.shape[1]
    ffn = hy_ffn_w1.shape[2]
    f_w1 = jnp.pad(hy_ffn_w1, ((0, 0), (0, LANE - emb), (0, LANE - ffn)))
    f_b1 = r3(lane_pad(hy_ffn_b1))
    f_w2 = jnp.pad(hy_ffn_w2, ((0, 0), (0, LANE - ffn), (0, LANE - ffn)))
    f_b2 = r3(lane_pad(hy_ffn_b2))
    f_w3 = jnp.pad(hy_ffn_w3, ((0, 0), (0, LANE - ffn), (0, 0)))

    freqs = jnp.linspace(1e-4, HYENA_BANDS - 1, HYENA_BANDS, dtype=F32)
    freqs_row = jnp.concatenate([jnp.zeros((1,), F32), freqs, freqs,
                                 jnp.zeros((LANE - 1 - 2 * HYENA_BANDS,), F32)])[None, :]
    deltas_row = jnp.linspace(HYENA_MIN_DECAY, HYENA_MAX_DECAY, hw, dtype=F32)[None, :]
    gi = jnp.arange(FOURIER_GROUP_W)
    ang = (gi[:, None] * gi[None, :] % FOURIER_GROUP_W).astype(F32) * (2.0 * math.pi / FOURIER_GROUP_W)
    n_fg = fw // FOURIER_GROUP_W
    eye = jnp.eye(n_fg, dtype=F32)
    wc = jnp.concatenate([jnp.kron(eye, jnp.cos(ang)), jnp.kron(eye, jnp.sin(ang))], axis=1).astype(BF16)
    band, invc = _pool_tables(n_lat, n_ctx)
    qi = jnp.arange(SSM_CHUNK)
    tri_f = (qi[:, None] >= qi[None, :]).astype(BF16)
    tri_b = (qi[:, None] <= qi[None, :]).astype(BF16)
    ti = jnp.arange(TOKEN_TILE)
    utri = (ti[:, None] < ti[None, :]).astype(BF16)

    seqs = {}
    for name, n, row_off in (("ctx", n_ctx, n_lat), ("lat", n_lat, 0)):
        cn, sn = _trig_tables(n, n)
        fwd, inv = _hyena_dft_tables(n)
        g_all, nrm = _filter_mlp(n, freqs_row, deltas_row, f_w1, f_b1, f_w2, f_b2, f_w3)
        kr, ki = _filter_spectrum(fwd, g_all, nrm, n, hw)
        seqs[name] = dict(n=n, off=row_off, cn=cn.astype(BF16), sn=sn.astype(BF16), fwd=fwd, inv=inv,
                          kr=kr, ki=ki)

    cc = jnp.concatenate([c, c_ctx[None, :]], axis=0)
    rows = -(-cc.shape[0] // 8) * 8
    mod = _modulation(jnp.pad(cc, ((0, rows - cc.shape[0]), (0, 0))), w_mod, b_mod)
    mod = mod.reshape(depth, rows, 6, 1, d)

    seq = n_lat + n_ctx

    def layer_step(tok, layer, x_all, moe):
        gb, n_tok = tok.batch, tok.n_tok
        res = _inproj(tok, layer, x_all, mod, norm_mix3, w_in_p, wc, band, invc, pool_w_bd, pool_scale3,
                      widths, moe=moe)
        if moe is not None:
            x_all = res[0]
            res = res[1:]
        xcs, hy, pool, z, xbc, dtr = res
        xcs3 = xcs.reshape(gb, seq, 2 * fw)
        hy3 = hy.reshape(gb, seq, 3 * hw)
        z3 = z.reshape(gb, seq, sw)
        xbc3 = xbc.reshape(gb, seq, cw)
        dt3 = dtr.reshape(gb, seq, LANE)
        outs = {}
        zero_state = jnp.zeros((gb, SSM_GROUPS, SSM_STATE, HEADS_PER_GROUP * SSM_HEAD_DIM), F32)
        states = (zero_state, zero_state)
        for name in ("ctx", "lat"):
            s = seqs[name]
            n, off = s["n"], s["off"]
            y_f = _fourier_seq(xcs3, s["cn"], s["sn"], n, off, fw)
            vv, x1 = _hy_pre(hy3, hy_conv_w, hy_conv_b3, layer, n, off, hw)
            y2 = _hy_fwd(s["fwd"], vv, s["kr"], s["ki"], layer, n, hw)
            y_h = _hy_inv(s["inv"], y2, x1, vv, hy_bias3, layer, n, hw)
            yf, st_f = _ssd_pass(xbc3, dt3, states[0], tri_f, ssm_conv_w, ssm_conv_b3, dt_bias3, a_row3,
                                 layer, n, off, reverse=False)
            y_s, st_b = _ssd_pass(xbc3, dt3, states[1], tri_b, ssm_conv_w, ssm_conv_b3, dt_bias3, a_row3,
                                  layer, n, off, reverse=True, fin=(z3, yf, d_row3, ssm_norm3))
            states = (st_f, st_b)
            outs[name] = (y_f, y_h, y_s)
        x_all = _merge(tok, layer, x_all, mod, norm_mix3, outs["lat"][0], outs["ctx"][0], outs["lat"][1],
                       outs["ctx"][1], pool, outs["lat"][2], outs["ctx"][2], w_gate_b, w_branch_b, w_out_b,
                       splits)
        h2, top_e, rank, top_w, counts = _router(tok, layer, x_all, mod, norm_ffn3, rw_hi, rw_lo, router_b_col,
                                                 utri)
        dest, row_tok, blk_e, n_active = _dispatch_plan(top_e[:TOP_K], rank[:TOP_K],
                                                        counts[:, 0].astype(jnp.int32), n_tok)
        x_sorted = h2[row_tok]
        y_rows = _moe(layer, x_sorted, blk_e, n_active, exp_w_up, exp_b_up4, exp_w_down, exp_b_down4)
        y_gath = y_rows[dest.reshape(-1)].reshape(TOP_K, n_tok, d)
        return x_all, (y_gath, top_w)

    n_groups = SAMPLE_GROUPS if batch % SAMPLE_GROUPS == 0 else 1
    gb = batch // n_groups
    groups = []
    for g in range(n_groups):
        tok = _Tok(gb, n_lat, n_ctx, mod_off=g * gb, mod_ctx=batch)
        x_all = jnp.concatenate([x[g * gb:(g + 1) * gb], ctx[g * gb:(g + 1) * gb]], axis=1)
        groups.append([tok, x_all.reshape(tok.n_tok, d), None])
    for layer in range(depth):
        for grp in groups:
            grp[1], grp[2] = layer_step(grp[0], layer, grp[1], grp[2])
    outs = [_final(tok, depth - 1, x_all, mod, moe[0], moe[1], norm_final[None, :])
            for tok, x_all, moe in groups]
    return outs[0] if n_groups == 1 else jnp.concatenate(outs, axis=0)
```

```python
import functools
import math

import jax
import jax.numpy as jnp
from jax import lax
from jax.experimental import pallas as pl
from jax.experimental.pallas import tpu as pltpu

F32 = jnp.float32
BF16 = jnp.bfloat16

GRID_W = 64
EPS = 1e-6
FOURIER_GROUP_W = 64
HYENA_BANDS = 16
HYENA_MIN_DECAY = -math.log(1e-2) / 1.5
HYENA_MAX_DECAY = -math.log(1e-2) / 0.3
POOL_WINDOWS = (2, 4, 8, 16)
POOL_GROUP_W = 64
SSM_GROUPS = 2
HEADS_PER_GROUP = 4
SSM_HEADS = SSM_GROUPS * HEADS_PER_GROUP
SSM_HEAD_DIM = 64
SSM_STATE = 128
SSM_CHUNK = 128
TOP_K = 4
SWIGLU_LIMIT = 7.0
SWIGLU_ALPHA = 1.702

LANE = 128
BF16_SUBLANE = 16
TOKEN_TILE = 256
SEQ_TILE = 512
MOE_ROWS = 512
SSD_SAMPLES = 2
SAMPLE_GROUPS = 1
VMEM_LIMIT = 56 * 1024 * 1024


def _cparams(sem):
    return pltpu.CompilerParams(dimension_semantics=sem, vmem_limit_bytes=VMEM_LIMIT)


def _sigmoid(x):
    return 1.0 / (1.0 + jnp.exp(-x))


def _dot(a, b):
    return jnp.dot(a, b, preferred_element_type=F32)


def _dot_hi(a, b):
    return jnp.dot(a, b, preferred_element_type=F32, precision=lax.Precision.HIGHEST)


def _split3(x):
    x1 = x.astype(BF16)
    r1 = x - x1.astype(F32)
    x2 = r1.astype(BF16)
    x3 = (r1 - x2.astype(F32)).astype(BF16)
    return x1, x2, x3


def _dot_exact01(m01, x):
    x1, x2, x3 = _split3(x)
    return _dot(m01, x1) + _dot(m01, x2) + _dot(m01, x3)


def _mod_kernel(c_ref, w_ref, b_ref, o_ref):
    c = c_ref[...]
    s = c * _sigmoid(c)
    o_ref[...] = _dot_hi(s, w_ref[...]) + b_ref[...]


def _modulation(cc, w_mod, b_mod):
    depth, d, six_d = w_mod.shape
    rows = cc.shape[0]
    nj = six_d // d
    return pl.pallas_call(
        _mod_kernel,
        grid=(depth, nj),
        in_specs=[
            pl.BlockSpec((rows, d), lambda l, j: (0, 0)),
            pl.BlockSpec((None, d, d), lambda l, j: (l, 0, j)),
            pl.BlockSpec((None, 1, d), lambda l, j: (l, 0, j)),
        ],
        out_specs=pl.BlockSpec((None, rows, d), lambda l, j: (l, 0, j)),
        out_shape=jax.ShapeDtypeStruct((depth, rows, six_d), F32),
        compiler_params=_cparams(("arbitrary", "arbitrary")),
        name="modulation",
    )(cc, w_mod, b_mod.reshape(depth, 1, six_d))


class _Tok:
    def __init__(self, batch, n_lat, n_ctx, mod_off, mod_ctx):
        assert n_ctx == TOKEN_TILE and n_lat % TOKEN_TILE == 0
        self.batch, self.n_lat, self.n_ctx = batch, n_lat, n_ctx
        self.mod_off, self.mod_ctx = mod_off, mod_ctx
        self.seq = n_lat + n_ctx
        self.tiles_per_b = self.seq // TOKEN_TILE
        self.lat_tiles = n_lat // TOKEN_TILE
        self.n_tiles = batch * self.tiles_per_b
        self.n_tok = batch * self.seq

    def b(self, t):
        return t // self.tiles_per_b

    def w(self, t):
        return t % self.tiles_per_b

    def is_ctx(self, t):
        return (t % self.tiles_per_b) >= self.lat_tiles

    def mod_row(self, t):
        return jnp.where(self.is_ctx(t), self.mod_ctx, self.mod_off + self.b(t))

    def mod_spec(self, layer, part, d):
        return pl.BlockSpec((None, None, None, 1, d), lambda t: (layer, self.mod_row(t), part, 0, 0))

    def lat_spec(self, width):
        return pl.BlockSpec((None, TOKEN_TILE, width),
                            lambda t: (self.b(t), jnp.minimum(self.w(t), self.lat_tiles - 1), 0))

    def ctx_spec(self, width):
        return pl.BlockSpec((None, TOKEN_TILE, width), lambda t: (self.b(t), 0, 0))


def _norm_mod(x, gain, scale, shift):
    ms = jnp.mean(x * x, axis=-1, keepdims=True)
    return (x * lax.rsqrt(ms + EPS)) * gain * (1.0 + scale) + shift


def _moe_residual(x, g2, yg_ref, w_ref):
    w = w_ref[...]
    y = None
    for k in range(TOP_K):
        term = w[:, k:k + 1] * yg_ref[k].astype(F32)
        y = term if y is None else y + term
    return x + g2 * y


def _inproj_kernel(*refs, widths, combine):
    if combine:
        (x_ref, g2_ref, yg_ref, w_ref, sc_ref, sh_ref, g_ref, win_ref, wc_ref, band_ref, invc_ref, pw_ref,
         ps_ref, xn_ref, xcs_ref, hy_ref, pool_ref, z_ref, xbc_ref, dt_ref) = refs
        x = _moe_residual(x_ref[...], g2_ref[...], yg_ref, w_ref)
        xn_ref[...] = x
    else:
        (x_ref, sc_ref, sh_ref, g_ref, win_ref, wc_ref, band_ref, invc_ref, pw_ref, ps_ref,
         xcs_ref, hy_ref, pool_ref, z_ref, xbc_ref, dt_ref) = refs
        x = x_ref[...]
    fw, hw, pw, zw, cw = widths
    hb = _norm_mod(x, g_ref[...], sc_ref[...], sh_ref[...]).astype(BF16)
    o = 0
    f_in = _dot(hb, win_ref[:, o:o + fw])
    xcs_ref[...] = _dot(f_in.astype(BF16), wc_ref[...]).astype(BF16)
    o += fw
    hy_ref[...] = _dot(hb, win_ref[:, o:o + hw]).astype(BF16)
    o += hw
    u = _dot(hb, win_ref[:, o:o + pw])
    o += pw
    z_ref[...] = _dot(hb, win_ref[:, o:o + zw]).astype(BF16)
    o += zw
    xbc_ref[...] = _dot(hb, win_ref[:, o:o + cw]).astype(BF16)
    o += cw
    dt_ref[...] = _dot(hb, win_ref[:, o:o + LANE])
    gw = POOL_GROUP_W
    parts = []
    for gi in range(len(POOL_WINDOWS)):
        ug = u[:, gi * gw:(gi + 1) * gw]
        parts.append(_dot_exact01(band_ref[gi], ug))
    pooled = jnp.concatenate(parts, axis=1) * invc_ref[...] - u
    pool_ref[...] = (_dot(pooled.astype(BF16), pw_ref[...]) * ps_ref[...]).astype(BF16)


def _pool_tables(n_lat, n_ctx):
    tm = TOKEN_TILE
    p = jnp.arange(tm)
    bands, invs = [], []
    for row_len in (GRID_W, n_ctx):
        assert tm % row_len == 0
        pr, rr = p % row_len, p // row_len
        bg, ig = [], []
        for win in POOL_WINDOWS:
            lo = jnp.clip(pr - win // 2, 0, row_len)
            hi = jnp.clip(pr + win // 2, 0, row_len)
            q = pr[None, :]
            m = (rr[:, None] == rr[None, :]) & (q >= lo[:, None]) & (q < hi[:, None])
            bg.append(m.astype(BF16))
            ig.append(jnp.broadcast_to((1.0 / (hi - lo).astype(F32))[:, None], (tm, POOL_GROUP_W)))
        bands.append(jnp.stack(bg))
        invs.append(jnp.concatenate(ig, axis=1))
    return jnp.stack(bands), jnp.stack(invs)


def _block_diag(w):
    g, a, b = w.shape
    out = jnp.zeros((g * a, g * b), w.dtype)
    for i in range(g):
        out = out.at[i * a:(i + 1) * a, i * b:(i + 1) * b].set(w[i])
    return out


def _inproj(tok, layer, x_all, mod, norm_mix, w_in_p, wc, band, invc, pool_w_bd, pool_scale, widths,
            moe=None):
    d = x_all.shape[-1]
    fw, hw, pw, zw, cw = widths
    tm = TOKEN_TILE
    n_tok = tok.n_tok
    kind = lambda t: jnp.where(tok.is_ctx(t), 1, 0)
    const2 = lambda t: (0, 0)
    pre_specs, pre_args, pre_outs = [], [], []
    if moe is not None:
        pre_specs = [tok.mod_spec(layer - 1, 5, d),
                     pl.BlockSpec((TOP_K, tm, d), lambda t: (0, t, 0)),
                     pl.BlockSpec((tm, LANE), lambda t: (t, 0))]
        pre_args = [mod, moe[0], moe[1]]
        pre_outs = [jax.ShapeDtypeStruct((n_tok, d), F32)]
    outs = pre_outs + [
        jax.ShapeDtypeStruct((n_tok, 2 * fw), BF16),
        jax.ShapeDtypeStruct((n_tok, hw), BF16),
        jax.ShapeDtypeStruct((n_tok, pw), BF16),
        jax.ShapeDtypeStruct((n_tok, zw), BF16),
        jax.ShapeDtypeStruct((n_tok, cw), BF16),
        jax.ShapeDtypeStruct((n_tok, LANE), F32),
    ]
    return pl.pallas_call(
        functools.partial(_inproj_kernel, widths=widths, combine=moe is not None),
        grid=(tok.n_tiles,),
        in_specs=[pl.BlockSpec((tm, d), lambda t: (t, 0))] + pre_specs + [
            tok.mod_spec(layer, 1, d),
            tok.mod_spec(layer, 0, d),
            pl.BlockSpec((None, 1, d), lambda t: (layer, 0, 0)),
            pl.BlockSpec((None,) + w_in_p.shape[1:], lambda t: (layer, 0, 0)),
            pl.BlockSpec(wc.shape, const2),
            pl.BlockSpec((None,) + band.shape[1:], lambda t: (kind(t), 0, 0, 0)),
            pl.BlockSpec((None,) + invc.shape[1:], lambda t: (kind(t), 0, 0)),
            pl.BlockSpec((None,) + pool_w_bd.shape[1:], lambda t: (layer, 0, 0)),
            pl.BlockSpec((None, 1, pw), lambda t: (layer, 0, 0)),
        ],
        out_specs=[pl.BlockSpec((tm, s.shape[1]), lambda t: (t, 0)) for s in outs],
        out_shape=outs,
        compiler_params=_cparams(("arbitrary",)),
        name="inproj",
    )(x_all, *pre_args, mod, mod, norm_mix, w_in_p, wc, band, invc, pool_w_bd, pool_scale)


def _trig_tables(n, period):
    k = jnp.arange(n, dtype=jnp.int32)

    def cs(j):
        m = (j[:, None] * k[None, :]) % period
        ang = m.astype(F32) * (2.0 * math.pi / period)
        return jnp.cos(ang), jnp.sin(ang)

    split = 64
    if n % split or n <= split:
        return cs(k)
    c1, s1 = cs(jnp.arange(n // split, dtype=jnp.int32) * split)
    c2, s2 = cs(jnp.arange(split, dtype=jnp.int32))
    c = c1[:, None, :] * c2[None, :, :] - s1[:, None, :] * s2[None, :, :]
    s = s1[:, None, :] * c2[None, :, :] + c1[:, None, :] * s2[None, :, :]
    return c.reshape(n, n), s.reshape(n, n)


def _hyena_dft_tables(n):
    c, s = _trig_tables(n, 2 * n)
    nyq = jnp.where(jnp.arange(n) % 2 == 0, 1.0, -1.0).astype(F32)
    s = (-s).at[0].set(nyq)
    fwd = jnp.concatenate([c, s], axis=0).astype(BF16)
    return fwd, fwd.T


def _fourier_kernel(c_ref, s_ref, xc_ref, xs_ref, o_ref, acc_ref, *, scale):
    k = pl.program_id(1)

    @pl.when(k == 0)
    def _():
        acc_ref[...] = jnp.zeros_like(acc_ref)

    nb = xc_ref.shape[0]
    rc = jnp.concatenate([xc_ref[b] for b in range(nb)], axis=1)
    rs = jnp.concatenate([xs_ref[b] for b in range(nb)], axis=1)
    acc_ref[...] += _dot(c_ref[...], rc) - _dot(s_ref[...], rs)

    @pl.when(k == pl.num_programs(1) - 1)
    def _():
        w = o_ref.shape[2]
        for b in range(nb):
            o_ref[b] = (acc_ref[:, b * w:(b + 1) * w] * scale).astype(o_ref.dtype)


def _fourier_seq(xcs3, cn, sn, n, row_off, fw):
    batch = xcs3.shape[0]
    tl = min(SEQ_TILE, n)
    nt = n // tl
    off = row_off // tl
    scale = 1.0 / math.sqrt(n * FOURIER_GROUP_W)
    return pl.pallas_call(
        functools.partial(_fourier_kernel, scale=scale),
        grid=(nt, nt),
        in_specs=[
            pl.BlockSpec((tl, tl), lambda i, k: (i, k)),
            pl.BlockSpec((tl, tl), lambda i, k: (i, k)),
            pl.BlockSpec((batch, tl, fw), lambda i, k: (0, off + k, 0)),
            pl.BlockSpec((batch, tl, fw), lambda i, k: (0, off + k, 1)),
        ],
        out_specs=pl.BlockSpec((batch, tl, fw), lambda i, k: (0, i, 0)),
        out_shape=jax.ShapeDtypeStruct((batch, n, fw), BF16),
        scratch_shapes=[pltpu.VMEM((tl, batch * fw), F32)],
        compiler_params=_cparams(("arbitrary", "arbitrary")),
        name="fourier_seq",
    )(cn, sn, xcs3, xcs3)


def _shift_rows(x, prev_row, next_row):
    n = x.shape[0]
    rows = lax.broadcasted_iota(jnp.int32, x.shape, 0)
    xm = jnp.where(rows == 0, prev_row, pltpu.roll(x, 1, 0))
    xp = jnp.where(rows == n - 1, next_row, pltpu.roll(x, n - 1, 0))
    return xm, xp


def _conv3(x, prev_row, next_row, w_ref, b_ref):
    xm, xp = _shift_rows(x, prev_row, next_row)
    return w_ref[0:1, :] * xm + w_ref[1:2, :] * x + w_ref[2:3, :] * xp + b_ref[...]


def _hy_pre_kernel(x_ref, xp_ref, xn_ref, w_ref, b_ref, vv_ref, x1_ref, *, hw):
    i = pl.program_id(1)
    last = pl.num_programs(1) - 1
    x = x_ref[...].astype(F32)
    hs = xp_ref.shape[0]
    prev_row = jnp.where(i == 0, 0.0, xp_ref[hs - 1:hs, :].astype(F32))
    next_row = jnp.where(i == last, 0.0, xn_ref[0:1, :].astype(F32))
    u = _conv3(x, prev_row, next_row, w_ref, b_ref)
    x1_ref[...] = u[:, :hw].astype(BF16)
    vv_ref[...] = (u[:, hw:2 * hw] * u[:, 2 * hw:]).astype(BF16)


def _halo_specs(width, tile, row_off, seq_rows):
    hs = BF16_SUBLANE
    per = tile // hs
    base = row_off // hs
    top = seq_rows // hs - 1
    prev = pl.BlockSpec((None, hs, width), lambda b, i: (b, jnp.maximum(base + i * per - 1, 0), 0))
    nxt = pl.BlockSpec((None, hs, width), lambda b, i: (b, jnp.minimum(base + (i + 1) * per, top), 0))
    return prev, nxt


def _hy_pre(hy3, conv_w, conv_b, layer, n, row_off, hw):
    batch, seq, w3 = hy3.shape
    tl = min(SEQ_TILE, n)
    off = row_off // tl
    prev, nxt = _halo_specs(w3, tl, row_off, seq)
    out = jax.ShapeDtypeStruct((batch, n, hw), BF16)
    return pl.pallas_call(
        functools.partial(_hy_pre_kernel, hw=hw),
        grid=(batch, n // tl),
        in_specs=[
            pl.BlockSpec((None, tl, w3), lambda b, i: (b, off + i, 0)),
            prev, nxt,
            pl.BlockSpec((None,) + conv_w.shape[1:], lambda b, i: (layer, 0, 0)),
            pl.BlockSpec((None, 1, w3), lambda b, i: (layer, 0, 0)),
        ],
        out_specs=[pl.BlockSpec((None, tl, hw), lambda b, i: (b, i, 0))] * 2,
        out_shape=[out, out],
        compiler_params=_cparams(("arbitrary", "arbitrary")),
        name="hyena_pre",
    )(hy3, hy3, hy3, conv_w, conv_b)


def _filter_mlp_kernel(fr_ref, dl_ref, w1_ref, b1_ref, w2_ref, b2_ref, w3_ref, g_ref, nrm_ref, *, n, hw):
    j = pl.program_id(1)
    rt = g_ref.shape[0]
    lane = lax.broadcasted_iota(jnp.int32, (rt, LANE), 1)
    pos = (lax.broadcasted_iota(jnp.int32, (rt, LANE), 0) + j * rt).astype(F32)
    t = pos / (n - 1)
    ang = 2.0 * math.pi * pos / n
    arg = ang * fr_ref[...]
    feats = jnp.where(lane == 0, t,
                      jnp.where(lane <= HYENA_BANDS, jnp.cos(arg),
                                jnp.where(lane <= 2 * HYENA_BANDS, -jnp.sin(arg), 0.0)))
    h1 = jnp.sin(_dot_hi(feats, w1_ref[...]) + b1_ref[...])
    h2 = jnp.sin(_dot_hi(h1, w2_ref[...]) + b2_ref[...])
    k = _dot_hi(h2, w3_ref[...])
    decay = jnp.exp(-t[:, 0:1] * dl_ref[...])
    kf = k[:, :hw] * decay
    kb = jnp.where(pos[:, 0:1] == 0.0, 0.0, k[:, hw:] * decay)
    g_ref[:, :hw] = (kf + kb).astype(BF16)
    g_ref[:, hw:] = (kf - kb).astype(BF16)
    part = jnp.sum(jnp.abs(kf) + jnp.abs(kb), axis=0, keepdims=True)

    @pl.when(j == 0)
    def _():
        nrm_ref[...] = part

    @pl.when(j > 0)
    def _():
        nrm_ref[...] += part


def _filter_mlp(n, freqs_row, deltas_row, w1, b1, w2, b2, w3):
    depth = w1.shape[0]
    hw = deltas_row.shape[1]
    rt = min(SEQ_TILE, n)
    per_layer = lambda l, j: (l, 0, 0)
    return pl.pallas_call(
        functools.partial(_filter_mlp_kernel, n=n, hw=hw),
        grid=(depth, n // rt),
        in_specs=[
            pl.BlockSpec((1, LANE), lambda l, j: (0, 0)),
            pl.BlockSpec((1, hw), lambda l, j: (0, 0)),
            pl.BlockSpec((None,) + w1.shape[1:], per_layer),
            pl.BlockSpec((None,) + b1.shape[1:], per_layer),
            pl.BlockSpec((None,) + w2.shape[1:], per_layer),
            pl.BlockSpec((None,) + b2.shape[1:], per_layer),
            pl.BlockSpec((None,) + w3.shape[1:], per_layer),
        ],
        out_specs=[
            pl.BlockSpec((rt, 2 * hw), lambda l, j: (j, l)),
            pl.BlockSpec((None, 1, hw), lambda l, j: (l, 0, 0)),
        ],
        out_shape=[
            jax.ShapeDtypeStruct((n, depth * 2 * hw), BF16),
            jax.ShapeDtypeStruct((depth, 1, hw), F32),
        ],
        compiler_params=_cparams(("arbitrary", "arbitrary")),
        name="hyena_filter_mlp",
    )(freqs_row, deltas_row, w1, b1, w2, b2, w3)


def _rdft_accumulate(fc_ref, fs_ref, rhs, accc_ref, accs_ref):
    k = pl.program_id(1)

    @pl.when(k == 0)
    def _():
        accc_ref[...] = jnp.zeros_like(accc_ref)
        accs_ref[...] = jnp.zeros_like(accs_ref)

    accc_ref[...] += _dot(fc_ref[...], rhs)
    accs_ref[...] += _dot(fs_ref[...], rhs)


def _filter_spec_kernel(fc_ref, fs_ref, g_ref, nrm_ref, kr_ref, ki_ref, accc_ref, accs_ref, *, hw):
    i = pl.program_id(0)
    _rdft_accumulate(fc_ref, fs_ref, g_ref[...], accc_ref, accs_ref)

    @pl.when(pl.program_id(1) == pl.num_programs(1) - 1)
    def _():
        tm = accc_ref.shape[0]
        row0 = (lax.broadcasted_iota(jnp.int32, (tm, hw), 0) + i * tm) == 0
        for l in range(kr_ref.shape[0]):
            inv = 1.0 / nrm_ref[l]
            o = l * 2 * hw
            kr_ref[l] = accc_ref[:, o:o + hw] * inv
            ki_ref[l] = jnp.where(row0, accs_ref[:, o:o + hw], accs_ref[:, o + hw:o + 2 * hw]) * inv


def _filter_spectrum(fwd, g_all, nrm, n, hw):
    depth = nrm.shape[0]
    tl = min(SEQ_TILE, n)
    nt = n // tl
    width = g_all.shape[1]
    out = jax.ShapeDtypeStruct((depth, n, hw), F32)
    return pl.pallas_call(
        functools.partial(_filter_spec_kernel, hw=hw),
        grid=(nt, nt),
        in_specs=[
            pl.BlockSpec((tl, tl), lambda i, k: (i, k)),
            pl.BlockSpec((tl, tl), lambda i, k: (nt + i, k)),
            pl.BlockSpec((tl, width), lambda i, k: (k, 0)),
            pl.BlockSpec(nrm.shape, lambda i, k: (0, 0, 0)),
        ],
        out_specs=[pl.BlockSpec((depth, tl, hw), lambda i, k: (0, i, 0))] * 2,
        out_shape=[out, out],
        scratch_shapes=[pltpu.VMEM((tl, width), F32)] * 2,
        compiler_params=_cparams(("arbitrary", "arbitrary")),
        name="hyena_filter_spectrum",
    )(fwd, fwd, g_all, nrm)


def _hy_fwd_kernel(fc_ref, fs_ref, vv_ref, kr_ref, ki_ref, y_ref, accc_ref, accs_ref):
    i = pl.program_id(0)
    nb = vv_ref.shape[0]
    rhs = jnp.concatenate([vv_ref[b] for b in range(nb)], axis=1)
    _rdft_accumulate(fc_ref, fs_ref, rhs, accc_ref, accs_ref)

    @pl.when(pl.program_id(1) == pl.num_programs(1) - 1)
    def _():
        tm, hw = kr_ref.shape
        row0 = (lax.broadcasted_iota(jnp.int32, (tm, hw), 0) + i * tm) == 0
        kr, ki = kr_ref[...], ki_ref[...]
        for b in range(nb):
            vr = accc_ref[:, b * hw:(b + 1) * hw]
            vi = accs_ref[:, b * hw:(b + 1) * hw]
            yr = jnp.where(row0, 0.5 * vr * kr, vr * kr - vi * ki)
            yi = jnp.where(row0, 0.5 * vi * ki, vr * ki + vi * kr)
            y_ref[0, :, b * hw:(b + 1) * hw] = yr.astype(BF16)
            y_ref[1, :, b * hw:(b + 1) * hw] = yi.astype(BF16)


def _hy_fwd(fwd, vv, kr, ki, layer, n, hw):
    batch = vv.shape[0]
    tl = min(SEQ_TILE, n)
    nt = n // tl
    return pl.pallas_call(
        _hy_fwd_kernel,
        grid=(nt, nt),
        in_specs=[
            pl.BlockSpec((tl, tl), lambda i, k: (i, k)),
            pl.BlockSpec((tl, tl), lambda i, k: (nt + i, k)),
            pl.BlockSpec((batch, tl, hw), lambda i, k: (0, k, 0)),
            pl.BlockSpec((None, tl, hw), lambda i, k: (layer, i, 0)),
            pl.BlockSpec((None, tl, hw), lambda i, k: (layer, i, 0)),
        ],
        out_specs=pl.BlockSpec((2, tl, batch * hw), lambda i, k: (0, i, 0)),
        out_shape=jax.ShapeDtypeStruct((2, n, batch * hw), BF16),
        scratch_shapes=[pltpu.VMEM((tl, batch * hw), F32)] * 2,
        compiler_params=_cparams(("arbitrary", "arbitrary")),
        name="hyena_fwd_dft",
    )(fwd, fwd, vv, kr, ki)


def _hy_inv_kernel(fi_ref, y_ref, x1_ref, vv_ref, bias_ref, o_ref, acc_ref, *, scale):
    k = pl.program_id(1)

    @pl.when(k == 0)
    def _():
        acc_ref[...] = jnp.zeros_like(acc_ref)

    acc_ref[...] += _dot(fi_ref[...], y_ref[...])

    @pl.when(k == pl.num_programs(1) - 1)
    def _():
        nb, _, hw = x1_ref.shape
        for b in range(nb):
            conv = acc_ref[:, b * hw:(b + 1) * hw] * scale
            vv = vv_ref[b].astype(F32)
            o_ref[b] = (x1_ref[b].astype(F32) * (conv + vv * bias_ref[...])).astype(BF16)


def _hy_inv(inv, y2, x1, vv, hy_bias, layer, n, hw):
    batch = vv.shape[0]
    tl = min(SEQ_TILE, n)
    nt = n // tl
    return pl.pallas_call(
        functools.partial(_hy_inv_kernel, scale=1.0 / n),
        grid=(nt, 2 * nt),
        in_specs=[
            pl.BlockSpec((tl, tl), lambda i, k: (i, k)),
            pl.BlockSpec((tl, batch * hw), lambda i, k: (k, 0)),
            pl.BlockSpec((batch, tl, hw), lambda i, k: (0, i, 0)),
            pl.BlockSpec((batch, tl, hw), lambda i, k: (0, i, 0)),
            pl.BlockSpec((None, 1, hw), lambda i, k: (layer, 0, 0)),
        ],
        out_specs=pl.BlockSpec((batch, tl, hw), lambda i, k: (0, i, 0)),
        out_shape=jax.ShapeDtypeStruct((batch, n, hw), BF16),
        scratch_shapes=[pltpu.VMEM((tl, batch * hw), F32)],
        compiler_params=_cparams(("arbitrary", "arbitrary")),
        name="hyena_inv_dft",
    )(inv, y2.reshape(2 * n, batch * hw), x1, vv, hy_bias)


RADIX = 64
RADIX_STEP = 8
MIN_FACTORED_ROWS = 16


def _use_factored(n):
    return n % RADIX == 0 and n // RADIX >= MIN_FACTORED_ROWS and (n // RADIX) % MIN_FACTORED_ROWS == 0


def _cs(rows, cols, period):
    m = (jnp.arange(rows, dtype=jnp.int32)[:, None] * jnp.arange(cols, dtype=jnp.int32)[None, :]) % period
    ang = m.astype(F32) * (2.0 * math.pi / period)
    return jnp.cos(ang), jnp.sin(ang)


def _lane_rep(t, width):
    return jnp.concatenate([t] * (width // LANE), axis=1)


def _factored_tables(n):
    r, m = RADIX, n // RADIX
    cm, sm = _cs(m, m, m)
    cr, sr = _cs(r, r, r)
    c2, s2 = _cs(2 * m, m, 2 * m)
    bc = lambda t: jnp.broadcast_to(t[:, :, None], t.shape + (LANE,))
    tfc, tfs = _cs(r, m, n)
    thc, ths = _cs(r, 2 * m, 2 * n)
    return dict(
        f_a=jnp.block([[cm, -sm], [-sm, -cm]]).astype(BF16),
        f_b=jnp.concatenate([cr, sr], axis=1).astype(BF16),
        f_tw=(bc(tfc), bc(tfs)),
        h_a=jnp.concatenate([c2, -s2], axis=0).astype(BF16),
        h_b=jnp.block([[cr, sr], [-sr, cr]]).astype(BF16),
        h_c=jnp.block([[cr, -sr], [sr, cr]]).astype(BF16),
        h_d=jnp.concatenate([c2.T, -s2.T], axis=1).astype(BF16),
        h_tw=(bc(thc), bc(ths)),
    )


def _fa_kernel(x_ref, ma_ref, tc_ref, ts_ref, z_ref, *, fw):
    m = x_ref.shape[0]
    for r in range(RADIX_STEP):
        a = x_ref[:, r * 2 * fw:r * 2 * fw + fw]
        b = x_ref[:, r * 2 * fw + fw:(r + 1) * 2 * fw]
        y = _dot(ma_ref[...], jnp.concatenate([a, b], axis=0))
        yre, yim = y[:m], y[m:]
        tc, ts = _lane_rep(tc_ref[r], fw), _lane_rep(ts_ref[r], fw)
        z_ref[r, :m, :] = (tc * yre + ts * yim).astype(BF16)
        z_ref[r, m:, :] = (tc * yim - ts * yre).astype(BF16)


def _fb_kernel(zre_ref, zim_ref, mb_ref, o_ref, *, scale):
    st = jnp.concatenate([zre_ref[...], zim_ref[...]], axis=0)
    o_ref[...] = (_dot(mb_ref[...], st) * scale).astype(o_ref.dtype)


def _fourier_factored(xcs3, tabs, n, fw):
    batch, seq, _ = xcs3.shape
    r, m = RADIX, n // RADIX
    xv = xcs3.reshape(batch, seq // r, r * 2 * fw)
    tc, ts = tabs["f_tw"]
    z = pl.pallas_call(
        functools.partial(_fa_kernel, fw=fw),
        grid=(batch, r // RADIX_STEP),
        in_specs=[
            pl.BlockSpec((None, m, RADIX_STEP * 2 * fw), lambda b, j: (b, 0, j)),
            pl.BlockSpec((2 * m, 2 * m), lambda b, j: (0, 0)),
            pl.BlockSpec((RADIX_STEP, m, LANE), lambda b, j: (j, 0, 0)),
            pl.BlockSpec((RADIX_STEP, m, LANE), lambda b, j: (j, 0, 0)),
        ],
        out_specs=pl.BlockSpec((None, RADIX_STEP, 2 * m, fw), lambda b, j: (b, j, 0, 0)),
        out_shape=jax.ShapeDtypeStruct((batch, r, 2 * m, fw), BF16),
        compiler_params=_cparams(("arbitrary", "arbitrary")),
        name="fourier_rows",
    )(xv, tabs["f_a"], tc, ts)
    lanes = m * fw
    lt = min(2048, lanes)
    nt = lanes // lt
    zv = z.reshape(batch, r, 2 * lanes)
    out = pl.pallas_call(
        functools.partial(_fb_kernel, scale=1.0 / math.sqrt(n * FOURIER_GROUP_W)),
        grid=(batch, nt),
        in_specs=[
            pl.BlockSpec((None, r, lt), lambda b, j: (b, 0, j)),
            pl.BlockSpec((None, r, lt), lambda b, j: (b, 0, nt + j)),
            pl.BlockSpec((r, 2 * r), lambda b, j: (0, 0)),
        ],
        out_specs=pl.BlockSpec((None, r, lt), lambda b, j: (b, 0, j)),
        out_shape=jax.ShapeDtypeStruct((batch, r, lanes), BF16),
        compiler_params=_cparams(("arbitrary", "arbitrary")),
        name="fourier_residues",
    )(zv, zv, tabs["f_b"])
    return out.reshape(batch, n, fw)


def _ha_kernel(v_ref, ma_ref, tc_ref, ts_ref, z_ref, *, hw):
    m2 = tc_ref.shape[1]
    for r in range(RADIX_STEP):
        a = _dot(ma_ref[...], v_ref[:, r * hw:(r + 1) * hw])
        are, aim = a[:m2], a[m2:]
        tc, ts = _lane_rep(tc_ref[r], hw), _lane_rep(ts_ref[r], hw)
        z_ref[r, :m2, :] = (tc * are + ts * aim).astype(BF16)
        z_ref[r, m2:, :] = (tc * aim - ts * are).astype(BF16)


def _hbc_kernel(zre_ref, zim_ref, kre_ref, kim_ref, mb_ref, mc_ref, u_ref):
    r = zre_ref.shape[0]
    v = _dot(mb_ref[...], jnp.concatenate([zre_ref[...], zim_ref[...]], axis=0))
    vre, vim = v[:r], v[r:]
    kre, kim = kre_ref[...], kim_ref[...]
    yre = vre * kre - vim * kim
    yim = vre * kim + vim * kre
    y = jnp.concatenate([yre, yim], axis=0).astype(BF16)
    u_ref[...] = _dot(mc_ref[...], y).astype(BF16)


def _hd_kernel(u_ref, md_ref, tc_ref, ts_ref, x1_ref, vv_ref, bias_ref, o_ref, *, hw, scale):
    for r in range(RADIX_STEP):
        ure = u_ref[0, r].astype(F32)
        uim = u_ref[1, r].astype(F32)
        tc, ts = _lane_rep(tc_ref[r], hw), _lane_rep(ts_ref[r], hw)
        t = jnp.concatenate([tc * ure - ts * uim, tc * uim + ts * ure], axis=0).astype(BF16)
        conv = _dot(md_ref[...], t) * scale
        cols = slice(r * hw, (r + 1) * hw)
        vv = vv_ref[:, cols].astype(F32)
        o_ref[:, cols] = (x1_ref[:, cols].astype(F32) * (conv + vv * bias_ref[...])).astype(BF16)


def _hyena_factored(vv, x1, tabs, kre, kim, hy_bias, layer, n, hw):
    batch = vv.shape[0]
    r, m = RADIX, n // RADIX
    tc, ts = tabs["h_tw"]
    vview = vv.reshape(batch, m, r * hw)
    z = pl.pallas_call(
        functools.partial(_ha_kernel, hw=hw),
        grid=(batch, r // RADIX_STEP),
        in_specs=[
            pl.BlockSpec((None, m, RADIX_STEP * hw), lambda b, j: (b, 0, j)),
            pl.BlockSpec((4 * m, m), lambda b, j: (0, 0)),
            pl.BlockSpec((RADIX_STEP, 2 * m, LANE), lambda b, j: (j, 0, 0)),
            pl.BlockSpec((RADIX_STEP, 2 * m, LANE), lambda b, j: (j, 0, 0)),
        ],
        out_specs=pl.BlockSpec((None, RADIX_STEP, 4 * m, hw), lambda b, j: (b, j, 0, 0)),
        out_shape=jax.ShapeDtypeStruct((batch, r, 4 * m, hw), BF16),
        compiler_params=_cparams(("arbitrary", "arbitrary")),
        name="hyena_rows_fwd",
    )(vview, tabs["h_a"], tc, ts)
    lanes = 2 * m * hw
    lt = min(4096, lanes)
    nt = lanes // lt
    zv = z.reshape(batch, r, 2 * lanes)
    kspec = pl.BlockSpec((None, r, lt), lambda b, j: (layer, 0, j))
    u = pl.pallas_call(
        _hbc_kernel,
        grid=(batch, nt),
        in_specs=[
            pl.BlockSpec((None, r, lt), lambda b, j: (b, 0, j)),
            pl.BlockSpec((None, r, lt), lambda b, j: (b, 0, nt + j)),
            kspec, kspec,
            pl.BlockSpec((2 * r, 2 * r), lambda b, j: (0, 0)),
            pl.BlockSpec((2 * r, 2 * r), lambda b, j: (0, 0)),
        ],
        out_specs=pl.BlockSpec((None, 2 * r, lt), lambda b, j: (b, 0, j)),
        out_shape=jax.ShapeDtypeStruct((batch, 2 * r, lanes), BF16),
        compiler_params=_cparams(("arbitrary", "arbitrary")),
        name="hyena_residues",
    )(zv, zv, kre.reshape(kre.shape[0], r, lanes), kim.reshape(kim.shape[0], r, lanes),
      tabs["h_b"], tabs["h_c"])
    uview = u.reshape(batch, 2, r, 2 * m, hw)
    tok_spec = pl.BlockSpec((None, m, RADIX_STEP * hw), lambda b, j: (b, 0, j))
    out = pl.pallas_call(
        functools.partial(_hd_kernel, hw=hw, scale=1.0 / (2 * n)),
        grid=(batch, r // RADIX_STEP),
        in_specs=[
            pl.BlockSpec((None, 2, RADIX_STEP, 2 * m, hw), lambda b, j: (b, 0, j, 0, 0)),
            pl.BlockSpec((m, 4 * m), lambda b, j: (0, 0)),
            pl.BlockSpec((RADIX_STEP, 2 * m, LANE), lambda b, j: (j, 0, 0)),
            pl.BlockSpec((RADIX_STEP, 2 * m, LANE), lambda b, j: (j, 0, 0)),
            tok_spec, tok_spec,
            pl.BlockSpec((None, 1, hw), lambda b, j: (layer, 0, 0)),
        ],
        out_specs=tok_spec,
        out_shape=jax.ShapeDtypeStruct((batch, m, r * hw), BF16),
        compiler_params=_cparams(("arbitrary", "arbitrary")),
        name="hyena_rows_inv",
    )(uview, tabs["h_d"], tc, ts, x1.reshape(batch, m, r * hw), vview, hy_bias)
    return out.reshape(batch, n, hw)


def _full_spectrum(kr, ki):
    zero = jnp.zeros_like(kr[:, :1])
    kre = jnp.concatenate([kr, ki[:, :1], jnp.flip(kr[:, 1:], axis=1)], axis=1)
    kim = jnp.concatenate([zero, ki[:, 1:], zero, -jnp.flip(ki[:, 1:], axis=1)], axis=1)
    return kre, kim


def _softplus(x):
    return jnp.maximum(x, 0.0) + jnp.log1p(jnp.exp(-jnp.abs(x)))


def _ssd_kernel(*refs, reverse, final, nc, lane0):
    if final:
        (xbc_ref, xp_ref, xn_ref, dt_ref, h0_ref, tri_ref, exp_ref, cw_ref, cb_ref, dtb_ref, a_ref,
         z_ref, yf_ref, d_ref, nrm_ref, out_ref, hout_ref, h_scr) = refs
    else:
        (xbc_ref, xp_ref, xn_ref, dt_ref, h0_ref, tri_ref, exp_ref, cw_ref, cb_ref, dtb_ref, a_ref,
         out_ref, hout_ref, h_scr) = refs
    step = pl.program_id(1)
    cc = (nc - 1 - step) if reverse else step

    @pl.when(step == 0)
    def _():
        h_scr[...] = h0_ref[...]

    nb, q = xbc_ref.shape[0], xbc_ref.shape[1]
    hs = xp_ref.shape[1]
    width = SSM_HEADS * SSM_HEAD_DIM
    gs = SSM_STATE
    pd = SSM_HEAD_DIM
    ri = lax.broadcasted_iota(jnp.int32, (q, q), 0)
    ci = lax.broadcasted_iota(jnp.int32, (q, q), 1)
    mask = (ri <= ci) if reverse else (ri >= ci)

    for s in range(nb):
        x = xbc_ref[s].astype(F32)
        prev_row = jnp.where(cc == 0, 0.0, xp_ref[s, hs - 1:hs, :].astype(F32))
        next_row = jnp.where(cc == nc - 1, 0.0, xn_ref[s, 0:1, :].astype(F32))
        u = _conv3(x, prev_row, next_row, cw_ref, cb_ref)
        u = u * _sigmoid(u)
        xs = u[:, :width]
        bm = u[:, width:width + SSM_GROUPS * gs]
        cm = u[:, width + SSM_GROUPS * gs:]

        dt = _softplus(dt_ref[s] + dtb_ref[...])
        acs = _dot_exact01(tri_ref[...], dt * a_ref[...])
        acs_t = acs.T
        tot = acs[0:1, :] if reverse else acs[q - 1:q, :]
        to_end = jnp.exp(tot - acs)
        frm = jnp.exp(acs)
        cdec = jnp.exp(tot)

        stacked = jnp.concatenate([dt, to_end, frm, jnp.broadcast_to(cdec, (8, LANE))], axis=0)
        s1, s2, s3 = _split3(stacked)
        ex = exp_ref[...]
        rep = _dot(s1, ex) + _dot(s2, ex) + _dot(s3, ex)
        xdt = xs * rep[0:q]
        wts = (xdt * rep[q:2 * q]).astype(BF16)
        frm_rep = rep[2 * q:3 * q]
        cdec_rep = rep[3 * q:3 * q + 1]
        xdt_b = xdt.astype(BF16)
        half = lax.broadcasted_iota(jnp.int32, (q, 2 * pd), 1) < pd

        ys = []
        gw = HEADS_PER_GROUP * pd
        for g in range(SSM_GROUPS):
            bg = bm[:, g * gs:(g + 1) * gs]
            cg = cm[:, g * gs:(g + 1) * gs].astype(BF16)
            scores = lax.dot_general(cg, bg.astype(BF16), (((1,), (1,)), ((), ())),
                                     preferred_element_type=F32)
            hg = h_scr[s, g]
            yoff = _dot(cg, hg.astype(BF16)) * frm_rep[:, g * gw:(g + 1) * gw]
            for pr in range(HEADS_PER_GROUP // 2):
                pair = []
                c0 = g * gw + pr * 2 * pd
                for j in range(2):
                    li = lane0 + g * HEADS_PER_GROUP + 2 * pr + j
                    lm = jnp.where(mask, jnp.exp(acs[:, li:li + 1] - acs_t[li:li + 1, :]), 0.0)
                    m = (scores * lm).astype(BF16)
                    pair.append(_dot(m, xdt_b[:, c0:c0 + 2 * pd]))
                ys.append(jnp.where(half, pair[0], pair[1]) + yoff[:, pr * 2 * pd:(pr + 1) * 2 * pd])
            st = _dot(bg.T.astype(BF16), wts[:, g * gw:(g + 1) * gw])
            h_scr[s, g] = hg * cdec_rep[:, g * gw:(g + 1) * gw] + st
        y = jnp.concatenate(ys, axis=1)

        if final:
            y = y + yf_ref[s] + xs * d_ref[...]
            z = z_ref[s].astype(F32)
            y = y * (z * _sigmoid(z))
            ms = jnp.mean(y * y, axis=-1, keepdims=True)
            out_ref[s] = ((y * lax.rsqrt(ms + EPS)) * nrm_ref[...]).astype(out_ref.dtype)
        else:
            out_ref[s] = y

    @pl.when(step == nc - 1)
    def _():
        hout_ref[...] = h_scr[...]


def _ssd_pass(xbc3, dt3, h0, tri, conv_w, conv_b, dt_bias, a_row, layer, n, row_off, *, reverse,
              fin=None):
    batch, seq, cw = xbc3.shape
    q = SSM_CHUNK
    nc = n // q
    off = row_off // q
    width = SSM_HEADS * SSM_HEAD_DIM

    def cidx(c):
        return off + ((nc - 1 - c) if reverse else c)

    hsz = BF16_SUBLANE
    per = q // hsz
    top = seq // hsz - 1
    nb = SSD_SAMPLES if batch % SSD_SAMPLES == 0 else 1
    per_layer = lambda b, c: (layer, 0, 0)
    in_specs = [
        pl.BlockSpec((nb, q, cw), lambda b, c: (b, cidx(c), 0)),
        pl.BlockSpec((nb, hsz, cw), lambda b, c: (b, jnp.maximum(cidx(c) * per - 1, 0), 0)),
        pl.BlockSpec((nb, hsz, cw), lambda b, c: (b, jnp.minimum((cidx(c) + 1) * per, top), 0)),
        pl.BlockSpec((nb, q, LANE), lambda b, c: (b, cidx(c), 0)),
        pl.BlockSpec((nb,) + h0.shape[1:], lambda b, c: (b, 0, 0, 0)),
        pl.BlockSpec((q, q), lambda b, c: (0, 0)),
        pl.BlockSpec((LANE, width), lambda b, c: (0, 0)),
        pl.BlockSpec((None,) + conv_w.shape[1:], per_layer),
        pl.BlockSpec((None, 1, cw), per_layer),
        pl.BlockSpec((None, 1, LANE), per_layer),
        pl.BlockSpec((None, 1, LANE), per_layer),
    ]
    lane0 = SSM_HEADS if reverse else 0
    head_rep = (jnp.arange(LANE)[:, None] == lane0 + jnp.arange(width)[None, :] // SSM_HEAD_DIM).astype(BF16)
    args = [xbc3, xbc3, xbc3, dt3, h0, tri, head_rep, conv_w, conv_b, dt_bias, a_row]
    if fin is not None:
        z3, yf, d_row, nrm_row = fin
        in_specs += [
            pl.BlockSpec((nb, q, width), lambda b, c: (b, cidx(c), 0)),
            pl.BlockSpec((nb, q, width), lambda b, c: (b, cidx(c) - off, 0)),
            pl.BlockSpec((None, 1, width), per_layer),
            pl.BlockSpec((None, 1, width), per_layer),
        ]
        args += [z3, yf, d_row, nrm_row]
    out_dtype = BF16 if fin is not None else F32
    return pl.pallas_call(
        functools.partial(_ssd_kernel, reverse=reverse, final=fin is not None, nc=nc, lane0=lane0),
        grid=(batch // nb, nc),
        in_specs=in_specs,
        out_specs=[
            pl.BlockSpec((nb, q, width), lambda b, c: (b, cidx(c) - off, 0)),
            pl.BlockSpec((nb,) + h0.shape[1:], lambda b, c: (b, 0, 0, 0)),
        ],
        out_shape=[
            jax.ShapeDtypeStruct((batch, n, width), out_dtype),
            jax.ShapeDtypeStruct(h0.shape, F32),
        ],
        scratch_shapes=[pltpu.VMEM((nb,) + h0.shape[1:], F32)],
        compiler_params=_cparams(("arbitrary", "arbitrary")),
        name="ssd_bwd_final" if reverse else "ssd_fwd",
    )(*args)


def _merge_kernel(x_ref, sc_ref, sh_ref, g1_ref, gn_ref, fl_ref, fc_ref, hl_ref, hc_ref, p_ref, sl_ref,
                  scx_ref, wg_ref, wb_ref, wo_ref, o_ref, *, tok, splits):
    ctx = tok.is_ctx(pl.program_id(0))
    x = x_ref[...]
    hb = _norm_mod(x, gn_ref[...], sc_ref[...], sh_ref[...]).astype(BF16)
    ys = (jnp.where(ctx, fc_ref[...], fl_ref[...]), jnp.where(ctx, hc_ref[...], hl_ref[...]),
          p_ref[...], jnp.where(ctx, scx_ref[...], sl_ref[...]))
    merged = None
    lo = 0
    for k, hi in enumerate(splits):
        gate = _sigmoid(_dot(hb, wg_ref[k]))
        term = gate * _dot(ys[k], wb_ref[lo:hi, :])
        merged = term if merged is None else merged + term
        lo = hi
    out = _dot(merged.astype(BF16), wo_ref[...])
    o_ref[...] = x + g1_ref[...] * out


def _merge(tok, layer, x_all, mod, norm_mix, f_lat, f_ctx, h_lat, h_ctx, pool, s_lat, s_ctx,
           w_gate, w_branch, w_out, splits):
    d = x_all.shape[-1]
    tm = TOKEN_TILE
    fw, hw, pw, sw = f_lat.shape[-1], h_lat.shape[-1], pool.shape[-1], s_lat.shape[-1]
    return pl.pallas_call(
        functools.partial(_merge_kernel, tok=tok, splits=splits),
        grid=(tok.n_tiles,),
        in_specs=[
            pl.BlockSpec((tm, d), lambda t: (t, 0)),
            tok.mod_spec(layer, 1, d),
            tok.mod_spec(layer, 0, d),
            tok.mod_spec(layer, 2, d),
            pl.BlockSpec((None, 1, d), lambda t: (layer, 0, 0)),
            tok.lat_spec(fw), tok.ctx_spec(fw),
            tok.lat_spec(hw), tok.ctx_spec(hw),
            pl.BlockSpec((tm, pw), lambda t: (t, 0)),
            tok.lat_spec(sw), tok.ctx_spec(sw),
            pl.BlockSpec((None,) + w_gate.shape[1:], lambda t: (layer, 0, 0, 0)),
            pl.BlockSpec((None,) + w_branch.shape[1:], lambda t: (layer, 0, 0)),
            pl.BlockSpec((None,) + w_out.shape[1:], lambda t: (layer, 0, 0)),
        ],
        out_specs=pl.BlockSpec((tm, d), lambda t: (t, 0)),
        out_shape=jax.ShapeDtypeStruct(x_all.shape, F32),
        compiler_params=_cparams(("arbitrary",)),
        name="merge",
    )(x_all, mod, mod, mod, norm_mix, f_lat, f_ctx, h_lat, h_ctx, pool, s_lat, s_ctx,
      w_gate, w_branch, w_out)


def _router_kernel(x_ref, sc_ref, sh_ref, gn_ref, rwh_ref, rwl_ref, rb_ref, utri_ref,
                   h_ref, e_ref, r_ref, w_ref, cnt_ref):
    t = pl.program_id(0)

    @pl.when(t == 0)
    def _():
        cnt_ref[...] = jnp.zeros_like(cnt_ref)

    h = _norm_mod(x_ref[...], gn_ref[...], sc_ref[...], sh_ref[...])
    hh = h.astype(BF16)
    h_ref[...] = hh
    hl = (h - hh.astype(F32)).astype(BF16)
    tm = h.shape[0]
    n_exp = rwh_ref.shape[0]
    nt = (((1,), (1,)), ((), ()))
    dg = lambda a, b: lax.dot_general(a, b, nt, preferred_element_type=F32)
    rwh = rwh_ref[...]
    logits = dg(rwh, hh) + dg(rwh, hl) + dg(rwl_ref[...], hh) + rb_ref[:, 0:1]
    eid = lax.broadcasted_iota(jnp.int32, (n_exp, tm), 0)
    carry = cnt_ref[:, 0:1]
    tops, sels = [], []
    cur = logits
    for _ in range(TOP_K):
        m = jnp.max(cur, axis=0, keepdims=True)
        idx = jnp.min(jnp.where(cur == m, eid, n_exp), axis=0, keepdims=True)
        sel = eid == idx
        tops.append((m, idx))
        sels.append(sel)
        cur = jnp.where(sel, -jnp.inf, cur)
    cnt = jnp.zeros((n_exp, tm), F32)
    for s in sels:
        cnt = jnp.where(s, 1.0, cnt)
    before = _dot(cnt.astype(BF16), utri_ref[...]) + carry
    exps = [jnp.exp(m - tops[0][0]) for m, _ in tops]
    den = exps[0]
    for v in exps[1:]:
        den = den + v
    row8 = lax.broadcasted_iota(jnp.int32, (8, tm), 0)
    rowl = lax.broadcasted_iota(jnp.int32, (LANE, tm), 0)
    e_out = jnp.zeros((8, tm), jnp.int32)
    r_out = jnp.zeros((8, tm), jnp.int32)
    w_t = jnp.zeros((LANE, tm), F32)
    for k in range(TOP_K):
        rank = jnp.sum(jnp.where(sels[k], before, 0.0), axis=0, keepdims=True)
        e_out = jnp.where(row8 == k, tops[k][1], e_out)
        r_out = jnp.where(row8 == k, rank.astype(jnp.int32), r_out)
        w_t = jnp.where(rowl == k, exps[k] / den, w_t)
    e_ref[...] = e_out
    r_ref[...] = r_out
    w_ref[...] = w_t.T
    cnt_ref[...] += jnp.broadcast_to(jnp.sum(cnt, axis=1, keepdims=True), cnt_ref.shape)


def _router(tok, layer, x_all, mod, norm_ffn, rw_hi, rw_lo, router_b_col, utri):
    d = x_all.shape[-1]
    tm = TOKEN_TILE
    n_tok = tok.n_tok
    n_exp = rw_hi.shape[1]
    row_i = jax.ShapeDtypeStruct((8, n_tok), jnp.int32)
    return pl.pallas_call(
        _router_kernel,
        grid=(tok.n_tiles,),
        in_specs=[
            pl.BlockSpec((tm, d), lambda t: (t, 0)),
            tok.mod_spec(layer, 4, d),
            tok.mod_spec(layer, 3, d),
            pl.BlockSpec((None, 1, d), lambda t: (layer, 0, 0)),
            pl.BlockSpec((None, n_exp, d), lambda t: (layer, 0, 0)),
            pl.BlockSpec((None, n_exp, d), lambda t: (layer, 0, 0)),
            pl.BlockSpec((None, n_exp, LANE), lambda t: (layer, 0, 0)),
            pl.BlockSpec((tm, tm), lambda t: (0, 0)),
        ],
        out_specs=[
            pl.BlockSpec((tm, d), lambda t: (t, 0)),
            pl.BlockSpec((8, tm), lambda t: (0, t)),
            pl.BlockSpec((8, tm), lambda t: (0, t)),
            pl.BlockSpec((tm, LANE), lambda t: (t, 0)),
            pl.BlockSpec((n_exp, LANE), lambda t: (0, 0)),
        ],
        out_shape=[
            jax.ShapeDtypeStruct((n_tok, d), BF16),
            row_i,
            row_i,
            jax.ShapeDtypeStruct((n_tok, LANE), F32),
            jax.ShapeDtypeStruct((n_exp, LANE), F32),
        ],
        compiler_params=_cparams(("arbitrary",)),
        name="router",
    )(x_all, mod, mod, norm_ffn, rw_hi, rw_lo, router_b_col, utri)


def _moe_kernel(be_ref, na_ref, x_ref, wu_ref, bu_ref, wd_ref, bd_ref, o_ref, wu_scr, wd_scr, *, ff):
    i = pl.program_id(0)
    active = i < na_ref[0]
    fresh = jnp.logical_or(i == 0, be_ref[i] != be_ref[jnp.maximum(i - 1, 0)])

    @pl.when(jnp.logical_and(active, fresh))
    def _():
        wu_scr[...] = wu_ref[...].astype(BF16)
        wd_scr[...] = wd_ref[...].astype(BF16)

    @pl.when(active)
    def _():
        hu = _dot(x_ref[...], wu_scr[...]) + bu_ref[...]
        glu = jnp.minimum(hu[:, :ff], SWIGLU_LIMIT)
        lin = jnp.clip(hu[:, ff:], -SWIGLU_LIMIT, SWIGLU_LIMIT)
        act = glu * _sigmoid(SWIGLU_ALPHA * glu) * (lin + 1.0)
        o_ref[...] = (_dot(act.astype(BF16), wd_scr[...]) + bd_ref[...]).astype(o_ref.dtype)

    @pl.when(jnp.logical_not(active))
    def _():
        o_ref[...] = jnp.zeros_like(o_ref)


def _moe(layer, x_sorted, blk_e, n_active, w_up, b_up, w_down, b_down):
    n_rows, d = x_sorted.shape
    ff = w_down.shape[2]
    bm = MOE_ROWS
    grid_spec = pltpu.PrefetchScalarGridSpec(
        num_scalar_prefetch=2,
        grid=(n_rows // bm,),
        in_specs=[
            pl.BlockSpec((bm, d), lambda i, be, na: (i, 0)),
            pl.BlockSpec((None, None, d, 2 * ff), lambda i, be, na: (layer, be[i], 0, 0)),
            pl.BlockSpec((None, None, 1, 2 * ff), lambda i, be, na: (layer, be[i], 0, 0)),
            pl.BlockSpec((None, None, ff, d), lambda i, be, na: (layer, be[i], 0, 0)),
            pl.BlockSpec((None, None, 1, d), lambda i, be, na: (layer, be[i], 0, 0)),
        ],
        out_specs=pl.BlockSpec((bm, d), lambda i, be, na: (i, 0)),
        scratch_shapes=[pltpu.VMEM((d, 2 * ff), BF16), pltpu.VMEM((ff, d), BF16)],
    )
    return pl.pallas_call(
        functools.partial(_moe_kernel, ff=ff),
        grid_spec=grid_spec,
        out_shape=jax.ShapeDtypeStruct((n_rows, d), BF16),
        compiler_params=_cparams(("arbitrary",)),
        name="moe_experts",
    )(blk_e, n_active, x_sorted, w_up, b_up, w_down, b_down)


def _final_kernel(x_ref, g2_ref, yg_ref, w_ref, g_ref, o_ref):
    x = _moe_residual(x_ref[...], g2_ref[...], yg_ref, w_ref)
    ms = jnp.mean(x * x, axis=-1, keepdims=True)
    o_ref[...] = (x * lax.rsqrt(ms + EPS)) * g_ref[...]


def _final(tok, layer, x_all, mod, y_gath, top_w, gain):
    batch, seq, n_lat = tok.batch, tok.seq, tok.n_lat
    d = x_all.shape[-1]
    tm = TOKEN_TILE
    return pl.pallas_call(
        _final_kernel,
        grid=(batch, n_lat // tm),
        in_specs=[
            pl.BlockSpec((None, tm, d), lambda b, i: (b, i, 0)),
            pl.BlockSpec((None, None, None, 1, d), lambda b, i: (layer, tok.mod_off + b, 5, 0, 0)),
            pl.BlockSpec((TOP_K, None, tm, d), lambda b, i: (0, b, i, 0)),
            pl.BlockSpec((None, tm, LANE), lambda b, i: (b, i, 0)),
            pl.BlockSpec((1, d), lambda b, i: (0, 0)),
        ],
        out_specs=pl.BlockSpec((None, tm, d), lambda b, i: (b, i, 0)),
        out_shape=jax.ShapeDtypeStruct((batch, n_lat, d), F32),
        compiler_params=_cparams(("arbitrary", "arbitrary")),
        name="final_combine_norm",
    )(x_all.reshape(batch, seq, d), mod, y_gath.reshape(TOP_K, batch, seq, d),
      top_w.reshape(batch, seq, LANE), gain)


def _dispatch_plan(top_e, rank, counts, n_tok):
    n_exp = counts.shape[0]
    bm = MOE_ROWS
    n_pairs = n_tok * TOP_K
    n_blk = -(-(n_pairs + n_exp * (bm - 1)) // bm)
    padded = (counts + bm - 1) // bm * bm
    pad_end = jnp.cumsum(padded)
    pad_start = pad_end - padded
    dest = rank
    for e in range(n_exp):
        dest = dest + jnp.where(top_e == e, pad_start[e], 0)
    n_active = (pad_end[-1] // bm).astype(jnp.int32).reshape(1)
    blk_start = jnp.arange(n_blk, dtype=jnp.int32) * bm
    blk_e = jnp.minimum(jnp.sum(pad_end[None, :] <= blk_start[:, None], axis=1), n_exp - 1).astype(jnp.int32)
    tok_ids = jnp.broadcast_to(jnp.arange(n_tok, dtype=jnp.int32)[None, :], dest.shape)
    row_tok = (jnp.arange(n_blk * bm, dtype=jnp.int32) % n_tok).at[dest.reshape(-1)].set(
        tok_ids.reshape(-1), unique_indices=True, mode='promise_in_bounds')
    return dest, row_tok, blk_e, n_active


def kernel(x, c, ctx, c_ctx, w_mod, b_mod, norm_mix, norm_ffn, w_in, hy_conv_w, hy_conv_b, hy_ffn_w1, hy_ffn_b1, hy_ffn_w2, hy_ffn_b2, hy_ffn_w3, hy_bias, pool_w, pool_scale, ssm_conv_w, ssm_conv_b, ssm_dt_bias, ssm_a_log, ssm_d, ssm_norm, w_branch, w_gate, w_out, router_w, router_b, exp_w_up, exp_b_up, exp_w_down, exp_b_down, norm_final):
    batch, n_lat, d = x.shape
    n_ctx = ctx.shape[1]
    depth = w_mod.shape[0]
    assert n_lat // GRID_W * GRID_W == n_lat and n_lat % n_ctx == 0

    hw = hy_bias.shape[1]
    pw = pool_scale.shape[1]
    sw = ssm_norm.shape[1]
    cw = ssm_conv_w.shape[2]
    fw = w_in.shape[2] - 3 * hw - pw - sw - cw - 2 * SSM_HEADS
    widths = (fw, 3 * hw, pw, sw, cw)
    splits = (fw, fw + hw, fw + hw + pw, fw + hw + pw + sw)
    n_exp = router_w.shape[2]

    w_in_p = jnp.pad(w_in, ((0, 0), (0, 0), (0, LANE - 2 * SSM_HEADS))).astype(BF16)
    w_gate_b = w_gate.astype(BF16)
    w_branch_b = w_branch.astype(BF16)
    w_out_b = w_out.astype(BF16)
    pool_w_bd = jax.vmap(_block_diag)(pool_w).astype(BF16)
    r3 = lambda a: a.reshape(a.shape[0], 1, a.shape[1])
    lane_pad = lambda a: jnp.pad(a, ((0, 0), (0, LANE - a.shape[1])))
    norm_mix3, norm_ffn3 = r3(norm_mix), r3(norm_ffn)
    hy_conv_b3, hy_bias3 = r3(hy_conv_b), r3(hy_bias)
    ssm_conv_b3 = r3(ssm_conv_b)
    pool_scale3 = r3(pool_scale)
    dt_bias3 = r3(lane_pad(ssm_dt_bias.reshape(depth, 2 * SSM_HEADS)))
    a_row3 = r3(lane_pad(-jnp.exp(ssm_a_log.astype(F32)).reshape(depth, 2 * SSM_HEADS)))
    d_row3 = r3(jnp.repeat(ssm_d, SSM_HEAD_DIM, axis=1))
    ssm_norm3 = r3(ssm_norm)
    rw_t = jnp.swapaxes(router_w, 1, 2)
    rw_hi = rw_t.astype(BF16)
    rw_lo = (rw_t - rw_hi.astype(F32)).astype(BF16)
    router_b_col = jnp.broadcast_to(router_b[:, :, None], (depth, n_exp, LANE))
    exp_b_up4 = exp_b_up.reshape(depth, n_exp, 1, -1)
    exp_b_down4 = exp_b_down.reshape(depth, n_exp, 1, -1)
    emb = hy_ffn_w1.shape[1]
    ffn = hy_ffn_w1.shape[2]
    f_w1 = jnp.pad(hy_ffn_w1, ((0, 0), (0, LANE - emb), (0, LANE - ffn)))
    f_b1 = r3(lane_pad(hy_ffn_b1))
    f_w2 = jnp.pad(hy_ffn_w2, ((0, 0), (0, LANE - ffn), (0, LANE - ffn)))
    f_b2 = r3(lane_pad(hy_ffn_b2))
    f_w3 = jnp.pad(hy_ffn_w3, ((0, 0), (0, LANE - ffn), (0, 0)))

    freqs = jnp.linspace(1e-4, HYENA_BANDS - 1, HYENA_BANDS, dtype=F32)
    freqs_row = jnp.concatenate([jnp.zeros((1,), F32), freqs, freqs,
                                 jnp.zeros((LANE - 1 - 2 * HYENA_BANDS,), F32)])[None, :]
    deltas_row = jnp.linspace(HYENA_MIN_DECAY, HYENA_MAX_DECAY, hw, dtype=F32)[None, :]
    gi = jnp.arange(FOURIER_GROUP_W)
    ang = (gi[:, None] * gi[None, :] % FOURIER_GROUP_W).astype(F32) * (2.0 * math.pi / FOURIER_GROUP_W)
    n_fg = fw // FOURIER_GROUP_W
    eye = jnp.eye(n_fg, dtype=F32)
    wc = jnp.concatenate([jnp.kron(eye, jnp.cos(ang)), jnp.kron(eye, jnp.sin(ang))], axis=1).astype(BF16)
    band, invc = _pool_tables(n_lat, n_ctx)
    qi = jnp.arange(SSM_CHUNK)
    tri_f = (qi[:, None] >= qi[None, :]).astype(BF16)
    tri_b = (qi[:, None] <= qi[None, :]).astype(BF16)
    ti = jnp.arange(TOKEN_TILE)
    utri = (ti[:, None] < ti[None, :]).astype(BF16)

    seqs = {}
    for name, n, row_off in (("ctx", n_ctx, n_lat), ("lat", n_lat, 0)):
        fwd, inv = _hyena_dft_tables(n)
        g_all, nrm = _filter_mlp(n, freqs_row, deltas_row, f_w1, f_b1, f_w2, f_b2, f_w3)
        kr, ki = _filter_spectrum(fwd, g_all, nrm, n, hw)
        if _use_factored(n) and row_off == 0:
            kre, kim = _full_spectrum(kr, ki)
            seqs[name] = dict(n=n, off=row_off, tabs=_factored_tables(n), kre=kre, kim=kim)
        else:
            cn, sn = _trig_tables(n, n)
            seqs[name] = dict(n=n, off=row_off, cn=cn.astype(BF16), sn=sn.astype(BF16), fwd=fwd, inv=inv,
                              kr=kr, ki=ki)

    cc = jnp.concatenate([c, c_ctx[None, :]], axis=0)
    rows = -(-cc.shape[0] // 8) * 8
    mod = _modulation(jnp.pad(cc, ((0, rows - cc.shape[0]), (0, 0))), w_mod, b_mod)
    mod = mod.reshape(depth, rows, 6, 1, d)

    seq = n_lat + n_ctx

    def layer_step(tok, layer, x_all, moe):
        gb, n_tok = tok.batch, tok.n_tok
        res = _inproj(tok, layer, x_all, mod, norm_mix3, w_in_p, wc, band, invc, pool_w_bd, pool_scale3,
                      widths, moe=moe)
        if moe is not None:
            x_all = res[0]
            res = res[1:]
        xcs, hy, pool, z, xbc, dtr = res
        xcs3 = xcs.reshape(gb, seq, 2 * fw)
        hy3 = hy.reshape(gb, seq, 3 * hw)
        z3 = z.reshape(gb, seq, sw)
        xbc3 = xbc.reshape(gb, seq, cw)
        dt3 = dtr.reshape(gb, seq, LANE)
        outs = {}
        zero_state = jnp.zeros((gb, SSM_GROUPS, SSM_STATE, HEADS_PER_GROUP * SSM_HEAD_DIM), F32)
        states = (zero_state, zero_state)
        for name in ("ctx", "lat"):
            s = seqs[name]
            n, off = s["n"], s["off"]
            vv, x1 = _hy_pre(hy3, hy_conv_w, hy_conv_b3, layer, n, off, hw)
            if "tabs" in s:
                y_f = _fourier_factored(xcs3, s["tabs"], n, fw)
                y_h = _hyena_factored(vv, x1, s["tabs"], s["kre"], s["kim"], hy_bias3, layer, n, hw)
            else:
                y_f = _fourier_seq(xcs3, s["cn"], s["sn"], n, off, fw)
                y2 = _hy_fwd(s["fwd"], vv, s["kr"], s["ki"], layer, n, hw)
                y_h = _hy_inv(s["inv"], y2, x1, vv, hy_bias3, layer, n, hw)
            yf, st_f = _ssd_pass(xbc3, dt3, states[0], tri_f, ssm_conv_w, ssm_conv_b3, dt_bias3, a_row3,
                                 layer, n, off, reverse=False)
            y_s, st_b = _ssd_pass(xbc3, dt3, states[1], tri_b, ssm_conv_w, ssm_conv_b3, dt_bias3, a_row3,
                                  layer, n, off, reverse=True, fin=(z3, yf, d_row3, ssm_norm3))
            states = (st_f, st_b)
            outs[name] = (y_f, y_h, y_s)
        x_all = _merge(tok, layer, x_all, mod, norm_mix3, outs["lat"][0], outs["ctx"][0], outs["lat"][1],
                       outs["ctx"][1], pool, outs["lat"][2], outs["ctx"][2], w_gate_b, w_branch_b, w_out_b,
                       splits)
        h2, top_e, rank, top_w, counts = _router(tok, layer, x_all, mod, norm_ffn3, rw_hi, rw_lo, router_b_col,
                                                 utri)
        dest, row_tok, blk_e, n_active = _dispatch_plan(top_e[:TOP_K], rank[:TOP_K],
                                                        counts[:, 0].astype(jnp.int32), n_tok)
        x_sorted = h2[row_tok]
        y_rows = _moe(layer, x_sorted, blk_e, n_active, exp_w_up, exp_b_up4, exp_w_down, exp_b_down4)
        y_gath = y_rows[dest.reshape(-1)].reshape(TOP_K, n_tok, d)
        return x_all, (y_gath, top_w)

    n_groups = SAMPLE_GROUPS if batch % SAMPLE_GROUPS == 0 else 1
    gb = batch // n_groups
    groups = []
    for g in range(n_groups):
        tok = _Tok(gb, n_lat, n_ctx, mod_off=g * gb, mod_ctx=batch)
        x_all = jnp.concatenate([x[g * gb:(g + 1) * gb], ctx[g * gb:(g + 1) * gb]], axis=1)
        groups.append([tok, x_all.reshape(tok.n_tok, d), None])
    for layer in range(depth):
        for grp in groups:
            grp[1], grp[2] = layer_step(grp[0], layer, grp[1], grp[2])
    outs = [_final(tok, depth - 1, x_all, mod, moe[0], moe[1], norm_final[None, :])
            for tok, x_all, moe in groups]
    return outs[0] if n_groups == 1 else jnp.concatenate(outs, axis=0)
```

```python
import functools
import math

import jax
import jax.numpy as jnp
from jax import lax
from jax.experimental import pallas as pl
from jax.experimental.pallas import tpu as pltpu

F32 = jnp.float32
BF16 = jnp.bfloat16

GRID_W = 64
EPS = 1e-6
FOURIER_GROUP_W = 64
HYENA_BANDS = 16
HYENA_MIN_DECAY = -math.log(1e-2) / 1.5
HYENA_MAX_DECAY = -math.log(1e-2) / 0.3
POOL_WINDOWS = (2, 4, 8, 16)
POOL_GROUP_W = 64
SSM_GROUPS = 2
HEADS_PER_GROUP = 4
SSM_HEADS = SSM_GROUPS * HEADS_PER_GROUP
SSM_HEAD_DIM = 64
SSM_STATE = 128
SSM_CHUNK = 128
TOP_K = 4
SWIGLU_LIMIT = 7.0
SWIGLU_ALPHA = 1.702

LANE = 128
BF16_SUBLANE = 16
TOKEN_TILE = 256
INPROJ_TILES = 2
SEQ_TILE = 512
SEQ_KTILE = 1024
MOE_ROWS = 512
MOE_FF_CHUNK = 256
SSD_SAMPLES = 2
SAMPLE_GROUPS = 1
VMEM_LIMIT = 56 * 1024 * 1024


def _cparams(sem):
    return pltpu.CompilerParams(dimension_semantics=sem, vmem_limit_bytes=VMEM_LIMIT)


def _sigmoid(x):
    return 1.0 / (1.0 + jnp.exp(-x))


def _dot(a, b):
    return jnp.dot(a, b, preferred_element_type=F32)


def _dot_hi(a, b):
    return jnp.dot(a, b, preferred_element_type=F32, precision=lax.Precision.HIGHEST)


def _split3(x):
    x1 = x.astype(BF16)
    r1 = x - x1.astype(F32)
    x2 = r1.astype(BF16)
    x3 = (r1 - x2.astype(F32)).astype(BF16)
    return x1, x2, x3


def _dot_exact01(m01, x):
    x1, x2, x3 = _split3(x)
    return _dot(m01, x1) + _dot(m01, x2) + _dot(m01, x3)


def _mod_kernel(c_ref, w_ref, b_ref, o_ref):
    c = c_ref[...]
    s = c * _sigmoid(c)
    o_ref[...] = _dot_hi(s, w_ref[...]) + b_ref[...]


def _modulation(cc, w_mod, b_mod):
    depth, d, six_d = w_mod.shape
    rows = cc.shape[0]
    nj = six_d // d
    return pl.pallas_call(
        _mod_kernel,
        grid=(depth, nj),
        in_specs=[
            pl.BlockSpec((rows, d), lambda l, j: (0, 0)),
            pl.BlockSpec((None, d, d), lambda l, j: (l, 0, j)),
            pl.BlockSpec((None, 1, d), lambda l, j: (l, 0, j)),
        ],
        out_specs=pl.BlockSpec((None, rows, d), lambda l, j: (l, 0, j)),
        out_shape=jax.ShapeDtypeStruct((depth, rows, six_d), F32),
        compiler_params=_cparams(("arbitrary", "arbitrary")),
        name="modulation",
    )(cc, w_mod, b_mod.reshape(depth, 1, six_d))


class _Tok:
    def __init__(self, batch, n_lat, n_ctx, mod_off, mod_ctx):
        assert n_ctx == TOKEN_TILE and n_lat % TOKEN_TILE == 0
        self.batch, self.n_lat, self.n_ctx = batch, n_lat, n_ctx
        self.mod_off, self.mod_ctx = mod_off, mod_ctx
        self.seq = n_lat + n_ctx
        self.tiles_per_b = self.seq // TOKEN_TILE
        self.lat_tiles = n_lat // TOKEN_TILE
        self.n_tiles = batch * self.tiles_per_b
        self.n_tok = batch * self.seq

    def b(self, t):
        return t // self.tiles_per_b

    def w(self, t):
        return t % self.tiles_per_b

    def is_ctx(self, t):
        return (t % self.tiles_per_b) >= self.lat_tiles

    def mod_row(self, t):
        return jnp.where(self.is_ctx(t), self.mod_ctx, self.mod_off + self.b(t))

    def mod_spec(self, layer, part, d):
        return pl.BlockSpec((None, None, None, 1, d), lambda t: (layer, self.mod_row(t), part, 0, 0))

    def lat_spec(self, width):
        return pl.BlockSpec((None, TOKEN_TILE, width),
                            lambda t: (self.b(t), jnp.minimum(self.w(t), self.lat_tiles - 1), 0))

    def ctx_spec(self, width):
        return pl.BlockSpec((None, TOKEN_TILE, width), lambda t: (self.b(t), 0, 0))


def _norm_mod(x, gain, scale, shift):
    ms = jnp.mean(x * x, axis=-1, keepdims=True)
    return (x * lax.rsqrt(ms + EPS)) * gain * (1.0 + scale) + shift


def _moe_residual(x, g2, yg_ref, w_ref, rows=slice(None)):
    w = w_ref[rows, :]
    y = None
    for k in range(TOP_K):
        term = w[:, k:k + 1] * yg_ref[k, rows, :].astype(F32)
        y = term if y is None else y + term
    return x + g2 * y


def _inproj_kernel(*refs, widths, combine, n_sub):
    it = iter(refs)
    take = lambda k: [next(it) for _ in range(k)]
    x_ref, = take(1)
    if combine:
        g2_refs = take(n_sub)
        yg_ref, w_ref = take(2)
    sc_refs, sh_refs = take(n_sub), take(n_sub)
    g_ref, win_ref, wc_ref = take(3)
    band_refs, invc_refs = take(n_sub), take(n_sub)
    pw_ref, ps_ref = take(2)
    if combine:
        xn_ref, = take(1)
    xcs_ref, hy_ref, pool_ref, z_ref, xbc_ref, dt_ref = take(6)
    fw, hw, pw, zw, cw = widths
    tm = TOKEN_TILE
    for s in range(n_sub):
        rows = slice(s * tm, (s + 1) * tm)
        x = x_ref[rows, :]
        if combine:
            x = _moe_residual(x, g2_refs[s][...], yg_ref, w_ref, rows)
            xn_ref[rows, :] = x
        hb = _norm_mod(x, g_ref[...], sc_refs[s][...], sh_refs[s][...]).astype(BF16)
        o = 0
        f_in = _dot(hb, win_ref[:, o:o + fw])
        xcs_ref[rows, :] = _dot(f_in.astype(BF16), wc_ref[...]).astype(BF16)
        o += fw
        hy_ref[rows, :] = _dot(hb, win_ref[:, o:o + hw]).astype(BF16)
        o += hw
        u = _dot(hb, win_ref[:, o:o + pw])
        o += pw
        z_ref[rows, :] = _dot(hb, win_ref[:, o:o + zw]).astype(BF16)
        o += zw
        xbc_ref[rows, :] = _dot(hb, win_ref[:, o:o + cw]).astype(BF16)
        o += cw
        dt_ref[rows, :] = _dot(hb, win_ref[:, o:o + LANE])
        gw = POOL_GROUP_W
        parts = []
        for gi in range(len(POOL_WINDOWS)):
            ug = u[:, gi * gw:(gi + 1) * gw]
            parts.append(_dot_exact01(band_refs[s][gi], ug))
        pooled = jnp.concatenate(parts, axis=1) * invc_refs[s][...] - u
        pool_ref[rows, :] = (_dot(pooled.astype(BF16), pw_ref[...]) * ps_ref[...]).astype(BF16)


def _pool_tables(n_lat, n_ctx):
    tm = TOKEN_TILE
    p = jnp.arange(tm)
    bands, invs = [], []
    for row_len in (GRID_W, n_ctx):
        assert tm % row_len == 0
        pr, rr = p % row_len, p // row_len
        bg, ig = [], []
        for win in POOL_WINDOWS:
            lo = jnp.clip(pr - win // 2, 0, row_len)
            hi = jnp.clip(pr + win // 2, 0, row_len)
            q = pr[None, :]
            m = (rr[:, None] == rr[None, :]) & (q >= lo[:, None]) & (q < hi[:, None])
            bg.append(m.astype(BF16))
            ig.append(jnp.broadcast_to((1.0 / (hi - lo).astype(F32))[:, None], (tm, POOL_GROUP_W)))
        bands.append(jnp.stack(bg))
        invs.append(jnp.concatenate(ig, axis=1))
    return jnp.stack(bands), jnp.stack(invs)


def _block_diag(w):
    g, a, b = w.shape
    out = jnp.zeros((g * a, g * b), w.dtype)
    for i in range(g):
        out = out.at[i * a:(i + 1) * a, i * b:(i + 1) * b].set(w[i])
    return out


def _inproj(tok, layer, x_all, mod, norm_mix, w_in_p, wc, band, invc, pool_w_bd, pool_scale, widths,
            moe=None):
    d = x_all.shape[-1]
    fw, hw, pw, zw, cw = widths
    tm = TOKEN_TILE
    n_tok = tok.n_tok
    n_sub = INPROJ_TILES if tok.n_tiles % INPROJ_TILES == 0 else 1
    rows = n_sub * tm
    subs = range(n_sub)
    tile = lambda i, s: i * n_sub + s
    kind = lambda t: jnp.where(tok.is_ctx(t), 1, 0)
    const2 = lambda i: (0, 0)
    mod_specs = lambda lyr, part: [
        pl.BlockSpec((None, None, None, 1, d), lambda i, s=s: (lyr, tok.mod_row(tile(i, s)), part, 0, 0))
        for s in subs]
    pre_specs, pre_args, pre_outs = [], [], []
    if moe is not None:
        pre_specs = mod_specs(layer - 1, 5) + [
            pl.BlockSpec((TOP_K, rows, d), lambda i: (0, i, 0)),
            pl.BlockSpec((rows, LANE), lambda i: (i, 0))]
        pre_args = [mod] * n_sub + [moe[0], moe[1]]
        pre_outs = [jax.ShapeDtypeStruct((n_tok, d), F32)]
    outs = pre_outs + [
        jax.ShapeDtypeStruct((n_tok, 2 * fw), BF16),
        jax.ShapeDtypeStruct((n_tok, hw), BF16),
        jax.ShapeDtypeStruct((n_tok, pw), BF16),
        jax.ShapeDtypeStruct((n_tok, zw), BF16),
        jax.ShapeDtypeStruct((n_tok, cw), BF16),
        jax.ShapeDtypeStruct((n_tok, LANE), F32),
    ]
    return pl.pallas_call(
        functools.partial(_inproj_kernel, widths=widths, combine=moe is not None, n_sub=n_sub),
        grid=(tok.n_tiles // n_sub,),
        in_specs=[pl.BlockSpec((rows, d), lambda i: (i, 0))] + pre_specs
        + mod_specs(layer, 1) + mod_specs(layer, 0) + [
            pl.BlockSpec((None, 1, d), lambda i: (layer, 0, 0)),
            pl.BlockSpec((None,) + w_in_p.shape[1:], lambda i: (layer, 0, 0)),
            pl.BlockSpec(wc.shape, const2),
        ]
        + [pl.BlockSpec((None,) + band.shape[1:], lambda i, s=s: (kind(tile(i, s)), 0, 0, 0)) for s in subs]
        + [pl.BlockSpec((None,) + invc.shape[1:], lambda i, s=s: (kind(tile(i, s)), 0, 0)) for s in subs]
        + [
            pl.BlockSpec((None,) + pool_w_bd.shape[1:], lambda i: (layer, 0, 0)),
            pl.BlockSpec((None, 1, pw), lambda i: (layer, 0, 0)),
        ],
        out_specs=[pl.BlockSpec((rows, o.shape[1]), lambda i: (i, 0)) for o in outs],
        out_shape=outs,
        compiler_params=_cparams(("arbitrary",)),
        name="inproj",
    )(x_all, *pre_args, *([mod] * (2 * n_sub)), norm_mix, w_in_p, wc, *([band] * n_sub), *([invc] * n_sub),
      pool_w_bd, pool_scale)


def _trig_tables(n, period):
    k = jnp.arange(n, dtype=jnp.int32)

    def cs(j):
        m = (j[:, None] * k[None, :]) % period
        ang = m.astype(F32) * (2.0 * math.pi / period)
        return jnp.cos(ang), jnp.sin(ang)

    split = 64
    if n % split or n <= split:
        return cs(k)
    c1, s1 = cs(jnp.arange(n // split, dtype=jnp.int32) * split)
    c2, s2 = cs(jnp.arange(split, dtype=jnp.int32))
    c = c1[:, None, :] * c2[None, :, :] - s1[:, None, :] * s2[None, :, :]
    s = s1[:, None, :] * c2[None, :, :] + c1[:, None, :] * s2[None, :, :]
    return c.reshape(n, n), s.reshape(n, n)


def _hyena_dft_tables(n):
    c, s = _trig_tables(n, 2 * n)
    nyq = jnp.where(jnp.arange(n) % 2 == 0, 1.0, -1.0).astype(F32)
    s = (-s).at[0].set(nyq)
    fwd = jnp.concatenate([c, s], axis=0).astype(BF16)
    return fwd, fwd.T


def _fourier_kernel(c_ref, s_ref, xc_ref, xs_ref, o_ref, acc_ref, *, scale):
    k = pl.program_id(1)

    @pl.when(k == 0)
    def _():
        acc_ref[...] = jnp.zeros_like(acc_ref)

    nb = xc_ref.shape[0]
    rc = jnp.concatenate([xc_ref[b] for b in range(nb)], axis=1)
    rs = jnp.concatenate([xs_ref[b] for b in range(nb)], axis=1)
    acc_ref[...] += _dot(c_ref[...], rc) - _dot(s_ref[...], rs)

    @pl.when(k == pl.num_programs(1) - 1)
    def _():
        w = o_ref.shape[2]
        for b in range(nb):
            o_ref[b] = (acc_ref[:, b * w:(b + 1) * w] * scale).astype(o_ref.dtype)


def _fourier_seq(xcs3, cn, sn, n, row_off, fw):
    batch = xcs3.shape[0]
    tl = min(SEQ_TILE, n)
    tk = min(SEQ_KTILE, n)
    nt = n // tl
    off = row_off // tk
    scale = 1.0 / math.sqrt(n * FOURIER_GROUP_W)
    return pl.pallas_call(
        functools.partial(_fourier_kernel, scale=scale),
        grid=(nt, n // tk),
        in_specs=[
            pl.BlockSpec((tl, tk), lambda i, k: (i, k)),
            pl.BlockSpec((tl, tk), lambda i, k: (i, k)),
            pl.BlockSpec((batch, tk, fw), lambda i, k: (0, off + k, 0)),
            pl.BlockSpec((batch, tk, fw), lambda i, k: (0, off + k, 1)),
        ],
        out_specs=pl.BlockSpec((batch, tl, fw), lambda i, k: (0, i, 0)),
        out_shape=jax.ShapeDtypeStruct((batch, n, fw), BF16),
        scratch_shapes=[pltpu.VMEM((tl, batch * fw), F32)],
        compiler_params=_cparams(("arbitrary", "arbitrary")),
        name="fourier_seq",
    )(cn, sn, xcs3, xcs3)


def _shift_rows(x, prev_row, next_row):
    n = x.shape[0]
    rows = lax.broadcasted_iota(jnp.int32, x.shape, 0)
    xm = jnp.where(rows == 0, prev_row, pltpu.roll(x, 1, 0))
    xp = jnp.where(rows == n - 1, next_row, pltpu.roll(x, n - 1, 0))
    return xm, xp


def _conv3(x, prev_row, next_row, w_ref, b_ref):
    xm, xp = _shift_rows(x, prev_row, next_row)
    return w_ref[0:1, :] * xm + w_ref[1:2, :] * x + w_ref[2:3, :] * xp + b_ref[...]


def _hy_pre_kernel(x_ref, xp_ref, xn_ref, w_ref, b_ref, vv_ref, x1_ref, *, hw):
    i = pl.program_id(1)
    last = pl.num_programs(1) - 1
    x = x_ref[...].astype(F32)
    hs = xp_ref.shape[0]
    prev_row = jnp.where(i == 0, 0.0, xp_ref[hs - 1:hs, :].astype(F32))
    next_row = jnp.where(i == last, 0.0, xn_ref[0:1, :].astype(F32))
    u = _conv3(x, prev_row, next_row, w_ref, b_ref)
    x1_ref[...] = u[:, :hw].astype(BF16)
    vv_ref[...] = (u[:, hw:2 * hw] * u[:, 2 * hw:]).astype(BF16)


def _halo_specs(width, tile, row_off, seq_rows):
    hs = BF16_SUBLANE
    per = tile // hs
    base = row_off // hs
    top = seq_rows // hs - 1
    prev = pl.BlockSpec((None, hs, width), lambda b, i: (b, jnp.maximum(base + i * per - 1, 0), 0))
    nxt = pl.BlockSpec((None, hs, width), lambda b, i: (b, jnp.minimum(base + (i + 1) * per, top), 0))
    return prev, nxt


def _hy_pre(hy3, conv_w, conv_b, layer, n, row_off, hw):
    batch, seq, w3 = hy3.shape
    tl = min(SEQ_TILE, n)
    off = row_off // tl
    prev, nxt = _halo_specs(w3, tl, row_off, seq)
    out = jax.ShapeDtypeStruct((batch, n, hw), BF16)
    return pl.pallas_call(
        functools.partial(_hy_pre_kernel, hw=hw),
        grid=(batch, n // tl),
        in_specs=[
            pl.BlockSpec((None, tl, w3), lambda b, i: (b, off + i, 0)),
            prev, nxt,
            pl.BlockSpec((None,) + conv_w.shape[1:], lambda b, i: (layer, 0, 0)),
            pl.BlockSpec((None, 1, w3), lambda b, i: (layer, 0, 0)),
        ],
        out_specs=[pl.BlockSpec((None, tl, hw), lambda b, i: (b, i, 0))] * 2,
        out_shape=[out, out],
        compiler_params=_cparams(("arbitrary", "arbitrary")),
        name="hyena_pre",
    )(hy3, hy3, hy3, conv_w, conv_b)


def _filter_mlp_kernel(fr_ref, dl_ref, w1_ref, b1_ref, w2_ref, b2_ref, w3_ref, g_ref, nrm_ref, *, n, hw):
    j = pl.program_id(1)
    rt = g_ref.shape[0]
    lane = lax.broadcasted_iota(jnp.int32, (rt, LANE), 1)
    pos = (lax.broadcasted_iota(jnp.int32, (rt, LANE), 0) + j * rt).astype(F32)
    t = pos / (n - 1)
    ang = 2.0 * math.pi * pos / n
    arg = ang * fr_ref[...]
    feats = jnp.where(lane == 0, t,
                      jnp.where(lane <= HYENA_BANDS, jnp.cos(arg),
                                jnp.where(lane <= 2 * HYENA_BANDS, -jnp.sin(arg), 0.0)))
    h1 = jnp.sin(_dot_hi(feats, w1_ref[...]) + b1_ref[...])
    h2 = jnp.sin(_dot_hi(h1, w2_ref[...]) + b2_ref[...])
    k = _dot_hi(h2, w3_ref[...])
    decay = jnp.exp(-t[:, 0:1] * dl_ref[...])
    kf = k[:, :hw] * decay
    kb = jnp.where(pos[:, 0:1] == 0.0, 0.0, k[:, hw:] * decay)
    g_ref[:, :hw] = (kf + kb).astype(BF16)
    g_ref[:, hw:] = (kf - kb).astype(BF16)
    part = jnp.sum(jnp.abs(kf) + jnp.abs(kb), axis=0, keepdims=True)

    @pl.when(j == 0)
    def _():
        nrm_ref[...] = part

    @pl.when(j > 0)
    def _():
        nrm_ref[...] += part


def _filter_mlp(n, freqs_row, deltas_row, w1, b1, w2, b2, w3):
    depth = w1.shape[0]
    hw = deltas_row.shape[1]
    rt = min(SEQ_TILE, n)
    per_layer = lambda l, j: (l, 0, 0)
    return pl.pallas_call(
        functools.partial(_filter_mlp_kernel, n=n, hw=hw),
        grid=(depth, n // rt),
        in_specs=[
            pl.BlockSpec((1, LANE), lambda l, j: (0, 0)),
            pl.BlockSpec((1, hw), lambda l, j: (0, 0)),
            pl.BlockSpec((None,) + w1.shape[1:], per_layer),
            pl.BlockSpec((None,) + b1.shape[1:], per_layer),
            pl.BlockSpec((None,) + w2.shape[1:], per_layer),
            pl.BlockSpec((None,) + b2.shape[1:], per_layer),
            pl.BlockSpec((None,) + w3.shape[1:], per_layer),
        ],
        out_specs=[
            pl.BlockSpec((rt, 2 * hw), lambda l, j: (j, l)),
            pl.BlockSpec((None, 1, hw), lambda l, j: (l, 0, 0)),
        ],
        out_shape=[
            jax.ShapeDtypeStruct((n, depth * 2 * hw), BF16),
            jax.ShapeDtypeStruct((depth, 1, hw), F32),
        ],
        compiler_params=_cparams(("arbitrary", "arbitrary")),
        name="hyena_filter_mlp",
    )(freqs_row, deltas_row, w1, b1, w2, b2, w3)


def _rdft_accumulate(fc_ref, fs_ref, rhs, accc_ref, accs_ref):
    k = pl.program_id(1)

    @pl.when(k == 0)
    def _():
        accc_ref[...] = jnp.zeros_like(accc_ref)
        accs_ref[...] = jnp.zeros_like(accs_ref)

    accc_ref[...] += _dot(fc_ref[...], rhs)
    accs_ref[...] += _dot(fs_ref[...], rhs)


def _filter_spec_kernel(fc_ref, fs_ref, g_ref, nrm_ref, kr_ref, ki_ref, accc_ref, accs_ref, *, hw):
    i = pl.program_id(0)
    _rdft_accumulate(fc_ref, fs_ref, g_ref[...], accc_ref, accs_ref)

    @pl.when(pl.program_id(1) == pl.num_programs(1) - 1)
    def _():
        tm = accc_ref.shape[0]
        row0 = (lax.broadcasted_iota(jnp.int32, (tm, hw), 0) + i * tm) == 0
        for l in range(kr_ref.shape[0]):
            inv = 1.0 / nrm_ref[l]
            o = l * 2 * hw
            kr_ref[l] = accc_ref[:, o:o + hw] * inv
            ki_ref[l] = jnp.where(row0, accs_ref[:, o:o + hw], accs_ref[:, o + hw:o + 2 * hw]) * inv


def _filter_spectrum(fwd, g_all, nrm, n, hw):
    depth = nrm.shape[0]
    tl = min(SEQ_TILE, n)
    nt = n // tl
    width = g_all.shape[1]
    out = jax.ShapeDtypeStruct((depth, n, hw), F32)
    return pl.pallas_call(
        functools.partial(_filter_spec_kernel, hw=hw),
        grid=(nt, nt),
        in_specs=[
            pl.BlockSpec((tl, tl), lambda i, k: (i, k)),
            pl.BlockSpec((tl, tl), lambda i, k: (nt + i, k)),
            pl.BlockSpec((tl, width), lambda i, k: (k, 0)),
            pl.BlockSpec(nrm.shape, lambda i, k: (0, 0, 0)),
        ],
        out_specs=[pl.BlockSpec((depth, tl, hw), lambda i, k: (0, i, 0))] * 2,
        out_shape=[out, out],
        scratch_shapes=[pltpu.VMEM((tl, width), F32)] * 2,
        compiler_params=_cparams(("arbitrary", "arbitrary")),
        name="hyena_filter_spectrum",
    )(fwd, fwd, g_all, nrm)


def _hy_fwd_kernel(fc_ref, fs_ref, vv_ref, kr_ref, ki_ref, y_ref, accc_ref, accs_ref):
    i = pl.program_id(0)
    nb = vv_ref.shape[0]
    rhs = jnp.concatenate([vv_ref[b] for b in range(nb)], axis=1)
    _rdft_accumulate(fc_ref, fs_ref, rhs, accc_ref, accs_ref)

    @pl.when(pl.program_id(1) == pl.num_programs(1) - 1)
    def _():
        tm, hw = kr_ref.shape
        row0 = (lax.broadcasted_iota(jnp.int32, (tm, hw), 0) + i * tm) == 0
        kr, ki = kr_ref[...], ki_ref[...]
        for b in range(nb):
            vr = accc_ref[:, b * hw:(b + 1) * hw]
            vi = accs_ref[:, b * hw:(b + 1) * hw]
            yr = jnp.where(row0, 0.5 * vr * kr, vr * kr - vi * ki)
            yi = jnp.where(row0, 0.5 * vi * ki, vr * ki + vi * kr)
            y_ref[0, :, b * hw:(b + 1) * hw] = yr.astype(BF16)
            y_ref[1, :, b * hw:(b + 1) * hw] = yi.astype(BF16)


def _hy_fwd(fwd, vv, kr, ki, layer, n, hw):
    batch = vv.shape[0]
    tl = min(SEQ_TILE, n)
    tk = min(SEQ_KTILE, n)
    nt = n // tl
    return pl.pallas_call(
        _hy_fwd_kernel,
        grid=(nt, n // tk),
        in_specs=[
            pl.BlockSpec((tl, tk), lambda i, k: (i, k)),
            pl.BlockSpec((tl, tk), lambda i, k: (nt + i, k)),
            pl.BlockSpec((batch, tk, hw), lambda i, k: (0, k, 0)),
            pl.BlockSpec((None, tl, hw), lambda i, k: (layer, i, 0)),
            pl.BlockSpec((None, tl, hw), lambda i, k: (layer, i, 0)),
        ],
        out_specs=pl.BlockSpec((2, tl, batch * hw), lambda i, k: (0, i, 0)),
        out_shape=jax.ShapeDtypeStruct((2, n, batch * hw), BF16),
        scratch_shapes=[pltpu.VMEM((tl, batch * hw), F32)] * 2,
        compiler_params=_cparams(("arbitrary", "arbitrary")),
        name="hyena_fwd_dft",
    )(fwd, fwd, vv, kr, ki)


def _hy_inv_kernel(fi_ref, y_ref, x1_ref, vv_ref, bias_ref, o_ref, acc_ref, *, scale):
    k = pl.program_id(1)

    @pl.when(k == 0)
    def _():
        acc_ref[...] = jnp.zeros_like(acc_ref)

    acc_ref[...] += _dot(fi_ref[...], y_ref[...])

    @pl.when(k == pl.num_programs(1) - 1)
    def _():
        nb, _, hw = x1_ref.shape
        for b in range(nb):
            conv = acc_ref[:, b * hw:(b + 1) * hw] * scale
            vv = vv_ref[b].astype(F32)
            o_ref[b] = (x1_ref[b].astype(F32) * (conv + vv * bias_ref[...])).astype(BF16)


def _hy_inv(inv, y2, x1, vv, hy_bias, layer, n, hw):
    batch = vv.shape[0]
    tl = min(SEQ_TILE, n)
    tk = min(SEQ_KTILE, n)
    nt = n // tl
    return pl.pallas_call(
        functools.partial(_hy_inv_kernel, scale=1.0 / n),
        grid=(nt, 2 * n // tk),
        in_specs=[
            pl.BlockSpec((tl, tk), lambda i, k: (i, k)),
            pl.BlockSpec((tk, batch * hw), lambda i, k: (k, 0)),
            pl.BlockSpec((batch, tl, hw), lambda i, k: (0, i, 0)),
            pl.BlockSpec((batch, tl, hw), lambda i, k: (0, i, 0)),
            pl.BlockSpec((None, 1, hw), lambda i, k: (layer, 0, 0)),
        ],
        out_specs=pl.BlockSpec((batch, tl, hw), lambda i, k: (0, i, 0)),
        out_shape=jax.ShapeDtypeStruct((batch, n, hw), BF16),
        scratch_shapes=[pltpu.VMEM((tl, batch * hw), F32)],
        compiler_params=_cparams(("arbitrary", "arbitrary")),
        name="hyena_inv_dft",
    )(inv, y2.reshape(2 * n, batch * hw), x1, vv, hy_bias)


def _softplus(x):
    return jnp.maximum(x, 0.0) + jnp.log1p(jnp.exp(-jnp.abs(x)))


def _ssd_kernel(*refs, reverse, final, nc, lane0):
    if final:
        (xbc_ref, xp_ref, xn_ref, dt_ref, h0_ref, tri_ref, exp_ref, cw_ref, cb_ref, dtb_ref, a_ref,
         z_ref, yf_ref, d_ref, nrm_ref, out_ref, hout_ref, h_scr) = refs
    else:
        (xbc_ref, xp_ref, xn_ref, dt_ref, h0_ref, tri_ref, exp_ref, cw_ref, cb_ref, dtb_ref, a_ref,
         out_ref, hout_ref, h_scr) = refs
    step = pl.program_id(1)
    cc = (nc - 1 - step) if reverse else step

    @pl.when(step == 0)
    def _():
        h_scr[...] = h0_ref[...]

    nb, q = xbc_ref.shape[0], xbc_ref.shape[1]
    hs = xp_ref.shape[1]
    width = SSM_HEADS * SSM_HEAD_DIM
    gs = SSM_STATE
    pd = SSM_HEAD_DIM
    ri = lax.broadcasted_iota(jnp.int32, (q, q), 0)
    ci = lax.broadcasted_iota(jnp.int32, (q, q), 1)
    mask = (ri <= ci) if reverse else (ri >= ci)

    for s in range(nb):
        x = xbc_ref[s].astype(F32)
        prev_row = jnp.where(cc == 0, 0.0, xp_ref[s, hs - 1:hs, :].astype(F32))
        next_row = jnp.where(cc == nc - 1, 0.0, xn_ref[s, 0:1, :].astype(F32))
        u = _conv3(x, prev_row, next_row, cw_ref, cb_ref)
        u = u * _sigmoid(u)
        xs = u[:, :width]
        bm = u[:, width:width + SSM_GROUPS * gs]
        cm = u[:, width + SSM_GROUPS * gs:]

        dt = _softplus(dt_ref[s] + dtb_ref[...])
        acs = _dot_exact01(tri_ref[...], dt * a_ref[...])
        acs_t = acs.T
        tot = acs[0:1, :] if reverse else acs[q - 1:q, :]
        to_end = jnp.exp(tot - acs)
        frm = jnp.exp(acs)
        cdec = jnp.exp(tot)

        stacked = jnp.concatenate([dt, to_end, frm, jnp.broadcast_to(cdec, (8, LANE))], axis=0)
        s1, s2, s3 = _split3(stacked)
        ex = exp_ref[...]
        rep = _dot(s1, ex) + _dot(s2, ex) + _dot(s3, ex)
        xdt = xs * rep[0:q]
        wts = (xdt * rep[q:2 * q]).astype(BF16)
        frm_rep = rep[2 * q:3 * q]
        cdec_rep = rep[3 * q:3 * q + 1]
        xdt_b = xdt.astype(BF16)
        half = lax.broadcasted_iota(jnp.int32, (q, 2 * pd), 1) < pd

        ys = []
        gw = HEADS_PER_GROUP * pd
        for g in range(SSM_GROUPS):
            bg = bm[:, g * gs:(g + 1) * gs]
            cg = cm[:, g * gs:(g + 1) * gs].astype(BF16)
            scores = lax.dot_general(cg, bg.astype(BF16), (((1,), (1,)), ((), ())),
                                     preferred_element_type=F32)
            hg = h_scr[s, g]
            yoff = _dot(cg, hg.astype(BF16)) * frm_rep[:, g * gw:(g + 1) * gw]
            for pr in range(HEADS_PER_GROUP // 2):
                pair = []
                c0 = g * gw + pr * 2 * pd
                for j in range(2):
                    li = lane0 + g * HEADS_PER_GROUP + 2 * pr + j
                    lm = jnp.where(mask, jnp.exp(acs[:, li:li + 1] - acs_t[li:li + 1, :]), 0.0)
                    m = (scores * lm).astype(BF16)
                    pair.append(_dot(m, xdt_b[:, c0:c0 + 2 * pd]))
                ys.append(jnp.where(half, pair[0], pair[1]) + yoff[:, pr * 2 * pd:(pr + 1) * 2 * pd])
            st = _dot(bg.T.astype(BF16), wts[:, g * gw:(g + 1) * gw])
            h_scr[s, g] = hg * cdec_rep[:, g * gw:(g + 1) * gw] + st
        y = jnp.concatenate(ys, axis=1)

        if final:
            y = y + yf_ref[s] + xs * d_ref[...]
            z = z_ref[s].astype(F32)
            y = y * (z * _sigmoid(z))
            ms = jnp.mean(y * y, axis=-1, keepdims=True)
            out_ref[s] = ((y * lax.rsqrt(ms + EPS)) * nrm_ref[...]).astype(out_ref.dtype)
        else:
            out_ref[s] = y

    @pl.when(step == nc - 1)
    def _():
        hout_ref[...] = h_scr[...]


def _ssd_pass(xbc3, dt3, h0, tri, conv_w, conv_b, dt_bias, a_row, layer, n, row_off, *, reverse,
              fin=None):
    batch, seq, cw = xbc3.shape
    q = SSM_CHUNK
    nc = n // q
    off = row_off // q
    width = SSM_HEADS * SSM_HEAD_DIM

    def cidx(c):
        return off + ((nc - 1 - c) if reverse else c)

    hsz = BF16_SUBLANE
    per = q // hsz
    top = seq // hsz - 1
    nb = SSD_SAMPLES if batch % SSD_SAMPLES == 0 else 1
    per_layer = lambda b, c: (layer, 0, 0)
    in_specs = [
        pl.BlockSpec((nb, q, cw), lambda b, c: (b, cidx(c), 0)),
        pl.BlockSpec((nb, hsz, cw), lambda b, c: (b, jnp.maximum(cidx(c) * per - 1, 0), 0)),
        pl.BlockSpec((nb, hsz, cw), lambda b, c: (b, jnp.minimum((cidx(c) + 1) * per, top), 0)),
        pl.BlockSpec((nb, q, LANE), lambda b, c: (b, cidx(c), 0)),
        pl.BlockSpec((nb,) + h0.shape[1:], lambda b, c: (b, 0, 0, 0)),
        pl.BlockSpec((q, q), lambda b, c: (0, 0)),
        pl.BlockSpec((LANE, width), lambda b, c: (0, 0)),
        pl.BlockSpec((None,) + conv_w.shape[1:], per_layer),
        pl.BlockSpec((None, 1, cw), per_layer),
        pl.BlockSpec((None, 1, LANE), per_layer),
        pl.BlockSpec((None, 1, LANE), per_layer),
    ]
    lane0 = SSM_HEADS if reverse else 0
    head_rep = (jnp.arange(LANE)[:, None] == lane0 + jnp.arange(width)[None, :] // SSM_HEAD_DIM).astype(BF16)
    args = [xbc3, xbc3, xbc3, dt3, h0, tri, head_rep, conv_w, conv_b, dt_bias, a_row]
    if fin is not None:
        z3, yf, d_row, nrm_row = fin
        in_specs += [
            pl.BlockSpec((nb, q, width), lambda b, c: (b, cidx(c), 0)),
            pl.BlockSpec((nb, q, width), lambda b, c: (b, cidx(c) - off, 0)),
            pl.BlockSpec((None, 1, width), per_layer),
            pl.BlockSpec((None, 1, width), per_layer),
        ]
        args += [z3, yf, d_row, nrm_row]
    out_dtype = BF16 if fin is not None else F32
    return pl.pallas_call(
        functools.partial(_ssd_kernel, reverse=reverse, final=fin is not None, nc=nc, lane0=lane0),
        grid=(batch // nb, nc),
        in_specs=in_specs,
        out_specs=[
            pl.BlockSpec((nb, q, width), lambda b, c: (b, cidx(c) - off, 0)),
            pl.BlockSpec((nb,) + h0.shape[1:], lambda b, c: (b, 0, 0, 0)),
        ],
        out_shape=[
            jax.ShapeDtypeStruct((batch, n, width), out_dtype),
            jax.ShapeDtypeStruct(h0.shape, F32),
        ],
        scratch_shapes=[pltpu.VMEM((nb,) + h0.shape[1:], F32)],
        compiler_params=_cparams(("arbitrary", "arbitrary")),
        name="ssd_bwd_final" if reverse else "ssd_fwd",
    )(*args)


def _merge_kernel(x_ref, sc_ref, sh_ref, g1_ref, gn_ref, fl_ref, fc_ref, hl_ref, hc_ref, p_ref, sl_ref,
                  scx_ref, wg_ref, wb_ref, wo_ref, o_ref, *, tok, splits):
    ctx = tok.is_ctx(pl.program_id(0))
    x = x_ref[...]
    hb = _norm_mod(x, gn_ref[...], sc_ref[...], sh_ref[...]).astype(BF16)
    ys = (jnp.where(ctx, fc_ref[...], fl_ref[...]), jnp.where(ctx, hc_ref[...], hl_ref[...]),
          p_ref[...], jnp.where(ctx, scx_ref[...], sl_ref[...]))
    merged = None
    lo = 0
    for k, hi in enumerate(splits):
        gate = _sigmoid(_dot(hb, wg_ref[k]))
        term = gate * _dot(ys[k], wb_ref[lo:hi, :])
        merged = term if merged is None else merged + term
        lo = hi
    out = _dot(merged.astype(BF16), wo_ref[...])
    o_ref[...] = x + g1_ref[...] * out


def _merge(tok, layer, x_all, mod, norm_mix, f_lat, f_ctx, h_lat, h_ctx, pool, s_lat, s_ctx,
           w_gate, w_branch, w_out, splits):
    d = x_all.shape[-1]
    tm = TOKEN_TILE
    fw, hw, pw, sw = f_lat.shape[-1], h_lat.shape[-1], pool.shape[-1], s_lat.shape[-1]
    return pl.pallas_call(
        functools.partial(_merge_kernel, tok=tok, splits=splits),
        grid=(tok.n_tiles,),
        in_specs=[
            pl.BlockSpec((tm, d), lambda t: (t, 0)),
            tok.mod_spec(layer, 1, d),
            tok.mod_spec(layer, 0, d),
            tok.mod_spec(layer, 2, d),
            pl.BlockSpec((None, 1, d), lambda t: (layer, 0, 0)),
            tok.lat_spec(fw), tok.ctx_spec(fw),
            tok.lat_spec(hw), tok.ctx_spec(hw),
            pl.BlockSpec((tm, pw), lambda t: (t, 0)),
            tok.lat_spec(sw), tok.ctx_spec(sw),
            pl.BlockSpec((None,) + w_gate.shape[1:], lambda t: (layer, 0, 0, 0)),
            pl.BlockSpec((None,) + w_branch.shape[1:], lambda t: (layer, 0, 0)),
            pl.BlockSpec((None,) + w_out.shape[1:], lambda t: (layer, 0, 0)),
        ],
        out_specs=pl.BlockSpec((tm, d), lambda t: (t, 0)),
        out_shape=jax.ShapeDtypeStruct(x_all.shape, F32),
        compiler_params=_cparams(("arbitrary",)),
        name="merge",
    )(x_all, mod, mod, mod, norm_mix, f_lat, f_ctx, h_lat, h_ctx, pool, s_lat, s_ctx,
      w_gate, w_branch, w_out)


def _router_kernel(x_ref, sc_ref, sh_ref, gn_ref, rwh_ref, rwl_ref, rb_ref, utri_ref,
                   h_ref, e_ref, r_ref, w_ref, cnt_ref):
    t = pl.program_id(0)

    @pl.when(t == 0)
    def _():
        cnt_ref[...] = jnp.zeros_like(cnt_ref)

    h = _norm_mod(x_ref[...], gn_ref[...], sc_ref[...], sh_ref[...])
    hh = h.astype(BF16)
    h_ref[...] = hh
    hl = (h - hh.astype(F32)).astype(BF16)
    tm = h.shape[0]
    n_exp = rwh_ref.shape[0]
    nt = (((1,), (1,)), ((), ()))
    dg = lambda a, b: lax.dot_general(a, b, nt, preferred_element_type=F32)
    rwh = rwh_ref[...]
    logits = dg(rwh, hh) + dg(rwh, hl) + dg(rwl_ref[...], hh) + rb_ref[:, 0:1]
    eid = lax.broadcasted_iota(jnp.int32, (n_exp, tm), 0)
    carry = cnt_ref[:, 0:1]
    tops, sels = [], []
    cur = logits
    for _ in range(TOP_K):
        m = jnp.max(cur, axis=0, keepdims=True)
        idx = jnp.min(jnp.where(cur == m, eid, n_exp), axis=0, keepdims=True)
        sel = eid == idx
        tops.append((m, idx))
        sels.append(sel)
        cur = jnp.where(sel, -jnp.inf, cur)
    cnt = jnp.zeros((n_exp, tm), F32)
    for s in sels:
        cnt = jnp.where(s, 1.0, cnt)
    before = _dot(cnt.astype(BF16), utri_ref[...]) + carry
    exps = [jnp.exp(m - tops[0][0]) for m, _ in tops]
    den = exps[0]
    for v in exps[1:]:
        den = den + v
    row8 = lax.broadcasted_iota(jnp.int32, (8, tm), 0)
    rowl = lax.broadcasted_iota(jnp.int32, (LANE, tm), 0)
    e_out = jnp.zeros((8, tm), jnp.int32)
    r_out = jnp.zeros((8, tm), jnp.int32)
    w_t = jnp.zeros((LANE, tm), F32)
    for k in range(TOP_K):
        rank = jnp.sum(jnp.where(sels[k], before, 0.0), axis=0, keepdims=True)
        e_out = jnp.where(row8 == k, tops[k][1], e_out)
        r_out = jnp.where(row8 == k, rank.astype(jnp.int32), r_out)
        w_t = jnp.where(rowl == k, exps[k] / den, w_t)
    e_ref[...] = e_out
    r_ref[...] = r_out
    w_ref[...] = w_t.T
    cnt_ref[...] += jnp.broadcast_to(jnp.sum(cnt, axis=1, keepdims=True), cnt_ref.shape)


def _router(tok, layer, x_all, mod, norm_ffn, rw_hi, rw_lo, router_b_col, utri):
    d = x_all.shape[-1]
    tm = TOKEN_TILE
    n_tok = tok.n_tok
    n_exp = rw_hi.shape[1]
    row_i = jax.ShapeDtypeStruct((8, n_tok), jnp.int32)
    return pl.pallas_call(
        _router_kernel,
        grid=(tok.n_tiles,),
        in_specs=[
            pl.BlockSpec((tm, d), lambda t: (t, 0)),
            tok.mod_spec(layer, 4, d),
            tok.mod_spec(layer, 3, d),
            pl.BlockSpec((None, 1, d), lambda t: (layer, 0, 0)),
            pl.BlockSpec((None, n_exp, d), lambda t: (layer, 0, 0)),
            pl.BlockSpec((None, n_exp, d), lambda t: (layer, 0, 0)),
            pl.BlockSpec((None, n_exp, LANE), lambda t: (layer, 0, 0)),
            pl.BlockSpec((tm, tm), lambda t: (0, 0)),
        ],
        out_specs=[
            pl.BlockSpec((tm, d), lambda t: (t, 0)),
            pl.BlockSpec((8, tm), lambda t: (0, t)),
            pl.BlockSpec((8, tm), lambda t: (0, t)),
            pl.BlockSpec((tm, LANE), lambda t: (t, 0)),
            pl.BlockSpec((n_exp, LANE), lambda t: (0, 0)),
        ],
        out_shape=[
            jax.ShapeDtypeStruct((n_tok, d), BF16),
            row_i,
            row_i,
            jax.ShapeDtypeStruct((n_tok, LANE), F32),
            jax.ShapeDtypeStruct((n_exp, LANE), F32),
        ],
        compiler_params=_cparams(("arbitrary",)),
        name="router",
    )(x_all, mod, mod, norm_ffn, rw_hi, rw_lo, router_b_col, utri)


def _moe_kernel(be_ref, na_ref, x_ref, wu_ref, bu_ref, wd_ref, bd_ref, o_ref, wu_scr, wd_scr, *, ff):
    i = pl.program_id(0)
    active = i < na_ref[0]
    fresh = jnp.logical_or(i == 0, be_ref[i] != be_ref[jnp.maximum(i - 1, 0)])

    @pl.when(jnp.logical_and(active, fresh))
    def _():
        wu_scr[...] = wu_ref[...].astype(BF16)
        wd_scr[...] = wd_ref[...].astype(BF16)

    @pl.when(active)
    def _():
        x = x_ref[...]
        y = None
        for lo in range(0, ff, MOE_FF_CHUNK):
            hi = lo + MOE_FF_CHUNK
            glu = jnp.minimum(_dot(x, wu_scr[:, lo:hi]) + bu_ref[:, lo:hi], SWIGLU_LIMIT)
            lin = jnp.clip(_dot(x, wu_scr[:, ff + lo:ff + hi]) + bu_ref[:, ff + lo:ff + hi],
                           -SWIGLU_LIMIT, SWIGLU_LIMIT)
            act = glu * _sigmoid(SWIGLU_ALPHA * glu) * (lin + 1.0)
            part = _dot(act.astype(BF16), wd_scr[lo:hi, :])
            y = part if y is None else y + part
        o_ref[...] = (y + bd_ref[...]).astype(o_ref.dtype)

    @pl.when(jnp.logical_not(active))
    def _():
        o_ref[...] = jnp.zeros_like(o_ref)


def _moe(layer, x_sorted, blk_e, n_active, w_up, b_up, w_down, b_down):
    n_rows, d = x_sorted.shape
    ff = w_down.shape[2]
    bm = MOE_ROWS
    grid_spec = pltpu.PrefetchScalarGridSpec(
        num_scalar_prefetch=2,
        grid=(n_rows // bm,),
        in_specs=[
            pl.BlockSpec((bm, d), lambda i, be, na: (i, 0)),
            pl.BlockSpec((None, None, d, 2 * ff), lambda i, be, na: (layer, be[i], 0, 0)),
            pl.BlockSpec((None, None, 1, 2 * ff), lambda i, be, na: (layer, be[i], 0, 0)),
            pl.BlockSpec((None, None, ff, d), lambda i, be, na: (layer, be[i], 0, 0)),
            pl.BlockSpec((None, None, 1, d), lambda i, be, na: (layer, be[i], 0, 0)),
        ],
        out_specs=pl.BlockSpec((bm, d), lambda i, be, na: (i, 0)),
        scratch_shapes=[pltpu.VMEM((d, 2 * ff), BF16), pltpu.VMEM((ff, d), BF16)],
    )
    return pl.pallas_call(
        functools.partial(_moe_kernel, ff=ff),
        grid_spec=grid_spec,
        out_shape=jax.ShapeDtypeStruct((n_rows, d), BF16),
        compiler_params=_cparams(("arbitrary",)),
        name="moe_experts",
    )(blk_e, n_active, x_sorted, w_up, b_up, w_down, b_down)


def _final_kernel(x_ref, g2_ref, yg_ref, w_ref, g_ref, o_ref):
    x = _moe_residual(x_ref[...], g2_ref[...], yg_ref, w_ref)
    ms = jnp.mean(x * x, axis=-1, keepdims=True)
    o_ref[...] = (x * lax.rsqrt(ms + EPS)) * g_ref[...]


def _final(tok, layer, x_all, mod, y_gath, top_w, gain):
    batch, seq, n_lat = tok.batch, tok.seq, tok.n_lat
    d = x_all.shape[-1]
    tm = TOKEN_TILE
    return pl.pallas_call(
        _final_kernel,
        grid=(batch, n_lat // tm),
        in_specs=[
            pl.BlockSpec((None, tm, d), lambda b, i: (b, i, 0)),
            pl.BlockSpec((None, None, None, 1, d), lambda b, i: (layer, tok.mod_off + b, 5, 0, 0)),
            pl.BlockSpec((TOP_K, None, tm, d), lambda b, i: (0, b, i, 0)),
            pl.BlockSpec((None, tm, LANE), lambda b, i: (b, i, 0)),
            pl.BlockSpec((1, d), lambda b, i: (0, 0)),
        ],
        out_specs=pl.BlockSpec((None, tm, d), lambda b, i: (b, i, 0)),
        out_shape=jax.ShapeDtypeStruct((batch, n_lat, d), F32),
        compiler_params=_cparams(("arbitrary", "arbitrary")),
        name="final_combine_norm",
    )(x_all.reshape(batch, seq, d), mod, y_gath.reshape(TOP_K, batch, seq, d),
      top_w.reshape(batch, seq, LANE), gain)


def _dispatch_plan(top_e, rank, counts, n_tok):
    n_exp = counts.shape[0]
    bm = MOE_ROWS
    n_pairs = n_tok * TOP_K
    n_blk = -(-(n_pairs + n_exp * (bm - 1)) // bm)
    padded = (counts + bm - 1) // bm * bm
    pad_end = jnp.cumsum(padded)
    pad_start = pad_end - padded
    dest = rank
    for e in range(n_exp):
        dest = dest + jnp.where(top_e == e, pad_start[e], 0)
    n_active = (pad_end[-1] // bm).astype(jnp.int32).reshape(1)
    blk_start = jnp.arange(n_blk, dtype=jnp.int32) * bm
    blk_e = jnp.minimum(jnp.sum(pad_end[None, :] <= blk_start[:, None], axis=1), n_exp - 1).astype(jnp.int32)
    tok_ids = jnp.broadcast_to(jnp.arange(n_tok, dtype=jnp.int32)[None, :], dest.shape)
    row_tok = (jnp.arange(n_blk * bm, dtype=jnp.int32) % n_tok).at[dest.reshape(-1)].set(
        tok_ids.reshape(-1), unique_indices=True, mode='promise_in_bounds')
    return dest, row_tok, blk_e, n_active


def kernel(x, c, ctx, c_ctx, w_mod, b_mod, norm_mix, norm_ffn, w_in, hy_conv_w, hy_conv_b, hy_ffn_w1, hy_ffn_b1, hy_ffn_w2, hy_ffn_b2, hy_ffn_w3, hy_bias, pool_w, pool_scale, ssm_conv_w, ssm_conv_b, ssm_dt_bias, ssm_a_log, ssm_d, ssm_norm, w_branch, w_gate, w_out, router_w, router_b, exp_w_up, exp_b_up, exp_w_down, exp_b_down, norm_final):
    batch, n_lat, d = x.shape
    n_ctx = ctx.shape[1]
    depth = w_mod.shape[0]
    assert n_lat // GRID_W * GRID_W == n_lat and n_lat % n_ctx == 0

    hw = hy_bias.shape[1]
    pw = pool_scale.shape[1]
    sw = ssm_norm.shape[1]
    cw = ssm_conv_w.shape[2]
    fw = w_in.shape[2] - 3 * hw - pw - sw - cw - 2 * SSM_HEADS
    widths = (fw, 3 * hw, pw, sw, cw)
    splits = (fw, fw + hw, fw + hw + pw, fw + hw + pw + sw)
    n_exp = router_w.shape[2]

    w_in_p = jnp.pad(w_in, ((0, 0), (0, 0), (0, LANE - 2 * SSM_HEADS))).astype(BF16)
    w_gate_b = w_gate.astype(BF16)
    w_branch_b = w_branch.astype(BF16)
    w_out_b = w_out.astype(BF16)
    pool_w_bd = jax.vmap(_block_diag)(pool_w).astype(BF16)
    r3 = lambda a: a.reshape(a.shape[0], 1, a.shape[1])
    lane_pad = lambda a: jnp.pad(a, ((0, 0), (0, LANE - a.shape[1])))
    norm_mix3, norm_ffn3 = r3(norm_mix), r3(norm_ffn)
    hy_conv_b3, hy_bias3 = r3(hy_conv_b), r3(hy_bias)
    ssm_conv_b3 = r3(ssm_conv_b)
    pool_scale3 = r3(pool_scale)
    dt_bias3 = r3(lane_pad(ssm_dt_bias.reshape(depth, 2 * SSM_HEADS)))
    a_row3 = r3(lane_pad(-jnp.exp(ssm_a_log.astype(F32)).reshape(depth, 2 * SSM_HEADS)))
    d_row3 = r3(jnp.repeat(ssm_d, SSM_HEAD_DIM, axis=1))
    ssm_norm3 = r3(ssm_norm)
    rw_t = jnp.swapaxes(router_w, 1, 2)
    rw_hi = rw_t.astype(BF16)
    rw_lo = (rw_t - rw_hi.astype(F32)).astype(BF16)
    router_b_col = jnp.broadcast_to(router_b[:, :, None], (depth, n_exp, LANE))
    exp_b_up4 = exp_b_up.reshape(depth, n_exp, 1, -1)
    exp_b_down4 = exp_b_down.reshape(depth, n_exp, 1, -1)
    emb = hy_ffn_w1.shape[1]
    ffn = hy_ffn_w1.shape[2]
    f_w1 = jnp.pad(hy_ffn_w1, ((0, 0), (0, LANE - emb), (0, LANE - ffn)))
    f_b1 = r3(lane_pad(hy_ffn_b1))
    f_w2 = jnp.pad(hy_ffn_w2, ((0, 0), (0, LANE - ffn), (0, LANE - ffn)))
    f_b2 = r3(lane_pad(hy_ffn_b2))
    f_w3 = jnp.pad(hy_ffn_w3, ((0, 0), (0, LANE - ffn), (0, 0)))

    freqs = jnp.linspace(1e-4, HYENA_BANDS - 1, HYENA_BANDS, dtype=F32)
    freqs_row = jnp.concatenate([jnp.zeros((1,), F32), freqs, freqs,
                                 jnp.zeros((LANE - 1 - 2 * HYENA_BANDS,), F32)])[None, :]
    deltas_row = jnp.linspace(HYENA_MIN_DECAY, HYENA_MAX_DECAY, hw, dtype=F32)[None, :]
    gi = jnp.arange(FOURIER_GROUP_W)
    ang = (gi[:, None] * gi[None, :] % FOURIER_GROUP_W).astype(F32) * (2.0 * math.pi / FOURIER_GROUP_W)
    n_fg = fw // FOURIER_GROUP_W
    eye = jnp.eye(n_fg, dtype=F32)
    wc = jnp.concatenate([jnp.kron(eye, jnp.cos(ang)), jnp.kron(eye, jnp.sin(ang))], axis=1).astype(BF16)
    band, invc = _pool_tables(n_lat, n_ctx)
    qi = jnp.arange(SSM_CHUNK)
    tri_f = (qi[:, None] >= qi[None, :]).astype(BF16)
    tri_b = (qi[:, None] <= qi[None, :]).astype(BF16)
    ti = jnp.arange(TOKEN_TILE)
    utri = (ti[:, None] < ti[None, :]).astype(BF16)

    seqs = {}
    for name, n, row_off in (("ctx", n_ctx, n_lat), ("lat", n_lat, 0)):
        fwd, inv = _hyena_dft_tables(n)
        g_all, nrm = _filter_mlp(n, freqs_row, deltas_row, f_w1, f_b1, f_w2, f_b2, f_w3)
        kr, ki = _filter_spectrum(fwd, g_all, nrm, n, hw)
        cn, sn = _trig_tables(n, n)
        seqs[name] = dict(n=n, off=row_off, cn=cn.astype(BF16), sn=sn.astype(BF16), fwd=fwd, inv=inv,
                          kr=kr, ki=ki)

    cc = jnp.concatenate([c, c_ctx[None, :]], axis=0)
    rows = -(-cc.shape[0] // 8) * 8
    mod = _modulation(jnp.pad(cc, ((0, rows - cc.shape[0]), (0, 0))), w_mod, b_mod)
    mod = mod.reshape(depth, rows, 6, 1, d)

    seq = n_lat + n_ctx

    def layer_step(tok, layer, x_all, moe):
        gb, n_tok = tok.batch, tok.n_tok
        res = _inproj(tok, layer, x_all, mod, norm_mix3, w_in_p, wc, band, invc, pool_w_bd, pool_scale3,
                      widths, moe=moe)
        if moe is not None:
            x_all = res[0]
            res = res[1:]
        xcs, hy, pool, z, xbc, dtr = res
        xcs3 = xcs.reshape(gb, seq, 2 * fw)
        hy3 = hy.reshape(gb, seq, 3 * hw)
        z3 = z.reshape(gb, seq, sw)
        xbc3 = xbc.reshape(gb, seq, cw)
        dt3 = dtr.reshape(gb, seq, LANE)
        outs = {}
        zero_state = jnp.zeros((gb, SSM_GROUPS, SSM_STATE, HEADS_PER_GROUP * SSM_HEAD_DIM), F32)
        states = (zero_state, zero_state)
        for name in ("ctx", "lat"):
            s = seqs[name]
            n, off = s["n"], s["off"]
            y_f = _fourier_seq(xcs3, s["cn"], s["sn"], n, off, fw)
            vv, x1 = _hy_pre(hy3, hy_conv_w, hy_conv_b3, layer, n, off, hw)
            y2 = _hy_fwd(s["fwd"], vv, s["kr"], s["ki"], layer, n, hw)
            y_h = _hy_inv(s["inv"], y2, x1, vv, hy_bias3, layer, n, hw)
            yf, st_f = _ssd_pass(xbc3, dt3, states[0], tri_f, ssm_conv_w, ssm_conv_b3, dt_bias3, a_row3,
                                 layer, n, off, reverse=False)
            y_s, st_b = _ssd_pass(xbc3, dt3, states[1], tri_b, ssm_conv_w, ssm_conv_b3, dt_bias3, a_row3,
                                  layer, n, off, reverse=True, fin=(z3, yf, d_row3, ssm_norm3))
            states = (st_f, st_b)
            outs[name] = (y_f, y_h, y_s)
        x_all = _merge(tok, layer, x_all, mod, norm_mix3, outs["lat"][0], outs["ctx"][0], outs["lat"][1],
                       outs["ctx"][1], pool, outs["lat"][2], outs["ctx"][2], w_gate_b, w_branch_b, w_out_b,
                       splits)
        h2, top_e, rank, top_w, counts = _router(tok, layer, x_all, mod, norm_ffn3, rw_hi, rw_lo, router_b_col,
                                                 utri)
        dest, row_tok, blk_e, n_active = _dispatch_plan(top_e[:TOP_K], rank[:TOP_K],
                                                        counts[:, 0].astype(jnp.int32), n_tok)
        x_sorted = h2[row_tok]
        y_rows = _moe(layer, x_sorted, blk_e, n_active, exp_w_up, exp_b_up4, exp_w_down, exp_b_down4)
        y_gath = y_rows[dest.reshape(-1)].reshape(TOP_K, n_tok, d)
        return x_all, (y_gath, top_w)

    n_groups = SAMPLE_GROUPS if batch % SAMPLE_GROUPS == 0 else 1
    gb = batch // n_groups
    groups = []
    for g in range(n_groups):
        tok = _Tok(gb, n_lat, n_ctx, mod_off=g * gb, mod_ctx=batch)
        x_all = jnp.concatenate([x[g * gb:(g + 1) * gb], ctx[g * gb:(g + 1) * gb]], axis=1)
        groups.append([tok, x_all.reshape(tok.n_tok, d), None])
    for layer in range(depth):
        for grp in groups:
            grp[1], grp[2] = layer_step(grp[0], layer, grp[1], grp[2])
    outs = [_final(tok, depth - 1, x_all, mod, moe[0], moe[1], norm_final[None, :])
            for tok, x_all, moe in groups]
    return outs[0] if n_groups == 1 else jnp.concatenate(outs, axis=0)
```

```python
import functools
import math

import jax
import jax.numpy as jnp
from jax import lax
from jax.experimental import pallas as pl
from jax.experimental.pallas import tpu as pltpu

F32 = jnp.float32
BF16 = jnp.bfloat16

GRID_W = 64
EPS = 1e-6
FOURIER_GROUP_W = 64
HYENA_BANDS = 16
HYENA_MIN_DECAY = -math.log(1e-2) / 1.5
HYENA_MAX_DECAY = -math.log(1e-2) / 0.3
POOL_WINDOWS = (2, 4, 8, 16)
POOL_GROUP_W = 64
SSM_GROUPS = 2
HEADS_PER_GROUP = 4
SSM_HEADS = SSM_GROUPS * HEADS_PER_GROUP
SSM_HEAD_DIM = 64
SSM_STATE = 128
SSM_CHUNK = 128
TOP_K = 4
SWIGLU_LIMIT = 7.0
SWIGLU_ALPHA = 1.702

LANE = 128
BF16_SUBLANE = 16
TOKEN_TILE = 256
INPROJ_TILES = 2
SEQ_TILE = 512
SEQ_KTILE = 1024
MOE_ROWS = 512
SSD_SAMPLES = 2
SAMPLE_GROUPS = 1
VMEM_LIMIT = 56 * 1024 * 1024


def _cparams(sem):
    return pltpu.CompilerParams(dimension_semantics=sem, vmem_limit_bytes=VMEM_LIMIT)


def _sigmoid(x):
    return 1.0 / (1.0 + jnp.exp(-x))


def _dot(a, b):
    return jnp.dot(a, b, preferred_element_type=F32)


def _dot_hi(a, b):
    return jnp.dot(a, b, preferred_element_type=F32, precision=lax.Precision.HIGHEST)


def _split3(x):
    x1 = x.astype(BF16)
    r1 = x - x1.astype(F32)
    x2 = r1.astype(BF16)
    x3 = (r1 - x2.astype(F32)).astype(BF16)
    return x1, x2, x3


def _dot_exact01(m01, x):
    x1, x2, x3 = _split3(x)
    return _dot(m01, x1) + _dot(m01, x2) + _dot(m01, x3)


def _mod_kernel(c_ref, w_ref, b_ref, o_ref):
    c = c_ref[...]
    s = c * _sigmoid(c)
    o_ref[...] = _dot_hi(s, w_ref[...]) + b_ref[...]


def _modulation(cc, w_mod, b_mod):
    depth, d, six_d = w_mod.shape
    rows = cc.shape[0]
    nj = six_d // d
    return pl.pallas_call(
        _mod_kernel,
        grid=(depth, nj),
        in_specs=[
            pl.BlockSpec((rows, d), lambda l, j: (0, 0)),
            pl.BlockSpec((None, d, d), lambda l, j: (l, 0, j)),
            pl.BlockSpec((None, 1, d), lambda l, j: (l, 0, j)),
        ],
        out_specs=pl.BlockSpec((None, rows, d), lambda l, j: (l, 0, j)),
        out_shape=jax.ShapeDtypeStruct((depth, rows, six_d), F32),
        compiler_params=_cparams(("arbitrary", "arbitrary")),
        name="modulation",
    )(cc, w_mod, b_mod.reshape(depth, 1, six_d))


class _Tok:
    def __init__(self, batch, n_lat, n_ctx, mod_off, mod_ctx):
        assert n_ctx == TOKEN_TILE and n_lat % TOKEN_TILE == 0
        self.batch, self.n_lat, self.n_ctx = batch, n_lat, n_ctx
        self.mod_off, self.mod_ctx = mod_off, mod_ctx
        self.seq = n_lat + n_ctx
        self.tiles_per_b = self.seq // TOKEN_TILE
        self.lat_tiles = n_lat // TOKEN_TILE
        self.n_tiles = batch * self.tiles_per_b
        self.n_tok = batch * self.seq

    def b(self, t):
        return t // self.tiles_per_b

    def w(self, t):
        return t % self.tiles_per_b

    def is_ctx(self, t):
        return (t % self.tiles_per_b) >= self.lat_tiles

    def mod_row(self, t):
        return jnp.where(self.is_ctx(t), self.mod_ctx, self.mod_off + self.b(t))

    def mod_spec(self, layer, part, d):
        return pl.BlockSpec((None, None, None, 1, d), lambda t: (layer, self.mod_row(t), part, 0, 0))

    def lat_spec(self, width):
        return pl.BlockSpec((None, TOKEN_TILE, width),
                            lambda t: (self.b(t), jnp.minimum(self.w(t), self.lat_tiles - 1), 0))

    def ctx_spec(self, width):
        return pl.BlockSpec((None, TOKEN_TILE, width), lambda t: (self.b(t), 0, 0))


def _norm_mod(x, gain, scale, shift):
    ms = jnp.mean(x * x, axis=-1, keepdims=True)
    return (x * lax.rsqrt(ms + EPS)) * gain * (1.0 + scale) + shift


def _moe_residual(x, g2, yg_ref, w_ref, rows=slice(None)):
    w = w_ref[rows, :]
    y = None
    for k in range(TOP_K):
        term = w[:, k:k + 1] * yg_ref[k, rows, :].astype(F32)
        y = term if y is None else y + term
    return x + g2 * y


def _inproj_kernel(*refs, widths, combine, n_sub):
    it = iter(refs)
    take = lambda k: [next(it) for _ in range(k)]
    x_ref, = take(1)
    if combine:
        g2_refs = take(n_sub)
        yg_ref, w_ref = take(2)
    sc_refs, sh_refs = take(n_sub), take(n_sub)
    g_ref, win_ref, wc_ref = take(3)
    band_refs, invc_refs = take(n_sub), take(n_sub)
    pw_ref, ps_ref = take(2)
    if combine:
        xn_ref, = take(1)
    xcs_ref, hy_ref, pool_ref, z_ref, xbc_ref, dt_ref = take(6)
    fw, hw, pw, zw, cw = widths
    tm = TOKEN_TILE
    for s in range(n_sub):
        rows = slice(s * tm, (s + 1) * tm)
        x = x_ref[rows, :]
        if combine:
            x = _moe_residual(x, g2_refs[s][...], yg_ref, w_ref, rows)
            xn_ref[rows, :] = x
        hb = _norm_mod(x, g_ref[...], sc_refs[s][...], sh_refs[s][...]).astype(BF16)
        o = 0
        f_in = _dot(hb, win_ref[:, o:o + fw])
        xcs_ref[rows, :] = _dot(f_in.astype(BF16), wc_ref[...]).astype(BF16)
        o += fw
        hy_ref[rows, :] = _dot(hb, win_ref[:, o:o + hw]).astype(BF16)
        o += hw
        u = _dot(hb, win_ref[:, o:o + pw])
        o += pw
        z_ref[rows, :] = _dot(hb, win_ref[:, o:o + zw]).astype(BF16)
        o += zw
        xbc_ref[rows, :] = _dot(hb, win_ref[:, o:o + cw]).astype(BF16)
        o += cw
        dt_ref[rows, :] = _dot(hb, win_ref[:, o:o + LANE])
        gw = POOL_GROUP_W
        parts = []
        for gi in range(len(POOL_WINDOWS)):
            ug = u[:, gi * gw:(gi + 1) * gw]
            parts.append(_dot_exact01(band_refs[s][gi], ug))
        pooled = jnp.concatenate(parts, axis=1) * invc_refs[s][...] - u
        pool_ref[rows, :] = (_dot(pooled.astype(BF16), pw_ref[...]) * ps_ref[...]).astype(BF16)


def _pool_tables(n_lat, n_ctx):
    tm = TOKEN_TILE
    p = jnp.arange(tm)
    bands, invs = [], []
    for row_len in (GRID_W, n_ctx):
        assert tm % row_len == 0
        pr, rr = p % row_len, p // row_len
        bg, ig = [], []
        for win in POOL_WINDOWS:
            lo = jnp.clip(pr - win // 2, 0, row_len)
            hi = jnp.clip(pr + win // 2, 0, row_len)
            q = pr[None, :]
            m = (rr[:, None] == rr[None, :]) & (q >= lo[:, None]) & (q < hi[:, None])
            bg.append(m.astype(BF16))
            ig.append(jnp.broadcast_to((1.0 / (hi - lo).astype(F32))[:, None], (tm, POOL_GROUP_W)))
        bands.append(jnp.stack(bg))
        invs.append(jnp.concatenate(ig, axis=1))
    return jnp.stack(bands), jnp.stack(invs)


def _block_diag(w):
    g, a, b = w.shape
    out = jnp.zeros((g * a, g * b), w.dtype)
    for i in range(g):
        out = out.at[i * a:(i + 1) * a, i * b:(i + 1) * b].set(w[i])
    return out


def _inproj(tok, layer, x_all, mod, norm_mix, w_in_p, wc, band, invc, pool_w_bd, pool_scale, widths,
            moe=None):
    d = x_all.shape[-1]
    fw, hw, pw, zw, cw = widths
    tm = TOKEN_TILE
    n_tok = tok.n_tok
    n_sub = INPROJ_TILES if tok.n_tiles % INPROJ_TILES == 0 else 1
    rows = n_sub * tm
    subs = range(n_sub)
    tile = lambda i, s: i * n_sub + s
    kind = lambda t: jnp.where(tok.is_ctx(t), 1, 0)
    const2 = lambda i: (0, 0)
    mod_specs = lambda lyr, part: [
        pl.BlockSpec((None, None, None, 1, d), lambda i, s=s: (lyr, tok.mod_row(tile(i, s)), part, 0, 0))
        for s in subs]
    pre_specs, pre_args, pre_outs = [], [], []
    if moe is not None:
        pre_specs = mod_specs(layer - 1, 5) + [
            pl.BlockSpec((TOP_K, rows, d), lambda i: (0, i, 0)),
            pl.BlockSpec((rows, LANE), lambda i: (i, 0))]
        pre_args = [mod] * n_sub + [moe[0], moe[1]]
        pre_outs = [jax.ShapeDtypeStruct((n_tok, d), F32)]
    outs = pre_outs + [
        jax.ShapeDtypeStruct((n_tok, 2 * fw), BF16),
        jax.ShapeDtypeStruct((n_tok, hw), BF16),
        jax.ShapeDtypeStruct((n_tok, pw), BF16),
        jax.ShapeDtypeStruct((n_tok, zw), BF16),
        jax.ShapeDtypeStruct((n_tok, cw), BF16),
        jax.ShapeDtypeStruct((n_tok, LANE), F32),
    ]
    return pl.pallas_call(
        functools.partial(_inproj_kernel, widths=widths, combine=moe is not None, n_sub=n_sub),
        grid=(tok.n_tiles // n_sub,),
        in_specs=[pl.BlockSpec((rows, d), lambda i: (i, 0))] + pre_specs
        + mod_specs(layer, 1) + mod_specs(layer, 0) + [
            pl.BlockSpec((None, 1, d), lambda i: (layer, 0, 0)),
            pl.BlockSpec((None,) + w_in_p.shape[1:], lambda i: (layer, 0, 0)),
            pl.BlockSpec(wc.shape, const2),
        ]
        + [pl.BlockSpec((None,) + band.shape[1:], lambda i, s=s: (kind(tile(i, s)), 0, 0, 0)) for s in subs]
        + [pl.BlockSpec((None,) + invc.shape[1:], lambda i, s=s: (kind(tile(i, s)), 0, 0)) for s in subs]
        + [
            pl.BlockSpec((None,) + pool_w_bd.shape[1:], lambda i: (layer, 0, 0)),
            pl.BlockSpec((None, 1, pw), lambda i: (layer, 0, 0)),
        ],
        out_specs=[pl.BlockSpec((rows, o.shape[1]), lambda i: (i, 0)) for o in outs],
        out_shape=outs,
        compiler_params=_cparams(("arbitrary",)),
        name="inproj",
    )(x_all, *pre_args, *([mod] * (2 * n_sub)), norm_mix, w_in_p, wc, *([band] * n_sub), *([invc] * n_sub),
      pool_w_bd, pool_scale)


def _trig_tables(n, period):
    k = jnp.arange(n, dtype=jnp.int32)

    def cs(j):
        m = (j[:, None] * k[None, :]) % period
        ang = m.astype(F32) * (2.0 * math.pi / period)
        return jnp.cos(ang), jnp.sin(ang)

    split = 64
    if n % split or n <= split:
        return cs(k)
    c1, s1 = cs(jnp.arange(n // split, dtype=jnp.int32) * split)
    c2, s2 = cs(jnp.arange(split, dtype=jnp.int32))
    c = c1[:, None, :] * c2[None, :, :] - s1[:, None, :] * s2[None, :, :]
    s = s1[:, None, :] * c2[None, :, :] + c1[:, None, :] * s2[None, :, :]
    return c.reshape(n, n), s.reshape(n, n)


def _hyena_dft_tables(n):
    c, s = _trig_tables(n, 2 * n)
    nyq = jnp.where(jnp.arange(n) % 2 == 0, 1.0, -1.0).astype(F32)
    s = (-s).at[0].set(nyq)
    fwd = jnp.concatenate([c, s], axis=0).astype(BF16)
    return fwd, fwd.T


def _fourier_kernel(c_ref, s_ref, xc_ref, xs_ref, o_ref, acc_ref, *, scale):
    k = pl.program_id(1)

    @pl.when(k == 0)
    def _():
        acc_ref[...] = jnp.zeros_like(acc_ref)

    nb = xc_ref.shape[0]
    rc = jnp.concatenate([xc_ref[b] for b in range(nb)], axis=1)
    rs = jnp.concatenate([xs_ref[b] for b in range(nb)], axis=1)
    acc_ref[...] += _dot(c_ref[...], rc) - _dot(s_ref[...], rs)

    @pl.when(k == pl.num_programs(1) - 1)
    def _():
        w = o_ref.shape[2]
        for b in range(nb):
            o_ref[b] = (acc_ref[:, b * w:(b + 1) * w] * scale).astype(o_ref.dtype)


def _fourier_seq(xcs3, cn, sn, n, row_off, fw):
    batch = xcs3.shape[0]
    tl = min(SEQ_TILE, n)
    tk = min(SEQ_KTILE, n)
    nt = n // tl
    off = row_off // tk
    scale = 1.0 / math.sqrt(n * FOURIER_GROUP_W)
    return pl.pallas_call(
        functools.partial(_fourier_kernel, scale=scale),
        grid=(nt, n // tk),
        in_specs=[
            pl.BlockSpec((tl, tk), lambda i, k: (i, k)),
            pl.BlockSpec((tl, tk), lambda i, k: (i, k)),
            pl.BlockSpec((batch, tk, fw), lambda i, k: (0, off + k, 0)),
            pl.BlockSpec((batch, tk, fw), lambda i, k: (0, off + k, 1)),
        ],
        out_specs=pl.BlockSpec((batch, tl, fw), lambda i, k: (0, i, 0)),
        out_shape=jax.ShapeDtypeStruct((batch, n, fw), BF16),
        scratch_shapes=[pltpu.VMEM((tl, batch * fw), F32)],
        compiler_params=_cparams(("arbitrary", "arbitrary")),
        name="fourier_seq",
    )(cn, sn, xcs3, xcs3)


def _shift_rows(x, prev_row, next_row):
    n = x.shape[0]
    rows = lax.broadcasted_iota(jnp.int32, x.shape, 0)
    xm = jnp.where(rows == 0, prev_row, pltpu.roll(x, 1, 0))
    xp = jnp.where(rows == n - 1, next_row, pltpu.roll(x, n - 1, 0))
    return xm, xp


def _conv3(x, prev_row, next_row, w_ref, b_ref):
    xm, xp = _shift_rows(x, prev_row, next_row)
    return w_ref[0:1, :] * xm + w_ref[1:2, :] * x + w_ref[2:3, :] * xp + b_ref[...]


def _hy_pre_kernel(x_ref, xp_ref, xn_ref, w_ref, b_ref, vv_ref, x1_ref, *, hw):
    i = pl.program_id(1)
    last = pl.num_programs(1) - 1
    x = x_ref[...].astype(F32)
    hs = xp_ref.shape[0]
    prev_row = jnp.where(i == 0, 0.0, xp_ref[hs - 1:hs, :].astype(F32))
    next_row = jnp.where(i == last, 0.0, xn_ref[0:1, :].astype(F32))
    u = _conv3(x, prev_row, next_row, w_ref, b_ref)
    x1_ref[...] = u[:, :hw].astype(BF16)
    vv_ref[...] = (u[:, hw:2 * hw] * u[:, 2 * hw:]).astype(BF16)


def _halo_specs(width, tile, row_off, seq_rows):
    hs = BF16_SUBLANE
    per = tile // hs
    base = row_off // hs
    top = seq_rows // hs - 1
    prev = pl.BlockSpec((None, hs, width), lambda b, i: (b, jnp.maximum(base + i * per - 1, 0), 0))
    nxt = pl.BlockSpec((None, hs, width), lambda b, i: (b, jnp.minimum(base + (i + 1) * per, top), 0))
    return prev, nxt


def _hy_pre(hy3, conv_w, conv_b, layer, n, row_off, hw):
    batch, seq, w3 = hy3.shape
    tl = min(SEQ_TILE, n)
    off = row_off // tl
    prev, nxt = _halo_specs(w3, tl, row_off, seq)
    out = jax.ShapeDtypeStruct((batch, n, hw), BF16)
    return pl.pallas_call(
        functools.partial(_hy_pre_kernel, hw=hw),
        grid=(batch, n // tl),
        in_specs=[
            pl.BlockSpec((None, tl, w3), lambda b, i: (b, off + i, 0)),
            prev, nxt,
            pl.BlockSpec((None,) + conv_w.shape[1:], lambda b, i: (layer, 0, 0)),
            pl.BlockSpec((None, 1, w3), lambda b, i: (layer, 0, 0)),
        ],
        out_specs=[pl.BlockSpec((None, tl, hw), lambda b, i: (b, i, 0))] * 2,
        out_shape=[out, out],
        compiler_params=_cparams(("arbitrary", "arbitrary")),
        name="hyena_pre",
    )(hy3, hy3, hy3, conv_w, conv_b)


def _filter_mlp_kernel(fr_ref, dl_ref, w1_ref, b1_ref, w2_ref, b2_ref, w3_ref, g_ref, nrm_ref, *, n, hw):
    j = pl.program_id(1)
    rt = g_ref.shape[0]
    lane = lax.broadcasted_iota(jnp.int32, (rt, LANE), 1)
    pos = (lax.broadcasted_iota(jnp.int32, (rt, LANE), 0) + j * rt).astype(F32)
    t = pos / (n - 1)
    ang = 2.0 * math.pi * pos / n
    arg = ang * fr_ref[...]
    feats = jnp.where(lane == 0, t,
                      jnp.where(lane <= HYENA_BANDS, jnp.cos(arg),
                                jnp.where(lane <= 2 * HYENA_BANDS, -jnp.sin(arg), 0.0)))
    h1 = jnp.sin(_dot_hi(feats, w1_ref[...]) + b1_ref[...])
    h2 = jnp.sin(_dot_hi(h1, w2_ref[...]) + b2_ref[...])
    k = _dot_hi(h2, w3_ref[...])
    decay = jnp.exp(-t[:, 0:1] * dl_ref[...])
    kf = k[:, :hw] * decay
    kb = jnp.where(pos[:, 0:1] == 0.0, 0.0, k[:, hw:] * decay)
    g_ref[:, :hw] = (kf + kb).astype(BF16)
    g_ref[:, hw:] = (kf - kb).astype(BF16)
    part = jnp.sum(jnp.abs(kf) + jnp.abs(kb), axis=0, keepdims=True)

    @pl.when(j == 0)
    def _():
        nrm_ref[...] = part

    @pl.when(j > 0)
    def _():
        nrm_ref[...] += part


def _filter_mlp(n, freqs_row, deltas_row, w1, b1, w2, b2, w3):
    depth = w1.shape[0]
    hw = deltas_row.shape[1]
    rt = min(SEQ_TILE, n)
    per_layer = lambda l, j: (l, 0, 0)
    return pl.pallas_call(
        functools.partial(_filter_mlp_kernel, n=n, hw=hw),
        grid=(depth, n // rt),
        in_specs=[
            pl.BlockSpec((1, LANE), lambda l, j: (0, 0)),
            pl.BlockSpec((1, hw), lambda l, j: (0, 0)),
            pl.BlockSpec((None,) + w1.shape[1:], per_layer),
            pl.BlockSpec((None,) + b1.shape[1:], per_layer),
            pl.BlockSpec((None,) + w2.shape[1:], per_layer),
            pl.BlockSpec((None,) + b2.shape[1:], per_layer),
            pl.BlockSpec((None,) + w3.shape[1:], per_layer),
        ],
        out_specs=[
            pl.BlockSpec((rt, 2 * hw), lambda l, j: (j, l)),
            pl.BlockSpec((None, 1, hw), lambda l, j: (l, 0, 0)),
        ],
        out_shape=[
            jax.ShapeDtypeStruct((n, depth * 2 * hw), BF16),
            jax.ShapeDtypeStruct((depth, 1, hw), F32),
        ],
        compiler_params=_cparams(("arbitrary", "arbitrary")),
        name="hyena_filter_mlp",
    )(freqs_row, deltas_row, w1, b1, w2, b2, w3)


def _rdft_accumulate(fc_ref, fs_ref, rhs, accc_ref, accs_ref):
    k = pl.program_id(1)

    @pl.when(k == 0)
    def _():
        accc_ref[...] = jnp.zeros_like(accc_ref)
        accs_ref[...] = jnp.zeros_like(accs_ref)

    accc_ref[...] += _dot(fc_ref[...], rhs)
    accs_ref[...] += _dot(fs_ref[...], rhs)


def _filter_spec_kernel(fc_ref, fs_ref, g_ref, nrm_ref, kr_ref, ki_ref, accc_ref, accs_ref, *, hw):
    i = pl.program_id(0)
    _rdft_accumulate(fc_ref, fs_ref, g_ref[...], accc_ref, accs_ref)

    @pl.when(pl.program_id(1) == pl.num_programs(1) - 1)
    def _():
        tm = accc_ref.shape[0]
        row0 = (lax.broadcasted_iota(jnp.int32, (tm, hw), 0) + i * tm) == 0
        for l in range(kr_ref.shape[0]):
            inv = 1.0 / nrm_ref[l]
            o = l * 2 * hw
            kr_ref[l] = accc_ref[:, o:o + hw] * inv
            ki_ref[l] = jnp.where(row0, accs_ref[:, o:o + hw], accs_ref[:, o + hw:o + 2 * hw]) * inv


def _filter_spectrum(fwd, g_all, nrm, n, hw):
    depth = nrm.shape[0]
    tl = min(SEQ_TILE, n)
    nt = n // tl
    width = g_all.shape[1]
    out = jax.ShapeDtypeStruct((depth, n, hw), F32)
    return pl.pallas_call(
        functools.partial(_filter_spec_kernel, hw=hw),
        grid=(nt, nt),
        in_specs=[
            pl.BlockSpec((tl, tl), lambda i, k: (i, k)),
            pl.BlockSpec((tl, tl), lambda i, k: (nt + i, k)),
            pl.BlockSpec((tl, width), lambda i, k: (k, 0)),
            pl.BlockSpec(nrm.shape, lambda i, k: (0, 0, 0)),
        ],
        out_specs=[pl.BlockSpec((depth, tl, hw), lambda i, k: (0, i, 0))] * 2,
        out_shape=[out, out],
        scratch_shapes=[pltpu.VMEM((tl, width), F32)] * 2,
        compiler_params=_cparams(("arbitrary", "arbitrary")),
        name="hyena_filter_spectrum",
    )(fwd, fwd, g_all, nrm)


def _hy_fwd_kernel(fc_ref, fs_ref, vv_ref, kr_ref, ki_ref, y_ref, accc_ref, accs_ref):
    i = pl.program_id(0)
    nb = vv_ref.shape[0]
    rhs = jnp.concatenate([vv_ref[b] for b in range(nb)], axis=1)
    _rdft_accumulate(fc_ref, fs_ref, rhs, accc_ref, accs_ref)

    @pl.when(pl.program_id(1) == pl.num_programs(1) - 1)
    def _():
        tm, hw = kr_ref.shape
        row0 = (lax.broadcasted_iota(jnp.int32, (tm, hw), 0) + i * tm) == 0
        kr, ki = kr_ref[...], ki_ref[...]
        for b in range(nb):
            vr = accc_ref[:, b * hw:(b + 1) * hw]
            vi = accs_ref[:, b * hw:(b + 1) * hw]
            yr = jnp.where(row0, 0.5 * vr * kr, vr * kr - vi * ki)
            yi = jnp.where(row0, 0.5 * vi * ki, vr * ki + vi * kr)
            y_ref[0, :, b * hw:(b + 1) * hw] = yr.astype(BF16)
            y_ref[1, :, b * hw:(b + 1) * hw] = yi.astype(BF16)


def _hy_fwd(fwd, vv, kr, ki, layer, n, hw):
    batch = vv.shape[0]
    tl = min(SEQ_TILE, n)
    tk = min(SEQ_KTILE, n)
    nt = n // tl
    return pl.pallas_call(
        _hy_fwd_kernel,
        grid=(nt, n // tk),
        in_specs=[
            pl.BlockSpec((tl, tk), lambda i, k: (i, k)),
            pl.BlockSpec((tl, tk), lambda i, k: (nt + i, k)),
            pl.BlockSpec((batch, tk, hw), lambda i, k: (0, k, 0)),
            pl.BlockSpec((None, tl, hw), lambda i, k: (layer, i, 0)),
            pl.BlockSpec((None, tl, hw), lambda i, k: (layer, i, 0)),
        ],
        out_specs=pl.BlockSpec((2, tl, batch * hw), lambda i, k: (0, i, 0)),
        out_shape=jax.ShapeDtypeStruct((2, n, batch * hw), BF16),
        scratch_shapes=[pltpu.VMEM((tl, batch * hw), F32)] * 2,
        compiler_params=_cparams(("arbitrary", "arbitrary")),
        name="hyena_fwd_dft",
    )(fwd, fwd, vv, kr, ki)


def _hy_inv_kernel(fi_ref, y_ref, x1_ref, vv_ref, bias_ref, o_ref, acc_ref, *, scale):
    k = pl.program_id(1)

    @pl.when(k == 0)
    def _():
        acc_ref[...] = jnp.zeros_like(acc_ref)

    acc_ref[...] += _dot(fi_ref[...], y_ref[...])

    @pl.when(k == pl.num_programs(1) - 1)
    def _():
        nb, _, hw = x1_ref.shape
        for b in range(nb):
            conv = acc_ref[:, b * hw:(b + 1) * hw] * scale
            vv = vv_ref[b].astype(F32)
            o_ref[b] = (x1_ref[b].astype(F32) * (conv + vv * bias_ref[...])).astype(BF16)


def _hy_inv(inv, y2, x1, vv, hy_bias, layer, n, hw):
    batch = vv.shape[0]
    tl = min(SEQ_TILE, n)
    tk = min(SEQ_KTILE, n)
    nt = n // tl
    return pl.pallas_call(
        functools.partial(_hy_inv_kernel, scale=1.0 / n),
        grid=(nt, 2 * n // tk),
        in_specs=[
            pl.BlockSpec((tl, tk), lambda i, k: (i, k)),
            pl.BlockSpec((tk, batch * hw), lambda i, k: (k, 0)),
            pl.BlockSpec((batch, tl, hw), lambda i, k: (0, i, 0)),
            pl.BlockSpec((batch, tl, hw), lambda i, k: (0, i, 0)),
            pl.BlockSpec((None, 1, hw), lambda i, k: (layer, 0, 0)),
        ],
        out_specs=pl.BlockSpec((batch, tl, hw), lambda i, k: (0, i, 0)),
        out_shape=jax.ShapeDtypeStruct((batch, n, hw), BF16),
        scratch_shapes=[pltpu.VMEM((tl, batch * hw), F32)],
        compiler_params=_cparams(("arbitrary", "arbitrary")),
        name="hyena_inv_dft",
    )(inv, y2.reshape(2 * n, batch * hw), x1, vv, hy_bias)


def _softplus(x):
    return jnp.maximum(x, 0.0) + jnp.log1p(jnp.exp(-jnp.abs(x)))


def _ssd_kernel(*refs, reverse, final, nc, lane0):
    if final:
        (xbc_ref, xp_ref, xn_ref, dt_ref, h0_ref, tri_ref, exp_ref, cw_ref, cb_ref, dtb_ref, a_ref,
         z_ref, yf_ref, d_ref, nrm_ref, out_ref, hout_ref, h_scr) = refs
    else:
        (xbc_ref, xp_ref, xn_ref, dt_ref, h0_ref, tri_ref, exp_ref, cw_ref, cb_ref, dtb_ref, a_ref,
         out_ref, hout_ref, h_scr) = refs
    step = pl.program_id(1)
    cc = (nc - 1 - step) if reverse else step

    @pl.when(step == 0)
    def _():
        h_scr[...] = h0_ref[...]

    nb, q = xbc_ref.shape[0], xbc_ref.shape[1]
    hs = xp_ref.shape[1]
    width = SSM_HEADS * SSM_HEAD_DIM
    gs = SSM_STATE
    pd = SSM_HEAD_DIM
    ri = lax.broadcasted_iota(jnp.int32, (q, q), 0)
    ci = lax.broadcasted_iota(jnp.int32, (q, q), 1)
    mask = (ri <= ci) if reverse else (ri >= ci)

    for s in range(nb):
        x = xbc_ref[s].astype(F32)
        prev_row = jnp.where(cc == 0, 0.0, xp_ref[s, hs - 1:hs, :].astype(F32))
        next_row = jnp.where(cc == nc - 1, 0.0, xn_ref[s, 0:1, :].astype(F32))
        u = _conv3(x, prev_row, next_row, cw_ref, cb_ref)
        u = u * _sigmoid(u)
        xs = u[:, :width]
        bm = u[:, width:width + SSM_GROUPS * gs]
        cm = u[:, width + SSM_GROUPS * gs:]

        dt = _softplus(dt_ref[s] + dtb_ref[...])
        acs = _dot_exact01(tri_ref[...], dt * a_ref[...])
        acs_t = acs.T
        tot = acs[0:1, :] if reverse else acs[q - 1:q, :]
        to_end = jnp.exp(tot - acs)
        frm = jnp.exp(acs)
        cdec = jnp.exp(tot)

        stacked = jnp.concatenate([dt, to_end, frm, jnp.broadcast_to(cdec, (8, LANE))], axis=0)
        s1, s2, s3 = _split3(stacked)
        ex = exp_ref[...]
        rep = _dot(s1, ex) + _dot(s2, ex) + _dot(s3, ex)
        xdt = xs * rep[0:q]
        wts = (xdt * rep[q:2 * q]).astype(BF16)
        frm_rep = rep[2 * q:3 * q]
        cdec_rep = rep[3 * q:3 * q + 1]
        xdt_b = xdt.astype(BF16)
        half = lax.broadcasted_iota(jnp.int32, (q, 2 * pd), 1) < pd

        ys = []
        gw = HEADS_PER_GROUP * pd
        for g in range(SSM_GROUPS):
            bg = bm[:, g * gs:(g + 1) * gs]
            cg = cm[:, g * gs:(g + 1) * gs].astype(BF16)
            scores = lax.dot_general(cg, bg.astype(BF16), (((1,), (1,)), ((), ())),
                                     preferred_element_type=F32)
            hg = h_scr[s, g]
            yoff = _dot(cg, hg.astype(BF16)) * frm_rep[:, g * gw:(g + 1) * gw]
            for pr in range(HEADS_PER_GROUP // 2):
                pair = []
                c0 = g * gw + pr * 2 * pd
                for j in range(2):
                    li = lane0 + g * HEADS_PER_GROUP + 2 * pr + j
                    lm = jnp.where(mask, jnp.exp(acs[:, li:li + 1] - acs_t[li:li + 1, :]), 0.0)
                    m = (scores * lm).astype(BF16)
                    pair.append(_dot(m, xdt_b[:, c0:c0 + 2 * pd]))
                ys.append(jnp.where(half, pair[0], pair[1]) + yoff[:, pr * 2 * pd:(pr + 1) * 2 * pd])
            st = _dot(bg.T.astype(BF16), wts[:, g * gw:(g + 1) * gw])
            h_scr[s, g] = hg * cdec_rep[:, g * gw:(g + 1) * gw] + st
        y = jnp.concatenate(ys, axis=1)

        if final:
            y = y + yf_ref[s] + xs * d_ref[...]
            z = z_ref[s].astype(F32)
            y = y * (z * _sigmoid(z))
            ms = jnp.mean(y * y, axis=-1, keepdims=True)
            out_ref[s] = ((y * lax.rsqrt(ms + EPS)) * nrm_ref[...]).astype(out_ref.dtype)
        else:
            out_ref[s] = y

    @pl.when(step == nc - 1)
    def _():
        hout_ref[...] = h_scr[...]


def _ssd_pass(xbc3, dt3, h0, tri, conv_w, conv_b, dt_bias, a_row, layer, n, row_off, *, reverse,
              fin=None):
    batch, seq, cw = xbc3.shape
    q = SSM_CHUNK
    nc = n // q
    off = row_off // q
    width = SSM_HEADS * SSM_HEAD_DIM

    def cidx(c):
        return off + ((nc - 1 - c) if reverse else c)

    hsz = BF16_SUBLANE
    per = q // hsz
    top = seq // hsz - 1
    nb = SSD_SAMPLES if batch % SSD_SAMPLES == 0 else 1
    per_layer = lambda b, c: (layer, 0, 0)
    in_specs = [
        pl.BlockSpec((nb, q, cw), lambda b, c: (b, cidx(c), 0)),
        pl.BlockSpec((nb, hsz, cw), lambda b, c: (b, jnp.maximum(cidx(c) * per - 1, 0), 0)),
        pl.BlockSpec((nb, hsz, cw), lambda b, c: (b, jnp.minimum((cidx(c) + 1) * per, top), 0)),
        pl.BlockSpec((nb, q, LANE), lambda b, c: (b, cidx(c), 0)),
        pl.BlockSpec((nb,) + h0.shape[1:], lambda b, c: (b, 0, 0, 0)),
        pl.BlockSpec((q, q), lambda b, c: (0, 0)),
        pl.BlockSpec((LANE, width), lambda b, c: (0, 0)),
        pl.BlockSpec((None,) + conv_w.shape[1:], per_layer),
        pl.BlockSpec((None, 1, cw), per_layer),
        pl.BlockSpec((None, 1, LANE), per_layer),
        pl.BlockSpec((None, 1, LANE), per_layer),
    ]
    lane0 = SSM_HEADS if reverse else 0
    head_rep = (jnp.arange(LANE)[:, None] == lane0 + jnp.arange(width)[None, :] // SSM_HEAD_DIM).astype(BF16)
    args = [xbc3, xbc3, xbc3, dt3, h0, tri, head_rep, conv_w, conv_b, dt_bias, a_row]
    if fin is not None:
        z3, yf, d_row, nrm_row = fin
        in_specs += [
            pl.BlockSpec((nb, q, width), lambda b, c: (b, cidx(c), 0)),
            pl.BlockSpec((nb, q, width), lambda b, c: (b, cidx(c) - off, 0)),
            pl.BlockSpec((None, 1, width), per_layer),
            pl.BlockSpec((None, 1, width), per_layer),
        ]
        args += [z3, yf, d_row, nrm_row]
    out_dtype = BF16 if fin is not None else F32
    return pl.pallas_call(
        functools.partial(_ssd_kernel, reverse=reverse, final=fin is not None, nc=nc, lane0=lane0),
        grid=(batch // nb, nc),
        in_specs=in_specs,
        out_specs=[
            pl.BlockSpec((nb, q, width), lambda b, c: (b, cidx(c) - off, 0)),
            pl.BlockSpec((nb,) + h0.shape[1:], lambda b, c: (b, 0, 0, 0)),
        ],
        out_shape=[
            jax.ShapeDtypeStruct((batch, n, width), out_dtype),
            jax.ShapeDtypeStruct(h0.shape, F32),
        ],
        scratch_shapes=[pltpu.VMEM((nb,) + h0.shape[1:], F32)],
        compiler_params=_cparams(("arbitrary", "arbitrary")),
        name="ssd_bwd_final" if reverse else "ssd_fwd",
    )(*args)


def _merge_kernel(x_ref, sc_ref, sh_ref, g1_ref, gn_ref, fl_ref, fc_ref, hl_ref, hc_ref, p_ref, sl_ref,
                  scx_ref, wg_ref, wb_ref, wo_ref, o_ref, *, tok, splits):
    ctx = tok.is_ctx(pl.program_id(0))
    x = x_ref[...]
    hb = _norm_mod(x, gn_ref[...], sc_ref[...], sh_ref[...]).astype(BF16)
    ys = (jnp.where(ctx, fc_ref[...], fl_ref[...]), jnp.where(ctx, hc_ref[...], hl_ref[...]),
          p_ref[...], jnp.where(ctx, scx_ref[...], sl_ref[...]))
    merged = None
    lo = 0
    for k, hi in enumerate(splits):
        gate = _sigmoid(_dot(hb, wg_ref[k]))
        term = gate * _dot(ys[k], wb_ref[lo:hi, :])
        merged = term if merged is None else merged + term
        lo = hi
    out = _dot(merged.astype(BF16), wo_ref[...])
    o_ref[...] = x + g1_ref[...] * out


def _merge(tok, layer, x_all, mod, norm_mix, f_lat, f_ctx, h_lat, h_ctx, pool, s_lat, s_ctx,
           w_gate, w_branch, w_out, splits):
    d = x_all.shape[-1]
    tm = TOKEN_TILE
    fw, hw, pw, sw = f_lat.shape[-1], h_lat.shape[-1], pool.shape[-1], s_lat.shape[-1]
    return pl.pallas_call(
        functools.partial(_merge_kernel, tok=tok, splits=splits),
        grid=(tok.n_tiles,),
        in_specs=[
            pl.BlockSpec((tm, d), lambda t: (t, 0)),
            tok.mod_spec(layer, 1, d),
            tok.mod_spec(layer, 0, d),
            tok.mod_spec(layer, 2, d),
            pl.BlockSpec((None, 1, d), lambda t: (layer, 0, 0)),
            tok.lat_spec(fw), tok.ctx_spec(fw),
            tok.lat_spec(hw), tok.ctx_spec(hw),
            pl.BlockSpec((tm, pw), lambda t: (t, 0)),
            tok.lat_spec(sw), tok.ctx_spec(sw),
            pl.BlockSpec((None,) + w_gate.shape[1:], lambda t: (layer, 0, 0, 0)),
            pl.BlockSpec((None,) + w_branch.shape[1:], lambda t: (layer, 0, 0)),
            pl.BlockSpec((None,) + w_out.shape[1:], lambda t: (layer, 0, 0)),
        ],
        out_specs=pl.BlockSpec((tm, d), lambda t: (t, 0)),
        out_shape=jax.ShapeDtypeStruct(x_all.shape, F32),
        compiler_params=_cparams(("arbitrary",)),
        name="merge",
    )(x_all, mod, mod, mod, norm_mix, f_lat, f_ctx, h_lat, h_ctx, pool, s_lat, s_ctx,
      w_gate, w_branch, w_out)


def _router_kernel(x_ref, sc_ref, sh_ref, gn_ref, rwh_ref, rwl_ref, rb_ref, utri_ref,
                   h_ref, e_ref, r_ref, w_ref, cnt_ref):
    t = pl.program_id(0)

    @pl.when(t == 0)
    def _():
        cnt_ref[...] = jnp.zeros_like(cnt_ref)

    h = _norm_mod(x_ref[...], gn_ref[...], sc_ref[...], sh_ref[...])
    hh = h.astype(BF16)
    h_ref[...] = hh
    hl = (h - hh.astype(F32)).astype(BF16)
    tm = h.shape[0]
    n_exp = rwh_ref.shape[0]
    nt = (((1,), (1,)), ((), ()))
    dg = lambda a, b: lax.dot_general(a, b, nt, preferred_element_type=F32)
    rwh = rwh_ref[...]
    logits = dg(rwh, hh) + dg(rwh, hl) + dg(rwl_ref[...], hh) + rb_ref[:, 0:1]
    eid = lax.broadcasted_iota(jnp.int32, (n_exp, tm), 0)
    carry = cnt_ref[:, 0:1]
    tops, sels = [], []
    cur = logits
    for _ in range(TOP_K):
        m = jnp.max(cur, axis=0, keepdims=True)
        idx = jnp.min(jnp.where(cur == m, eid, n_exp), axis=0, keepdims=True)
        sel = eid == idx
        tops.append((m, idx))
        sels.append(sel)
        cur = jnp.where(sel, -jnp.inf, cur)
    cnt = jnp.zeros((n_exp, tm), F32)
    for s in sels:
        cnt = jnp.where(s, 1.0, cnt)
    before = _dot(cnt.astype(BF16), utri_ref[...]) + carry
    exps = [jnp.exp(m - tops[0][0]) for m, _ in tops]
    den = exps[0]
    for v in exps[1:]:
        den = den + v
    row8 = lax.broadcasted_iota(jnp.int32, (8, tm), 0)
    rowl = lax.broadcasted_iota(jnp.int32, (LANE, tm), 0)
    e_out = jnp.zeros((8, tm), jnp.int32)
    r_out = jnp.zeros((8, tm), jnp.int32)
    w_t = jnp.zeros((LANE, tm), F32)
    for k in range(TOP_K):
        rank = jnp.sum(jnp.where(sels[k], before, 0.0), axis=0, keepdims=True)
        e_out = jnp.where(row8 == k, tops[k][1], e_out)
        r_out = jnp.where(row8 == k, rank.astype(jnp.int32), r_out)
        w_t = jnp.where(rowl == k, exps[k] / den, w_t)
    e_ref[...] = e_out
    r_ref[...] = r_out
    w_ref[...] = w_t.T
    cnt_ref[...] += jnp.broadcast_to(jnp.sum(cnt, axis=1, keepdims=True), cnt_ref.shape)


def _router(tok, layer, x_all, mod, norm_ffn, rw_hi, rw_lo, router_b_col, utri):
    d = x_all.shape[-1]
    tm = TOKEN_TILE
    n_tok = tok.n_tok
    n_exp = rw_hi.shape[1]
    row_i = jax.ShapeDtypeStruct((8, n_tok), jnp.int32)
    return pl.pallas_call(
        _router_kernel,
        grid=(tok.n_tiles,),
        in_specs=[
            pl.BlockSpec((tm, d), lambda t: (t, 0)),
            tok.mod_spec(layer, 4, d),
            tok.mod_spec(layer, 3, d),
            pl.BlockSpec((None, 1, d), lambda t: (layer, 0, 0)),
            pl.BlockSpec((None, n_exp, d), lambda t: (layer, 0, 0)),
            pl.BlockSpec((None, n_exp, d), lambda t: (layer, 0, 0)),
            pl.BlockSpec((None, n_exp, LANE), lambda t: (layer, 0, 0)),
            pl.BlockSpec((tm, tm), lambda t: (0, 0)),
        ],
        out_specs=[
            pl.BlockSpec((tm, d), lambda t: (t, 0)),
            pl.BlockSpec((8, tm), lambda t: (0, t)),
            pl.BlockSpec((8, tm), lambda t: (0, t)),
            pl.BlockSpec((tm, LANE), lambda t: (t, 0)),
            pl.BlockSpec((n_exp, LANE), lambda t: (0, 0)),
        ],
        out_shape=[
            jax.ShapeDtypeStruct((n_tok, d), BF16),
            row_i,
            row_i,
            jax.ShapeDtypeStruct((n_tok, LANE), F32),
            jax.ShapeDtypeStruct((n_exp, LANE), F32),
        ],
        compiler_params=_cparams(("arbitrary",)),
        name="router",
    )(x_all, mod, mod, norm_ffn, rw_hi, rw_lo, router_b_col, utri)


def _moe_kernel(be_ref, na_ref, x_ref, wu_ref, bu_ref, wd_ref, bd_ref, o_ref, wu_scr, wd_scr, *, ff):
    i = pl.program_id(0)
    active = i < na_ref[0]
    fresh = jnp.logical_or(i == 0, be_ref[i] != be_ref[jnp.maximum(i - 1, 0)])

    @pl.when(jnp.logical_and(active, fresh))
    def _():
        wu_scr[...] = wu_ref[...].astype(BF16)
        wd_scr[...] = wd_ref[...].astype(BF16)

    @pl.when(active)
    def _():
        hu = _dot(x_ref[...], wu_scr[...]) + bu_ref[...]
        glu = jnp.minimum(hu[:, :ff], SWIGLU_LIMIT)
        lin = jnp.clip(hu[:, ff:], -SWIGLU_LIMIT, SWIGLU_LIMIT)
        act = glu * _sigmoid(SWIGLU_ALPHA * glu) * (lin + 1.0)
        o_ref[...] = (_dot(act.astype(BF16), wd_scr[...]) + bd_ref[...]).astype(o_ref.dtype)

    @pl.when(jnp.logical_not(active))
    def _():
        o_ref[...] = jnp.zeros_like(o_ref)


def _moe(layer, x_sorted, blk_e, n_active, w_up, b_up, w_down, b_down):
    n_rows, d = x_sorted.shape
    ff = w_down.shape[2]
    bm = MOE_ROWS
    grid_spec = pltpu.PrefetchScalarGridSpec(
        num_scalar_prefetch=2,
        grid=(n_rows // bm,),
        in_specs=[
            pl.BlockSpec((bm, d), lambda i, be, na: (i, 0)),
            pl.BlockSpec((None, None, d, 2 * ff), lambda i, be, na: (layer, be[i], 0, 0)),
            pl.BlockSpec((None, None, 1, 2 * ff), lambda i, be, na: (layer, be[i], 0, 0)),
            pl.BlockSpec((None, None, ff, d), lambda i, be, na: (layer, be[i], 0, 0)),
            pl.BlockSpec((None, None, 1, d), lambda i, be, na: (layer, be[i], 0, 0)),
        ],
        out_specs=pl.BlockSpec((bm, d), lambda i, be, na: (i, 0)),
        scratch_shapes=[pltpu.VMEM((d, 2 * ff), BF16), pltpu.VMEM((ff, d), BF16)],
    )
    return pl.pallas_call(
        functools.partial(_moe_kernel, ff=ff),
        grid_spec=grid_spec,
        out_shape=jax.ShapeDtypeStruct((n_rows, d), BF16),
        compiler_params=_cparams(("arbitrary",)),
        name="moe_experts",
    )(blk_e, n_active, x_sorted, w_up, b_up, w_down, b_down)


def _final_kernel(x_ref, g2_ref, yg_ref, w_ref, g_ref, o_ref):
    x = _moe_residual(x_ref[...], g2_ref[...], yg_ref, w_ref)
    ms = jnp.mean(x * x, axis=-1, keepdims=True)
    o_ref[...] = (x * lax.rsqrt(ms + EPS)) * g_ref[...]


def _final(tok, layer, x_all, mod, y_gath, top_w, gain):
    batch, seq, n_lat = tok.batch, tok.seq, tok.n_lat
    d = x_all.shape[-1]
    tm = TOKEN_TILE
    return pl.pallas_call(
        _final_kernel,
        grid=(batch, n_lat // tm),
        in_specs=[
            pl.BlockSpec((None, tm, d), lambda b, i: (b, i, 0)),
            pl.BlockSpec((None, None, None, 1, d), lambda b, i: (layer, tok.mod_off + b, 5, 0, 0)),
            pl.BlockSpec((TOP_K, None, tm, d), lambda b, i: (0, b, i, 0)),
            pl.BlockSpec((None, tm, LANE), lambda b, i: (b, i, 0)),
            pl.BlockSpec((1, d), lambda b, i: (0, 0)),
        ],
        out_specs=pl.BlockSpec((None, tm, d), lambda b, i: (b, i, 0)),
        out_shape=jax.ShapeDtypeStruct((batch, n_lat, d), F32),
        compiler_params=_cparams(("arbitrary", "arbitrary")),
        name="final_combine_norm",
    )(x_all.reshape(batch, seq, d), mod, y_gath.reshape(TOP_K, batch, seq, d),
      top_w.reshape(batch, seq, LANE), gain)


def _dispatch_plan(top_e, rank, counts, n_tok):
    n_exp = counts.shape[0]
    bm = MOE_ROWS
    n_pairs = n_tok * TOP_K
    n_blk = -(-(n_pairs + n_exp * (bm - 1)) // bm)
    padded = (counts + bm - 1) // bm * bm
    pad_end = jnp.cumsum(padded)
    pad_start = pad_end - padded
    dest = rank
    for e in range(n_exp):
        dest = dest + jnp.where(top_e == e, pad_start[e], 0)
    n_active = (pad_end[-1] // bm).astype(jnp.int32).reshape(1)
    blk_start = jnp.arange(n_blk, dtype=jnp.int32) * bm
    blk_e = jnp.minimum(jnp.sum(pad_end[None, :] <= blk_start[:, None], axis=1), n_exp - 1).astype(jnp.int32)
    n_rows = n_blk * bm
    shift = (n_rows - 1).bit_length()
    assert (n_exp + 1) << shift < 2 ** 31
    t_ids = jnp.arange(n_tok, dtype=jnp.int32)[None, :]
    k_ids = jnp.arange(TOP_K, dtype=jnp.int32)[:, None]
    pair_keys = (top_e << shift) | (t_ids * TOP_K + k_ids)
    n_fill = n_rows - n_pairs
    per = bm - 1
    need = jnp.repeat(padded - counts, per)
    fill_e = jnp.where(jnp.tile(jnp.arange(per, dtype=jnp.int32), n_exp) < need,
                       jnp.repeat(jnp.arange(n_exp, dtype=jnp.int32), per), n_exp)
    fill_e = jnp.concatenate([fill_e, jnp.full((n_fill - n_exp * per,), n_exp, jnp.int32)])
    fill_keys = (fill_e << shift) | (n_pairs + jnp.arange(n_fill, dtype=jnp.int32))
    pos = lax.sort(jnp.concatenate([pair_keys.reshape(-1), fill_keys])) & ((1 << shift) - 1)
    row_tok = jnp.where(pos < n_pairs, pos // TOP_K, pos % n_tok)
    return dest, row_tok, blk_e, n_active


def kernel(x, c, ctx, c_ctx, w_mod, b_mod, norm_mix, norm_ffn, w_in, hy_conv_w, hy_conv_b, hy_ffn_w1, hy_ffn_b1, hy_ffn_w2, hy_ffn_b2, hy_ffn_w3, hy_bias, pool_w, pool_scale, ssm_conv_w, ssm_conv_b, ssm_dt_bias, ssm_a_log, ssm_d, ssm_norm, w_branch, w_gate, w_out, router_w, router_b, exp_w_up, exp_b_up, exp_w_down, exp_b_down, norm_final):
    batch, n_lat, d = x.shape
    n_ctx = ctx.shape[1]
    depth = w_mod.shape[0]
    assert n_lat // GRID_W * GRID_W == n_lat and n_lat % n_ctx == 0

    hw = hy_bias.shape[1]
    pw = pool_scale.shape[1]
    sw = ssm_norm.shape[1]
    cw = ssm_conv_w.shape[2]
    fw = w_in.shape[2] - 3 * hw - pw - sw - cw - 2 * SSM_HEADS
    widths = (fw, 3 * hw, pw, sw, cw)
    splits = (fw, fw + hw, fw + hw + pw, fw + hw + pw + sw)
    n_exp = router_w.shape[2]

    w_in_p = jnp.pad(w_in, ((0, 0), (0, 0), (0, LANE - 2 * SSM_HEADS))).astype(BF16)
    w_gate_b = w_gate.astype(BF16)
    w_branch_b = w_branch.astype(BF16)
    w_out_b = w_out.astype(BF16)
    pool_w_bd = jax.vmap(_block_diag)(pool_w).astype(BF16)
    r3 = lambda a: a.reshape(a.shape[0], 1, a.shape[1])
    lane_pad = lambda a: jnp.pad(a, ((0, 0), (0, LANE - a.shape[1])))
    norm_mix3, norm_ffn3 = r3(norm_mix), r3(norm_ffn)
    hy_conv_b3, hy_bias3 = r3(hy_conv_b), r3(hy_bias)
    ssm_conv_b3 = r3(ssm_conv_b)
    pool_scale3 = r3(pool_scale)
    dt_bias3 = r3(lane_pad(ssm_dt_bias.reshape(depth, 2 * SSM_HEADS)))
    a_row3 = r3(lane_pad(-jnp.exp(ssm_a_log.astype(F32)).reshape(depth, 2 * SSM_HEADS)))
    d_row3 = r3(jnp.repeat(ssm_d, SSM_HEAD_DIM, axis=1))
    ssm_norm3 = r3(ssm_norm)
    rw_t = jnp.swapaxes(router_w, 1, 2)
    rw_hi = rw_t.astype(BF16)
    rw_lo = (rw_t - rw_hi.astype(F32)).astype(BF16)
    router_b_col = jnp.broadcast_to(router_b[:, :, None], (depth, n_exp, LANE))
    exp_b_up4 = exp_b_up.reshape(depth, n_exp, 1, -1)
    exp_b_down4 = exp_b_down.reshape(depth, n_exp, 1, -1)
    emb = hy_ffn_w1.shape[1]
    ffn = hy_ffn_w1.shape[2]
    f_w1 = jnp.pad(hy_ffn_w1, ((0, 0), (0, LANE - emb), (0, LANE - ffn)))
    f_b1 = r3(lane_pad(hy_ffn_b1))
    f_w2 = jnp.pad(hy_ffn_w2, ((0, 0), (0, LANE - ffn), (0, LANE - ffn)))
    f_b2 = r3(lane_pad(hy_ffn_b2))
    f_w3 = jnp.pad(hy_ffn_w3, ((0, 0), (0, LANE - ffn), (0, 0)))

    freqs = jnp.linspace(1e-4, HYENA_BANDS - 1, HYENA_BANDS, dtype=F32)
    freqs_row = jnp.concatenate([jnp.zeros((1,), F32), freqs, freqs,
                                 jnp.zeros((LANE - 1 - 2 * HYENA_BANDS,), F32)])[None, :]
    deltas_row = jnp.linspace(HYENA_MIN_DECAY, HYENA_MAX_DECAY, hw, dtype=F32)[None, :]
    gi = jnp.arange(FOURIER_GROUP_W)
    ang = (gi[:, None] * gi[None, :] % FOURIER_GROUP_W).astype(F32) * (2.0 * math.pi / FOURIER_GROUP_W)
    n_fg = fw // FOURIER_GROUP_W
    eye = jnp.eye(n_fg, dtype=F32)
    wc = jnp.concatenate([jnp.kron(eye, jnp.cos(ang)), jnp.kron(eye, jnp.sin(ang))], axis=1).astype(BF16)
    band, invc = _pool_tables(n_lat, n_ctx)
    qi = jnp.arange(SSM_CHUNK)
    tri_f = (qi[:, None] >= qi[None, :]).astype(BF16)
    tri_b = (qi[:, None] <= qi[None, :]).astype(BF16)
    ti = jnp.arange(TOKEN_TILE)
    utri = (ti[:, None] < ti[None, :]).astype(BF16)

    seqs = {}
    for name, n, row_off in (("ctx", n_ctx, n_lat), ("lat", n_lat, 0)):
        fwd, inv = _hyena_dft_tables(n)
        g_all, nrm = _filter_mlp(n, freqs_row, deltas_row, f_w1, f_b1, f_w2, f_b2, f_w3)
        kr, ki = _filter_spectrum(fwd, g_all, nrm, n, hw)
        cn, sn = _trig_tables(n, n)
        seqs[name] = dict(n=n, off=row_off, cn=cn.astype(BF16), sn=sn.astype(BF16), fwd=fwd, inv=inv,
                          kr=kr, ki=ki)

    cc = jnp.concatenate([c, c_ctx[None, :]], axis=0)
    rows = -(-cc.shape[0] // 8) * 8
    mod = _modulation(jnp.pad(cc, ((0, rows - cc.shape[0]), (0, 0))), w_mod, b_mod)
    mod = mod.reshape(depth, rows, 6, 1, d)

    seq = n_lat + n_ctx

    def layer_step(tok, layer, x_all, moe):
        gb, n_tok = tok.batch, tok.n_tok
        res = _inproj(tok, layer, x_all, mod, norm_mix3, w_in_p, wc, band, invc, pool_w_bd, pool_scale3,
                      widths, moe=moe)
        if moe is not None:
            x_all = res[0]
            res = res[1:]
        xcs, hy, pool, z, xbc, dtr = res
        xcs3 = xcs.reshape(gb, seq, 2 * fw)
        hy3 = hy.reshape(gb, seq, 3 * hw)
        z3 = z.reshape(gb, seq, sw)
        xbc3 = xbc.reshape(gb, seq, cw)
        dt3 = dtr.reshape(gb, seq, LANE)
        outs = {}
        zero_state = jnp.zeros((gb, SSM_GROUPS, SSM_STATE, HEADS_PER_GROUP * SSM_HEAD_DIM), F32)
        states = (zero_state, zero_state)
        for name in ("ctx", "lat"):
            s = seqs[name]
            n, off = s["n"], s["off"]
            y_f = _fourier_seq(xcs3, s["cn"], s["sn"], n, off, fw)
            vv, x1 = _hy_pre(hy3, hy_conv_w, hy_conv_b3, layer, n, off, hw)
            y2 = _hy_fwd(s["fwd"], vv, s["kr"], s["ki"], layer, n, hw)
            y_h = _hy_inv(s["inv"], y2, x1, vv, hy_bias3, layer, n, hw)
            yf, st_f = _ssd_pass(xbc3, dt3, states[0], tri_f, ssm_conv_w, ssm_conv_b3, dt_bias3, a_row3,
                                 layer, n, off, reverse=False)
            y_s, st_b = _ssd_pass(xbc3, dt3, states[1], tri_b, ssm_conv_w, ssm_conv_b3, dt_bias3, a_row3,
                                  layer, n, off, reverse=True, fin=(z3, yf, d_row3, ssm_norm3))
            states = (st_f, st_b)
            outs[name] = (y_f, y_h, y_s)
        x_all = _merge(tok, layer, x_all, mod, norm_mix3, outs["lat"][0], outs["ctx"][0], outs["lat"][1],
                       outs["ctx"][1], pool, outs["lat"][2], outs["ctx"][2], w_gate_b, w_branch_b, w_out_b,
                       splits)
        h2, top_e, rank, top_w, counts = _router(tok, layer, x_all, mod, norm_ffn3, rw_hi, rw_lo, router_b_col,
                                                 utri)
        dest, row_tok, blk_e, n_active = _dispatch_plan(top_e[:TOP_K], rank[:TOP_K],
                                                        counts[:, 0].astype(jnp.int32), n_tok)
        x_sorted = h2[row_tok]
        y_rows = _moe(layer, x_sorted, blk_e, n_active, exp_w_up, exp_b_up4, exp_w_down, exp_b_down4)
        y_gath = y_rows[dest.reshape(-1)].reshape(TOP_K, n_tok, d)
        return x_all, (y_gath, top_w)

    n_groups = SAMPLE_GROUPS if batch % SAMPLE_GROUPS == 0 else 1
    gb = batch // n_groups
    groups = []
    for g in range(n_groups):
        tok = _Tok(gb, n_lat, n_ctx, mod_off=g * gb, mod_ctx=batch)
        x_all = jnp.concatenate([x[g * gb:(g + 1) * gb], ctx[g * gb:(g + 1) * gb]], axis=1)
        groups.append([tok, x_all.reshape(tok.n_tok, d), None])
    for layer in range(depth):
        for grp in groups:
            grp[1], grp[2] = layer_step(grp[0], layer, grp[1], grp[2])
    outs = [_final(tok, depth - 1, x_all, mod, moe[0], moe[1], norm_final[None, :])
            for tok, x_all, moe in groups]
    return outs[0] if n_groups == 1 else jnp.concatenate(outs, axis=0)
```

```python
import functools
import math

import jax
import jax.numpy as jnp
from jax import lax
from jax.experimental import pallas as pl
from jax.experimental.pallas import tpu as pltpu

F32 = jnp.float32
BF16 = jnp.bfloat16

GRID_W = 64
EPS = 1e-6
FOURIER_GROUP_W = 64
HYENA_BANDS = 16
HYENA_MIN_DECAY = -math.log(1e-2) / 1.5
HYENA_MAX_DECAY = -math.log(1e-2) / 0.3
POOL_WINDOWS = (2, 4, 8, 16)
POOL_GROUP_W = 64
SSM_GROUPS = 2
HEADS_PER_GROUP = 4
SSM_HEADS = SSM_GROUPS * HEADS_PER_GROUP
SSM_HEAD_DIM = 64
SSM_STATE = 128
SSM_CHUNK = 128
TOP_K = 4
SWIGLU_LIMIT = 7.0
SWIGLU_ALPHA = 1.702

LANE = 128
BF16_SUBLANE = 16
TOKEN_TILE = 256
INPROJ_TILES = 2
SEQ_TILE = 512
SEQ_KTILE = 1024
MOE_ROWS = 512
SSD_SAMPLES = 2
SAMPLE_GROUPS = 1
VMEM_LIMIT = 56 * 1024 * 1024


def _cparams(sem):
    return pltpu.CompilerParams(dimension_semantics=sem, vmem_limit_bytes=VMEM_LIMIT)


def _sigmoid(x):
    return 1.0 / (1.0 + jnp.exp(-x))


def _dot(a, b):
    return jnp.dot(a, b, preferred_element_type=F32)


def _dot_hi(a, b):
    return jnp.dot(a, b, preferred_element_type=F32, precision=lax.Precision.HIGHEST)


def _split3(x):
    x1 = x.astype(BF16)
    r1 = x - x1.astype(F32)
    x2 = r1.astype(BF16)
    x3 = (r1 - x2.astype(F32)).astype(BF16)
    return x1, x2, x3


def _dot_exact01(m01, x):
    x1, x2, x3 = _split3(x)
    return _dot(m01, x1) + _dot(m01, x2) + _dot(m01, x3)


def _mod_kernel(c_ref, w_ref, b_ref, o_ref):
    c = c_ref[...]
    s = c * _sigmoid(c)
    o_ref[...] = _dot_hi(s, w_ref[...]) + b_ref[...]


def _modulation(cc, w_mod, b_mod):
    depth, d, six_d = w_mod.shape
    rows = cc.shape[0]
    nj = six_d // d
    return pl.pallas_call(
        _mod_kernel,
        grid=(depth, nj),
        in_specs=[
            pl.BlockSpec((rows, d), lambda l, j: (0, 0)),
            pl.BlockSpec((None, d, d), lambda l, j: (l, 0, j)),
            pl.BlockSpec((None, 1, d), lambda l, j: (l, 0, j)),
        ],
        out_specs=pl.BlockSpec((None, rows, d), lambda l, j: (l, 0, j)),
        out_shape=jax.ShapeDtypeStruct((depth, rows, six_d), F32),
        compiler_params=_cparams(("arbitrary", "arbitrary")),
        name="modulation",
    )(cc, w_mod, b_mod.reshape(depth, 1, six_d))


class _Tok:
    def __init__(self, batch, n_lat, n_ctx, mod_off, mod_ctx):
        assert n_ctx == TOKEN_TILE and n_lat % TOKEN_TILE == 0
        self.batch, self.n_lat, self.n_ctx = batch, n_lat, n_ctx
        self.mod_off, self.mod_ctx = mod_off, mod_ctx
        self.seq = n_lat + n_ctx
        self.tiles_per_b = self.seq // TOKEN_TILE
        self.lat_tiles = n_lat // TOKEN_TILE
        self.n_tiles = batch * self.tiles_per_b
        self.n_tok = batch * self.seq

    def b(self, t):
        return t // self.tiles_per_b

    def w(self, t):
        return t % self.tiles_per_b

    def is_ctx(self, t):
        return (t % self.tiles_per_b) >= self.lat_tiles

    def mod_row(self, t):
        return jnp.where(self.is_ctx(t), self.mod_ctx, self.mod_off + self.b(t))

    def mod_spec(self, layer, part, d):
        return pl.BlockSpec((None, None, None, 1, d), lambda t: (layer, self.mod_row(t), part, 0, 0))

    def lat_spec(self, width):
        return pl.BlockSpec((None, TOKEN_TILE, width),
                            lambda t: (self.b(t), jnp.minimum(self.w(t), self.lat_tiles - 1), 0))

    def ctx_spec(self, width):
        return pl.BlockSpec((None, TOKEN_TILE, width), lambda t: (self.b(t), 0, 0))


def _norm_mod(x, gain, scale, shift):
    ms = jnp.mean(x * x, axis=-1, keepdims=True)
    return (x * lax.rsqrt(ms + EPS)) * gain * (1.0 + scale) + shift


def _moe_residual(x, g2, yg_ref, w_ref, rows=slice(None)):
    w = w_ref[rows, :]
    y = None
    for k in range(TOP_K):
        term = w[:, k:k + 1] * yg_ref[k, rows, :].astype(F32)
        y = term if y is None else y + term
    return x + g2 * y


def _inproj_kernel(*refs, widths, combine, n_sub):
    it = iter(refs)
    take = lambda k: [next(it) for _ in range(k)]
    x_ref, = take(1)
    if combine:
        g2_refs = take(n_sub)
        yg_ref, w_ref = take(2)
    sc_refs, sh_refs = take(n_sub), take(n_sub)
    g_ref, win_ref, wc_ref = take(3)
    band_refs, invc_refs = take(n_sub), take(n_sub)
    pw_ref, ps_ref = take(2)
    if combine:
        xn_ref, = take(1)
    xcs_ref, hy_ref, pool_ref, z_ref, xbc_ref, dt_ref = take(6)
    fw, hw, pw, zw, cw = widths
    tm = TOKEN_TILE
    for s in range(n_sub):
        rows = slice(s * tm, (s + 1) * tm)
        x = x_ref[rows, :]
        if combine:
            x = _moe_residual(x, g2_refs[s][...], yg_ref, w_ref, rows)
            xn_ref[rows, :] = x
        hb = _norm_mod(x, g_ref[...], sc_refs[s][...], sh_refs[s][...]).astype(BF16)
        o = 0
        f_in = _dot(hb, win_ref[:, o:o + fw])
        xcs_ref[rows, :] = _dot(f_in.astype(BF16), wc_ref[...]).astype(BF16)
        o += fw
        hy_ref[rows, :] = _dot(hb, win_ref[:, o:o + hw]).astype(BF16)
        o += hw
        u = _dot(hb, win_ref[:, o:o + pw])
        o += pw
        z_ref[rows, :] = _dot(hb, win_ref[:, o:o + zw]).astype(BF16)
        o += zw
        xbc_ref[rows, :] = _dot(hb, win_ref[:, o:o + cw]).astype(BF16)
        o += cw
        dt_ref[rows, :] = _dot(hb, win_ref[:, o:o + LANE])
        gw = POOL_GROUP_W
        parts = []
        for gi in range(len(POOL_WINDOWS)):
            ug = u[:, gi * gw:(gi + 1) * gw]
            parts.append(_dot_exact01(band_refs[s][gi], ug))
        pooled = jnp.concatenate(parts, axis=1) * invc_refs[s][...] - u
        pool_ref[rows, :] = (_dot(pooled.astype(BF16), pw_ref[...]) * ps_ref[...]).astype(BF16)


def _pool_tables(n_lat, n_ctx):
    tm = TOKEN_TILE
    p = jnp.arange(tm)
    bands, invs = [], []
    for row_len in (GRID_W, n_ctx):
        assert tm % row_len == 0
        pr, rr = p % row_len, p // row_len
        bg, ig = [], []
        for win in POOL_WINDOWS:
            lo = jnp.clip(pr - win // 2, 0, row_len)
            hi = jnp.clip(pr + win // 2, 0, row_len)
            q = pr[None, :]
            m = (rr[:, None] == rr[None, :]) & (q >= lo[:, None]) & (q < hi[:, None])
            bg.append(m.astype(BF16))
            ig.append(jnp.broadcast_to((1.0 / (hi - lo).astype(F32))[:, None], (tm, POOL_GROUP_W)))
        bands.append(jnp.stack(bg))
        invs.append(jnp.concatenate(ig, axis=1))
    return jnp.stack(bands), jnp.stack(invs)


def _block_diag(w):
    g, a, b = w.shape
    out = jnp.zeros((g * a, g * b), w.dtype)
    for i in range(g):
        out = out.at[i * a:(i + 1) * a, i * b:(i + 1) * b].set(w[i])
    return out


def _inproj(tok, layer, x_all, mod, norm_mix, w_in_p, wc, band, invc, pool_w_bd, pool_scale, widths,
            moe=None):
    d = x_all.shape[-1]
    fw, hw, pw, zw, cw = widths
    tm = TOKEN_TILE
    n_tok = tok.n_tok
    n_sub = INPROJ_TILES if tok.n_tiles % INPROJ_TILES == 0 else 1
    rows = n_sub * tm
    subs = range(n_sub)
    tile = lambda i, s: i * n_sub + s
    kind = lambda t: jnp.where(tok.is_ctx(t), 1, 0)
    const2 = lambda i: (0, 0)
    mod_specs = lambda lyr, part: [
        pl.BlockSpec((None, None, None, 1, d), lambda i, s=s: (lyr, tok.mod_row(tile(i, s)), part, 0, 0))
        for s in subs]
    pre_specs, pre_args, pre_outs = [], [], []
    if moe is not None:
        pre_specs = mod_specs(layer - 1, 5) + [
            pl.BlockSpec((TOP_K, rows, d), lambda i: (0, i, 0)),
            pl.BlockSpec((rows, LANE), lambda i: (i, 0))]
        pre_args = [mod] * n_sub + [moe[0], moe[1]]
        pre_outs = [jax.ShapeDtypeStruct((n_tok, d), F32)]
    outs = pre_outs + [
        jax.ShapeDtypeStruct((n_tok, 2 * fw), BF16),
        jax.ShapeDtypeStruct((n_tok, hw), BF16),
        jax.ShapeDtypeStruct((n_tok, pw), BF16),
        jax.ShapeDtypeStruct((n_tok, zw), BF16),
        jax.ShapeDtypeStruct((n_tok, cw), BF16),
        jax.ShapeDtypeStruct((n_tok, LANE), F32),
    ]
    return pl.pallas_call(
        functools.partial(_inproj_kernel, widths=widths, combine=moe is not None, n_sub=n_sub),
        grid=(tok.n_tiles // n_sub,),
        in_specs=[pl.BlockSpec((rows, d), lambda i: (i, 0))] + pre_specs
        + mod_specs(layer, 1) + mod_specs(layer, 0) + [
            pl.BlockSpec((None, 1, d), lambda i: (layer, 0, 0)),
            pl.BlockSpec((None,) + w_in_p.shape[1:], lambda i: (layer, 0, 0)),
            pl.BlockSpec(wc.shape, const2),
        ]
        + [pl.BlockSpec((None,) + band.shape[1:], lambda i, s=s: (kind(tile(i, s)), 0, 0, 0)) for s in subs]
        + [pl.BlockSpec((None,) + invc.shape[1:], lambda i, s=s: (kind(tile(i, s)), 0, 0)) for s in subs]
        + [
            pl.BlockSpec((None,) + pool_w_bd.shape[1:], lambda i: (layer, 0, 0)),
            pl.BlockSpec((None, 1, pw), lambda i: (layer, 0, 0)),
        ],
        out_specs=[pl.BlockSpec((rows, o.shape[1]), lambda i: (i, 0)) for o in outs],
        out_shape=outs,
        compiler_params=_cparams(("arbitrary",)),
        name="inproj",
    )(x_all, *pre_args, *([mod] * (2 * n_sub)), norm_mix, w_in_p, wc, *([band] * n_sub), *([invc] * n_sub),
      pool_w_bd, pool_scale)


def _trig_tables(n, period):
    k = jnp.arange(n, dtype=jnp.int32)

    def cs(j):
        m = (j[:, None] * k[None, :]) % period
        ang = m.astype(F32) * (2.0 * math.pi / period)
        return jnp.cos(ang), jnp.sin(ang)

    split = 64
    if n % split or n <= split:
        return cs(k)
    c1, s1 = cs(jnp.arange(n // split, dtype=jnp.int32) * split)
    c2, s2 = cs(jnp.arange(split, dtype=jnp.int32))
    c = c1[:, None, :] * c2[None, :, :] - s1[:, None, :] * s2[None, :, :]
    s = s1[:, None, :] * c2[None, :, :] + c1[:, None, :] * s2[None, :, :]
    return c.reshape(n, n), s.reshape(n, n)


def _hyena_dft_tables(n):
    c, s = _trig_tables(n, 2 * n)
    nyq = jnp.where(jnp.arange(n) % 2 == 0, 1.0, -1.0).astype(F32)
    fwd = jnp.concatenate([c, (-s).at[0].set(nyq)], axis=0).astype(BF16)
    inv = jnp.concatenate([c, (-s).at[:, 0].set(nyq)], axis=1).astype(BF16)
    return fwd, inv


def _fourier_kernel(c_ref, s_ref, xc_ref, xs_ref, o_ref, acc_ref, *, scale):
    k = pl.program_id(1)

    @pl.when(k == 0)
    def _():
        acc_ref[...] = jnp.zeros_like(acc_ref)

    nb = xc_ref.shape[0]
    rc = jnp.concatenate([xc_ref[b] for b in range(nb)], axis=1)
    rs = jnp.concatenate([xs_ref[b] for b in range(nb)], axis=1)
    acc_ref[...] += _dot(c_ref[...], rc) - _dot(s_ref[...], rs)

    @pl.when(k == pl.num_programs(1) - 1)
    def _():
        w = o_ref.shape[2]
        for b in range(nb):
            o_ref[b] = (acc_ref[:, b * w:(b + 1) * w] * scale).astype(o_ref.dtype)


def _fourier_seq(xcs3, cn, sn, n, row_off, fw):
    batch = xcs3.shape[0]
    tl = min(SEQ_TILE, n)
    tk = min(SEQ_KTILE, n)
    nt = n // tl
    off = row_off // tk
    scale = 1.0 / math.sqrt(n * FOURIER_GROUP_W)
    return pl.pallas_call(
        functools.partial(_fourier_kernel, scale=scale),
        grid=(nt, n // tk),
        in_specs=[
            pl.BlockSpec((tl, tk), lambda i, k: (i, k)),
            pl.BlockSpec((tl, tk), lambda i, k: (i, k)),
            pl.BlockSpec((batch, tk, fw), lambda i, k: (0, off + k, 0)),
            pl.BlockSpec((batch, tk, fw), lambda i, k: (0, off + k, 1)),
        ],
        out_specs=pl.BlockSpec((batch, tl, fw), lambda i, k: (0, i, 0)),
        out_shape=jax.ShapeDtypeStruct((batch, n, fw), BF16),
        scratch_shapes=[pltpu.VMEM((tl, batch * fw), F32)],
        compiler_params=_cparams(("arbitrary", "arbitrary")),
        name="fourier_seq",
    )(cn, sn, xcs3, xcs3)


def _shift_rows(x, prev_row, next_row):
    n = x.shape[0]
    rows = lax.broadcasted_iota(jnp.int32, x.shape, 0)
    xm = jnp.where(rows == 0, prev_row, pltpu.roll(x, 1, 0))
    xp = jnp.where(rows == n - 1, next_row, pltpu.roll(x, n - 1, 0))
    return xm, xp


def _conv3(x, prev_row, next_row, w_ref, b_ref):
    xm, xp = _shift_rows(x, prev_row, next_row)
    return w_ref[0:1, :] * xm + w_ref[1:2, :] * x + w_ref[2:3, :] * xp + b_ref[...]


def _hy_pre_kernel(x_ref, xp_ref, xn_ref, w_ref, b_ref, vv_ref, x1_ref, *, hw):
    i = pl.program_id(1)
    last = pl.num_programs(1) - 1
    x = x_ref[...].astype(F32)
    hs = xp_ref.shape[0]
    prev_row = jnp.where(i == 0, 0.0, xp_ref[hs - 1:hs, :].astype(F32))
    next_row = jnp.where(i == last, 0.0, xn_ref[0:1, :].astype(F32))
    u = _conv3(x, prev_row, next_row, w_ref, b_ref)
    x1_ref[...] = u[:, :hw].astype(BF16)
    vv_ref[...] = (u[:, hw:2 * hw] * u[:, 2 * hw:]).astype(BF16)


def _halo_specs(width, tile, row_off, seq_rows):
    hs = BF16_SUBLANE
    per = tile // hs
    base = row_off // hs
    top = seq_rows // hs - 1
    prev = pl.BlockSpec((None, hs, width), lambda b, i: (b, jnp.maximum(base + i * per - 1, 0), 0))
    nxt = pl.BlockSpec((None, hs, width), lambda b, i: (b, jnp.minimum(base + (i + 1) * per, top), 0))
    return prev, nxt


def _hy_pre(hy3, conv_w, conv_b, layer, n, row_off, hw):
    batch, seq, w3 = hy3.shape
    tl = min(SEQ_TILE, n)
    off = row_off // tl
    prev, nxt = _halo_specs(w3, tl, row_off, seq)
    out = jax.ShapeDtypeStruct((batch, n, hw), BF16)
    return pl.pallas_call(
        functools.partial(_hy_pre_kernel, hw=hw),
        grid=(batch, n // tl),
        in_specs=[
            pl.BlockSpec((None, tl, w3), lambda b, i: (b, off + i, 0)),
            prev, nxt,
            pl.BlockSpec((None,) + conv_w.shape[1:], lambda b, i: (layer, 0, 0)),
            pl.BlockSpec((None, 1, w3), lambda b, i: (layer, 0, 0)),
        ],
        out_specs=[pl.BlockSpec((None, tl, hw), lambda b, i: (b, i, 0))] * 2,
        out_shape=[out, out],
        compiler_params=_cparams(("arbitrary", "arbitrary")),
        name="hyena_pre",
    )(hy3, hy3, hy3, conv_w, conv_b)


def _filter_mlp_kernel(fr_ref, dl_ref, w1_ref, b1_ref, w2_ref, b2_ref, w3_ref, g_ref, nrm_ref, *, n, hw):
    j = pl.program_id(1)
    rt = g_ref.shape[0]
    lane = lax.broadcasted_iota(jnp.int32, (rt, LANE), 1)
    pos = (lax.broadcasted_iota(jnp.int32, (rt, LANE), 0) + j * rt).astype(F32)
    t = pos / (n - 1)
    ang = 2.0 * math.pi * pos / n
    arg = ang * fr_ref[...]
    feats = jnp.where(lane == 0, t,
                      jnp.where(lane <= HYENA_BANDS, jnp.cos(arg),
                                jnp.where(lane <= 2 * HYENA_BANDS, -jnp.sin(arg), 0.0)))
    h1 = jnp.sin(_dot_hi(feats, w1_ref[...]) + b1_ref[...])
    h2 = jnp.sin(_dot_hi(h1, w2_ref[...]) + b2_ref[...])
    k = _dot_hi(h2, w3_ref[...])
    decay = jnp.exp(-t[:, 0:1] * dl_ref[...])
    kf = k[:, :hw] * decay
    kb = jnp.where(pos[:, 0:1] == 0.0, 0.0, k[:, hw:] * decay)
    g_ref[:, :hw] = (kf + kb).astype(BF16)
    g_ref[:, hw:] = (kf - kb).astype(BF16)
    part = jnp.sum(jnp.abs(kf) + jnp.abs(kb), axis=0, keepdims=True)

    @pl.when(j == 0)
    def _():
        nrm_ref[...] = part

    @pl.when(j > 0)
    def _():
        nrm_ref[...] += part


def _filter_mlp(n, freqs_row, deltas_row, w1, b1, w2, b2, w3):
    depth = w1.shape[0]
    hw = deltas_row.shape[1]
    rt = min(SEQ_TILE, n)
    per_layer = lambda l, j: (l, 0, 0)
    return pl.pallas_call(
        functools.partial(_filter_mlp_kernel, n=n, hw=hw),
        grid=(depth, n // rt),
        in_specs=[
            pl.BlockSpec((1, LANE), lambda l, j: (0, 0)),
            pl.BlockSpec((1, hw), lambda l, j: (0, 0)),
            pl.BlockSpec((None,) + w1.shape[1:], per_layer),
            pl.BlockSpec((None,) + b1.shape[1:], per_layer),
            pl.BlockSpec((None,) + w2.shape[1:], per_layer),
            pl.BlockSpec((None,) + b2.shape[1:], per_layer),
            pl.BlockSpec((None,) + w3.shape[1:], per_layer),
        ],
        out_specs=[
            pl.BlockSpec((rt, 2 * hw), lambda l, j: (j, l)),
            pl.BlockSpec((None, 1, hw), lambda l, j: (l, 0, 0)),
        ],
        out_shape=[
            jax.ShapeDtypeStruct((n, depth * 2 * hw), BF16),
            jax.ShapeDtypeStruct((depth, 1, hw), F32),
        ],
        compiler_params=_cparams(("arbitrary", "arbitrary")),
        name="hyena_filter_mlp",
    )(freqs_row, deltas_row, w1, b1, w2, b2, w3)


def _rdft_accumulate(fc_ref, fs_ref, rhs, accc_ref, accs_ref):
    k = pl.program_id(1)

    @pl.when(k == 0)
    def _():
        accc_ref[...] = jnp.zeros_like(accc_ref)
        accs_ref[...] = jnp.zeros_like(accs_ref)

    accc_ref[...] += _dot(fc_ref[...], rhs)
    accs_ref[...] += _dot(fs_ref[...], rhs)


def _filter_spec_kernel(fc_ref, fs_ref, g_ref, nrm_ref, kr_ref, ki_ref, accc_ref, accs_ref, *, hw):
    i = pl.program_id(0)
    _rdft_accumulate(fc_ref, fs_ref, g_ref[...], accc_ref, accs_ref)

    @pl.when(pl.program_id(1) == pl.num_programs(1) - 1)
    def _():
        tm = accc_ref.shape[0]
        row0 = (lax.broadcasted_iota(jnp.int32, (tm, hw), 0) + i * tm) == 0
        for l in range(kr_ref.shape[0]):
            inv = 1.0 / nrm_ref[l]
            o = l * 2 * hw
            kr_ref[l] = accc_ref[:, o:o + hw] * inv
            ki_ref[l] = jnp.where(row0, accs_ref[:, o:o + hw], accs_ref[:, o + hw:o + 2 * hw]) * inv


def _filter_spectrum(fwd, g_all, nrm, n, hw):
    depth = nrm.shape[0]
    tl = min(SEQ_TILE, n)
    nt = n // tl
    width = g_all.shape[1]
    out = jax.ShapeDtypeStruct((depth, n, hw), F32)
    return pl.pallas_call(
        functools.partial(_filter_spec_kernel, hw=hw),
        grid=(nt, nt),
        in_specs=[
            pl.BlockSpec((tl, tl), lambda i, k: (i, k)),
            pl.BlockSpec((tl, tl), lambda i, k: (nt + i, k)),
            pl.BlockSpec((tl, width), lambda i, k: (k, 0)),
            pl.BlockSpec(nrm.shape, lambda i, k: (0, 0, 0)),
        ],
        out_specs=[pl.BlockSpec((depth, tl, hw), lambda i, k: (0, i, 0))] * 2,
        out_shape=[out, out],
        scratch_shapes=[pltpu.VMEM((tl, width), F32)] * 2,
        compiler_params=_cparams(("arbitrary", "arbitrary")),
        name="hyena_filter_spectrum",
    )(fwd, fwd, g_all, nrm)


def _hy_fwd_kernel(fc_ref, fs_ref, vv_ref, kr_ref, ki_ref, y_ref, accc_ref, accs_ref):
    i = pl.program_id(0)
    nb = vv_ref.shape[0]
    rhs = jnp.concatenate([vv_ref[b] for b in range(nb)], axis=1)
    _rdft_accumulate(fc_ref, fs_ref, rhs, accc_ref, accs_ref)

    @pl.when(pl.program_id(1) == pl.num_programs(1) - 1)
    def _():
        tm, hw = kr_ref.shape
        row0 = (lax.broadcasted_iota(jnp.int32, (tm, hw), 0) + i * tm) == 0
        kr, ki = kr_ref[...], ki_ref[...]
        for b in range(nb):
            vr = accc_ref[:, b * hw:(b + 1) * hw]
            vi = accs_ref[:, b * hw:(b + 1) * hw]
            yr = jnp.where(row0, 0.5 * vr * kr, vr * kr - vi * ki)
            yi = jnp.where(row0, 0.5 * vi * ki, vr * ki + vi * kr)
            y_ref[0, :, b * hw:(b + 1) * hw] = yr.astype(BF16)
            y_ref[1, :, b * hw:(b + 1) * hw] = yi.astype(BF16)


def _hy_fwd(fwd, vv, kr, ki, layer, n, hw):
    batch = vv.shape[0]
    tl = min(SEQ_TILE, n)
    tk = min(SEQ_KTILE, n)
    nt = n // tl
    return pl.pallas_call(
        _hy_fwd_kernel,
        grid=(nt, n // tk),
        in_specs=[
            pl.BlockSpec((tl, tk), lambda i, k: (i, k)),
            pl.BlockSpec((tl, tk), lambda i, k: (nt + i, k)),
            pl.BlockSpec((batch, tk, hw), lambda i, k: (0, k, 0)),
            pl.BlockSpec((None, tl, hw), lambda i, k: (layer, i, 0)),
            pl.BlockSpec((None, tl, hw), lambda i, k: (layer, i, 0)),
        ],
        out_specs=pl.BlockSpec((2, tl, batch * hw), lambda i, k: (0, i, 0)),
        out_shape=jax.ShapeDtypeStruct((2, n, batch * hw), BF16),
        scratch_shapes=[pltpu.VMEM((tl, batch * hw), F32)] * 2,
        compiler_params=_cparams(("arbitrary", "arbitrary")),
        name="hyena_fwd_dft",
    )(fwd, fwd, vv, kr, ki)


def _hy_inv_kernel(fi_ref, y_ref, x1_ref, vv_ref, bias_ref, o_ref, acc_ref, *, scale):
    k = pl.program_id(1)

    @pl.when(k == 0)
    def _():
        acc_ref[...] = jnp.zeros_like(acc_ref)

    acc_ref[...] += _dot(fi_ref[...], y_ref[...])

    @pl.when(k == pl.num_programs(1) - 1)
    def _():
        nb, _, hw = x1_ref.shape
        for b in range(nb):
            conv = acc_ref[:, b * hw:(b + 1) * hw] * scale
            vv = vv_ref[b].astype(F32)
            o_ref[b] = (x1_ref[b].astype(F32) * (conv + vv * bias_ref[...])).astype(BF16)


def _hy_inv(inv, y2, x1, vv, hy_bias, layer, n, hw):
    batch = vv.shape[0]
    tl = min(SEQ_TILE, n)
    tk = min(SEQ_KTILE, n)
    nt = n // tl
    return pl.pallas_call(
        functools.partial(_hy_inv_kernel, scale=1.0 / n),
        grid=(nt, 2 * n // tk),
        in_specs=[
            pl.BlockSpec((tl, tk), lambda i, k: (i, k)),
            pl.BlockSpec((tk, batch * hw), lambda i, k: (k, 0)),
            pl.BlockSpec((batch, tl, hw), lambda i, k: (0, i, 0)),
            pl.BlockSpec((batch, tl, hw), lambda i, k: (0, i, 0)),
            pl.BlockSpec((None, 1, hw), lambda i, k: (layer, 0, 0)),
        ],
        out_specs=pl.BlockSpec((batch, tl, hw), lambda i, k: (0, i, 0)),
        out_shape=jax.ShapeDtypeStruct((batch, n, hw), BF16),
        scratch_shapes=[pltpu.VMEM((tl, batch * hw), F32)],
        compiler_params=_cparams(("arbitrary", "arbitrary")),
        name="hyena_inv_dft",
    )(inv, y2.reshape(2 * n, batch * hw), x1, vv, hy_bias)


def _softplus(x):
    return jnp.maximum(x, 0.0) + jnp.log1p(jnp.exp(-jnp.abs(x)))


def _ssd_kernel(*refs, reverse, final, nc, lane0):
    if final:
        (xbc_ref, xp_ref, xn_ref, dt_ref, h0_ref, tri_ref, exp_ref, shift_ref, cw_ref, cb_ref, dtb_ref, a_ref,
         z_ref, yf_ref, d_ref, nrm_ref, out_ref, hout_ref, h_scr) = refs
    else:
        (xbc_ref, xp_ref, xn_ref, dt_ref, h0_ref, tri_ref, exp_ref, shift_ref, cw_ref, cb_ref, dtb_ref, a_ref,
         out_ref, hout_ref, h_scr) = refs
    step = pl.program_id(1)
    cc = (nc - 1 - step) if reverse else step

    @pl.when(step == 0)
    def _():
        h_scr[...] = h0_ref[...]

    nb, q = xbc_ref.shape[0], xbc_ref.shape[1]
    hs = xp_ref.shape[1]
    width = SSM_HEADS * SSM_HEAD_DIM
    gs = SSM_STATE
    pd = SSM_HEAD_DIM
    ri = lax.broadcasted_iota(jnp.int32, (q, q), 0)
    ci = lax.broadcasted_iota(jnp.int32, (q, q), 1)
    mask = (ri <= ci) if reverse else (ri >= ci)

    rows = lax.broadcasted_iota(jnp.int32, (q, 1), 0)
    for s in range(nb):
        xb = xbc_ref[s]
        prev_row = jnp.where(cc == 0, 0.0, xp_ref[s, hs - 1:hs, :].astype(F32))
        next_row = jnp.where(cc == nc - 1, 0.0, xn_ref[s, 0:1, :].astype(F32))
        sh = _dot(shift_ref[...], xb)
        xm = jnp.where(rows == 0, prev_row, sh[:q])
        xp = jnp.where(rows == q - 1, next_row, sh[q:])
        u = cw_ref[0:1, :] * xm + cw_ref[1:2, :] * xb.astype(F32) + cw_ref[2:3, :] * xp + cb_ref[...]
        u = u * _sigmoid(u)
        xs = u[:, :width]
        bm = u[:, width:width + SSM_GROUPS * gs]
        cm = u[:, width + SSM_GROUPS * gs:]

        dt = _softplus(dt_ref[s] + dtb_ref[...])
        acs = _dot_exact01(tri_ref[...], dt * a_ref[...])
        acs_t = acs.T
        tot = acs[0:1, :] if reverse else acs[q - 1:q, :]
        to_end = jnp.exp(tot - acs)
        frm = jnp.exp(acs)
        cdec = jnp.exp(tot)

        stacked = jnp.concatenate([dt, to_end, frm, jnp.broadcast_to(cdec, (8, LANE))], axis=0)
        s1, s2, s3 = _split3(stacked)
        ex = exp_ref[...]
        rep = _dot(s1, ex) + _dot(s2, ex) + _dot(s3, ex)
        xdt = xs * rep[0:q]
        wts = (xdt * rep[q:2 * q]).astype(BF16)
        frm_rep = rep[2 * q:3 * q]
        cdec_rep = rep[3 * q:3 * q + 1]
        xdt_b = xdt.astype(BF16)
        half = lax.broadcasted_iota(jnp.int32, (q, 2 * pd), 1) < pd

        ys = []
        gw = HEADS_PER_GROUP * pd
        for g in range(SSM_GROUPS):
            bg = bm[:, g * gs:(g + 1) * gs]
            cg = cm[:, g * gs:(g + 1) * gs].astype(BF16)
            scores = lax.dot_general(cg, bg.astype(BF16), (((1,), (1,)), ((), ())),
                                     preferred_element_type=F32)
            hg = h_scr[s, g]
            yoff = _dot(cg, hg.astype(BF16)) * frm_rep[:, g * gw:(g + 1) * gw]
            for pr in range(HEADS_PER_GROUP // 2):
                pair = []
                c0 = g * gw + pr * 2 * pd
                for j in range(2):
                    li = lane0 + g * HEADS_PER_GROUP + 2 * pr + j
                    lm = jnp.where(mask, jnp.exp(acs[:, li:li + 1] - acs_t[li:li + 1, :]), 0.0)
                    m = (scores * lm).astype(BF16)
                    pair.append(_dot(m, xdt_b[:, c0:c0 + 2 * pd]))
                ys.append(jnp.where(half, pair[0], pair[1]) + yoff[:, pr * 2 * pd:(pr + 1) * 2 * pd])
            st = _dot(bg.T.astype(BF16), wts[:, g * gw:(g + 1) * gw])
            h_scr[s, g] = hg * cdec_rep[:, g * gw:(g + 1) * gw] + st
        y = jnp.concatenate(ys, axis=1)

        if final:
            y = y + yf_ref[s].astype(F32) + xs * d_ref[...]
            z = z_ref[s].astype(F32)
            y = y * (z * _sigmoid(z))
            ms = jnp.mean(y * y, axis=-1, keepdims=True)
            out_ref[s] = ((y * lax.rsqrt(ms + EPS)) * nrm_ref[...]).astype(out_ref.dtype)
        else:
            out_ref[s] = y.astype(out_ref.dtype)

    @pl.when(step == nc - 1)
    def _():
        hout_ref[...] = h_scr[...]


def _ssd_pass(xbc3, dt3, h0, tri, conv_w, conv_b, dt_bias, a_row, layer, n, row_off, *, reverse,
              fin=None):
    batch, seq, cw = xbc3.shape
    q = SSM_CHUNK
    nc = n // q
    off = row_off // q
    width = SSM_HEADS * SSM_HEAD_DIM

    def cidx(c):
        return off + ((nc - 1 - c) if reverse else c)

    hsz = BF16_SUBLANE
    per = q // hsz
    top = seq // hsz - 1
    nb = SSD_SAMPLES if batch % SSD_SAMPLES == 0 else 1
    per_layer = lambda b, c: (layer, 0, 0)
    in_specs = [
        pl.BlockSpec((nb, q, cw), lambda b, c: (b, cidx(c), 0)),
        pl.BlockSpec((nb, hsz, cw), lambda b, c: (b, jnp.maximum(cidx(c) * per - 1, 0), 0)),
        pl.BlockSpec((nb, hsz, cw), lambda b, c: (b, jnp.minimum((cidx(c) + 1) * per, top), 0)),
        pl.BlockSpec((nb, q, LANE), lambda b, c: (b, cidx(c), 0)),
        pl.BlockSpec((nb,) + h0.shape[1:], lambda b, c: (b, 0, 0, 0)),
        pl.BlockSpec((q, q), lambda b, c: (0, 0)),
        pl.BlockSpec((LANE, width), lambda b, c: (0, 0)),
        pl.BlockSpec((2 * q, q), lambda b, c: (0, 0)),
        pl.BlockSpec((None,) + conv_w.shape[1:], per_layer),
        pl.BlockSpec((None, 1, cw), per_layer),
        pl.BlockSpec((None, 1, LANE), per_layer),
        pl.BlockSpec((None, 1, LANE), per_layer),
    ]
    lane0 = SSM_HEADS if reverse else 0
    head_rep = (jnp.arange(LANE)[:, None] == lane0 + jnp.arange(width)[None, :] // SSM_HEAD_DIM).astype(BF16)
    qi = jnp.arange(q)
    shift = jnp.concatenate([qi[:, None] - 1 == qi[None, :], qi[:, None] + 1 == qi[None, :]], axis=0).astype(BF16)
    args = [xbc3, xbc3, xbc3, dt3, h0, tri, head_rep, shift, conv_w, conv_b, dt_bias, a_row]
    if fin is not None:
        z3, yf, d_row, nrm_row = fin
        in_specs += [
            pl.BlockSpec((nb, q, width), lambda b, c: (b, cidx(c), 0)),
            pl.BlockSpec((nb, q, width), lambda b, c: (b, cidx(c) - off, 0)),
            pl.BlockSpec((None, 1, width), per_layer),
            pl.BlockSpec((None, 1, width), per_layer),
        ]
        args += [z3, yf, d_row, nrm_row]
    out_dtype = BF16
    return pl.pallas_call(
        functools.partial(_ssd_kernel, reverse=reverse, final=fin is not None, nc=nc, lane0=lane0),
        grid=(batch // nb, nc),
        in_specs=in_specs,
        out_specs=[
            pl.BlockSpec((nb, q, width), lambda b, c: (b, cidx(c) - off, 0)),
            pl.BlockSpec((nb,) + h0.shape[1:], lambda b, c: (b, 0, 0, 0)),
        ],
        out_shape=[
            jax.ShapeDtypeStruct((batch, n, width), out_dtype),
            jax.ShapeDtypeStruct(h0.shape, F32),
        ],
        scratch_shapes=[pltpu.VMEM((nb,) + h0.shape[1:], F32)],
        compiler_params=_cparams(("arbitrary", "arbitrary")),
        name="ssd_bwd_final" if reverse else "ssd_fwd",
    )(*args)


def _merge_kernel(x_ref, sc_ref, sh_ref, g1_ref, gn_ref, fl_ref, fc_ref, hl_ref, hc_ref, p_ref, sl_ref,
                  scx_ref, wg_ref, wb_ref, wo_ref, o_ref, *, tok, splits):
    ctx = tok.is_ctx(pl.program_id(0))
    x = x_ref[...]
    hb = _norm_mod(x, gn_ref[...], sc_ref[...], sh_ref[...]).astype(BF16)
    ys = (jnp.where(ctx, fc_ref[...], fl_ref[...]), jnp.where(ctx, hc_ref[...], hl_ref[...]),
          p_ref[...], jnp.where(ctx, scx_ref[...], sl_ref[...]))
    merged = None
    lo = 0
    for k, hi in enumerate(splits):
        gate = _sigmoid(_dot(hb, wg_ref[k]))
        term = gate * _dot(ys[k], wb_ref[lo:hi, :])
        merged = term if merged is None else merged + term
        lo = hi
    out = _dot(merged.astype(BF16), wo_ref[...])
    o_ref[...] = x + g1_ref[...] * out


def _merge(tok, layer, x_all, mod, norm_mix, f_lat, f_ctx, h_lat, h_ctx, pool, s_lat, s_ctx,
           w_gate, w_branch, w_out, splits):
    d = x_all.shape[-1]
    tm = TOKEN_TILE
    fw, hw, pw, sw = f_lat.shape[-1], h_lat.shape[-1], pool.shape[-1], s_lat.shape[-1]
    return pl.pallas_call(
        functools.partial(_merge_kernel, tok=tok, splits=splits),
        grid=(tok.n_tiles,),
        in_specs=[
            pl.BlockSpec((tm, d), lambda t: (t, 0)),
            tok.mod_spec(layer, 1, d),
            tok.mod_spec(layer, 0, d),
            tok.mod_spec(layer, 2, d),
            pl.BlockSpec((None, 1, d), lambda t: (layer, 0, 0)),
            tok.lat_spec(fw), tok.ctx_spec(fw),
            tok.lat_spec(hw), tok.ctx_spec(hw),
            pl.BlockSpec((tm, pw), lambda t: (t, 0)),
            tok.lat_spec(sw), tok.ctx_spec(sw),
            pl.BlockSpec((None,) + w_gate.shape[1:], lambda t: (layer, 0, 0, 0)),
            pl.BlockSpec((None,) + w_branch.shape[1:], lambda t: (layer, 0, 0)),
            pl.BlockSpec((None,) + w_out.shape[1:], lambda t: (layer, 0, 0)),
        ],
        out_specs=pl.BlockSpec((tm, d), lambda t: (t, 0)),
        out_shape=jax.ShapeDtypeStruct(x_all.shape, F32),
        compiler_params=_cparams(("arbitrary",)),
        name="merge",
    )(x_all, mod, mod, mod, norm_mix, f_lat, f_ctx, h_lat, h_ctx, pool, s_lat, s_ctx,
      w_gate, w_branch, w_out)


def _router_kernel(x_ref, sc_ref, sh_ref, gn_ref, rwh_ref, rwl_ref, rb_ref, utri_ref,
                   h_ref, e_ref, r_ref, w_ref, cnt_ref):
    t = pl.program_id(0)

    @pl.when(t == 0)
    def _():
        cnt_ref[...] = jnp.zeros_like(cnt_ref)

    h = _norm_mod(x_ref[...], gn_ref[...], sc_ref[...], sh_ref[...])
    hh = h.astype(BF16)
    h_ref[...] = hh
    hl = (h - hh.astype(F32)).astype(BF16)
    tm = h.shape[0]
    n_exp = rwh_ref.shape[0]
    nt = (((1,), (1,)), ((), ()))
    dg = lambda a, b: lax.dot_general(a, b, nt, preferred_element_type=F32)
    rwh = rwh_ref[...]
    logits = dg(rwh, hh) + dg(rwh, hl) + dg(rwl_ref[...], hh) + rb_ref[:, 0:1]
    eid = lax.broadcasted_iota(jnp.int32, (n_exp, tm), 0)
    carry = cnt_ref[:, 0:1]
    tops, sels = [], []
    cur = logits
    for _ in range(TOP_K):
        m = jnp.max(cur, axis=0, keepdims=True)
        idx = jnp.min(jnp.where(cur == m, eid, n_exp), axis=0, keepdims=True)
        sel = eid == idx
        tops.append((m, idx))
        sels.append(sel)
        cur = jnp.where(sel, -jnp.inf, cur)
    cnt = jnp.zeros((n_exp, tm), F32)
    for s in sels:
        cnt = jnp.where(s, 1.0, cnt)
    before = _dot(cnt.astype(BF16), utri_ref[...]) + carry
    exps = [jnp.exp(m - tops[0][0]) for m, _ in tops]
    den = exps[0]
    for v in exps[1:]:
        den = den + v
    row8 = lax.broadcasted_iota(jnp.int32, (8, tm), 0)
    rowl = lax.broadcasted_iota(jnp.int32, (LANE, tm), 0)
    e_out = jnp.zeros((8, tm), jnp.int32)
    r_out = jnp.zeros((8, tm), jnp.int32)
    w_t = jnp.zeros((LANE, tm), F32)
    for k in range(TOP_K):
        rank = jnp.sum(jnp.where(sels[k], before, 0.0), axis=0, keepdims=True)
        e_out = jnp.where(row8 == k, tops[k][1], e_out)
        r_out = jnp.where(row8 == k, rank.astype(jnp.int32), r_out)
        w_t = jnp.where(rowl == k, exps[k] / den, w_t)
    e_ref[...] = e_out
    r_ref[...] = r_out
    w_ref[...] = w_t.T
    cnt_ref[...] += jnp.broadcast_to(jnp.sum(cnt, axis=1, keepdims=True), cnt_ref.shape)


def _router(tok, layer, x_all, mod, norm_ffn, rw_hi, rw_lo, router_b_col, utri):
    d = x_all.shape[-1]
    tm = TOKEN_TILE
    n_tok = tok.n_tok
    n_exp = rw_hi.shape[1]
    row_i = jax.ShapeDtypeStruct((8, n_tok), jnp.int32)
    return pl.pallas_call(
        _router_kernel,
        grid=(tok.n_tiles,),
        in_specs=[
            pl.BlockSpec((tm, d), lambda t: (t, 0)),
            tok.mod_spec(layer, 4, d),
            tok.mod_spec(layer, 3, d),
            pl.BlockSpec((None, 1, d), lambda t: (layer, 0, 0)),
            pl.BlockSpec((None, n_exp, d), lambda t: (layer, 0, 0)),
            pl.BlockSpec((None, n_exp, d), lambda t: (layer, 0, 0)),
            pl.BlockSpec((None, n_exp, LANE), lambda t: (layer, 0, 0)),
            pl.BlockSpec((tm, tm), lambda t: (0, 0)),
        ],
        out_specs=[
            pl.BlockSpec((tm, d), lambda t: (t, 0)),
            pl.BlockSpec((8, tm), lambda t: (0, t)),
            pl.BlockSpec((8, tm), lambda t: (0, t)),
            pl.BlockSpec((tm, LANE), lambda t: (t, 0)),
            pl.BlockSpec((n_exp, LANE), lambda t: (0, 0)),
        ],
        out_shape=[
            jax.ShapeDtypeStruct((n_tok, d), BF16),
            row_i,
            row_i,
            jax.ShapeDtypeStruct((n_tok, LANE), F32),
            jax.ShapeDtypeStruct((n_exp, LANE), F32),
        ],
        compiler_params=_cparams(("arbitrary",)),
        name="router",
    )(x_all, mod, mod, norm_ffn, rw_hi, rw_lo, router_b_col, utri)


def _moe_kernel(be_ref, na_ref, x_ref, wu_ref, bu_ref, wd_ref, bd_ref, o_ref, wu_scr, wd_scr, *, ff):
    i = pl.program_id(0)
    active = i < na_ref[0]
    fresh = jnp.logical_or(i == 0, be_ref[i] != be_ref[jnp.maximum(i - 1, 0)])

    @pl.when(jnp.logical_and(active, fresh))
    def _():
        wu_scr[...] = wu_ref[...].astype(BF16)
        wd_scr[...] = wd_ref[...].astype(BF16)

    @pl.when(active)
    def _():
        hu = _dot(x_ref[...], wu_scr[...]) + bu_ref[...]
        glu = jnp.minimum(hu[:, :ff], SWIGLU_LIMIT)
        lin = jnp.clip(hu[:, ff:], -SWIGLU_LIMIT, SWIGLU_LIMIT)
        act = glu * _sigmoid(SWIGLU_ALPHA * glu) * (lin + 1.0)
        o_ref[...] = (_dot(act.astype(BF16), wd_scr[...]) + bd_ref[...]).astype(o_ref.dtype)

    @pl.when(jnp.logical_not(active))
    def _():
        o_ref[...] = jnp.zeros_like(o_ref)


def _moe(layer, x_sorted, blk_e, n_active, w_up, b_up, w_down, b_down):
    n_rows, d = x_sorted.shape
    ff = w_down.shape[2]
    bm = MOE_ROWS
    grid_spec = pltpu.PrefetchScalarGridSpec(
        num_scalar_prefetch=2,
        grid=(n_rows // bm,),
        in_specs=[
            pl.BlockSpec((bm, d), lambda i, be, na: (i, 0)),
            pl.BlockSpec((None, None, d, 2 * ff), lambda i, be, na: (layer, be[i], 0, 0)),
            pl.BlockSpec((None, None, 1, 2 * ff), lambda i, be, na: (layer, be[i], 0, 0)),
            pl.BlockSpec((None, None, ff, d), lambda i, be, na: (layer, be[i], 0, 0)),
            pl.BlockSpec((None, None, 1, d), lambda i, be, na: (layer, be[i], 0, 0)),
        ],
        out_specs=pl.BlockSpec((bm, d), lambda i, be, na: (i, 0)),
        scratch_shapes=[pltpu.VMEM((d, 2 * ff), BF16), pltpu.VMEM((ff, d), BF16)],
    )
    return pl.pallas_call(
        functools.partial(_moe_kernel, ff=ff),
        grid_spec=grid_spec,
        out_shape=jax.ShapeDtypeStruct((n_rows, d), BF16),
        compiler_params=_cparams(("arbitrary",)),
        name="moe_experts",
    )(blk_e, n_active, x_sorted, w_up, b_up, w_down, b_down)


def _final_kernel(x_ref, g2_ref, yg_ref, w_ref, g_ref, o_ref):
    x = _moe_residual(x_ref[...], g2_ref[...], yg_ref, w_ref)
    ms = jnp.mean(x * x, axis=-1, keepdims=True)
    o_ref[...] = (x * lax.rsqrt(ms + EPS)) * g_ref[...]


def _final(tok, layer, x_all, mod, y_gath, top_w, gain):
    batch, seq, n_lat = tok.batch, tok.seq, tok.n_lat
    d = x_all.shape[-1]
    tm = TOKEN_TILE
    return pl.pallas_call(
        _final_kernel,
        grid=(batch, n_lat // tm),
        in_specs=[
            pl.BlockSpec((None, tm, d), lambda b, i: (b, i, 0)),
            pl.BlockSpec((None, None, None, 1, d), lambda b, i: (layer, tok.mod_off + b, 5, 0, 0)),
            pl.BlockSpec((TOP_K, None, tm, d), lambda b, i: (0, b, i, 0)),
            pl.BlockSpec((None, tm, LANE), lambda b, i: (b, i, 0)),
            pl.BlockSpec((1, d), lambda b, i: (0, 0)),
        ],
        out_specs=pl.BlockSpec((None, tm, d), lambda b, i: (b, i, 0)),
        out_shape=jax.ShapeDtypeStruct((batch, n_lat, d), F32),
        compiler_params=_cparams(("arbitrary", "arbitrary")),
        name="final_combine_norm",
    )(x_all.reshape(batch, seq, d), mod, y_gath.reshape(TOP_K, batch, seq, d),
      top_w.reshape(batch, seq, LANE), gain)


def _dispatch_plan(top_e, rank, counts, n_tok):
    n_exp = counts.shape[0]
    bm = MOE_ROWS
    n_pairs = n_tok * TOP_K
    n_blk = -(-(n_pairs + n_exp * (bm - 1)) // bm)
    padded = (counts + bm - 1) // bm * bm
    pad_end = jnp.cumsum(padded)
    pad_start = pad_end - padded
    dest = rank
    for e in range(n_exp):
        dest = dest + jnp.where(top_e == e, pad_start[e], 0)
    n_active = (pad_end[-1] // bm).astype(jnp.int32).reshape(1)
    blk_start = jnp.arange(n_blk, dtype=jnp.int32) * bm
    blk_e = jnp.minimum(jnp.sum(pad_end[None, :] <= blk_start[:, None], axis=1), n_exp - 1).astype(jnp.int32)
    n_rows = n_blk * bm
    shift = (n_rows - 1).bit_length()
    assert (n_exp + 1) << shift < 2 ** 31
    t_ids = jnp.arange(n_tok, dtype=jnp.int32)[None, :]
    k_ids = jnp.arange(TOP_K, dtype=jnp.int32)[:, None]
    pair_keys = (top_e << shift) | (t_ids * TOP_K + k_ids)
    n_fill = n_rows - n_pairs
    per = bm - 1
    need = jnp.repeat(padded - counts, per)
    fill_e = jnp.where(jnp.tile(jnp.arange(per, dtype=jnp.int32), n_exp) < need,
                       jnp.repeat(jnp.arange(n_exp, dtype=jnp.int32), per), n_exp)
    fill_e = jnp.concatenate([fill_e, jnp.full((n_fill - n_exp * per,), n_exp, jnp.int32)])
    fill_keys = (fill_e << shift) | (n_pairs + jnp.arange(n_fill, dtype=jnp.int32))
    pos = lax.sort(jnp.concatenate([pair_keys.reshape(-1), fill_keys])) & ((1 << shift) - 1)
    row_tok = jnp.where(pos < n_pairs, pos // TOP_K, pos % n_tok)
    return dest, row_tok, blk_e, n_active


def kernel(x, c, ctx, c_ctx, w_mod, b_mod, norm_mix, norm_ffn, w_in, hy_conv_w, hy_conv_b, hy_ffn_w1, hy_ffn_b1, hy_ffn_w2, hy_ffn_b2, hy_ffn_w3, hy_bias, pool_w, pool_scale, ssm_conv_w, ssm_conv_b, ssm_dt_bias, ssm_a_log, ssm_d, ssm_norm, w_branch, w_gate, w_out, router_w, router_b, exp_w_up, exp_b_up, exp_w_down, exp_b_down, norm_final):
    batch, n_lat, d = x.shape
    n_ctx = ctx.shape[1]
    depth = w_mod.shape[0]
    assert n_lat // GRID_W * GRID_W == n_lat and n_lat % n_ctx == 0

    hw = hy_bias.shape[1]
    pw = pool_scale.shape[1]
    sw = ssm_norm.shape[1]
    cw = ssm_conv_w.shape[2]
    fw = w_in.shape[2] - 3 * hw - pw - sw - cw - 2 * SSM_HEADS
    widths = (fw, 3 * hw, pw, sw, cw)
    splits = (fw, fw + hw, fw + hw + pw, fw + hw + pw + sw)
    n_exp = router_w.shape[2]

    w_in_p = jnp.pad(w_in, ((0, 0), (0, 0), (0, LANE - 2 * SSM_HEADS))).astype(BF16)
    w_gate_b = w_gate.astype(BF16)
    w_branch_b = w_branch.astype(BF16)
    w_out_b = w_out.astype(BF16)
    pool_w_bd = jax.vmap(_block_diag)(pool_w).astype(BF16)
    r3 = lambda a: a.reshape(a.shape[0], 1, a.shape[1])
    lane_pad = lambda a: jnp.pad(a, ((0, 0), (0, LANE - a.shape[1])))
    norm_mix3, norm_ffn3 = r3(norm_mix), r3(norm_ffn)
    hy_conv_b3, hy_bias3 = r3(hy_conv_b), r3(hy_bias)
    ssm_conv_b3 = r3(ssm_conv_b)
    pool_scale3 = r3(pool_scale)
    dt_bias3 = r3(lane_pad(ssm_dt_bias.reshape(depth, 2 * SSM_HEADS)))
    a_row3 = r3(lane_pad(-jnp.exp(ssm_a_log.astype(F32)).reshape(depth, 2 * SSM_HEADS)))
    d_row3 = r3(jnp.repeat(ssm_d, SSM_HEAD_DIM, axis=1))
    ssm_norm3 = r3(ssm_norm)
    rw_t = jnp.swapaxes(router_w, 1, 2)
    rw_hi = rw_t.astype(BF16)
    rw_lo = (rw_t - rw_hi.astype(F32)).astype(BF16)
    router_b_col = jnp.broadcast_to(router_b[:, :, None], (depth, n_exp, LANE))
    exp_b_up4 = exp_b_up.reshape(depth, n_exp, 1, -1)
    exp_b_down4 = exp_b_down.reshape(depth, n_exp, 1, -1)
    emb = hy_ffn_w1.shape[1]
    ffn = hy_ffn_w1.shape[2]
    f_w1 = jnp.pad(hy_ffn_w1, ((0, 0), (0, LANE - emb), (0, LANE - ffn)))
    f_b1 = r3(lane_pad(hy_ffn_b1))
    f_w2 = jnp.pad(hy_ffn_w2, ((0, 0), (0, LANE - ffn), (0, LANE - ffn)))
    f_b2 = r3(lane_pad(hy_ffn_b2))
    f_w3 = jnp.pad(hy_ffn_w3, ((0, 0), (0, LANE - ffn), (0, 0)))

    freqs = jnp.linspace(1e-4, HYENA_BANDS - 1, HYENA_BANDS, dtype=F32)
    freqs_row = jnp.concatenate([jnp.zeros((1,), F32), freqs, freqs,
                                 jnp.zeros((LANE - 1 - 2 * HYENA_BANDS,), F32)])[None, :]
    deltas_row = jnp.linspace(HYENA_MIN_DECAY, HYENA_MAX_DECAY, hw, dtype=F32)[None, :]
    gi = jnp.arange(FOURIER_GROUP_W)
    ang = (gi[:, None] * gi[None, :] % FOURIER_GROUP_W).astype(F32) * (2.0 * math.pi / FOURIER_GROUP_W)
    n_fg = fw // FOURIER_GROUP_W
    eye = jnp.eye(n_fg, dtype=F32)
    wc = jnp.concatenate([jnp.kron(eye, jnp.cos(ang)), jnp.kron(eye, jnp.sin(ang))], axis=1).astype(BF16)
    band, invc = _pool_tables(n_lat, n_ctx)
    qi = jnp.arange(SSM_CHUNK)
    tri_f = (qi[:, None] >= qi[None, :]).astype(BF16)
    tri_b = (qi[:, None] <= qi[None, :]).astype(BF16)
    ti = jnp.arange(TOKEN_TILE)
    utri = (ti[:, None] < ti[None, :]).astype(BF16)

    seqs = {}
    for name, n, row_off in (("ctx", n_ctx, n_lat), ("lat", n_lat, 0)):
        fwd, inv = _hyena_dft_tables(n)
        g_all, nrm = _filter_mlp(n, freqs_row, deltas_row, f_w1, f_b1, f_w2, f_b2, f_w3)
        kr, ki = _filter_spectrum(fwd, g_all, nrm, n, hw)
        cn, sn = _trig_tables(n, n)
        seqs[name] = dict(n=n, off=row_off, cn=cn.astype(BF16), sn=sn.astype(BF16), fwd=fwd, inv=inv,
                          kr=kr, ki=ki)

    cc = jnp.concatenate([c, c_ctx[None, :]], axis=0)
    rows = -(-cc.shape[0] // 8) * 8
    mod = _modulation(jnp.pad(cc, ((0, rows - cc.shape[0]), (0, 0))), w_mod, b_mod)
    mod = mod.reshape(depth, rows, 6, 1, d)

    seq = n_lat + n_ctx

    def layer_step(tok, layer, x_all, moe):
        gb, n_tok = tok.batch, tok.n_tok
        res = _inproj(tok, layer, x_all, mod, norm_mix3, w_in_p, wc, band, invc, pool_w_bd, pool_scale3,
                      widths, moe=moe)
        if moe is not None:
            x_all = res[0]
            res = res[1:]
        xcs, hy, pool, z, xbc, dtr = res
        xcs3 = xcs.reshape(gb, seq, 2 * fw)
        hy3 = hy.reshape(gb, seq, 3 * hw)
        z3 = z.reshape(gb, seq, sw)
        xbc3 = xbc.reshape(gb, seq, cw)
        dt3 = dtr.reshape(gb, seq, LANE)
        outs = {}
        zero_state = jnp.zeros((gb, SSM_GROUPS, SSM_STATE, HEADS_PER_GROUP * SSM_HEAD_DIM), F32)
        states = (zero_state, zero_state)
        for name in ("ctx", "lat"):
            s = seqs[name]
            n, off = s["n"], s["off"]
            y_f = _fourier_seq(xcs3, s["cn"], s["sn"], n, off, fw)
            vv, x1 = _hy_pre(hy3, hy_conv_w, hy_conv_b3, layer, n, off, hw)
            y2 = _hy_fwd(s["fwd"], vv, s["kr"], s["ki"], layer, n, hw)
            y_h = _hy_inv(s["inv"], y2, x1, vv, hy_bias3, layer, n, hw)
            yf, st_f = _ssd_pass(xbc3, dt3, states[0], tri_f, ssm_conv_w, ssm_conv_b3, dt_bias3, a_row3,
                                 layer, n, off, reverse=False)
            y_s, st_b = _ssd_pass(xbc3, dt3, states[1], tri_b, ssm_conv_w, ssm_conv_b3, dt_bias3, a_row3,
                                  layer, n, off, reverse=True, fin=(z3, yf, d_row3, ssm_norm3))
            states = (st_f, st_b)
            outs[name] = (y_f, y_h, y_s)
        x_all = _merge(tok, layer, x_all, mod, norm_mix3, outs["lat"][0], outs["ctx"][0], outs["lat"][1],
                       outs["ctx"][1], pool, outs["lat"][2], outs["ctx"][2], w_gate_b, w_branch_b, w_out_b,
                       splits)
        h2, top_e, rank, top_w, counts = _router(tok, layer, x_all, mod, norm_ffn3, rw_hi, rw_lo, router_b_col,
                                                 utri)
        dest, row_tok, blk_e, n_active = _dispatch_plan(top_e[:TOP_K], rank[:TOP_K],
                                                        counts[:, 0].astype(jnp.int32), n_tok)
        x_sorted = h2[row_tok]
        y_rows = _moe(layer, x_sorted, blk_e, n_active, exp_w_up, exp_b_up4, exp_w_down, exp_b_down4)
        y_gath = y_rows[dest.reshape(-1)].reshape(TOP_K, n_tok, d)
        return x_all, (y_gath, top_w)

    n_groups = SAMPLE_GROUPS if batch % SAMPLE_GROUPS == 0 else 1
    gb = batch // n_groups
    groups = []
    for g in range(n_groups):
        tok = _Tok(gb, n_lat, n_ctx, mod_off=g * gb, mod_ctx=batch)
        x_all = jnp.concatenate([x[g * gb:(g + 1) * gb], ctx[g * gb:(g + 1) * gb]], axis=1)
        groups.append([tok, x_all.reshape(tok.n_tok, d), None])
    for layer in range(depth):
        for grp in groups:
            grp[1], grp[2] = layer_step(grp[0], layer, grp[1], grp[2])
    outs = [_final(tok, depth - 1, x_all, mod, moe[0], moe[1], norm_final[None, :])
            for tok, x_all, moe in groups]
    return outs[0] if n_groups == 1 else jnp.concatenate(outs, axis=0)
```

```python
import functools
import math

import jax
import jax.numpy as jnp
from jax import lax
from jax.experimental import pallas as pl
from jax.experimental.pallas import tpu as pltpu

F32 = jnp.float32
BF16 = jnp.bfloat16

GRID_W = 64
EPS = 1e-6
FOURIER_GROUP_W = 64
HYENA_BANDS = 16
HYENA_MIN_DECAY = -math.log(1e-2) / 1.5
HYENA_MAX_DECAY = -math.log(1e-2) / 0.3
POOL_WINDOWS = (2, 4, 8, 16)
POOL_GROUP_W = 64
SSM_GROUPS = 2
HEADS_PER_GROUP = 4
SSM_HEADS = SSM_GROUPS * HEADS_PER_GROUP
SSM_HEAD_DIM = 64
SSM_STATE = 128
SSM_CHUNK = 128
TOP_K = 4
SWIGLU_LIMIT = 7.0
SWIGLU_ALPHA = 1.702

LANE = 128
BF16_SUBLANE = 16
TOKEN_TILE = 256
INPROJ_TILES = 2
MERGE_TILES = 2
SEQ_TILE = 512
SEQ_KTILE = 1024
MOE_ROWS = 512
SSD_SAMPLES = 4
SAMPLE_GROUPS = 1
VMEM_LIMIT = 56 * 1024 * 1024


def _cparams(sem):
    return pltpu.CompilerParams(dimension_semantics=sem, vmem_limit_bytes=VMEM_LIMIT)


def _sigmoid(x):
    return 1.0 / (1.0 + jnp.exp(-x))


def _dot(a, b):
    return jnp.dot(a, b, preferred_element_type=F32)


def _dot_hi(a, b):
    return jnp.dot(a, b, preferred_element_type=F32, precision=lax.Precision.HIGHEST)


def _split3(x):
    x1 = x.astype(BF16)
    r1 = x - x1.astype(F32)
    x2 = r1.astype(BF16)
    x3 = (r1 - x2.astype(F32)).astype(BF16)
    return x1, x2, x3


def _dot_exact01(m01, x):
    x1, x2, x3 = _split3(x)
    return _dot(m01, x1) + _dot(m01, x2) + _dot(m01, x3)


def _mod_kernel(c_ref, w_ref, b_ref, o_ref):
    c = c_ref[...]
    s = c * _sigmoid(c)
    o_ref[...] = _dot_hi(s, w_ref[...]) + b_ref[...]


def _modulation(cc, w_mod, b_mod):
    depth, d, six_d = w_mod.shape
    rows = cc.shape[0]
    nj = six_d // d
    return pl.pallas_call(
        _mod_kernel,
        grid=(depth, nj),
        in_specs=[
            pl.BlockSpec((rows, d), lambda l, j: (0, 0)),
            pl.BlockSpec((None, d, d), lambda l, j: (l, 0, j)),
            pl.BlockSpec((None, 1, d), lambda l, j: (l, 0, j)),
        ],
        out_specs=pl.BlockSpec((None, rows, d), lambda l, j: (l, 0, j)),
        out_shape=jax.ShapeDtypeStruct((depth, rows, six_d), F32),
        compiler_params=_cparams(("arbitrary", "arbitrary")),
        name="modulation",
    )(cc, w_mod, b_mod.reshape(depth, 1, six_d))


class _Tok:
    def __init__(self, batch, n_lat, n_ctx, mod_off, mod_ctx):
        assert n_ctx == TOKEN_TILE and n_lat % TOKEN_TILE == 0
        self.batch, self.n_lat, self.n_ctx = batch, n_lat, n_ctx
        self.mod_off, self.mod_ctx = mod_off, mod_ctx
        self.seq = n_lat + n_ctx
        self.tiles_per_b = self.seq // TOKEN_TILE
        self.lat_tiles = n_lat // TOKEN_TILE
        self.n_tiles = batch * self.tiles_per_b
        self.n_tok = batch * self.seq

    def b(self, t):
        return t // self.tiles_per_b

    def w(self, t):
        return t % self.tiles_per_b

    def is_ctx(self, t):
        return (t % self.tiles_per_b) >= self.lat_tiles

    def mod_row(self, t):
        return jnp.where(self.is_ctx(t), self.mod_ctx, self.mod_off + self.b(t))

    def mod_spec(self, layer, part, d, tile=lambda i: i):
        return pl.BlockSpec((None, None, None, 1, d), lambda i: (layer, self.mod_row(tile(i)), part, 0, 0))

    def lat_spec(self, width, tile=lambda i: i):
        return pl.BlockSpec((None, TOKEN_TILE, width),
                            lambda i: (self.b(tile(i)), jnp.minimum(self.w(tile(i)), self.lat_tiles - 1), 0))

    def ctx_spec(self, width, tile=lambda i: i):
        return pl.BlockSpec((None, TOKEN_TILE, width), lambda i: (self.b(tile(i)), 0, 0))


def _norm_mod(x, gain, scale, shift):
    ms = jnp.mean(x * x, axis=-1, keepdims=True)
    return (x * lax.rsqrt(ms + EPS)) * gain * (1.0 + scale) + shift


def _moe_residual(x, g2, yg_ref, w_ref, rows=slice(None)):
    w = w_ref[rows, :]
    y = None
    for k in range(TOP_K):
        term = w[:, k:k + 1] * yg_ref[k, rows, :].astype(F32)
        y = term if y is None else y + term
    return x + g2 * y


def _inproj_kernel(*refs, widths, combine, n_sub):
    it = iter(refs)
    take = lambda k: [next(it) for _ in range(k)]
    x_ref, = take(1)
    if combine:
        g2_refs = take(n_sub)
        yg_ref, w_ref = take(2)
    sc_refs, sh_refs = take(n_sub), take(n_sub)
    g_ref, win_ref, wc_ref = take(3)
    band_refs, invc_refs = take(n_sub), take(n_sub)
    pw_ref, ps_ref = take(2)
    if combine:
        xn_ref, = take(1)
    xcs_ref, hy_ref, pool_ref, z_ref, xbc_ref, dt_ref = take(6)
    fw, hw, pw, zw, cw = widths
    tm = TOKEN_TILE
    for s in range(n_sub):
        rows = slice(s * tm, (s + 1) * tm)
        x = x_ref[rows, :]
        if combine:
            x = _moe_residual(x, g2_refs[s][...], yg_ref, w_ref, rows)
            xn_ref[rows, :] = x
        hb = _norm_mod(x, g_ref[...], sc_refs[s][...], sh_refs[s][...]).astype(BF16)
        o = 0
        f_in = _dot(hb, win_ref[:, o:o + fw])
        xcs_ref[rows, :] = _dot(f_in.astype(BF16), wc_ref[...]).astype(BF16)
        o += fw
        hy_ref[rows, :] = _dot(hb, win_ref[:, o:o + hw]).astype(BF16)
        o += hw
        u = _dot(hb, win_ref[:, o:o + pw])
        o += pw
        z_ref[rows, :] = _dot(hb, win_ref[:, o:o + zw]).astype(BF16)
        o += zw
        xbc_ref[rows, :] = _dot(hb, win_ref[:, o:o + cw]).astype(BF16)
        o += cw
        dt_ref[rows, :] = _dot(hb, win_ref[:, o:o + LANE])
        gw = POOL_GROUP_W
        parts = []
        for gi in range(len(POOL_WINDOWS)):
            ug = u[:, gi * gw:(gi + 1) * gw]
            parts.append(_dot_exact01(band_refs[s][gi], ug))
        pooled = jnp.concatenate(parts, axis=1) * invc_refs[s][...] - u
        pool_ref[rows, :] = (_dot(pooled.astype(BF16), pw_ref[...]) * ps_ref[...]).astype(BF16)


def _pool_tables(n_lat, n_ctx):
    tm = TOKEN_TILE
    p = jnp.arange(tm)
    bands, invs = [], []
    for row_len in (GRID_W, n_ctx):
        assert tm % row_len == 0
        pr, rr = p % row_len, p // row_len
        bg, ig = [], []
        for win in POOL_WINDOWS:
            lo = jnp.clip(pr - win // 2, 0, row_len)
            hi = jnp.clip(pr + win // 2, 0, row_len)
            q = pr[None, :]
            m = (rr[:, None] == rr[None, :]) & (q >= lo[:, None]) & (q < hi[:, None])
            bg.append(m.astype(BF16))
            ig.append(jnp.broadcast_to((1.0 / (hi - lo).astype(F32))[:, None], (tm, POOL_GROUP_W)))
        bands.append(jnp.stack(bg))
        invs.append(jnp.concatenate(ig, axis=1))
    return jnp.stack(bands), jnp.stack(invs)


def _block_diag(w):
    g, a, b = w.shape
    out = jnp.zeros((g * a, g * b), w.dtype)
    for i in range(g):
        out = out.at[i * a:(i + 1) * a, i * b:(i + 1) * b].set(w[i])
    return out


def _inproj(tok, layer, x_all, mod, norm_mix, w_in_p, wc, band, invc, pool_w_bd, pool_scale, widths,
            moe=None):
    d = x_all.shape[-1]
    fw, hw, pw, zw, cw = widths
    tm = TOKEN_TILE
    n_tok = tok.n_tok
    n_sub = INPROJ_TILES if tok.n_tiles % INPROJ_TILES == 0 else 1
    rows = n_sub * tm
    subs = range(n_sub)
    tile = lambda i, s: i * n_sub + s
    kind = lambda t: jnp.where(tok.is_ctx(t), 1, 0)
    const2 = lambda i: (0, 0)
    mod_specs = lambda lyr, part: [
        pl.BlockSpec((None, None, None, 1, d), lambda i, s=s: (lyr, tok.mod_row(tile(i, s)), part, 0, 0))
        for s in subs]
    pre_specs, pre_args, pre_outs = [], [], []
    if moe is not None:
        pre_specs = mod_specs(layer - 1, 5) + [
            pl.BlockSpec((TOP_K, rows, d), lambda i: (0, i, 0)),
            pl.BlockSpec((rows, LANE), lambda i: (i, 0))]
        pre_args = [mod] * n_sub + [moe[0], moe[1]]
        pre_outs = [jax.ShapeDtypeStruct((n_tok, d), F32)]
    outs = pre_outs + [
        jax.ShapeDtypeStruct((n_tok, 2 * fw), BF16),
        jax.ShapeDtypeStruct((n_tok, hw), BF16),
        jax.ShapeDtypeStruct((n_tok, pw), BF16),
        jax.ShapeDtypeStruct((n_tok, zw), BF16),
        jax.ShapeDtypeStruct((n_tok, cw), BF16),
        jax.ShapeDtypeStruct((n_tok, LANE), F32),
    ]
    return pl.pallas_call(
        functools.partial(_inproj_kernel, widths=widths, combine=moe is not None, n_sub=n_sub),
        grid=(tok.n_tiles // n_sub,),
        in_specs=[pl.BlockSpec((rows, d), lambda i: (i, 0))] + pre_specs
        + mod_specs(layer, 1) + mod_specs(layer, 0) + [
            pl.BlockSpec((None, 1, d), lambda i: (layer, 0, 0)),
            pl.BlockSpec((None,) + w_in_p.shape[1:], lambda i: (layer, 0, 0)),
            pl.BlockSpec(wc.shape, const2),
        ]
        + [pl.BlockSpec((None,) + band.shape[1:], lambda i, s=s: (kind(tile(i, s)), 0, 0, 0)) for s in subs]
        + [pl.BlockSpec((None,) + invc.shape[1:], lambda i, s=s: (kind(tile(i, s)), 0, 0)) for s in subs]
        + [
            pl.BlockSpec((None,) + pool_w_bd.shape[1:], lambda i: (layer, 0, 0)),
            pl.BlockSpec((None, 1, pw), lambda i: (layer, 0, 0)),
        ],
        out_specs=[pl.BlockSpec((rows, o.shape[1]), lambda i: (i, 0)) for o in outs],
        out_shape=outs,
        compiler_params=_cparams(("arbitrary",)),
        name="inproj",
    )(x_all, *pre_args, *([mod] * (2 * n_sub)), norm_mix, w_in_p, wc, *([band] * n_sub), *([invc] * n_sub),
      pool_w_bd, pool_scale)


def _trig_tables(n, period):
    k = jnp.arange(n, dtype=jnp.int32)

    def cs(j):
        m = (j[:, None] * k[None, :]) % period
        ang = m.astype(F32) * (2.0 * math.pi / period)
        return jnp.cos(ang), jnp.sin(ang)

    split = 64
    if n % split or n <= split:
        return cs(k)
    c1, s1 = cs(jnp.arange(n // split, dtype=jnp.int32) * split)
    c2, s2 = cs(jnp.arange(split, dtype=jnp.int32))
    c = c1[:, None, :] * c2[None, :, :] - s1[:, None, :] * s2[None, :, :]
    s = s1[:, None, :] * c2[None, :, :] + c1[:, None, :] * s2[None, :, :]
    return c.reshape(n, n), s.reshape(n, n)


def _hyena_dft_tables(n):
    c, s = _trig_tables(n, 2 * n)
    nyq = jnp.where(jnp.arange(n) % 2 == 0, 1.0, -1.0).astype(F32)
    fwd = jnp.concatenate([c, (-s).at[0].set(nyq)], axis=0).astype(BF16)
    inv = jnp.concatenate([c, (-s).at[:, 0].set(nyq)], axis=1).astype(BF16)
    return fwd, inv


def _fourier_kernel(c_ref, s_ref, xc_ref, xs_ref, o_ref, acc_ref, *, scale):
    k = pl.program_id(1)

    @pl.when(k == 0)
    def _():
        acc_ref[...] = jnp.zeros_like(acc_ref)

    nb = xc_ref.shape[0]
    rc = jnp.concatenate([xc_ref[b] for b in range(nb)], axis=1)
    rs = jnp.concatenate([xs_ref[b] for b in range(nb)], axis=1)
    acc_ref[...] += _dot(c_ref[...], rc) - _dot(s_ref[...], rs)

    @pl.when(k == pl.num_programs(1) - 1)
    def _():
        w = o_ref.shape[2]
        for b in range(nb):
            o_ref[b] = (acc_ref[:, b * w:(b + 1) * w] * scale).astype(o_ref.dtype)


def _fourier_seq(xcs3, cn, sn, n, row_off, fw):
    batch = xcs3.shape[0]
    tl = min(SEQ_TILE, n)
    tk = min(SEQ_KTILE, n)
    nt = n // tl
    off = row_off // tk
    scale = 1.0 / math.sqrt(n * FOURIER_GROUP_W)
    return pl.pallas_call(
        functools.partial(_fourier_kernel, scale=scale),
        grid=(nt, n // tk),
        in_specs=[
            pl.BlockSpec((tl, tk), lambda i, k: (i, k)),
            pl.BlockSpec((tl, tk), lambda i, k: (i, k)),
            pl.BlockSpec((batch, tk, fw), lambda i, k: (0, off + k, 0)),
            pl.BlockSpec((batch, tk, fw), lambda i, k: (0, off + k, 1)),
        ],
        out_specs=pl.BlockSpec((batch, tl, fw), lambda i, k: (0, i, 0)),
        out_shape=jax.ShapeDtypeStruct((batch, n, fw), BF16),
        scratch_shapes=[pltpu.VMEM((tl, batch * fw), F32)],
        compiler_params=_cparams(("arbitrary", "arbitrary")),
        name="fourier_seq",
    )(cn, sn, xcs3, xcs3)


def _shift_rows(x, prev_row, next_row):
    n = x.shape[0]
    rows = lax.broadcasted_iota(jnp.int32, x.shape, 0)
    xm = jnp.where(rows == 0, prev_row, pltpu.roll(x, 1, 0))
    xp = jnp.where(rows == n - 1, next_row, pltpu.roll(x, n - 1, 0))
    return xm, xp


def _conv3(x, prev_row, next_row, w_ref, b_ref):
    xm, xp = _shift_rows(x, prev_row, next_row)
    return w_ref[0:1, :] * xm + w_ref[1:2, :] * x + w_ref[2:3, :] * xp + b_ref[...]


def _hy_pre_kernel(x_ref, xp_ref, xn_ref, w_ref, b_ref, vv_ref, x1_ref, *, hw):
    i = pl.program_id(1)
    last = pl.num_programs(1) - 1
    x = x_ref[...].astype(F32)
    hs = xp_ref.shape[0]
    prev_row = jnp.where(i == 0, 0.0, xp_ref[hs - 1:hs, :].astype(F32))
    next_row = jnp.where(i == last, 0.0, xn_ref[0:1, :].astype(F32))
    u = _conv3(x, prev_row, next_row, w_ref, b_ref)
    x1_ref[...] = u[:, :hw].astype(BF16)
    vv_ref[...] = (u[:, hw:2 * hw] * u[:, 2 * hw:]).astype(BF16)


def _halo_specs(width, tile, row_off, seq_rows):
    hs = BF16_SUBLANE
    per = tile // hs
    base = row_off // hs
    top = seq_rows // hs - 1
    prev = pl.BlockSpec((None, hs, width), lambda b, i: (b, jnp.maximum(base + i * per - 1, 0), 0))
    nxt = pl.BlockSpec((None, hs, width), lambda b, i: (b, jnp.minimum(base + (i + 1) * per, top), 0))
    return prev, nxt


def _hy_pre(hy3, conv_w, conv_b, layer, n, row_off, hw):
    batch, seq, w3 = hy3.shape
    tl = min(SEQ_TILE, n)
    off = row_off // tl
    prev, nxt = _halo_specs(w3, tl, row_off, seq)
    out = jax.ShapeDtypeStruct((batch, n, hw), BF16)
    return pl.pallas_call(
        functools.partial(_hy_pre_kernel, hw=hw),
        grid=(batch, n // tl),
        in_specs=[
            pl.BlockSpec((None, tl, w3), lambda b, i: (b, off + i, 0)),
            prev, nxt,
            pl.BlockSpec((None,) + conv_w.shape[1:], lambda b, i: (layer, 0, 0)),
            pl.BlockSpec((None, 1, w3), lambda b, i: (layer, 0, 0)),
        ],
        out_specs=[pl.BlockSpec((None, tl, hw), lambda b, i: (b, i, 0))] * 2,
        out_shape=[out, out],
        compiler_params=_cparams(("arbitrary", "arbitrary")),
        name="hyena_pre",
    )(hy3, hy3, hy3, conv_w, conv_b)


def _filter_mlp_kernel(fr_ref, dl_ref, w1_ref, b1_ref, w2_ref, b2_ref, w3_ref, g_ref, nrm_ref, *, n, hw):
    j = pl.program_id(1)
    rt = g_ref.shape[0]
    lane = lax.broadcasted_iota(jnp.int32, (rt, LANE), 1)
    pos = (lax.broadcasted_iota(jnp.int32, (rt, LANE), 0) + j * rt).astype(F32)
    t = pos / (n - 1)
    ang = 2.0 * math.pi * pos / n
    arg = ang * fr_ref[...]
    feats = jnp.where(lane == 0, t,
                      jnp.where(lane <= HYENA_BANDS, jnp.cos(arg),
                                jnp.where(lane <= 2 * HYENA_BANDS, -jnp.sin(arg), 0.0)))
    h1 = jnp.sin(_dot_hi(feats, w1_ref[...]) + b1_ref[...])
    h2 = jnp.sin(_dot_hi(h1, w2_ref[...]) + b2_ref[...])
    k = _dot_hi(h2, w3_ref[...])
    decay = jnp.exp(-t[:, 0:1] * dl_ref[...])
    kf = k[:, :hw] * decay
    kb = jnp.where(pos[:, 0:1] == 0.0, 0.0, k[:, hw:] * decay)
    g_ref[:, :hw] = (kf + kb).astype(BF16)
    g_ref[:, hw:] = (kf - kb).astype(BF16)
    part = jnp.sum(jnp.abs(kf) + jnp.abs(kb), axis=0, keepdims=True)

    @pl.when(j == 0)
    def _():
        nrm_ref[...] = part

    @pl.when(j > 0)
    def _():
        nrm_ref[...] += part


def _filter_mlp(n, freqs_row, deltas_row, w1, b1, w2, b2, w3):
    depth = w1.shape[0]
    hw = deltas_row.shape[1]
    rt = min(SEQ_TILE, n)
    per_layer = lambda l, j: (l, 0, 0)
    return pl.pallas_call(
        functools.partial(_filter_mlp_kernel, n=n, hw=hw),
        grid=(depth, n // rt),
        in_specs=[
            pl.BlockSpec((1, LANE), lambda l, j: (0, 0)),
            pl.BlockSpec((1, hw), lambda l, j: (0, 0)),
            pl.BlockSpec((None,) + w1.shape[1:], per_layer),
            pl.BlockSpec((None,) + b1.shape[1:], per_layer),
            pl.BlockSpec((None,) + w2.shape[1:], per_layer),
            pl.BlockSpec((None,) + b2.shape[1:], per_layer),
            pl.BlockSpec((None,) + w3.shape[1:], per_layer),
        ],
        out_specs=[
            pl.BlockSpec((rt, 2 * hw), lambda l, j: (j, l)),
            pl.BlockSpec((None, 1, hw), lambda l, j: (l, 0, 0)),
        ],
        out_shape=[
            jax.ShapeDtypeStruct((n, depth * 2 * hw), BF16),
            jax.ShapeDtypeStruct((depth, 1, hw), F32),
        ],
        compiler_params=_cparams(("arbitrary", "arbitrary")),
        name="hyena_filter_mlp",
    )(freqs_row, deltas_row, w1, b1, w2, b2, w3)


def _rdft_accumulate(fc_ref, fs_ref, rhs, accc_ref, accs_ref):
    k = pl.program_id(1)

    @pl.when(k == 0)
    def _():
        accc_ref[...] = jnp.zeros_like(accc_ref)
        accs_ref[...] = jnp.zeros_like(accs_ref)

    accc_ref[...] += _dot(fc_ref[...], rhs)
    accs_ref[...] += _dot(fs_ref[...], rhs)


def _filter_spec_kernel(fc_ref, fs_ref, g_ref, nrm_ref, kr_ref, ki_ref, accc_ref, accs_ref, *, hw):
    i = pl.program_id(0)
    _rdft_accumulate(fc_ref, fs_ref, g_ref[...], accc_ref, accs_ref)

    @pl.when(pl.program_id(1) == pl.num_programs(1) - 1)
    def _():
        tm = accc_ref.shape[0]
        row0 = (lax.broadcasted_iota(jnp.int32, (tm, hw), 0) + i * tm) == 0
        for l in range(kr_ref.shape[0]):
            inv = 1.0 / nrm_ref[l]
            o = l * 2 * hw
            kr_ref[l] = accc_ref[:, o:o + hw] * inv
            ki_ref[l] = jnp.where(row0, accs_ref[:, o:o + hw], accs_ref[:, o + hw:o + 2 * hw]) * inv


def _filter_spectrum(fwd, g_all, nrm, n, hw):
    depth = nrm.shape[0]
    tl = min(SEQ_TILE, n)
    nt = n // tl
    width = g_all.shape[1]
    out = jax.ShapeDtypeStruct((depth, n, hw), F32)
    return pl.pallas_call(
        functools.partial(_filter_spec_kernel, hw=hw),
        grid=(nt, nt),
        in_specs=[
            pl.BlockSpec((tl, tl), lambda i, k: (i, k)),
            pl.BlockSpec((tl, tl), lambda i, k: (nt + i, k)),
            pl.BlockSpec((tl, width), lambda i, k: (k, 0)),
            pl.BlockSpec(nrm.shape, lambda i, k: (0, 0, 0)),
        ],
        out_specs=[pl.BlockSpec((depth, tl, hw), lambda i, k: (0, i, 0))] * 2,
        out_shape=[out, out],
        scratch_shapes=[pltpu.VMEM((tl, width), F32)] * 2,
        compiler_params=_cparams(("arbitrary", "arbitrary")),
        name="hyena_filter_spectrum",
    )(fwd, fwd, g_all, nrm)


def _hy_fwd_kernel(fc_ref, fs_ref, vv_ref, kr_ref, ki_ref, y_ref, accc_ref, accs_ref):
    i = pl.program_id(0)
    nb = vv_ref.shape[0]
    rhs = jnp.concatenate([vv_ref[b] for b in range(nb)], axis=1)
    _rdft_accumulate(fc_ref, fs_ref, rhs, accc_ref, accs_ref)

    @pl.when(pl.program_id(1) == pl.num_programs(1) - 1)
    def _():
        tm, hw = kr_ref.shape
        row0 = (lax.broadcasted_iota(jnp.int32, (tm, hw), 0) + i * tm) == 0
        kr, ki = kr_ref[...], ki_ref[...]
        for b in range(nb):
            vr = accc_ref[:, b * hw:(b + 1) * hw]
            vi = accs_ref[:, b * hw:(b + 1) * hw]
            yr = jnp.where(row0, 0.5 * vr * kr, vr * kr - vi * ki)
            yi = jnp.where(row0, 0.5 * vi * ki, vr * ki + vi * kr)
            y_ref[0, :, b * hw:(b + 1) * hw] = yr.astype(BF16)
            y_ref[1, :, b * hw:(b + 1) * hw] = yi.astype(BF16)


def _hy_fwd(fwd, vv, kr, ki, layer, n, hw):
    batch = vv.shape[0]
    tl = min(SEQ_TILE, n)
    tk = min(SEQ_KTILE, n)
    nt = n // tl
    return pl.pallas_call(
        _hy_fwd_kernel,
        grid=(nt, n // tk),
        in_specs=[
            pl.BlockSpec((tl, tk), lambda i, k: (i, k)),
            pl.BlockSpec((tl, tk), lambda i, k: (nt + i, k)),
            pl.BlockSpec((batch, tk, hw), lambda i, k: (0, k, 0)),
            pl.BlockSpec((None, tl, hw), lambda i, k: (layer, i, 0)),
            pl.BlockSpec((None, tl, hw), lambda i, k: (layer, i, 0)),
        ],
        out_specs=pl.BlockSpec((2, tl, batch * hw), lambda i, k: (0, i, 0)),
        out_shape=jax.ShapeDtypeStruct((2, n, batch * hw), BF16),
        scratch_shapes=[pltpu.VMEM((tl, batch * hw), F32)] * 2,
        compiler_params=_cparams(("arbitrary", "arbitrary")),
        name="hyena_fwd_dft",
    )(fwd, fwd, vv, kr, ki)


def _hy_inv_kernel(fi_ref, y_ref, x1_ref, vv_ref, bias_ref, o_ref, acc_ref, *, scale):
    k = pl.program_id(1)

    @pl.when(k == 0)
    def _():
        acc_ref[...] = jnp.zeros_like(acc_ref)

    acc_ref[...] += _dot(fi_ref[...], y_ref[...])

    @pl.when(k == pl.num_programs(1) - 1)
    def _():
        nb, _, hw = x1_ref.shape
        for b in range(nb):
            conv = acc_ref[:, b * hw:(b + 1) * hw] * scale
            vv = vv_ref[b].astype(F32)
            o_ref[b] = (x1_ref[b].astype(F32) * (conv + vv * bias_ref[...])).astype(BF16)


def _hy_inv(inv, y2, x1, vv, hy_bias, layer, n, hw):
    batch = vv.shape[0]
    tl = min(SEQ_TILE, n)
    tk = min(SEQ_KTILE, n)
    nt = n // tl
    return pl.pallas_call(
        functools.partial(_hy_inv_kernel, scale=1.0 / n),
        grid=(nt, 2 * n // tk),
        in_specs=[
            pl.BlockSpec((tl, tk), lambda i, k: (i, k)),
            pl.BlockSpec((tk, batch * hw), lambda i, k: (k, 0)),
            pl.BlockSpec((batch, tl, hw), lambda i, k: (0, i, 0)),
            pl.BlockSpec((batch, tl, hw), lambda i, k: (0, i, 0)),
            pl.BlockSpec((None, 1, hw), lambda i, k: (layer, 0, 0)),
        ],
        out_specs=pl.BlockSpec((batch, tl, hw), lambda i, k: (0, i, 0)),
        out_shape=jax.ShapeDtypeStruct((batch, n, hw), BF16),
        scratch_shapes=[pltpu.VMEM((tl, batch * hw), F32)],
        compiler_params=_cparams(("arbitrary", "arbitrary")),
        name="hyena_inv_dft",
    )(inv, y2.reshape(2 * n, batch * hw), x1, vv, hy_bias)


def _softplus(x):
    return jnp.maximum(x, 0.0) + jnp.log1p(jnp.exp(-jnp.abs(x)))


def _ssd_kernel(*refs, reverse, final, nc, lane0):
    if final:
        (xbc_ref, xp_ref, xn_ref, dt_ref, h0_ref, tri_ref, exp_ref, cw_ref, cb_ref, dtb_ref, a_ref,
         z_ref, yf_ref, d_ref, nrm_ref, out_ref, hout_ref, h_scr) = refs
    else:
        (xbc_ref, xp_ref, xn_ref, dt_ref, h0_ref, tri_ref, exp_ref, cw_ref, cb_ref, dtb_ref, a_ref,
         out_ref, hout_ref, h_scr) = refs
    step = pl.program_id(1)
    cc = (nc - 1 - step) if reverse else step

    @pl.when(step == 0)
    def _():
        h_scr[...] = h0_ref[...]

    nb, q = xbc_ref.shape[0], xbc_ref.shape[1]
    hs = xp_ref.shape[1]
    width = SSM_HEADS * SSM_HEAD_DIM
    gs = SSM_STATE
    pd = SSM_HEAD_DIM
    ri = lax.broadcasted_iota(jnp.int32, (q, q), 0)
    ci = lax.broadcasted_iota(jnp.int32, (q, q), 1)
    mask = (ri <= ci) if reverse else (ri >= ci)

    for s in range(nb):
        x = xbc_ref[s].astype(F32)
        prev_row = jnp.where(cc == 0, 0.0, xp_ref[s, hs - 1:hs, :].astype(F32))
        next_row = jnp.where(cc == nc - 1, 0.0, xn_ref[s, 0:1, :].astype(F32))
        u = _conv3(x, prev_row, next_row, cw_ref, cb_ref)
        u = u * _sigmoid(u)
        xs = u[:, :width]
        bm = u[:, width:width + SSM_GROUPS * gs]
        cm = u[:, width + SSM_GROUPS * gs:]

        dt = _softplus(dt_ref[s] + dtb_ref[...])
        acs = _dot_exact01(tri_ref[...], dt * a_ref[...])
        acs_t = acs.T
        tot = acs[0:1, :] if reverse else acs[q - 1:q, :]
        to_end = jnp.exp(tot - acs)
        frm = jnp.exp(acs)
        cdec = jnp.exp(tot)

        stacked = jnp.concatenate([dt, to_end, frm, jnp.broadcast_to(cdec, (8, LANE))], axis=0)
        s1, s2, s3 = _split3(stacked)
        ex = exp_ref[...]
        rep = _dot(s1, ex) + _dot(s2, ex) + _dot(s3, ex)
        xdt = xs * rep[0:q]
        wts = (xdt * rep[q:2 * q]).astype(BF16)
        frm_rep = rep[2 * q:3 * q]
        cdec_rep = rep[3 * q:3 * q + 1]
        xdt_b = xdt.astype(BF16)
        half = lax.broadcasted_iota(jnp.int32, (q, 2 * pd), 1) < pd

        ys = []
        gw = HEADS_PER_GROUP * pd
        for g in range(SSM_GROUPS):
            bg = bm[:, g * gs:(g + 1) * gs]
            cg = cm[:, g * gs:(g + 1) * gs].astype(BF16)
            scores = lax.dot_general(cg, bg.astype(BF16), (((1,), (1,)), ((), ())),
                                     preferred_element_type=F32)
            hg = h_scr[s, g]
            yoff = _dot(cg, hg.astype(BF16)) * frm_rep[:, g * gw:(g + 1) * gw]
            for pr in range(HEADS_PER_GROUP // 2):
                pair = []
                c0 = g * gw + pr * 2 * pd
                for j in range(2):
                    li = lane0 + g * HEADS_PER_GROUP + 2 * pr + j
                    lm = jnp.where(mask, jnp.exp(acs[:, li:li + 1] - acs_t[li:li + 1, :]), 0.0)
                    m = (scores * lm).astype(BF16)
                    pair.append(_dot(m, xdt_b[:, c0:c0 + 2 * pd]))
                ys.append(jnp.where(half, pair[0], pair[1]) + yoff[:, pr * 2 * pd:(pr + 1) * 2 * pd])
            st = _dot(bg.T.astype(BF16), wts[:, g * gw:(g + 1) * gw])
            h_scr[s, g] = hg * cdec_rep[:, g * gw:(g + 1) * gw] + st
        y = jnp.concatenate(ys, axis=1)

        if final:
            y = y + yf_ref[s].astype(F32) + xs * d_ref[...]
            z = z_ref[s].astype(F32)
            y = y * (z * _sigmoid(z))
            ms = jnp.mean(y * y, axis=-1, keepdims=True)
            out_ref[s] = ((y * lax.rsqrt(ms + EPS)) * nrm_ref[...]).astype(out_ref.dtype)
        else:
            out_ref[s] = y.astype(out_ref.dtype)

    @pl.when(step == nc - 1)
    def _():
        hout_ref[...] = h_scr[...]


def _ssd_pass(xbc3, dt3, h0, tri, conv_w, conv_b, dt_bias, a_row, layer, n, row_off, *, reverse,
              fin=None):
    batch, seq, cw = xbc3.shape
    q = SSM_CHUNK
    nc = n // q
    off = row_off // q
    width = SSM_HEADS * SSM_HEAD_DIM

    def cidx(c):
        return off + ((nc - 1 - c) if reverse else c)

    hsz = BF16_SUBLANE
    per = q // hsz
    top = seq // hsz - 1
    nb = SSD_SAMPLES if batch % SSD_SAMPLES == 0 else 1
    per_layer = lambda b, c: (layer, 0, 0)
    in_specs = [
        pl.BlockSpec((nb, q, cw), lambda b, c: (b, cidx(c), 0)),
        pl.BlockSpec((nb, hsz, cw), lambda b, c: (b, jnp.maximum(cidx(c) * per - 1, 0), 0)),
        pl.BlockSpec((nb, hsz, cw), lambda b, c: (b, jnp.minimum((cidx(c) + 1) * per, top), 0)),
        pl.BlockSpec((nb, q, LANE), lambda b, c: (b, cidx(c), 0)),
        pl.BlockSpec((nb,) + h0.shape[1:], lambda b, c: (b, 0, 0, 0)),
        pl.BlockSpec((q, q), lambda b, c: (0, 0)),
        pl.BlockSpec((LANE, width), lambda b, c: (0, 0)),
        pl.BlockSpec((None,) + conv_w.shape[1:], per_layer),
        pl.BlockSpec((None, 1, cw), per_layer),
        pl.BlockSpec((None, 1, LANE), per_layer),
        pl.BlockSpec((None, 1, LANE), per_layer),
    ]
    lane0 = SSM_HEADS if reverse else 0
    head_rep = (jnp.arange(LANE)[:, None] == lane0 + jnp.arange(width)[None, :] // SSM_HEAD_DIM).astype(BF16)
    args = [xbc3, xbc3, xbc3, dt3, h0, tri, head_rep, conv_w, conv_b, dt_bias, a_row]
    if fin is not None:
        z3, yf, d_row, nrm_row = fin
        in_specs += [
            pl.BlockSpec((nb, q, width), lambda b, c: (b, cidx(c), 0)),
            pl.BlockSpec((nb, q, width), lambda b, c: (b, cidx(c) - off, 0)),
            pl.BlockSpec((None, 1, width), per_layer),
            pl.BlockSpec((None, 1, width), per_layer),
        ]
        args += [z3, yf, d_row, nrm_row]
    out_dtype = BF16 if fin is not None else F32
    return pl.pallas_call(
        functools.partial(_ssd_kernel, reverse=reverse, final=fin is not None, nc=nc, lane0=lane0),
        grid=(batch // nb, nc),
        in_specs=in_specs,
        out_specs=[
            pl.BlockSpec((nb, q, width), lambda b, c: (b, cidx(c) - off, 0)),
            pl.BlockSpec((nb,) + h0.shape[1:], lambda b, c: (b, 0, 0, 0)),
        ],
        out_shape=[
            jax.ShapeDtypeStruct((batch, n, width), out_dtype),
            jax.ShapeDtypeStruct(h0.shape, F32),
        ],
        scratch_shapes=[pltpu.VMEM((nb,) + h0.shape[1:], F32)],
        compiler_params=_cparams(("arbitrary", "arbitrary")),
        name="ssd_bwd_final" if reverse else "ssd_fwd",
    )(*args)


def _merge_kernel(*refs, tok, splits, n_sub):
    it = iter(refs)
    take = lambda k: [next(it) for _ in range(k)]
    x_ref, = take(1)
    per_sub = [take(9) for _ in range(n_sub)]
    gn_ref, p_ref, wg_ref, wb_ref, wo_ref, o_ref = take(6)
    tm = TOKEN_TILE
    for s in range(n_sub):
        sc_ref, sh_ref, g1_ref, fl_ref, fc_ref, hl_ref, hc_ref, sl_ref, scx_ref = per_sub[s]
        rows = slice(s * tm, (s + 1) * tm)
        ctx = tok.is_ctx(pl.program_id(0) * n_sub + s)
        x = x_ref[rows, :]
        hb = _norm_mod(x, gn_ref[...], sc_ref[...], sh_ref[...]).astype(BF16)
        ys = (jnp.where(ctx, fc_ref[...], fl_ref[...]), jnp.where(ctx, hc_ref[...], hl_ref[...]),
              p_ref[rows, :], jnp.where(ctx, scx_ref[...], sl_ref[...]))
        merged = None
        lo = 0
        for k, hi in enumerate(splits):
            gate = _sigmoid(_dot(hb, wg_ref[k]))
            term = gate * _dot(ys[k], wb_ref[lo:hi, :])
            merged = term if merged is None else merged + term
            lo = hi
        out = _dot(merged.astype(BF16), wo_ref[...])
        o_ref[rows, :] = x + g1_ref[...] * out


def _merge(tok, layer, x_all, mod, norm_mix, f_lat, f_ctx, h_lat, h_ctx, pool, s_lat, s_ctx,
           w_gate, w_branch, w_out, splits):
    d = x_all.shape[-1]
    tm = TOKEN_TILE
    fw, hw, pw, sw = f_lat.shape[-1], h_lat.shape[-1], pool.shape[-1], s_lat.shape[-1]
    n_sub = MERGE_TILES if tok.n_tiles % MERGE_TILES == 0 else 1
    rows = n_sub * tm
    sub_specs, sub_args = [], []
    for s in range(n_sub):
        tile = lambda i, s=s: i * n_sub + s
        sub_specs += [
            tok.mod_spec(layer, 1, d, tile), tok.mod_spec(layer, 0, d, tile), tok.mod_spec(layer, 2, d, tile),
            tok.lat_spec(fw, tile), tok.ctx_spec(fw, tile),
            tok.lat_spec(hw, tile), tok.ctx_spec(hw, tile),
            tok.lat_spec(sw, tile), tok.ctx_spec(sw, tile),
        ]
        sub_args += [mod, mod, mod, f_lat, f_ctx, h_lat, h_ctx, s_lat, s_ctx]
    return pl.pallas_call(
        functools.partial(_merge_kernel, tok=tok, splits=splits, n_sub=n_sub),
        grid=(tok.n_tiles // n_sub,),
        in_specs=[pl.BlockSpec((rows, d), lambda i: (i, 0))] + sub_specs + [
            pl.BlockSpec((None, 1, d), lambda i: (layer, 0, 0)),
            pl.BlockSpec((rows, pw), lambda i: (i, 0)),
            pl.BlockSpec((None,) + w_gate.shape[1:], lambda i: (layer, 0, 0, 0)),
            pl.BlockSpec((None,) + w_branch.shape[1:], lambda i: (layer, 0, 0)),
            pl.BlockSpec((None,) + w_out.shape[1:], lambda i: (layer, 0, 0)),
        ],
        out_specs=pl.BlockSpec((rows, d), lambda i: (i, 0)),
        out_shape=jax.ShapeDtypeStruct(x_all.shape, F32),
        compiler_params=_cparams(("arbitrary",)),
        name="merge",
    )(x_all, *sub_args, norm_mix, pool, w_gate, w_branch, w_out)


def _router_kernel(x_ref, sc_ref, sh_ref, gn_ref, rwh_ref, rwl_ref, rb_ref, utri_ref,
                   h_ref, e_ref, r_ref, w_ref, cnt_ref):
    t = pl.program_id(0)

    @pl.when(t == 0)
    def _():
        cnt_ref[...] = jnp.zeros_like(cnt_ref)

    h = _norm_mod(x_ref[...], gn_ref[...], sc_ref[...], sh_ref[...])
    hh = h.astype(BF16)
    h_ref[...] = hh
    hl = (h - hh.astype(F32)).astype(BF16)
    tm = h.shape[0]
    n_exp = rwh_ref.shape[0]
    nt = (((1,), (1,)), ((), ()))
    dg = lambda a, b: lax.dot_general(a, b, nt, preferred_element_type=F32)
    rwh = rwh_ref[...]
    logits = dg(rwh, hh) + dg(rwh, hl) + dg(rwl_ref[...], hh) + rb_ref[:, 0:1]
    eid = lax.broadcasted_iota(jnp.int32, (n_exp, tm), 0)
    carry = cnt_ref[:, 0:1]
    tops, sels = [], []
    cur = logits
    for _ in range(TOP_K):
        m = jnp.max(cur, axis=0, keepdims=True)
        idx = jnp.min(jnp.where(cur == m, eid, n_exp), axis=0, keepdims=True)
        sel = eid == idx
        tops.append((m, idx))
        sels.append(sel)
        cur = jnp.where(sel, -jnp.inf, cur)
    cnt = jnp.zeros((n_exp, tm), F32)
    for s in sels:
        cnt = jnp.where(s, 1.0, cnt)
    before = _dot(cnt.astype(BF16), utri_ref[...]) + carry
    exps = [jnp.exp(m - tops[0][0]) for m, _ in tops]
    den = exps[0]
    for v in exps[1:]:
        den = den + v
    row8 = lax.broadcasted_iota(jnp.int32, (8, tm), 0)
    rowl = lax.broadcasted_iota(jnp.int32, (LANE, tm), 0)
    e_out = jnp.zeros((8, tm), jnp.int32)
    r_out = jnp.zeros((8, tm), jnp.int32)
    w_t = jnp.zeros((LANE, tm), F32)
    for k in range(TOP_K):
        rank = jnp.sum(jnp.where(sels[k], before, 0.0), axis=0, keepdims=True)
        e_out = jnp.where(row8 == k, tops[k][1], e_out)
        r_out = jnp.where(row8 == k, rank.astype(jnp.int32), r_out)
        w_t = jnp.where(rowl == k, exps[k] / den, w_t)
    e_ref[...] = e_out
    r_ref[...] = r_out
    w_ref[...] = w_t.T
    cnt_ref[...] += jnp.broadcast_to(jnp.sum(cnt, axis=1, keepdims=True), cnt_ref.shape)


def _router(tok, layer, x_all, mod, norm_ffn, rw_hi, rw_lo, router_b_col, utri):
    d = x_all.shape[-1]
    tm = TOKEN_TILE
    n_tok = tok.n_tok
    n_exp = rw_hi.shape[1]
    row_i = jax.ShapeDtypeStruct((8, n_tok), jnp.int32)
    return pl.pallas_call(
        _router_kernel,
        grid=(tok.n_tiles,),
        in_specs=[
            pl.BlockSpec((tm, d), lambda t: (t, 0)),
            tok.mod_spec(layer, 4, d),
            tok.mod_spec(layer, 3, d),
            pl.BlockSpec((None, 1, d), lambda t: (layer, 0, 0)),
            pl.BlockSpec((None, n_exp, d), lambda t: (layer, 0, 0)),
            pl.BlockSpec((None, n_exp, d), lambda t: (layer, 0, 0)),
            pl.BlockSpec((None, n_exp, LANE), lambda t: (layer, 0, 0)),
            pl.BlockSpec((tm, tm), lambda t: (0, 0)),
        ],
        out_specs=[
            pl.BlockSpec((tm, d), lambda t: (t, 0)),
            pl.BlockSpec((8, tm), lambda t: (0, t)),
            pl.BlockSpec((8, tm), lambda t: (0, t)),
            pl.BlockSpec((tm, LANE), lambda t: (t, 0)),
            pl.BlockSpec((n_exp, LANE), lambda t: (0, 0)),
        ],
        out_shape=[
            jax.ShapeDtypeStruct((n_tok, d), BF16),
            row_i,
            row_i,
            jax.ShapeDtypeStruct((n_tok, LANE), F32),
            jax.ShapeDtypeStruct((n_exp, LANE), F32),
        ],
        compiler_params=_cparams(("arbitrary",)),
        name="router",
    )(x_all, mod, mod, norm_ffn, rw_hi, rw_lo, router_b_col, utri)


def _moe_kernel(be_ref, na_ref, x_ref, wu_ref, bu_ref, wd_ref, bd_ref, o_ref, wu_scr, wd_scr, *, ff):
    i = pl.program_id(0)
    active = i < na_ref[0]
    fresh = jnp.logical_or(i == 0, be_ref[i] != be_ref[jnp.maximum(i - 1, 0)])

    @pl.when(jnp.logical_and(active, fresh))
    def _():
        wu_scr[...] = wu_ref[...].astype(BF16)
        wd_scr[...] = wd_ref[...].astype(BF16)

    @pl.when(active)
    def _():
        hu = _dot(x_ref[...], wu_scr[...]) + bu_ref[...]
        glu = jnp.minimum(hu[:, :ff], SWIGLU_LIMIT)
        lin = jnp.clip(hu[:, ff:], -SWIGLU_LIMIT, SWIGLU_LIMIT)
        act = glu * _sigmoid(SWIGLU_ALPHA * glu) * (lin + 1.0)
        o_ref[...] = (_dot(act.astype(BF16), wd_scr[...]) + bd_ref[...]).astype(o_ref.dtype)

    @pl.when(jnp.logical_not(active))
    def _():
        o_ref[...] = jnp.zeros_like(o_ref)


def _moe(layer, x_sorted, blk_e, n_active, w_up, b_up, w_down, b_down):
    n_rows, d = x_sorted.shape
    ff = w_down.shape[2]
    bm = MOE_ROWS
    grid_spec = pltpu.PrefetchScalarGridSpec(
        num_scalar_prefetch=2,
        grid=(n_rows // bm,),
        in_specs=[
            pl.BlockSpec((bm, d), lambda i, be, na: (i, 0)),
            pl.BlockSpec((None, None, d, 2 * ff), lambda i, be, na: (layer, be[i], 0, 0)),
            pl.BlockSpec((None, None, 1, 2 * ff), lambda i, be, na: (layer, be[i], 0, 0)),
            pl.BlockSpec((None, None, ff, d), lambda i, be, na: (layer, be[i], 0, 0)),
            pl.BlockSpec((None, None, 1, d), lambda i, be, na: (layer, be[i], 0, 0)),
        ],
        out_specs=pl.BlockSpec((bm, d), lambda i, be, na: (i, 0)),
        scratch_shapes=[pltpu.VMEM((d, 2 * ff), BF16), pltpu.VMEM((ff, d), BF16)],
    )
    return pl.pallas_call(
        functools.partial(_moe_kernel, ff=ff),
        grid_spec=grid_spec,
        out_shape=jax.ShapeDtypeStruct((n_rows, d), BF16),
        compiler_params=_cparams(("arbitrary",)),
        name="moe_experts",
    )(blk_e, n_active, x_sorted, w_up, b_up, w_down, b_down)


def _final_kernel(x_ref, g2_ref, yg_ref, w_ref, g_ref, o_ref):
    x = _moe_residual(x_ref[...], g2_ref[...], yg_ref, w_ref)
    ms = jnp.mean(x * x, axis=-1, keepdims=True)
    o_ref[...] = (x * lax.rsqrt(ms + EPS)) * g_ref[...]


def _final(tok, layer, x_all, mod, y_gath, top_w, gain):
    batch, seq, n_lat = tok.batch, tok.seq, tok.n_lat
    d = x_all.shape[-1]
    tm = TOKEN_TILE
    return pl.pallas_call(
        _final_kernel,
        grid=(batch, n_lat // tm),
        in_specs=[
            pl.BlockSpec((None, tm, d), lambda b, i: (b, i, 0)),
            pl.BlockSpec((None, None, None, 1, d), lambda b, i: (layer, tok.mod_off + b, 5, 0, 0)),
            pl.BlockSpec((TOP_K, None, tm, d), lambda b, i: (0, b, i, 0)),
            pl.BlockSpec((None, tm, LANE), lambda b, i: (b, i, 0)),
            pl.BlockSpec((1, d), lambda b, i: (0, 0)),
        ],
        out_specs=pl.BlockSpec((None, tm, d), lambda b, i: (b, i, 0)),
        out_shape=jax.ShapeDtypeStruct((batch, n_lat, d), F32),
        compiler_params=_cparams(("arbitrary", "arbitrary")),
        name="final_combine_norm",
    )(x_all.reshape(batch, seq, d), mod, y_gath.reshape(TOP_K, batch, seq, d),
      top_w.reshape(batch, seq, LANE), gain)


def _dispatch_plan(top_e, rank, counts, n_tok):
    n_exp = counts.shape[0]
    bm = MOE_ROWS
    n_pairs = n_tok * TOP_K
    n_blk = -(-(n_pairs + n_exp * (bm - 1)) // bm)
    padded = (counts + bm - 1) // bm * bm
    pad_end = jnp.cumsum(padded)
    pad_start = pad_end - padded
    dest = rank
    for e in range(n_exp):
        dest = dest + jnp.where(top_e == e, pad_start[e], 0)
    n_active = (pad_end[-1] // bm).astype(jnp.int32).reshape(1)
    blk_start = jnp.arange(n_blk, dtype=jnp.int32) * bm
    blk_e = jnp.minimum(jnp.sum(pad_end[None, :] <= blk_start[:, None], axis=1), n_exp - 1).astype(jnp.int32)
    n_rows = n_blk * bm
    shift = (n_rows - 1).bit_length()
    assert (n_exp + 1) << shift < 2 ** 31
    t_ids = jnp.arange(n_tok, dtype=jnp.int32)[None, :]
    k_ids = jnp.arange(TOP_K, dtype=jnp.int32)[:, None]
    pair_keys = (top_e << shift) | (t_ids * TOP_K + k_ids)
    n_fill = n_rows - n_pairs
    per = bm - 1
    need = jnp.repeat(padded - counts, per)
    fill_e = jnp.where(jnp.tile(jnp.arange(per, dtype=jnp.int32), n_exp) < need,
                       jnp.repeat(jnp.arange(n_exp, dtype=jnp.int32), per), n_exp)
    fill_e = jnp.concatenate([fill_e, jnp.full((n_fill - n_exp * per,), n_exp, jnp.int32)])
    fill_keys = (fill_e << shift) | (n_pairs + jnp.arange(n_fill, dtype=jnp.int32))
    pos = lax.sort(jnp.concatenate([pair_keys.reshape(-1), fill_keys])) & ((1 << shift) - 1)
    row_tok = jnp.where(pos < n_pairs, pos // TOP_K, pos % n_tok)
    return dest, row_tok, blk_e, n_active


def kernel(x, c, ctx, c_ctx, w_mod, b_mod, norm_mix, norm_ffn, w_in, hy_conv_w, hy_conv_b, hy_ffn_w1, hy_ffn_b1, hy_ffn_w2, hy_ffn_b2, hy_ffn_w3, hy_bias, pool_w, pool_scale, ssm_conv_w, ssm_conv_b, ssm_dt_bias, ssm_a_log, ssm_d, ssm_norm, w_branch, w_gate, w_out, router_w, router_b, exp_w_up, exp_b_up, exp_w_down, exp_b_down, norm_final):
    batch, n_lat, d = x.shape
    n_ctx = ctx.shape[1]
    depth = w_mod.shape[0]
    assert n_lat // GRID_W * GRID_W == n_lat and n_lat % n_ctx == 0

    hw = hy_bias.shape[1]
    pw = pool_scale.shape[1]
    sw = ssm_norm.shape[1]
    cw = ssm_conv_w.shape[2]
    fw = w_in.shape[2] - 3 * hw - pw - sw - cw - 2 * SSM_HEADS
    widths = (fw, 3 * hw, pw, sw, cw)
    splits = (fw, fw + hw, fw + hw + pw, fw + hw + pw + sw)
    n_exp = router_w.shape[2]

    w_in_p = jnp.pad(w_in, ((0, 0), (0, 0), (0, LANE - 2 * SSM_HEADS))).astype(BF16)
    w_gate_b = w_gate.astype(BF16)
    w_branch_b = w_branch.astype(BF16)
    w_out_b = w_out.astype(BF16)
    pool_w_bd = jax.vmap(_block_diag)(pool_w).astype(BF16)
    r3 = lambda a: a.reshape(a.shape[0], 1, a.shape[1])
    lane_pad = lambda a: jnp.pad(a, ((0, 0), (0, LANE - a.shape[1])))
    norm_mix3, norm_ffn3 = r3(norm_mix), r3(norm_ffn)
    hy_conv_b3, hy_bias3 = r3(hy_conv_b), r3(hy_bias)
    ssm_conv_b3 = r3(ssm_conv_b)
    pool_scale3 = r3(pool_scale)
    dt_bias3 = r3(lane_pad(ssm_dt_bias.reshape(depth, 2 * SSM_HEADS)))
    a_row3 = r3(lane_pad(-jnp.exp(ssm_a_log.astype(F32)).reshape(depth, 2 * SSM_HEADS)))
    d_row3 = r3(jnp.repeat(ssm_d, SSM_HEAD_DIM, axis=1))
    ssm_norm3 = r3(ssm_norm)
    rw_t = jnp.swapaxes(router_w, 1, 2)
    rw_hi = rw_t.astype(BF16)
    rw_lo = (rw_t - rw_hi.astype(F32)).astype(BF16)
    router_b_col = jnp.broadcast_to(router_b[:, :, None], (depth, n_exp, LANE))
    exp_b_up4 = exp_b_up.reshape(depth, n_exp, 1, -1)
    exp_b_down4 = exp_b_down.reshape(depth, n_exp, 1, -1)
    emb = hy_ffn_w1.shape[1]
    ffn = hy_ffn_w1.shape[2]
    f_w1 = jnp.pad(hy_ffn_w1, ((0, 0), (0, LANE - emb), (0, LANE - ffn)))
    f_b1 = r3(lane_pad(hy_ffn_b1))
    f_w2 = jnp.pad(hy_ffn_w2, ((0, 0), (0, LANE - ffn), (0, LANE - ffn)))
    f_b2 = r3(lane_pad(hy_ffn_b2))
    f_w3 = jnp.pad(hy_ffn_w3, ((0, 0), (0, LANE - ffn), (0, 0)))

    freqs = jnp.linspace(1e-4, HYENA_BANDS - 1, HYENA_BANDS, dtype=F32)
    freqs_row = jnp.concatenate([jnp.zeros((1,), F32), freqs, freqs,
                                 jnp.zeros((LANE - 1 - 2 * HYENA_BANDS,), F32)])[None, :]
    deltas_row = jnp.linspace(HYENA_MIN_DECAY, HYENA_MAX_DECAY, hw, dtype=F32)[None, :]
    gi = jnp.arange(FOURIER_GROUP_W)
    ang = (gi[:, None] * gi[None, :] % FOURIER_GROUP_W).astype(F32) * (2.0 * math.pi / FOURIER_GROUP_W)
    n_fg = fw // FOURIER_GROUP_W
    eye = jnp.eye(n_fg, dtype=F32)
    wc = jnp.concatenate([jnp.kron(eye, jnp.cos(ang)), jnp.kron(eye, jnp.sin(ang))], axis=1).astype(BF16)
    band, invc = _pool_tables(n_lat, n_ctx)
    qi = jnp.arange(SSM_CHUNK)
    tri_f = (qi[:, None] >= qi[None, :]).astype(BF16)
    tri_b = (qi[:, None] <= qi[None, :]).astype(BF16)
    ti = jnp.arange(TOKEN_TILE)
    utri = (ti[:, None] < ti[None, :]).astype(BF16)

    seqs = {}
    for name, n, row_off in (("ctx", n_ctx, n_lat), ("lat", n_lat, 0)):
        fwd, inv = _hyena_dft_tables(n)
        g_all, nrm = _filter_mlp(n, freqs_row, deltas_row, f_w1, f_b1, f_w2, f_b2, f_w3)
        kr, ki = _filter_spectrum(fwd, g_all, nrm, n, hw)
        cn, sn = _trig_tables(n, n)
        seqs[name] = dict(n=n, off=row_off, cn=cn.astype(BF16), sn=sn.astype(BF16), fwd=fwd, inv=inv,
                          kr=kr, ki=ki)

    cc = jnp.concatenate([c, c_ctx[None, :]], axis=0)
    rows = -(-cc.shape[0] // 8) * 8
    mod = _modulation(jnp.pad(cc, ((0, rows - cc.shape[0]), (0, 0))), w_mod, b_mod)
    mod = mod.reshape(depth, rows, 6, 1, d)

    seq = n_lat + n_ctx

    def layer_step(tok, layer, x_all, moe):
        gb, n_tok = tok.batch, tok.n_tok
        res = _inproj(tok, layer, x_all, mod, norm_mix3, w_in_p, wc, band, invc, pool_w_bd, pool_scale3,
                      widths, moe=moe)
        if moe is not None:
            x_all = res[0]
            res = res[1:]
        xcs, hy, pool, z, xbc, dtr = res
        xcs3 = xcs.reshape(gb, seq, 2 * fw)
        hy3 = hy.reshape(gb, seq, 3 * hw)
        z3 = z.reshape(gb, seq, sw)
        xbc3 = xbc.reshape(gb, seq, cw)
        dt3 = dtr.reshape(gb, seq, LANE)
        outs = {}
        zero_state = jnp.zeros((gb, SSM_GROUPS, SSM_STATE, HEADS_PER_GROUP * SSM_HEAD_DIM), F32)
        states = (zero_state, zero_state)
        for name in ("ctx", "lat"):
            s = seqs[name]
            n, off = s["n"], s["off"]
            y_f = _fourier_seq(xcs3, s["cn"], s["sn"], n, off, fw)
            vv, x1 = _hy_pre(hy3, hy_conv_w, hy_conv_b3, layer, n, off, hw)
            y2 = _hy_fwd(s["fwd"], vv, s["kr"], s["ki"], layer, n, hw)
            y_h = _hy_inv(s["inv"], y2, x1, vv, hy_bias3, layer, n, hw)
            yf, st_f = _ssd_pass(xbc3, dt3, states[0], tri_f, ssm_conv_w, ssm_conv_b3, dt_bias3, a_row3,
                                 layer, n, off, reverse=False)
            y_s, st_b = _ssd_pass(xbc3, dt3, states[1], tri_b, ssm_conv_w, ssm_conv_b3, dt_bias3, a_row3,
                                  layer, n, off, reverse=True, fin=(z3, yf, d_row3, ssm_norm3))
            states = (st_f, st_b)
            outs[name] = (y_f, y_h, y_s)
        x_all = _merge(tok, layer, x_all, mod, norm_mix3, outs["lat"][0], outs["ctx"][0], outs["lat"][1],
                       outs["ctx"][1], pool, outs["lat"][2], outs["ctx"][2], w_gate_b, w_branch_b, w_out_b,
                       splits)
        h2, top_e, rank, top_w, counts = _router(tok, layer, x_all, mod, norm_ffn3, rw_hi, rw_lo, router_b_col,
                                                 utri)
        dest, row_tok, blk_e, n_active = _dispatch_plan(top_e[:TOP_K], rank[:TOP_K],
                                                        counts[:, 0].astype(jnp.int32), n_tok)
        x_sorted = h2[row_tok]
        y_rows = _moe(layer, x_sorted, blk_e, n_active, exp_w_up, exp_b_up4, exp_w_down, exp_b_down4)
        y_gath = y_rows[dest.reshape(-1)].reshape(TOP_K, n_tok, d)
        return x_all, (y_gath, top_w)

    n_groups = SAMPLE_GROUPS if batch % SAMPLE_GROUPS == 0 else 1
    gb = batch // n_groups
    groups = []
    for g in range(n_groups):
        tok = _Tok(gb, n_lat, n_ctx, mod_off=g * gb, mod_ctx=batch)
        x_all = jnp.concatenate([x[g * gb:(g + 1) * gb], ctx[g * gb:(g + 1) * gb]], axis=1)
        groups.append([tok, x_all.reshape(tok.n_tok, d), None])
    for layer in range(depth):
        for grp in groups:
            grp[1], grp[2] = layer_step(grp[0], layer, grp[1], grp[2])
    outs = [_final(tok, depth - 1, x_all, mod, moe[0], moe[1], norm_final[None, :])
            for tok, x_all, moe in groups]
    return outs[0] if n_groups == 1 else jnp.concatenate(outs, axis=0)
```

```python
import functools
import math

import jax
import jax.numpy as jnp
from jax import lax
from jax.experimental import pallas as pl
from jax.experimental.pallas import tpu as pltpu

F32 = jnp.float32
BF16 = jnp.bfloat16

GRID_W = 64
EPS = 1e-6
FOURIER_GROUP_W = 64
HYENA_BANDS = 16
HYENA_MIN_DECAY = -math.log(1e-2) / 1.5
HYENA_MAX_DECAY = -math.log(1e-2) / 0.3
POOL_WINDOWS = (2, 4, 8, 16)
POOL_GROUP_W = 64
SSM_GROUPS = 2
HEADS_PER_GROUP = 4
SSM_HEADS = SSM_GROUPS * HEADS_PER_GROUP
SSM_HEAD_DIM = 64
SSM_STATE = 128
SSM_CHUNK = 128
TOP_K = 4
SWIGLU_LIMIT = 7.0
SWIGLU_ALPHA = 1.702

LANE = 128
BF16_SUBLANE = 16
TOKEN_TILE = 256
INPROJ_TILES = 2
MERGE_TILES = 2
SEQ_TILE = 512
SEQ_TILE_1ACC = 1024
SEQ_KTILE = 1024
MOE_ROWS = 512
SSD_SAMPLES = 4
SAMPLE_GROUPS = 1
VMEM_LIMIT = 56 * 1024 * 1024


def _cparams(sem):
    return pltpu.CompilerParams(dimension_semantics=sem, vmem_limit_bytes=VMEM_LIMIT)


def _sigmoid(x):
    return 1.0 / (1.0 + jnp.exp(-x))


def _dot(a, b):
    return jnp.dot(a, b, preferred_element_type=F32)


def _dot_hi(a, b):
    return jnp.dot(a, b, preferred_element_type=F32, precision=lax.Precision.HIGHEST)


def _split3(x):
    x1 = x.astype(BF16)
    r1 = x - x1.astype(F32)
    x2 = r1.astype(BF16)
    x3 = (r1 - x2.astype(F32)).astype(BF16)
    return x1, x2, x3


def _dot_exact01(m01, x):
    x1, x2, x3 = _split3(x)
    return _dot(m01, x1) + _dot(m01, x2) + _dot(m01, x3)


def _mod_kernel(c_ref, w_ref, b_ref, o_ref):
    c = c_ref[...]
    s = c * _sigmoid(c)
    o_ref[...] = _dot_hi(s, w_ref[...]) + b_ref[...]


def _modulation(cc, w_mod, b_mod):
    depth, d, six_d = w_mod.shape
    rows = cc.shape[0]
    nj = six_d // d
    return pl.pallas_call(
        _mod_kernel,
        grid=(depth, nj),
        in_specs=[
            pl.BlockSpec((rows, d), lambda l, j: (0, 0)),
            pl.BlockSpec((None, d, d), lambda l, j: (l, 0, j)),
            pl.BlockSpec((None, 1, d), lambda l, j: (l, 0, j)),
        ],
        out_specs=pl.BlockSpec((None, rows, d), lambda l, j: (l, 0, j)),
        out_shape=jax.ShapeDtypeStruct((depth, rows, six_d), F32),
        compiler_params=_cparams(("arbitrary", "arbitrary")),
        name="modulation",
    )(cc, w_mod, b_mod.reshape(depth, 1, six_d))


class _Tok:
    def __init__(self, batch, n_lat, n_ctx, mod_off, mod_ctx):
        assert n_ctx == TOKEN_TILE and n_lat % TOKEN_TILE == 0
        self.batch, self.n_lat, self.n_ctx = batch, n_lat, n_ctx
        self.mod_off, self.mod_ctx = mod_off, mod_ctx
        self.seq = n_lat + n_ctx
        self.tiles_per_b = self.seq // TOKEN_TILE
        self.lat_tiles = n_lat // TOKEN_TILE
        self.n_tiles = batch * self.tiles_per_b
        self.n_tok = batch * self.seq

    def b(self, t):
        return t // self.tiles_per_b

    def w(self, t):
        return t % self.tiles_per_b

    def is_ctx(self, t):
        return (t % self.tiles_per_b) >= self.lat_tiles

    def mod_row(self, t):
        return jnp.where(self.is_ctx(t), self.mod_ctx, self.mod_off + self.b(t))

    def mod_spec(self, layer, part, d, tile=lambda i: i):
        return pl.BlockSpec((None, None, None, 1, d), lambda i: (layer, self.mod_row(tile(i)), part, 0, 0))

    def lat_spec(self, width, tile=lambda i: i):
        return pl.BlockSpec((None, TOKEN_TILE, width),
                            lambda i: (self.b(tile(i)), jnp.minimum(self.w(tile(i)), self.lat_tiles - 1), 0))

    def ctx_spec(self, width, tile=lambda i: i):
        return pl.BlockSpec((None, TOKEN_TILE, width), lambda i: (self.b(tile(i)), 0, 0))


def _norm_mod(x, gain, scale, shift):
    ms = jnp.mean(x * x, axis=-1, keepdims=True)
    return (x * lax.rsqrt(ms + EPS)) * gain * (1.0 + scale) + shift


def _moe_residual(x, g2, yg_ref, w_ref, rows=slice(None)):
    w = w_ref[rows, :]
    y = None
    for k in range(TOP_K):
        term = w[:, k:k + 1] * yg_ref[k, rows, :].astype(F32)
        y = term if y is None else y + term
    return x + g2 * y


def _inproj_kernel(*refs, widths, combine, n_sub):
    it = iter(refs)
    take = lambda k: [next(it) for _ in range(k)]
    x_ref, = take(1)
    if combine:
        g2_refs = take(n_sub)
        yg_ref, w_ref = take(2)
    sc_refs, sh_refs = take(n_sub), take(n_sub)
    g_ref, win_ref, wc_ref = take(3)
    band_refs, invc_refs = take(n_sub), take(n_sub)
    pw_ref, ps_ref = take(2)
    if combine:
        xn_ref, = take(1)
    xcs_ref, hy_ref, pool_ref, z_ref, xbc_ref, dt_ref = take(6)
    fw, hw, pw, zw, cw = widths
    tm = TOKEN_TILE
    for s in range(n_sub):
        rows = slice(s * tm, (s + 1) * tm)
        x = x_ref[rows, :]
        if combine:
            x = _moe_residual(x, g2_refs[s][...], yg_ref, w_ref, rows)
            xn_ref[rows, :] = x
        hb = _norm_mod(x, g_ref[...], sc_refs[s][...], sh_refs[s][...]).astype(BF16)
        o = 0
        f_in = _dot(hb, win_ref[:, o:o + fw])
        xcs_ref[rows, :] = _dot(f_in.astype(BF16), wc_ref[...]).astype(BF16)
        o += fw
        hy_ref[rows, :] = _dot(hb, win_ref[:, o:o + hw]).astype(BF16)
        o += hw
        u = _dot(hb, win_ref[:, o:o + pw])
        o += pw
        z_ref[rows, :] = _dot(hb, win_ref[:, o:o + zw]).astype(BF16)
        o += zw
        xbc_ref[rows, :] = _dot(hb, win_ref[:, o:o + cw]).astype(BF16)
        o += cw
        dt_ref[rows, :] = _dot(hb, win_ref[:, o:o + LANE])
        gw = POOL_GROUP_W
        parts = []
        for gi in range(len(POOL_WINDOWS)):
            ug = u[:, gi * gw:(gi + 1) * gw]
            parts.append(_dot_exact01(band_refs[s][gi], ug))
        pooled = jnp.concatenate(parts, axis=1) * invc_refs[s][...] - u
        pool_ref[rows, :] = (_dot(pooled.astype(BF16), pw_ref[...]) * ps_ref[...]).astype(BF16)


def _pool_tables(n_lat, n_ctx):
    tm = TOKEN_TILE
    p = jnp.arange(tm)
    bands, invs = [], []
    for row_len in (GRID_W, n_ctx):
        assert tm % row_len == 0
        pr, rr = p % row_len, p // row_len
        bg, ig = [], []
        for win in POOL_WINDOWS:
            lo = jnp.clip(pr - win // 2, 0, row_len)
            hi = jnp.clip(pr + win // 2, 0, row_len)
            q = pr[None, :]
            m = (rr[:, None] == rr[None, :]) & (q >= lo[:, None]) & (q < hi[:, None])
            bg.append(m.astype(BF16))
            ig.append(jnp.broadcast_to((1.0 / (hi - lo).astype(F32))[:, None], (tm, POOL_GROUP_W)))
        bands.append(jnp.stack(bg))
        invs.append(jnp.concatenate(ig, axis=1))
    return jnp.stack(bands), jnp.stack(invs)


def _block_diag(w):
    g, a, b = w.shape
    out = jnp.zeros((g * a, g * b), w.dtype)
    for i in range(g):
        out = out.at[i * a:(i + 1) * a, i * b:(i + 1) * b].set(w[i])
    return out


def _inproj(tok, layer, x_all, mod, norm_mix, w_in_p, wc, band, invc, pool_w_bd, pool_scale, widths,
            moe=None):
    d = x_all.shape[-1]
    fw, hw, pw, zw, cw = widths
    tm = TOKEN_TILE
    n_tok = tok.n_tok
    n_sub = INPROJ_TILES if tok.n_tiles % INPROJ_TILES == 0 else 1
    rows = n_sub * tm
    subs = range(n_sub)
    tile = lambda i, s: i * n_sub + s
    kind = lambda t: jnp.where(tok.is_ctx(t), 1, 0)
    const2 = lambda i: (0, 0)
    mod_specs = lambda lyr, part: [
        pl.BlockSpec((None, None, None, 1, d), lambda i, s=s: (lyr, tok.mod_row(tile(i, s)), part, 0, 0))
        for s in subs]
    pre_specs, pre_args, pre_outs = [], [], []
    if moe is not None:
        pre_specs = mod_specs(layer - 1, 5) + [
            pl.BlockSpec((TOP_K, rows, d), lambda i: (0, i, 0)),
            pl.BlockSpec((rows, LANE), lambda i: (i, 0))]
        pre_args = [mod] * n_sub + [moe[0], moe[1]]
        pre_outs = [jax.ShapeDtypeStruct((n_tok, d), F32)]
    outs = pre_outs + [
        jax.ShapeDtypeStruct((n_tok, 2 * fw), BF16),
        jax.ShapeDtypeStruct((n_tok, hw), BF16),
        jax.ShapeDtypeStruct((n_tok, pw), BF16),
        jax.ShapeDtypeStruct((n_tok, zw), BF16),
        jax.ShapeDtypeStruct((n_tok, cw), BF16),
        jax.ShapeDtypeStruct((n_tok, LANE), F32),
    ]
    return pl.pallas_call(
        functools.partial(_inproj_kernel, widths=widths, combine=moe is not None, n_sub=n_sub),
        grid=(tok.n_tiles // n_sub,),
        in_specs=[pl.BlockSpec((rows, d), lambda i: (i, 0))] + pre_specs
        + mod_specs(layer, 1) + mod_specs(layer, 0) + [
            pl.BlockSpec((None, 1, d), lambda i: (layer, 0, 0)),
            pl.BlockSpec((None,) + w_in_p.shape[1:], lambda i: (layer, 0, 0)),
            pl.BlockSpec(wc.shape, const2),
        ]
        + [pl.BlockSpec((None,) + band.shape[1:], lambda i, s=s: (kind(tile(i, s)), 0, 0, 0)) for s in subs]
        + [pl.BlockSpec((None,) + invc.shape[1:], lambda i, s=s: (kind(tile(i, s)), 0, 0)) for s in subs]
        + [
            pl.BlockSpec((None,) + pool_w_bd.shape[1:], lambda i: (layer, 0, 0)),
            pl.BlockSpec((None, 1, pw), lambda i: (layer, 0, 0)),
        ],
        out_specs=[pl.BlockSpec((rows, o.shape[1]), lambda i: (i, 0)) for o in outs],
        out_shape=outs,
        compiler_params=_cparams(("arbitrary",)),
        name="inproj",
    )(x_all, *pre_args, *([mod] * (2 * n_sub)), norm_mix, w_in_p, wc, *([band] * n_sub), *([invc] * n_sub),
      pool_w_bd, pool_scale)


def _trig_tables(n, period):
    k = jnp.arange(n, dtype=jnp.int32)

    def cs(j):
        m = (j[:, None] * k[None, :]) % period
        ang = m.astype(F32) * (2.0 * math.pi / period)
        return jnp.cos(ang), jnp.sin(ang)

    split = 64
    if n % split or n <= split:
        return cs(k)
    c1, s1 = cs(jnp.arange(n // split, dtype=jnp.int32) * split)
    c2, s2 = cs(jnp.arange(split, dtype=jnp.int32))
    c = c1[:, None, :] * c2[None, :, :] - s1[:, None, :] * s2[None, :, :]
    s = s1[:, None, :] * c2[None, :, :] + c1[:, None, :] * s2[None, :, :]
    return c.reshape(n, n), s.reshape(n, n)


def _hyena_dft_tables(n):
    c, s = _trig_tables(n, 2 * n)
    nyq = jnp.where(jnp.arange(n) % 2 == 0, 1.0, -1.0).astype(F32)
    fwd = jnp.concatenate([c, (-s).at[0].set(nyq)], axis=0).astype(BF16)
    inv = jnp.concatenate([c, (-s).at[:, 0].set(nyq)], axis=1).astype(BF16)
    return fwd, inv


def _fourier_kernel(c_ref, s_ref, xc_ref, xs_ref, o_ref, acc_ref, *, scale):
    k = pl.program_id(1)

    @pl.when(k == 0)
    def _():
        acc_ref[...] = jnp.zeros_like(acc_ref)

    nb = xc_ref.shape[0]
    rc = jnp.concatenate([xc_ref[b] for b in range(nb)], axis=1)
    rs = jnp.concatenate([xs_ref[b] for b in range(nb)], axis=1)
    acc_ref[...] += _dot(c_ref[...], rc) - _dot(s_ref[...], rs)

    @pl.when(k == pl.num_programs(1) - 1)
    def _():
        w = o_ref.shape[2]
        for b in range(nb):
            o_ref[b] = (acc_ref[:, b * w:(b + 1) * w] * scale).astype(o_ref.dtype)


def _fourier_seq(xcs3, cn, sn, n, row_off, fw):
    batch = xcs3.shape[0]
    tl = min(SEQ_TILE_1ACC, n)
    tk = min(SEQ_KTILE, n)
    nt = n // tl
    off = row_off // tk
    scale = 1.0 / math.sqrt(n * FOURIER_GROUP_W)
    return pl.pallas_call(
        functools.partial(_fourier_kernel, scale=scale),
        grid=(nt, n // tk),
        in_specs=[
            pl.BlockSpec((tl, tk), lambda i, k: (i, k)),
            pl.BlockSpec((tl, tk), lambda i, k: (i, k)),
            pl.BlockSpec((batch, tk, fw), lambda i, k: (0, off + k, 0)),
            pl.BlockSpec((batch, tk, fw), lambda i, k: (0, off + k, 1)),
        ],
        out_specs=pl.BlockSpec((batch, tl, fw), lambda i, k: (0, i, 0)),
        out_shape=jax.ShapeDtypeStruct((batch, n, fw), BF16),
        scratch_shapes=[pltpu.VMEM((tl, batch * fw), F32)],
        compiler_params=_cparams(("arbitrary", "arbitrary")),
        name="fourier_seq",
    )(cn, sn, xcs3, xcs3)


def _shift_rows(x, prev_row, next_row):
    n = x.shape[0]
    rows = lax.broadcasted_iota(jnp.int32, x.shape, 0)
    xm = jnp.where(rows == 0, prev_row, pltpu.roll(x, 1, 0))
    xp = jnp.where(rows == n - 1, next_row, pltpu.roll(x, n - 1, 0))
    return xm, xp


def _conv3(x, prev_row, next_row, w_ref, b_ref):
    xm, xp = _shift_rows(x, prev_row, next_row)
    return w_ref[0:1, :] * xm + w_ref[1:2, :] * x + w_ref[2:3, :] * xp + b_ref[...]


def _hy_pre_kernel(x_ref, xp_ref, xn_ref, w_ref, b_ref, vv_ref, x1_ref, *, hw):
    i = pl.program_id(1)
    last = pl.num_programs(1) - 1
    x = x_ref[...].astype(F32)
    hs = xp_ref.shape[0]
    prev_row = jnp.where(i == 0, 0.0, xp_ref[hs - 1:hs, :].astype(F32))
    next_row = jnp.where(i == last, 0.0, xn_ref[0:1, :].astype(F32))
    u = _conv3(x, prev_row, next_row, w_ref, b_ref)
    x1_ref[...] = u[:, :hw].astype(BF16)
    vv_ref[...] = (u[:, hw:2 * hw] * u[:, 2 * hw:]).astype(BF16)


def _halo_specs(width, tile, row_off, seq_rows):
    hs = BF16_SUBLANE
    per = tile // hs
    base = row_off // hs
    top = seq_rows // hs - 1
    prev = pl.BlockSpec((None, hs, width), lambda b, i: (b, jnp.maximum(base + i * per - 1, 0), 0))
    nxt = pl.BlockSpec((None, hs, width), lambda b, i: (b, jnp.minimum(base + (i + 1) * per, top), 0))
    return prev, nxt


def _hy_pre(hy3, conv_w, conv_b, layer, n, row_off, hw):
    batch, seq, w3 = hy3.shape
    tl = min(SEQ_TILE, n)
    off = row_off // tl
    prev, nxt = _halo_specs(w3, tl, row_off, seq)
    out = jax.ShapeDtypeStruct((batch, n, hw), BF16)
    return pl.pallas_call(
        functools.partial(_hy_pre_kernel, hw=hw),
        grid=(batch, n // tl),
        in_specs=[
            pl.BlockSpec((None, tl, w3), lambda b, i: (b, off + i, 0)),
            prev, nxt,
            pl.BlockSpec((None,) + conv_w.shape[1:], lambda b, i: (layer, 0, 0)),
            pl.BlockSpec((None, 1, w3), lambda b, i: (layer, 0, 0)),
        ],
        out_specs=[pl.BlockSpec((None, tl, hw), lambda b, i: (b, i, 0))] * 2,
        out_shape=[out, out],
        compiler_params=_cparams(("arbitrary", "arbitrary")),
        name="hyena_pre",
    )(hy3, hy3, hy3, conv_w, conv_b)


def _filter_mlp_kernel(fr_ref, dl_ref, w1_ref, b1_ref, w2_ref, b2_ref, w3_ref, g_ref, nrm_ref, *, n, hw):
    j = pl.program_id(1)
    rt = g_ref.shape[0]
    lane = lax.broadcasted_iota(jnp.int32, (rt, LANE), 1)
    pos = (lax.broadcasted_iota(jnp.int32, (rt, LANE), 0) + j * rt).astype(F32)
    t = pos / (n - 1)
    ang = 2.0 * math.pi * pos / n
    arg = ang * fr_ref[...]
    feats = jnp.where(lane == 0, t,
                      jnp.where(lane <= HYENA_BANDS, jnp.cos(arg),
                                jnp.where(lane <= 2 * HYENA_BANDS, -jnp.sin(arg), 0.0)))
    h1 = jnp.sin(_dot_hi(feats, w1_ref[...]) + b1_ref[...])
    h2 = jnp.sin(_dot_hi(h1, w2_ref[...]) + b2_ref[...])
    k = _dot_hi(h2, w3_ref[...])
    decay = jnp.exp(-t[:, 0:1] * dl_ref[...])
    kf = k[:, :hw] * decay
    kb = jnp.where(pos[:, 0:1] == 0.0, 0.0, k[:, hw:] * decay)
    g_ref[:, :hw] = (kf + kb).astype(BF16)
    g_ref[:, hw:] = (kf - kb).astype(BF16)
    part = jnp.sum(jnp.abs(kf) + jnp.abs(kb), axis=0, keepdims=True)

    @pl.when(j == 0)
    def _():
        nrm_ref[...] = part

    @pl.when(j > 0)
    def _():
        nrm_ref[...] += part


def _filter_mlp(n, freqs_row, deltas_row, w1, b1, w2, b2, w3):
    depth = w1.shape[0]
    hw = deltas_row.shape[1]
    rt = min(SEQ_TILE, n)
    per_layer = lambda l, j: (l, 0, 0)
    return pl.pallas_call(
        functools.partial(_filter_mlp_kernel, n=n, hw=hw),
        grid=(depth, n // rt),
        in_specs=[
            pl.BlockSpec((1, LANE), lambda l, j: (0, 0)),
            pl.BlockSpec((1, hw), lambda l, j: (0, 0)),
            pl.BlockSpec((None,) + w1.shape[1:], per_layer),
            pl.BlockSpec((None,) + b1.shape[1:], per_layer),
            pl.BlockSpec((None,) + w2.shape[1:], per_layer),
            pl.BlockSpec((None,) + b2.shape[1:], per_layer),
            pl.BlockSpec((None,) + w3.shape[1:], per_layer),
        ],
        out_specs=[
            pl.BlockSpec((rt, 2 * hw), lambda l, j: (j, l)),
            pl.BlockSpec((None, 1, hw), lambda l, j: (l, 0, 0)),
        ],
        out_shape=[
            jax.ShapeDtypeStruct((n, depth * 2 * hw), BF16),
            jax.ShapeDtypeStruct((depth, 1, hw), F32),
        ],
        compiler_params=_cparams(("arbitrary", "arbitrary")),
        name="hyena_filter_mlp",
    )(freqs_row, deltas_row, w1, b1, w2, b2, w3)


def _rdft_accumulate(fc_ref, fs_ref, rhs, accc_ref, accs_ref):
    k = pl.program_id(1)

    @pl.when(k == 0)
    def _():
        accc_ref[...] = jnp.zeros_like(accc_ref)
        accs_ref[...] = jnp.zeros_like(accs_ref)

    accc_ref[...] += _dot(fc_ref[...], rhs)
    accs_ref[...] += _dot(fs_ref[...], rhs)


def _filter_spec_kernel(fc_ref, fs_ref, g_ref, nrm_ref, kr_ref, ki_ref, accc_ref, accs_ref, *, hw):
    i = pl.program_id(0)
    _rdft_accumulate(fc_ref, fs_ref, g_ref[...], accc_ref, accs_ref)

    @pl.when(pl.program_id(1) == pl.num_programs(1) - 1)
    def _():
        tm = accc_ref.shape[0]
        row0 = (lax.broadcasted_iota(jnp.int32, (tm, hw), 0) + i * tm) == 0
        for l in range(kr_ref.shape[0]):
            inv = 1.0 / nrm_ref[l]
            o = l * 2 * hw
            kr_ref[l] = accc_ref[:, o:o + hw] * inv
            ki_ref[l] = jnp.where(row0, accs_ref[:, o:o + hw], accs_ref[:, o + hw:o + 2 * hw]) * inv


def _filter_spectrum(fwd, g_all, nrm, n, hw):
    depth = nrm.shape[0]
    tl = min(SEQ_TILE, n)
    nt = n // tl
    width = g_all.shape[1]
    out = jax.ShapeDtypeStruct((depth, n, hw), F32)
    return pl.pallas_call(
        functools.partial(_filter_spec_kernel, hw=hw),
        grid=(nt, nt),
        in_specs=[
            pl.BlockSpec((tl, tl), lambda i, k: (i, k)),
            pl.BlockSpec((tl, tl), lambda i, k: (nt + i, k)),
            pl.BlockSpec((tl, width), lambda i, k: (k, 0)),
            pl.BlockSpec(nrm.shape, lambda i, k: (0, 0, 0)),
        ],
        out_specs=[pl.BlockSpec((depth, tl, hw), lambda i, k: (0, i, 0))] * 2,
        out_shape=[out, out],
        scratch_shapes=[pltpu.VMEM((tl, width), F32)] * 2,
        compiler_params=_cparams(("arbitrary", "arbitrary")),
        name="hyena_filter_spectrum",
    )(fwd, fwd, g_all, nrm)


def _hy_fwd_kernel(fc_ref, fs_ref, vv_ref, kr_ref, ki_ref, y_ref, accc_ref, accs_ref):
    i = pl.program_id(0)
    nb = vv_ref.shape[0]
    rhs = jnp.concatenate([vv_ref[b] for b in range(nb)], axis=1)
    _rdft_accumulate(fc_ref, fs_ref, rhs, accc_ref, accs_ref)

    @pl.when(pl.program_id(1) == pl.num_programs(1) - 1)
    def _():
        tm, hw = kr_ref.shape
        row0 = (lax.broadcasted_iota(jnp.int32, (tm, hw), 0) + i * tm) == 0
        kr, ki = kr_ref[...], ki_ref[...]
        for b in range(nb):
            vr = accc_ref[:, b * hw:(b + 1) * hw]
            vi = accs_ref[:, b * hw:(b + 1) * hw]
            yr = jnp.where(row0, 0.5 * vr * kr, vr * kr - vi * ki)
            yi = jnp.where(row0, 0.5 * vi * ki, vr * ki + vi * kr)
            y_ref[0, :, b * hw:(b + 1) * hw] = yr.astype(BF16)
            y_ref[1, :, b * hw:(b + 1) * hw] = yi.astype(BF16)


def _hy_fwd(fwd, vv, kr, ki, layer, n, hw):
    batch = vv.shape[0]
    tl = min(SEQ_TILE, n)
    tk = min(SEQ_KTILE, n)
    nt = n // tl
    return pl.pallas_call(
        _hy_fwd_kernel,
        grid=(nt, n // tk),
        in_specs=[
            pl.BlockSpec((tl, tk), lambda i, k: (i, k)),
            pl.BlockSpec((tl, tk), lambda i, k: (nt + i, k)),
            pl.BlockSpec((batch, tk, hw), lambda i, k: (0, k, 0)),
            pl.BlockSpec((None, tl, hw), lambda i, k: (layer, i, 0)),
            pl.BlockSpec((None, tl, hw), lambda i, k: (layer, i, 0)),
        ],
        out_specs=pl.BlockSpec((2, tl, batch * hw), lambda i, k: (0, i, 0)),
        out_shape=jax.ShapeDtypeStruct((2, n, batch * hw), BF16),
        scratch_shapes=[pltpu.VMEM((tl, batch * hw), F32)] * 2,
        compiler_params=_cparams(("arbitrary", "arbitrary")),
        name="hyena_fwd_dft",
    )(fwd, fwd, vv, kr, ki)


def _hy_inv_kernel(fi_ref, y_ref, x1_ref, vv_ref, bias_ref, o_ref, acc_ref, *, scale):
    k = pl.program_id(1)

    @pl.when(k == 0)
    def _():
        acc_ref[...] = jnp.zeros_like(acc_ref)

    acc_ref[...] += _dot(fi_ref[...], y_ref[...])

    @pl.when(k == pl.num_programs(1) - 1)
    def _():
        nb, _, hw = x1_ref.shape
        for b in range(nb):
            conv = acc_ref[:, b * hw:(b + 1) * hw] * scale
            vv = vv_ref[b].astype(F32)
            o_ref[b] = (x1_ref[b].astype(F32) * (conv + vv * bias_ref[...])).astype(BF16)


def _hy_inv(inv, y2, x1, vv, hy_bias, layer, n, hw):
    batch = vv.shape[0]
    tl = min(SEQ_TILE_1ACC, n)
    tk = min(SEQ_KTILE, n)
    nt = n // tl
    return pl.pallas_call(
        functools.partial(_hy_inv_kernel, scale=1.0 / n),
        grid=(nt, 2 * n // tk),
        in_specs=[
            pl.BlockSpec((tl, tk), lambda i, k: (i, k)),
            pl.BlockSpec((tk, batch * hw), lambda i, k: (k, 0)),
            pl.BlockSpec((batch, tl, hw), lambda i, k: (0, i, 0)),
            pl.BlockSpec((batch, tl, hw), lambda i, k: (0, i, 0)),
            pl.BlockSpec((None, 1, hw), lambda i, k: (layer, 0, 0)),
        ],
        out_specs=pl.BlockSpec((batch, tl, hw), lambda i, k: (0, i, 0)),
        out_shape=jax.ShapeDtypeStruct((batch, n, hw), BF16),
        scratch_shapes=[pltpu.VMEM((tl, batch * hw), F32)],
        compiler_params=_cparams(("arbitrary", "arbitrary")),
        name="hyena_inv_dft",
    )(inv, y2.reshape(2 * n, batch * hw), x1, vv, hy_bias)


def _softplus(x):
    return jnp.maximum(x, 0.0) + jnp.log1p(jnp.exp(-jnp.abs(x)))


def _ssd_kernel(*refs, reverse, final, nc, lane0):
    if final:
        (xbc_ref, xp_ref, xn_ref, dt_ref, h0_ref, tri_ref, exp_ref, cw_ref, cb_ref, dtb_ref, a_ref,
         z_ref, yf_ref, d_ref, nrm_ref, out_ref, hout_ref, h_scr) = refs
    else:
        (xbc_ref, xp_ref, xn_ref, dt_ref, h0_ref, tri_ref, exp_ref, cw_ref, cb_ref, dtb_ref, a_ref,
         out_ref, hout_ref, h_scr) = refs
    step = pl.program_id(1)
    cc = (nc - 1 - step) if reverse else step

    @pl.when(step == 0)
    def _():
        h_scr[...] = h0_ref[...]

    nb, q = xbc_ref.shape[0], xbc_ref.shape[1]
    hs = xp_ref.shape[1]
    width = SSM_HEADS * SSM_HEAD_DIM
    gs = SSM_STATE
    pd = SSM_HEAD_DIM
    ri = lax.broadcasted_iota(jnp.int32, (q, q), 0)
    ci = lax.broadcasted_iota(jnp.int32, (q, q), 1)
    mask = (ri <= ci) if reverse else (ri >= ci)

    for s in range(nb):
        x = xbc_ref[s].astype(F32)
        prev_row = jnp.where(cc == 0, 0.0, xp_ref[s, hs - 1:hs, :].astype(F32))
        next_row = jnp.where(cc == nc - 1, 0.0, xn_ref[s, 0:1, :].astype(F32))
        u = _conv3(x, prev_row, next_row, cw_ref, cb_ref)
        u = u * _sigmoid(u)
        xs = u[:, :width]
        bm = u[:, width:width + SSM_GROUPS * gs]
        cm = u[:, width + SSM_GROUPS * gs:]

        dt = _softplus(dt_ref[s] + dtb_ref[...])
        acs = _dot_exact01(tri_ref[...], dt * a_ref[...])
        acs_t = acs.T
        tot = acs[0:1, :] if reverse else acs[q - 1:q, :]
        to_end = jnp.exp(tot - acs)
        frm = jnp.exp(acs)
        cdec = jnp.exp(tot)

        stacked = jnp.concatenate([dt, to_end, frm, jnp.broadcast_to(cdec, (8, LANE))], axis=0)
        s1, s2, s3 = _split3(stacked)
        ex = exp_ref[...]
        rep = _dot(s1, ex) + _dot(s2, ex) + _dot(s3, ex)
        xdt = xs * rep[0:q]
        wts = (xdt * rep[q:2 * q]).astype(BF16)
        frm_rep = rep[2 * q:3 * q]
        cdec_rep = rep[3 * q:3 * q + 1]
        xdt_b = xdt.astype(BF16)
        half = lax.broadcasted_iota(jnp.int32, (q, 2 * pd), 1) < pd

        ys = []
        gw = HEADS_PER_GROUP * pd
        for g in range(SSM_GROUPS):
            bg = bm[:, g * gs:(g + 1) * gs]
            cg = cm[:, g * gs:(g + 1) * gs].astype(BF16)
            scores = lax.dot_general(cg, bg.astype(BF16), (((1,), (1,)), ((), ())),
                                     preferred_element_type=F32)
            hg = h_scr[s, g]
            yoff = _dot(cg, hg.astype(BF16)) * frm_rep[:, g * gw:(g + 1) * gw]
            for pr in range(HEADS_PER_GROUP // 2):
                pair = []
                c0 = g * gw + pr * 2 * pd
                for j in range(2):
                    li = lane0 + g * HEADS_PER_GROUP + 2 * pr + j
                    lm = jnp.where(mask, jnp.exp(acs[:, li:li + 1] - acs_t[li:li + 1, :]), 0.0)
                    m = (scores * lm).astype(BF16)
                    pair.append(_dot(m, xdt_b[:, c0:c0 + 2 * pd]))
                ys.append(jnp.where(half, pair[0], pair[1]) + yoff[:, pr * 2 * pd:(pr + 1) * 2 * pd])
            st = _dot(bg.T.astype(BF16), wts[:, g * gw:(g + 1) * gw])
            h_scr[s, g] = hg * cdec_rep[:, g * gw:(g + 1) * gw] + st
        y = jnp.concatenate(ys, axis=1)

        if final:
            y = y + yf_ref[s].astype(F32) + xs * d_ref[...]
            z = z_ref[s].astype(F32)
            y = y * (z * _sigmoid(z))
            ms = jnp.mean(y * y, axis=-1, keepdims=True)
            out_ref[s] = ((y * lax.rsqrt(ms + EPS)) * nrm_ref[...]).astype(out_ref.dtype)
        else:
            out_ref[s] = y.astype(out_ref.dtype)

    @pl.when(step == nc - 1)
    def _():
        hout_ref[...] = h_scr[...]


def _ssd_pass(xbc3, dt3, h0, tri, conv_w, conv_b, dt_bias, a_row, layer, n, row_off, *, reverse,
              fin=None):
    batch, seq, cw = xbc3.shape
    q = SSM_CHUNK
    nc = n // q
    off = row_off // q
    width = SSM_HEADS * SSM_HEAD_DIM

    def cidx(c):
        return off + ((nc - 1 - c) if reverse else c)

    hsz = BF16_SUBLANE
    per = q // hsz
    top = seq // hsz - 1
    nb = SSD_SAMPLES if batch % SSD_SAMPLES == 0 else 1
    per_layer = lambda b, c: (layer, 0, 0)
    in_specs = [
        pl.BlockSpec((nb, q, cw), lambda b, c: (b, cidx(c), 0)),
        pl.BlockSpec((nb, hsz, cw), lambda b, c: (b, jnp.maximum(cidx(c) * per - 1, 0), 0)),
        pl.BlockSpec((nb, hsz, cw), lambda b, c: (b, jnp.minimum((cidx(c) + 1) * per, top), 0)),
        pl.BlockSpec((nb, q, LANE), lambda b, c: (b, cidx(c), 0)),
        pl.BlockSpec((nb,) + h0.shape[1:], lambda b, c: (b, 0, 0, 0)),
        pl.BlockSpec((q, q), lambda b, c: (0, 0)),
        pl.BlockSpec((LANE, width), lambda b, c: (0, 0)),
        pl.BlockSpec((None,) + conv_w.shape[1:], per_layer),
        pl.BlockSpec((None, 1, cw), per_layer),
        pl.BlockSpec((None, 1, LANE), per_layer),
        pl.BlockSpec((None, 1, LANE), per_layer),
    ]
    lane0 = SSM_HEADS if reverse else 0
    head_rep = (jnp.arange(LANE)[:, None] == lane0 + jnp.arange(width)[None, :] // SSM_HEAD_DIM).astype(BF16)
    args = [xbc3, xbc3, xbc3, dt3, h0, tri, head_rep, conv_w, conv_b, dt_bias, a_row]
    if fin is not None:
        z3, yf, d_row, nrm_row = fin
        in_specs += [
            pl.BlockSpec((nb, q, width), lambda b, c: (b, cidx(c), 0)),
            pl.BlockSpec((nb, q, width), lambda b, c: (b, cidx(c) - off, 0)),
            pl.BlockSpec((None, 1, width), per_layer),
            pl.BlockSpec((None, 1, width), per_layer),
        ]
        args += [z3, yf, d_row, nrm_row]
    out_dtype = BF16 if fin is not None else F32
    return pl.pallas_call(
        functools.partial(_ssd_kernel, reverse=reverse, final=fin is not None, nc=nc, lane0=lane0),
        grid=(batch // nb, nc),
        in_specs=in_specs,
        out_specs=[
            pl.BlockSpec((nb, q, width), lambda b, c: (b, cidx(c) - off, 0)),
            pl.BlockSpec((nb,) + h0.shape[1:], lambda b, c: (b, 0, 0, 0)),
        ],
        out_shape=[
            jax.ShapeDtypeStruct((batch, n, width), out_dtype),
            jax.ShapeDtypeStruct(h0.shape, F32),
        ],
        scratch_shapes=[pltpu.VMEM((nb,) + h0.shape[1:], F32)],
        compiler_params=_cparams(("arbitrary", "arbitrary")),
        name="ssd_bwd_final" if reverse else "ssd_fwd",
    )(*args)


def _merge_kernel(*refs, tok, splits, n_sub):
    it = iter(refs)
    take = lambda k: [next(it) for _ in range(k)]
    x_ref, = take(1)
    per_sub = [take(9) for _ in range(n_sub)]
    gn_ref, p_ref, wg_ref, wb_ref, wo_ref, o_ref = take(6)
    tm = TOKEN_TILE
    for s in range(n_sub):
        sc_ref, sh_ref, g1_ref, fl_ref, fc_ref, hl_ref, hc_ref, sl_ref, scx_ref = per_sub[s]
        rows = slice(s * tm, (s + 1) * tm)
        ctx = tok.is_ctx(pl.program_id(0) * n_sub + s)
        x = x_ref[rows, :]
        hb = _norm_mod(x, gn_ref[...], sc_ref[...], sh_ref[...]).astype(BF16)
        ys = (jnp.where(ctx, fc_ref[...], fl_ref[...]), jnp.where(ctx, hc_ref[...], hl_ref[...]),
              p_ref[rows, :], jnp.where(ctx, scx_ref[...], sl_ref[...]))
        merged = None
        lo = 0
        for k, hi in enumerate(splits):
            gate = _sigmoid(_dot(hb, wg_ref[k]))
            term = gate * _dot(ys[k], wb_ref[lo:hi, :])
            merged = term if merged is None else merged + term
            lo = hi
        out = _dot(merged.astype(BF16), wo_ref[...])
        o_ref[rows, :] = x + g1_ref[...] * out


def _merge(tok, layer, x_all, mod, norm_mix, f_lat, f_ctx, h_lat, h_ctx, pool, s_lat, s_ctx,
           w_gate, w_branch, w_out, splits):
    d = x_all.shape[-1]
    tm = TOKEN_TILE
    fw, hw, pw, sw = f_lat.shape[-1], h_lat.shape[-1], pool.shape[-1], s_lat.shape[-1]
    n_sub = MERGE_TILES if tok.n_tiles % MERGE_TILES == 0 else 1
    rows = n_sub * tm
    sub_specs, sub_args = [], []
    for s in range(n_sub):
        tile = lambda i, s=s: i * n_sub + s
        sub_specs += [
            tok.mod_spec(layer, 1, d, tile), tok.mod_spec(layer, 0, d, tile), tok.mod_spec(layer, 2, d, tile),
            tok.lat_spec(fw, tile), tok.ctx_spec(fw, tile),
            tok.lat_spec(hw, tile), tok.ctx_spec(hw, tile),
            tok.lat_spec(sw, tile), tok.ctx_spec(sw, tile),
        ]
        sub_args += [mod, mod, mod, f_lat, f_ctx, h_lat, h_ctx, s_lat, s_ctx]
    return pl.pallas_call(
        functools.partial(_merge_kernel, tok=tok, splits=splits, n_sub=n_sub),
        grid=(tok.n_tiles // n_sub,),
        in_specs=[pl.BlockSpec((rows, d), lambda i: (i, 0))] + sub_specs + [
            pl.BlockSpec((None, 1, d), lambda i: (layer, 0, 0)),
            pl.BlockSpec((rows, pw), lambda i: (i, 0)),
            pl.BlockSpec((None,) + w_gate.shape[1:], lambda i: (layer, 0, 0, 0)),
            pl.BlockSpec((None,) + w_branch.shape[1:], lambda i: (layer, 0, 0)),
            pl.BlockSpec((None,) + w_out.shape[1:], lambda i: (layer, 0, 0)),
        ],
        out_specs=pl.BlockSpec((rows, d), lambda i: (i, 0)),
        out_shape=jax.ShapeDtypeStruct(x_all.shape, F32),
        compiler_params=_cparams(("arbitrary",)),
        name="merge",
    )(x_all, *sub_args, norm_mix, pool, w_gate, w_branch, w_out)


def _router_kernel(x_ref, sc_ref, sh_ref, gn_ref, rwh_ref, rwl_ref, rb_ref, utri_ref,
                   h_ref, e_ref, r_ref, w_ref, cnt_ref):
    t = pl.program_id(0)

    @pl.when(t == 0)
    def _():
        cnt_ref[...] = jnp.zeros_like(cnt_ref)

    h = _norm_mod(x_ref[...], gn_ref[...], sc_ref[...], sh_ref[...])
    hh = h.astype(BF16)
    h_ref[...] = hh
    hl = (h - hh.astype(F32)).astype(BF16)
    tm = h.shape[0]
    n_exp = rwh_ref.shape[0]
    nt = (((1,), (1,)), ((), ()))
    dg = lambda a, b: lax.dot_general(a, b, nt, preferred_element_type=F32)
    rwh = rwh_ref[...]
    logits = dg(rwh, hh) + dg(rwh, hl) + dg(rwl_ref[...], hh) + rb_ref[:, 0:1]
    eid = lax.broadcasted_iota(jnp.int32, (n_exp, tm), 0)
    carry = cnt_ref[:, 0:1]
    tops, sels = [], []
    cur = logits
    for _ in range(TOP_K):
        m = jnp.max(cur, axis=0, keepdims=True)
        idx = jnp.min(jnp.where(cur == m, eid, n_exp), axis=0, keepdims=True)
        sel = eid == idx
        tops.append((m, idx))
        sels.append(sel)
        cur = jnp.where(sel, -jnp.inf, cur)
    cnt = jnp.zeros((n_exp, tm), F32)
    for s in sels:
        cnt = jnp.where(s, 1.0, cnt)
    before = _dot(cnt.astype(BF16), utri_ref[...]) + carry
    exps = [jnp.exp(m - tops[0][0]) for m, _ in tops]
    den = exps[0]
    for v in exps[1:]:
        den = den + v
    row8 = lax.broadcasted_iota(jnp.int32, (8, tm), 0)
    rowl = lax.broadcasted_iota(jnp.int32, (LANE, tm), 0)
    e_out = jnp.zeros((8, tm), jnp.int32)
    r_out = jnp.zeros((8, tm), jnp.int32)
    w_t = jnp.zeros((LANE, tm), F32)
    for k in range(TOP_K):
        rank = jnp.sum(jnp.where(sels[k], before, 0.0), axis=0, keepdims=True)
        e_out = jnp.where(row8 == k, tops[k][1], e_out)
        r_out = jnp.where(row8 == k, rank.astype(jnp.int32), r_out)
        w_t = jnp.where(rowl == k, exps[k] / den, w_t)
    e_ref[...] = e_out
    r_ref[...] = r_out
    w_ref[...] = w_t.T
    cnt_ref[...] += jnp.broadcast_to(jnp.sum(cnt, axis=1, keepdims=True), cnt_ref.shape)


def _router(tok, layer, x_all, mod, norm_ffn, rw_hi, rw_lo, router_b_col, utri):
    d = x_all.shape[-1]
    tm = TOKEN_TILE
    n_tok = tok.n_tok
    n_exp = rw_hi.shape[1]
    row_i = jax.ShapeDtypeStruct((8, n_tok), jnp.int32)
    return pl.pallas_call(
        _router_kernel,
        grid=(tok.n_tiles,),
        in_specs=[
            pl.BlockSpec((tm, d), lambda t: (t, 0)),
            tok.mod_spec(layer, 4, d),
            tok.mod_spec(layer, 3, d),
            pl.BlockSpec((None, 1, d), lambda t: (layer, 0, 0)),
            pl.BlockSpec((None, n_exp, d), lambda t: (layer, 0, 0)),
            pl.BlockSpec((None, n_exp, d), lambda t: (layer, 0, 0)),
            pl.BlockSpec((None, n_exp, LANE), lambda t: (layer, 0, 0)),
            pl.BlockSpec((tm, tm), lambda t: (0, 0)),
        ],
        out_specs=[
            pl.BlockSpec((tm, d), lambda t: (t, 0)),
            pl.BlockSpec((8, tm), lambda t: (0, t)),
            pl.BlockSpec((8, tm), lambda t: (0, t)),
            pl.BlockSpec((tm, LANE), lambda t: (t, 0)),
            pl.BlockSpec((n_exp, LANE), lambda t: (0, 0)),
        ],
        out_shape=[
            jax.ShapeDtypeStruct((n_tok, d), BF16),
            row_i,
            row_i,
            jax.ShapeDtypeStruct((n_tok, LANE), F32),
            jax.ShapeDtypeStruct((n_exp, LANE), F32),
        ],
        compiler_params=_cparams(("arbitrary",)),
        name="router",
    )(x_all, mod, mod, norm_ffn, rw_hi, rw_lo, router_b_col, utri)


def _moe_kernel(be_ref, na_ref, x_ref, wu_ref, bu_ref, wd_ref, bd_ref, o_ref, wu_scr, wd_scr, *, ff):
    i = pl.program_id(0)
    active = i < na_ref[0]
    fresh = jnp.logical_or(i == 0, be_ref[i] != be_ref[jnp.maximum(i - 1, 0)])

    @pl.when(jnp.logical_and(active, fresh))
    def _():
        wu_scr[...] = wu_ref[...].astype(BF16)
        wd_scr[...] = wd_ref[...].astype(BF16)

    @pl.when(active)
    def _():
        hu = _dot(x_ref[...], wu_scr[...]) + bu_ref[...]
        glu = jnp.minimum(hu[:, :ff], SWIGLU_LIMIT)
        lin = jnp.clip(hu[:, ff:], -SWIGLU_LIMIT, SWIGLU_LIMIT)
        act = glu * _sigmoid(SWIGLU_ALPHA * glu) * (lin + 1.0)
        o_ref[...] = (_dot(act.astype(BF16), wd_scr[...]) + bd_ref[...]).astype(o_ref.dtype)

    @pl.when(jnp.logical_not(active))
    def _():
        o_ref[...] = jnp.zeros_like(o_ref)


def _moe(layer, x_sorted, blk_e, n_active, w_up, b_up, w_down, b_down):
    n_rows, d = x_sorted.shape
    ff = w_down.shape[2]
    bm = MOE_ROWS
    grid_spec = pltpu.PrefetchScalarGridSpec(
        num_scalar_prefetch=2,
        grid=(n_rows // bm,),
        in_specs=[
            pl.BlockSpec((bm, d), lambda i, be, na: (i, 0)),
            pl.BlockSpec((None, None, d, 2 * ff), lambda i, be, na: (layer, be[i], 0, 0)),
            pl.BlockSpec((None, None, 1, 2 * ff), lambda i, be, na: (layer, be[i], 0, 0)),
            pl.BlockSpec((None, None, ff, d), lambda i, be, na: (layer, be[i], 0, 0)),
            pl.BlockSpec((None, None, 1, d), lambda i, be, na: (layer, be[i], 0, 0)),
        ],
        out_specs=pl.BlockSpec((bm, d), lambda i, be, na: (i, 0)),
        scratch_shapes=[pltpu.VMEM((d, 2 * ff), BF16), pltpu.VMEM((ff, d), BF16)],
    )
    return pl.pallas_call(
        functools.partial(_moe_kernel, ff=ff),
        grid_spec=grid_spec,
        out_shape=jax.ShapeDtypeStruct((n_rows, d), BF16),
        compiler_params=_cparams(("arbitrary",)),
        name="moe_experts",
    )(blk_e, n_active, x_sorted, w_up, b_up, w_down, b_down)


def _final_kernel(x_ref, g2_ref, yg_ref, w_ref, g_ref, o_ref):
    x = _moe_residual(x_ref[...], g2_ref[...], yg_ref, w_ref)
    ms = jnp.mean(x * x, axis=-1, keepdims=True)
    o_ref[...] = (x * lax.rsqrt(ms + EPS)) * g_ref[...]


def _final(tok, layer, x_all, mod, y_gath, top_w, gain):
    batch, seq, n_lat = tok.batch, tok.seq, tok.n_lat
    d = x_all.shape[-1]
    tm = TOKEN_TILE
    return pl.pallas_call(
        _final_kernel,
        grid=(batch, n_lat // tm),
        in_specs=[
            pl.BlockSpec((None, tm, d), lambda b, i: (b, i, 0)),
            pl.BlockSpec((None, None, None, 1, d), lambda b, i: (layer, tok.mod_off + b, 5, 0, 0)),
            pl.BlockSpec((TOP_K, None, tm, d), lambda b, i: (0, b, i, 0)),
            pl.BlockSpec((None, tm, LANE), lambda b, i: (b, i, 0)),
            pl.BlockSpec((1, d), lambda b, i: (0, 0)),
        ],
        out_specs=pl.BlockSpec((None, tm, d), lambda b, i: (b, i, 0)),
        out_shape=jax.ShapeDtypeStruct((batch, n_lat, d), F32),
        compiler_params=_cparams(("arbitrary", "arbitrary")),
        name="final_combine_norm",
    )(x_all.reshape(batch, seq, d), mod, y_gath.reshape(TOP_K, batch, seq, d),
      top_w.reshape(batch, seq, LANE), gain)


def _dispatch_plan(top_e, rank, counts, n_tok):
    n_exp = counts.shape[0]
    bm = MOE_ROWS
    n_pairs = n_tok * TOP_K
    n_blk = -(-(n_pairs + n_exp * (bm - 1)) // bm)
    padded = (counts + bm - 1) // bm * bm
    pad_end = jnp.cumsum(padded)
    pad_start = pad_end - padded
    dest = rank
    for e in range(n_exp):
        dest = dest + jnp.where(top_e == e, pad_start[e], 0)
    n_active = (pad_end[-1] // bm).astype(jnp.int32).reshape(1)
    blk_start = jnp.arange(n_blk, dtype=jnp.int32) * bm
    blk_e = jnp.minimum(jnp.sum(pad_end[None, :] <= blk_start[:, None], axis=1), n_exp - 1).astype(jnp.int32)
    n_rows = n_blk * bm
    shift = (n_rows - 1).bit_length()
    assert (n_exp + 1) << shift < 2 ** 31
    t_ids = jnp.arange(n_tok, dtype=jnp.int32)[None, :]
    k_ids = jnp.arange(TOP_K, dtype=jnp.int32)[:, None]
    pair_keys = (top_e << shift) | (t_ids * TOP_K + k_ids)
    n_fill = n_rows - n_pairs
    per = bm - 1
    need = jnp.repeat(padded - counts, per)
    fill_e = jnp.where(jnp.tile(jnp.arange(per, dtype=jnp.int32), n_exp) < need,
                       jnp.repeat(jnp.arange(n_exp, dtype=jnp.int32), per), n_exp)
    fill_e = jnp.concatenate([fill_e, jnp.full((n_fill - n_exp * per,), n_exp, jnp.int32)])
    fill_keys = (fill_e << shift) | (n_pairs + jnp.arange(n_fill, dtype=jnp.int32))
    pos = lax.sort(jnp.concatenate([pair_keys.reshape(-1), fill_keys])) & ((1 << shift) - 1)
    row_tok = jnp.where(pos < n_pairs, pos // TOP_K, pos % n_tok)
    return dest, row_tok, blk_e, n_active


def kernel(x, c, ctx, c_ctx, w_mod, b_mod, norm_mix, norm_ffn, w_in, hy_conv_w, hy_conv_b, hy_ffn_w1, hy_ffn_b1, hy_ffn_w2, hy_ffn_b2, hy_ffn_w3, hy_bias, pool_w, pool_scale, ssm_conv_w, ssm_conv_b, ssm_dt_bias, ssm_a_log, ssm_d, ssm_norm, w_branch, w_gate, w_out, router_w, router_b, exp_w_up, exp_b_up, exp_w_down, exp_b_down, norm_final):
    batch, n_lat, d = x.shape
    n_ctx = ctx.shape[1]
    depth = w_mod.shape[0]
    assert n_lat // GRID_W * GRID_W == n_lat and n_lat % n_ctx == 0

    hw = hy_bias.shape[1]
    pw = pool_scale.shape[1]
    sw = ssm_norm.shape[1]
    cw = ssm_conv_w.shape[2]
    fw = w_in.shape[2] - 3 * hw - pw - sw - cw - 2 * SSM_HEADS
    widths = (fw, 3 * hw, pw, sw, cw)
    splits = (fw, fw + hw, fw + hw + pw, fw + hw + pw + sw)
    n_exp = router_w.shape[2]

    w_in_p = jnp.pad(w_in, ((0, 0), (0, 0), (0, LANE - 2 * SSM_HEADS))).astype(BF16)
    w_gate_b = w_gate.astype(BF16)
    w_branch_b = w_branch.astype(BF16)
    w_out_b = w_out.astype(BF16)
    pool_w_bd = jax.vmap(_block_diag)(pool_w).astype(BF16)
    r3 = lambda a: a.reshape(a.shape[0], 1, a.shape[1])
    lane_pad = lambda a: jnp.pad(a, ((0, 0), (0, LANE - a.shape[1])))
    norm_mix3, norm_ffn3 = r3(norm_mix), r3(norm_ffn)
    hy_conv_b3, hy_bias3 = r3(hy_conv_b), r3(hy_bias)
    ssm_conv_b3 = r3(ssm_conv_b)
    pool_scale3 = r3(pool_scale)
    dt_bias3 = r3(lane_pad(ssm_dt_bias.reshape(depth, 2 * SSM_HEADS)))
    a_row3 = r3(lane_pad(-jnp.exp(ssm_a_log.astype(F32)).reshape(depth, 2 * SSM_HEADS)))
    d_row3 = r3(jnp.repeat(ssm_d, SSM_HEAD_DIM, axis=1))
    ssm_norm3 = r3(ssm_norm)
    rw_t = jnp.swapaxes(router_w, 1, 2)
    rw_hi = rw_t.astype(BF16)
    rw_lo = (rw_t - rw_hi.astype(F32)).astype(BF16)
    router_b_col = jnp.broadcast_to(router_b[:, :, None], (depth, n_exp, LANE))
    exp_b_up4 = exp_b_up.reshape(depth, n_exp, 1, -1)
    exp_b_down4 = exp_b_down.reshape(depth, n_exp, 1, -1)
    emb = hy_ffn_w1.shape[1]
    ffn = hy_ffn_w1.shape[2]
    f_w1 = jnp.pad(hy_ffn_w1, ((0, 0), (0, LANE - emb), (0, LANE - ffn)))
    f_b1 = r3(lane_pad(hy_ffn_b1))
    f_w2 = jnp.pad(hy_ffn_w2, ((0, 0), (0, LANE - ffn), (0, LANE - ffn)))
    f_b2 = r3(lane_pad(hy_ffn_b2))
    f_w3 = jnp.pad(hy_ffn_w3, ((0, 0), (0, LANE - ffn), (0, 0)))

    freqs = jnp.linspace(1e-4, HYENA_BANDS - 1, HYENA_BANDS, dtype=F32)
    freqs_row = jnp.concatenate([jnp.zeros((1,), F32), freqs, freqs,
                                 jnp.zeros((LANE - 1 - 2 * HYENA_BANDS,), F32)])[None, :]
    deltas_row = jnp.linspace(HYENA_MIN_DECAY, HYENA_MAX_DECAY, hw, dtype=F32)[None, :]
    gi = jnp.arange(FOURIER_GROUP_W)
    ang = (gi[:, None] * gi[None, :] % FOURIER_GROUP_W).astype(F32) * (2.0 * math.pi / FOURIER_GROUP_W)
    n_fg = fw // FOURIER_GROUP_W
    eye = jnp.eye(n_fg, dtype=F32)
    wc = jnp.concatenate([jnp.kron(eye, jnp.cos(ang)), jnp.kron(eye, jnp.sin(ang))], axis=1).astype(BF16)
    band, invc = _pool_tables(n_lat, n_ctx)
    qi = jnp.arange(SSM_CHUNK)
    tri_f = (qi[:, None] >= qi[None, :]).astype(BF16)
    tri_b = (qi[:, None] <= qi[None, :]).astype(BF16)
    ti = jnp.arange(TOKEN_TILE)
    utri = (ti[:, None] < ti[None, :]).astype(BF16)

    seqs = {}
    for name, n, row_off in (("ctx", n_ctx, n_lat), ("lat", n_lat, 0)):
        fwd, inv = _hyena_dft_tables(n)
        g_all, nrm = _filter_mlp(n, freqs_row, deltas_row, f_w1, f_b1, f_w2, f_b2, f_w3)
        kr, ki = _filter_spectrum(fwd, g_all, nrm, n, hw)
        cn, sn = _trig_tables(n, n)
        seqs[name] = dict(n=n, off=row_off, cn=cn.astype(BF16), sn=sn.astype(BF16), fwd=fwd, inv=inv,
                          kr=kr, ki=ki)

    cc = jnp.concatenate([c, c_ctx[None, :]], axis=0)
    rows = -(-cc.shape[0] // 8) * 8
    mod = _modulation(jnp.pad(cc, ((0, rows - cc.shape[0]), (0, 0))), w_mod, b_mod)
    mod = mod.reshape(depth, rows, 6, 1, d)

    seq = n_lat + n_ctx

    def layer_step(tok, layer, x_all, moe):
        gb, n_tok = tok.batch, tok.n_tok
        res = _inproj(tok, layer, x_all, mod, norm_mix3, w_in_p, wc, band, invc, pool_w_bd, pool_scale3,
                      widths, moe=moe)
        if moe is not None:
            x_all = res[0]
            res = res[1:]
        xcs, hy, pool, z, xbc, dtr = res
        xcs3 = xcs.reshape(gb, seq, 2 * fw)
        hy3 = hy.reshape(gb, seq, 3 * hw)
        z3 = z.reshape(gb, seq, sw)
        xbc3 = xbc.reshape(gb, seq, cw)
        dt3 = dtr.reshape(gb, seq, LANE)
        outs = {}
        zero_state = jnp.zeros((gb, SSM_GROUPS, SSM_STATE, HEADS_PER_GROUP * SSM_HEAD_DIM), F32)
        states = (zero_state, zero_state)
        for name in ("ctx", "lat"):
            s = seqs[name]
            n, off = s["n"], s["off"]
            y_f = _fourier_seq(xcs3, s["cn"], s["sn"], n, off, fw)
            vv, x1 = _hy_pre(hy3, hy_conv_w, hy_conv_b3, layer, n, off, hw)
            y2 = _hy_fwd(s["fwd"], vv, s["kr"], s["ki"], layer, n, hw)
            y_h = _hy_inv(s["inv"], y2, x1, vv, hy_bias3, layer, n, hw)
            yf, st_f = _ssd_pass(xbc3, dt3, states[0], tri_f, ssm_conv_w, ssm_conv_b3, dt_bias3, a_row3,
                                 layer, n, off, reverse=False)
            y_s, st_b = _ssd_pass(xbc3, dt3, states[1], tri_b, ssm_conv_w, ssm_conv_b3, dt_bias3, a_row3,
                                  layer, n, off, reverse=True, fin=(z3, yf, d_row3, ssm_norm3))
            states = (st_f, st_b)
            outs[name] = (y_f, y_h, y_s)
        x_all = _merge(tok, layer, x_all, mod, norm_mix3, outs["lat"][0], outs["ctx"][0], outs["lat"][1],
                       outs["ctx"][1], pool, outs["lat"][2], outs["ctx"][2], w_gate_b, w_branch_b, w_out_b,
                       splits)
        h2, top_e, rank, top_w, counts = _router(tok, layer, x_all, mod, norm_ffn3, rw_hi, rw_lo, router_b_col,
                                                 utri)
        dest, row_tok, blk_e, n_active = _dispatch_plan(top_e[:TOP_K], rank[:TOP_K],
                                                        counts[:, 0].astype(jnp.int32), n_tok)
        x_sorted = h2[row_tok]
        y_rows = _moe(layer, x_sorted, blk_e, n_active, exp_w_up, exp_b_up4, exp_w_down, exp_b_down4)
        y_gath = y_rows[dest.reshape(-1)].reshape(TOP_K, n_tok, d)
        return x_all, (y_gath, top_w)

    n_groups = SAMPLE_GROUPS if batch % SAMPLE_GROUPS == 0 else 1
    gb = batch // n_groups
    groups = []
    for g in range(n_groups):
        tok = _Tok(gb, n_lat, n_ctx, mod_off=g * gb, mod_ctx=batch)
        x_all = jnp.concatenate([x[g * gb:(g + 1) * gb], ctx[g * gb:(g + 1) * gb]], axis=1)
        groups.append([tok, x_all.reshape(tok.n_tok, d), None])
    for layer in range(depth):
        for grp in groups:
            grp[1], grp[2] = layer_step(grp[0], layer, grp[1], grp[2])
    outs = [_final(tok, depth - 1, x_all, mod, moe[0], moe[1], norm_final[None, :])
            for tok, x_all, moe in groups]
    return outs[0] if n_groups == 1 else jnp.concatenate(outs, axis=0)
```

```python
import functools
import math

import jax
import jax.numpy as jnp
from jax import lax
from jax.experimental import pallas as pl
from jax.experimental.pallas import tpu as pltpu

F32 = jnp.float32
BF16 = jnp.bfloat16

GRID_W = 64
EPS = 1e-6
FOURIER_GROUP_W = 64
HYENA_BANDS = 16
HYENA_MIN_DECAY = -math.log(1e-2) / 1.5
HYENA_MAX_DECAY = -math.log(1e-2) / 0.3
POOL_WINDOWS = (2, 4, 8, 16)
POOL_GROUP_W = 64
SSM_GROUPS = 2
HEADS_PER_GROUP = 4
SSM_HEADS = SSM_GROUPS * HEADS_PER_GROUP
SSM_HEAD_DIM = 64
SSM_STATE = 128
SSM_CHUNK = 128
TOP_K = 4
SWIGLU_LIMIT = 7.0
SWIGLU_ALPHA = 1.702

LANE = 128
BF16_SUBLANE = 16
TOKEN_TILE = 256
INPROJ_TILES = 2
MERGE_TILES = 2
SEQ_TILE = 512
SEQ_TILE_1ACC = 1024
SEQ_KTILE = 1024
MOE_ROWS = 512
SSD_SAMPLES = 8
SAMPLE_GROUPS = 1
VMEM_LIMIT = 56 * 1024 * 1024


def _cparams(sem):
    return pltpu.CompilerParams(dimension_semantics=sem, vmem_limit_bytes=VMEM_LIMIT)


def _sigmoid(x):
    return 1.0 / (1.0 + jnp.exp(-x))


def _dot(a, b):
    return jnp.dot(a, b, preferred_element_type=F32)


def _dot_hi(a, b):
    return jnp.dot(a, b, preferred_element_type=F32, precision=lax.Precision.HIGHEST)


def _split3(x):
    x1 = x.astype(BF16)
    r1 = x - x1.astype(F32)
    x2 = r1.astype(BF16)
    x3 = (r1 - x2.astype(F32)).astype(BF16)
    return x1, x2, x3


def _dot_exact01(m01, x):
    x1, x2, x3 = _split3(x)
    return _dot(m01, x1) + _dot(m01, x2) + _dot(m01, x3)


def _mod_kernel(c_ref, w_ref, b_ref, o_ref):
    c = c_ref[...]
    s = c * _sigmoid(c)
    o_ref[...] = _dot_hi(s, w_ref[...]) + b_ref[...]


def _modulation(cc, w_mod, b_mod):
    depth, d, six_d = w_mod.shape
    rows = cc.shape[0]
    nj = six_d // d
    return pl.pallas_call(
        _mod_kernel,
        grid=(depth, nj),
        in_specs=[
            pl.BlockSpec((rows, d), lambda l, j: (0, 0)),
            pl.BlockSpec((None, d, d), lambda l, j: (l, 0, j)),
            pl.BlockSpec((None, 1, d), lambda l, j: (l, 0, j)),
        ],
        out_specs=pl.BlockSpec((None, rows, d), lambda l, j: (l, 0, j)),
        out_shape=jax.ShapeDtypeStruct((depth, rows, six_d), F32),
        compiler_params=_cparams(("arbitrary", "arbitrary")),
        name="modulation",
    )(cc, w_mod, b_mod.reshape(depth, 1, six_d))


class _Tok:
    def __init__(self, batch, n_lat, n_ctx, mod_off, mod_ctx):
        assert n_ctx == TOKEN_TILE and n_lat % TOKEN_TILE == 0
        self.batch, self.n_lat, self.n_ctx = batch, n_lat, n_ctx
        self.mod_off, self.mod_ctx = mod_off, mod_ctx
        self.seq = n_lat + n_ctx
        self.tiles_per_b = self.seq // TOKEN_TILE
        self.lat_tiles = n_lat // TOKEN_TILE
        self.n_tiles = batch * self.tiles_per_b
        self.n_tok = batch * self.seq

    def b(self, t):
        return t // self.tiles_per_b

    def w(self, t):
        return t % self.tiles_per_b

    def is_ctx(self, t):
        return (t % self.tiles_per_b) >= self.lat_tiles

    def mod_row(self, t):
        return jnp.where(self.is_ctx(t), self.mod_ctx, self.mod_off + self.b(t))

    def mod_spec(self, layer, part, d, tile=lambda i: i):
        return pl.BlockSpec((None, None, None, 1, d), lambda i: (layer, self.mod_row(tile(i)), part, 0, 0))

    def lat_spec(self, width, tile=lambda i: i):
        return pl.BlockSpec((None, TOKEN_TILE, width),
                            lambda i: (self.b(tile(i)), jnp.minimum(self.w(tile(i)), self.lat_tiles - 1), 0))

    def ctx_spec(self, width, tile=lambda i: i):
        return pl.BlockSpec((None, TOKEN_TILE, width), lambda i: (self.b(tile(i)), 0, 0))


def _norm_mod(x, gain, scale, shift):
    ms = jnp.mean(x * x, axis=-1, keepdims=True)
    return (x * lax.rsqrt(ms + EPS)) * gain * (1.0 + scale) + shift


def _moe_residual(x, g2, yg_ref, w_ref, rows=slice(None)):
    w = w_ref[rows, :]
    y = None
    for k in range(TOP_K):
        term = w[:, k:k + 1] * yg_ref[k, rows, :].astype(F32)
        y = term if y is None else y + term
    return x + g2 * y


def _inproj_kernel(*refs, widths, combine, n_sub):
    it = iter(refs)
    take = lambda k: [next(it) for _ in range(k)]
    x_ref, = take(1)
    if combine:
        g2_refs = take(n_sub)
        yg_ref, w_ref = take(2)
    sc_refs, sh_refs = take(n_sub), take(n_sub)
    g_ref, win_ref, wc_ref = take(3)
    band_refs, invc_refs = take(n_sub), take(n_sub)
    pw_ref, ps_ref = take(2)
    if combine:
        xn_ref, = take(1)
    xcs_ref, hy_ref, pool_ref, z_ref, xbc_ref, dt_ref = take(6)
    fw, hw, pw, zw, cw = widths
    tm = TOKEN_TILE
    for s in range(n_sub):
        rows = slice(s * tm, (s + 1) * tm)
        x = x_ref[rows, :]
        if combine:
            x = _moe_residual(x, g2_refs[s][...], yg_ref, w_ref, rows)
            xn_ref[rows, :] = x
        hb = _norm_mod(x, g_ref[...], sc_refs[s][...], sh_refs[s][...]).astype(BF16)
        o = 0
        f_in = _dot(hb, win_ref[:, o:o + fw])
        xcs_ref[rows, :] = _dot(f_in.astype(BF16), wc_ref[...]).astype(BF16)
        o += fw
        hy_ref[rows, :] = _dot(hb, win_ref[:, o:o + hw]).astype(BF16)
        o += hw
        u = _dot(hb, win_ref[:, o:o + pw])
        o += pw
        z_ref[rows, :] = _dot(hb, win_ref[:, o:o + zw]).astype(BF16)
        o += zw
        xbc_ref[rows, :] = _dot(hb, win_ref[:, o:o + cw]).astype(BF16)
        o += cw
        dt_ref[rows, :] = _dot(hb, win_ref[:, o:o + LANE])
        gw = POOL_GROUP_W
        parts = []
        for gi in range(len(POOL_WINDOWS)):
            ug = u[:, gi * gw:(gi + 1) * gw]
            parts.append(_dot_exact01(band_refs[s][gi], ug))
        pooled = jnp.concatenate(parts, axis=1) * invc_refs[s][...] - u
        pool_ref[rows, :] = (_dot(pooled.astype(BF16), pw_ref[...]) * ps_ref[...]).astype(BF16)


def _pool_tables(n_lat, n_ctx):
    tm = TOKEN_TILE
    p = jnp.arange(tm)
    bands, invs = [], []
    for row_len in (GRID_W, n_ctx):
        assert tm % row_len == 0
        pr, rr = p % row_len, p // row_len
        bg, ig = [], []
        for win in POOL_WINDOWS:
            lo = jnp.clip(pr - win // 2, 0, row_len)
            hi = jnp.clip(pr + win // 2, 0, row_len)
            q = pr[None, :]
            m = (rr[:, None] == rr[None, :]) & (q >= lo[:, None]) & (q < hi[:, None])
            bg.append(m.astype(BF16))
            ig.append(jnp.broadcast_to((1.0 / (hi - lo).astype(F32))[:, None], (tm, POOL_GROUP_W)))
        bands.append(jnp.stack(bg))
        invs.append(jnp.concatenate(ig, axis=1))
    return jnp.stack(bands), jnp.stack(invs)


def _block_diag(w):
    g, a, b = w.shape
    out = jnp.zeros((g * a, g * b), w.dtype)
    for i in range(g):
        out = out.at[i * a:(i + 1) * a, i * b:(i + 1) * b].set(w[i])
    return out


def _inproj(tok, layer, x_all, mod, norm_mix, w_in_p, wc, band, invc, pool_w_bd, pool_scale, widths,
            moe=None):
    d = x_all.shape[-1]
    fw, hw, pw, zw, cw = widths
    tm = TOKEN_TILE
    n_tok = tok.n_tok
    n_sub = INPROJ_TILES if tok.n_tiles % INPROJ_TILES == 0 else 1
    rows = n_sub * tm
    subs = range(n_sub)
    tile = lambda i, s: i * n_sub + s
    kind = lambda t: jnp.where(tok.is_ctx(t), 1, 0)
    const2 = lambda i: (0, 0)
    mod_specs = lambda lyr, part: [
        pl.BlockSpec((None, None, None, 1, d), lambda i, s=s: (lyr, tok.mod_row(tile(i, s)), part, 0, 0))
        for s in subs]
    pre_specs, pre_args, pre_outs = [], [], []
    if moe is not None:
        pre_specs = mod_specs(layer - 1, 5) + [
            pl.BlockSpec((TOP_K, rows, d), lambda i: (0, i, 0)),
            pl.BlockSpec((rows, LANE), lambda i: (i, 0))]
        pre_args = [mod] * n_sub + [moe[0], moe[1]]
        pre_outs = [jax.ShapeDtypeStruct((n_tok, d), F32)]
    outs = pre_outs + [
        jax.ShapeDtypeStruct((n_tok, 2 * fw), BF16),
        jax.ShapeDtypeStruct((n_tok, hw), BF16),
        jax.ShapeDtypeStruct((n_tok, pw), BF16),
        jax.ShapeDtypeStruct((n_tok, zw), BF16),
        jax.ShapeDtypeStruct((n_tok, cw), BF16),
        jax.ShapeDtypeStruct((n_tok, LANE), F32),
    ]
    return pl.pallas_call(
        functools.partial(_inproj_kernel, widths=widths, combine=moe is not None, n_sub=n_sub),
        grid=(tok.n_tiles // n_sub,),
        in_specs=[pl.BlockSpec((rows, d), lambda i: (i, 0))] + pre_specs
        + mod_specs(layer, 1) + mod_specs(layer, 0) + [
            pl.BlockSpec((None, 1, d), lambda i: (layer, 0, 0)),
            pl.BlockSpec((None,) + w_in_p.shape[1:], lambda i: (layer, 0, 0)),
            pl.BlockSpec(wc.shape, const2),
        ]
        + [pl.BlockSpec((None,) + band.shape[1:], lambda i, s=s: (kind(tile(i, s)), 0, 0, 0)) for s in subs]
        + [pl.BlockSpec((None,) + invc.shape[1:], lambda i, s=s: (kind(tile(i, s)), 0, 0)) for s in subs]
        + [
            pl.BlockSpec((None,) + pool_w_bd.shape[1:], lambda i: (layer, 0, 0)),
            pl.BlockSpec((None, 1, pw), lambda i: (layer, 0, 0)),
        ],
        out_specs=[pl.BlockSpec((rows, o.shape[1]), lambda i: (i, 0)) for o in outs],
        out_shape=outs,
        compiler_params=_cparams(("arbitrary",)),
        name="inproj",
    )(x_all, *pre_args, *([mod] * (2 * n_sub)), norm_mix, w_in_p, wc, *([band] * n_sub), *([invc] * n_sub),
      pool_w_bd, pool_scale)


def _trig_tables(n, period):
    k = jnp.arange(n, dtype=jnp.int32)

    def cs(j):
        m = (j[:, None] * k[None, :]) % period
        ang = m.astype(F32) * (2.0 * math.pi / period)
        return jnp.cos(ang), jnp.sin(ang)

    split = 64
    if n % split or n <= split:
        return cs(k)
    c1, s1 = cs(jnp.arange(n // split, dtype=jnp.int32) * split)
    c2, s2 = cs(jnp.arange(split, dtype=jnp.int32))
    c = c1[:, None, :] * c2[None, :, :] - s1[:, None, :] * s2[None, :, :]
    s = s1[:, None, :] * c2[None, :, :] + c1[:, None, :] * s2[None, :, :]
    return c.reshape(n, n), s.reshape(n, n)


def _hyena_dft_tables(n):
    c, s = _trig_tables(n, 2 * n)
    nyq = jnp.where(jnp.arange(n) % 2 == 0, 1.0, -1.0).astype(F32)
    fwd = jnp.concatenate([c, (-s).at[0].set(nyq)], axis=0).astype(BF16)
    inv = jnp.concatenate([c, (-s).at[:, 0].set(nyq)], axis=1).astype(BF16)
    return fwd, inv


def _fourier_kernel(c_ref, s_ref, xc_ref, xs_ref, o_ref, acc_ref, *, scale):
    k = pl.program_id(1)

    @pl.when(k == 0)
    def _():
        acc_ref[...] = jnp.zeros_like(acc_ref)

    nb = xc_ref.shape[0]
    rc = jnp.concatenate([xc_ref[b] for b in range(nb)], axis=1)
    rs = jnp.concatenate([xs_ref[b] for b in range(nb)], axis=1)
    acc_ref[...] += _dot(c_ref[...], rc) - _dot(s_ref[...], rs)

    @pl.when(k == pl.num_programs(1) - 1)
    def _():
        w = o_ref.shape[2]
        for b in range(nb):
            o_ref[b] = (acc_ref[:, b * w:(b + 1) * w] * scale).astype(o_ref.dtype)


def _fourier_seq(xcs3, cn, sn, n, row_off, fw):
    batch = xcs3.shape[0]
    tl = min(SEQ_TILE_1ACC, n)
    tk = min(SEQ_KTILE, n)
    nt = n // tl
    off = row_off // tk
    scale = 1.0 / math.sqrt(n * FOURIER_GROUP_W)
    return pl.pallas_call(
        functools.partial(_fourier_kernel, scale=scale),
        grid=(nt, n // tk),
        in_specs=[
            pl.BlockSpec((tl, tk), lambda i, k: (i, k)),
            pl.BlockSpec((tl, tk), lambda i, k: (i, k)),
            pl.BlockSpec((batch, tk, fw), lambda i, k: (0, off + k, 0)),
            pl.BlockSpec((batch, tk, fw), lambda i, k: (0, off + k, 1)),
        ],
        out_specs=pl.BlockSpec((batch, tl, fw), lambda i, k: (0, i, 0)),
        out_shape=jax.ShapeDtypeStruct((batch, n, fw), BF16),
        scratch_shapes=[pltpu.VMEM((tl, batch * fw), F32)],
        compiler_params=_cparams(("arbitrary", "arbitrary")),
        name="fourier_seq",
    )(cn, sn, xcs3, xcs3)


def _shift_rows(x, prev_row, next_row):
    n = x.shape[0]
    rows = lax.broadcasted_iota(jnp.int32, x.shape, 0)
    xm = jnp.where(rows == 0, prev_row, pltpu.roll(x, 1, 0))
    xp = jnp.where(rows == n - 1, next_row, pltpu.roll(x, n - 1, 0))
    return xm, xp


def _conv3(x, prev_row, next_row, w_ref, b_ref):
    xm, xp = _shift_rows(x, prev_row, next_row)
    return w_ref[0:1, :] * xm + w_ref[1:2, :] * x + w_ref[2:3, :] * xp + b_ref[...]


def _hy_pre_kernel(x_ref, xp_ref, xn_ref, w_ref, b_ref, vv_ref, x1_ref, *, hw):
    i = pl.program_id(1)
    last = pl.num_programs(1) - 1
    x = x_ref[...].astype(F32)
    hs = xp_ref.shape[0]
    prev_row = jnp.where(i == 0, 0.0, xp_ref[hs - 1:hs, :].astype(F32))
    next_row = jnp.where(i == last, 0.0, xn_ref[0:1, :].astype(F32))
    u = _conv3(x, prev_row, next_row, w_ref, b_ref)
    x1_ref[...] = u[:, :hw].astype(BF16)
    vv_ref[...] = (u[:, hw:2 * hw] * u[:, 2 * hw:]).astype(BF16)


def _halo_specs(width, tile, row_off, seq_rows):
    hs = BF16_SUBLANE
    per = tile // hs
    base = row_off // hs
    top = seq_rows // hs - 1
    prev = pl.BlockSpec((None, hs, width), lambda b, i: (b, jnp.maximum(base + i * per - 1, 0), 0))
    nxt = pl.BlockSpec((None, hs, width), lambda b, i: (b, jnp.minimum(base + (i + 1) * per, top), 0))
    return prev, nxt


def _hy_pre(hy3, conv_w, conv_b, layer, n, row_off, hw):
    batch, seq, w3 = hy3.shape
    tl = min(SEQ_TILE, n)
    off = row_off // tl
    prev, nxt = _halo_specs(w3, tl, row_off, seq)
    out = jax.ShapeDtypeStruct((batch, n, hw), BF16)
    return pl.pallas_call(
        functools.partial(_hy_pre_kernel, hw=hw),
        grid=(batch, n // tl),
        in_specs=[
            pl.BlockSpec((None, tl, w3), lambda b, i: (b, off + i, 0)),
            prev, nxt,
            pl.BlockSpec((None,) + conv_w.shape[1:], lambda b, i: (layer, 0, 0)),
            pl.BlockSpec((None, 1, w3), lambda b, i: (layer, 0, 0)),
        ],
        out_specs=[pl.BlockSpec((None, tl, hw), lambda b, i: (b, i, 0))] * 2,
        out_shape=[out, out],
        compiler_params=_cparams(("arbitrary", "arbitrary")),
        name="hyena_pre",
    )(hy3, hy3, hy3, conv_w, conv_b)


def _filter_mlp_kernel(fr_ref, dl_ref, w1_ref, b1_ref, w2_ref, b2_ref, w3_ref, g_ref, nrm_ref, *, n, hw):
    j = pl.program_id(1)
    rt = g_ref.shape[0]
    lane = lax.broadcasted_iota(jnp.int32, (rt, LANE), 1)
    pos = (lax.broadcasted_iota(jnp.int32, (rt, LANE), 0) + j * rt).astype(F32)
    t = pos / (n - 1)
    ang = 2.0 * math.pi * pos / n
    arg = ang * fr_ref[...]
    feats = jnp.where(lane == 0, t,
                      jnp.where(lane <= HYENA_BANDS, jnp.cos(arg),
                                jnp.where(lane <= 2 * HYENA_BANDS, -jnp.sin(arg), 0.0)))
    h1 = jnp.sin(_dot_hi(feats, w1_ref[...]) + b1_ref[...])
    h2 = jnp.sin(_dot_hi(h1, w2_ref[...]) + b2_ref[...])
    k = _dot_hi(h2, w3_ref[...])
    decay = jnp.exp(-t[:, 0:1] * dl_ref[...])
    kf = k[:, :hw] * decay
    kb = jnp.where(pos[:, 0:1] == 0.0, 0.0, k[:, hw:] * decay)
    g_ref[:, :hw] = (kf + kb).astype(BF16)
    g_ref[:, hw:] = (kf - kb).astype(BF16)
    part = jnp.sum(jnp.abs(kf) + jnp.abs(kb), axis=0, keepdims=True)

    @pl.when(j == 0)
    def _():
        nrm_ref[...] = part

    @pl.when(j > 0)
    def _():
        nrm_ref[...] += part


def _filter_mlp(n, freqs_row, deltas_row, w1, b1, w2, b2, w3):
    depth = w1.shape[0]
    hw = deltas_row.shape[1]
    rt = min(SEQ_TILE, n)
    per_layer = lambda l, j: (l, 0, 0)
    return pl.pallas_call(
        functools.partial(_filter_mlp_kernel, n=n, hw=hw),
        grid=(depth, n // rt),
        in_specs=[
            pl.BlockSpec((1, LANE), lambda l, j: (0, 0)),
            pl.BlockSpec((1, hw), lambda l, j: (0, 0)),
            pl.BlockSpec((None,) + w1.shape[1:], per_layer),
            pl.BlockSpec((None,) + b1.shape[1:], per_layer),
            pl.BlockSpec((None,) + w2.shape[1:], per_layer),
            pl.BlockSpec((None,) + b2.shape[1:], per_layer),
            pl.BlockSpec((None,) + w3.shape[1:], per_layer),
        ],
        out_specs=[
            pl.BlockSpec((rt, 2 * hw), lambda l, j: (j, l)),
            pl.BlockSpec((None, 1, hw), lambda l, j: (l, 0, 0)),
        ],
        out_shape=[
            jax.ShapeDtypeStruct((n, depth * 2 * hw), BF16),
            jax.ShapeDtypeStruct((depth, 1, hw), F32),
        ],
        compiler_params=_cparams(("arbitrary", "arbitrary")),
        name="hyena_filter_mlp",
    )(freqs_row, deltas_row, w1, b1, w2, b2, w3)


def _rdft_accumulate(fc_ref, fs_ref, rhs, accc_ref, accs_ref):
    k = pl.program_id(1)

    @pl.when(k == 0)
    def _():
        accc_ref[...] = jnp.zeros_like(accc_ref)
        accs_ref[...] = jnp.zeros_like(accs_ref)

    accc_ref[...] += _dot(fc_ref[...], rhs)
    accs_ref[...] += _dot(fs_ref[...], rhs)


def _filter_spec_kernel(fc_ref, fs_ref, g_ref, nrm_ref, kr_ref, ki_ref, accc_ref, accs_ref, *, hw):
    i = pl.program_id(0)
    _rdft_accumulate(fc_ref, fs_ref, g_ref[...], accc_ref, accs_ref)

    @pl.when(pl.program_id(1) == pl.num_programs(1) - 1)
    def _():
        tm = accc_ref.shape[0]
        row0 = (lax.broadcasted_iota(jnp.int32, (tm, hw), 0) + i * tm) == 0
        for l in range(kr_ref.shape[0]):
            inv = 1.0 / nrm_ref[l]
            o = l * 2 * hw
            kr_ref[l] = accc_ref[:, o:o + hw] * inv
            ki_ref[l] = jnp.where(row0, accs_ref[:, o:o + hw], accs_ref[:, o + hw:o + 2 * hw]) * inv


def _filter_spectrum(fwd, g_all, nrm, n, hw):
    depth = nrm.shape[0]
    tl = min(SEQ_TILE, n)
    nt = n // tl
    width = g_all.shape[1]
    out = jax.ShapeDtypeStruct((depth, n, hw), F32)
    return pl.pallas_call(
        functools.partial(_filter_spec_kernel, hw=hw),
        grid=(nt, nt),
        in_specs=[
            pl.BlockSpec((tl, tl), lambda i, k: (i, k)),
            pl.BlockSpec((tl, tl), lambda i, k: (nt + i, k)),
            pl.BlockSpec((tl, width), lambda i, k: (k, 0)),
            pl.BlockSpec(nrm.shape, lambda i, k: (0, 0, 0)),
        ],
        out_specs=[pl.BlockSpec((depth, tl, hw), lambda i, k: (0, i, 0))] * 2,
        out_shape=[out, out],
        scratch_shapes=[pltpu.VMEM((tl, width), F32)] * 2,
        compiler_params=_cparams(("arbitrary", "arbitrary")),
        name="hyena_filter_spectrum",
    )(fwd, fwd, g_all, nrm)


def _hy_fwd_kernel(fc_ref, fs_ref, vv_ref, kr_ref, ki_ref, y_ref, accc_ref, accs_ref):
    i = pl.program_id(0)
    nb = vv_ref.shape[0]
    rhs = jnp.concatenate([vv_ref[b] for b in range(nb)], axis=1)
    _rdft_accumulate(fc_ref, fs_ref, rhs, accc_ref, accs_ref)

    @pl.when(pl.program_id(1) == pl.num_programs(1) - 1)
    def _():
        tm, hw = kr_ref.shape
        row0 = (lax.broadcasted_iota(jnp.int32, (tm, hw), 0) + i * tm) == 0
        kr, ki = kr_ref[...], ki_ref[...]
        for b in range(nb):
            vr = accc_ref[:, b * hw:(b + 1) * hw]
            vi = accs_ref[:, b * hw:(b + 1) * hw]
            yr = jnp.where(row0, 0.5 * vr * kr, vr * kr - vi * ki)
            yi = jnp.where(row0, 0.5 * vi * ki, vr * ki + vi * kr)
            y_ref[0, :, b * hw:(b + 1) * hw] = yr.astype(BF16)
            y_ref[1, :, b * hw:(b + 1) * hw] = yi.astype(BF16)


def _hy_fwd(fwd, vv, kr, ki, layer, n, hw):
    batch = vv.shape[0]
    tl = min(SEQ_TILE, n)
    tk = min(SEQ_KTILE, n)
    nt = n // tl
    return pl.pallas_call(
        _hy_fwd_kernel,
        grid=(nt, n // tk),
        in_specs=[
            pl.BlockSpec((tl, tk), lambda i, k: (i, k)),
            pl.BlockSpec((tl, tk), lambda i, k: (nt + i, k)),
            pl.BlockSpec((batch, tk, hw), lambda i, k: (0, k, 0)),
            pl.BlockSpec((None, tl, hw), lambda i, k: (layer, i, 0)),
            pl.BlockSpec((None, tl, hw), lambda i, k: (layer, i, 0)),
        ],
        out_specs=pl.BlockSpec((2, tl, batch * hw), lambda i, k: (0, i, 0)),
        out_shape=jax.ShapeDtypeStruct((2, n, batch * hw), BF16),
        scratch_shapes=[pltpu.VMEM((tl, batch * hw), F32)] * 2,
        compiler_params=_cparams(("arbitrary", "arbitrary")),
        name="hyena_fwd_dft",
    )(fwd, fwd, vv, kr, ki)


def _hy_inv_kernel(fi_ref, y_ref, x1_ref, vv_ref, bias_ref, o_ref, acc_ref, *, scale):
    k = pl.program_id(1)

    @pl.when(k == 0)
    def _():
        acc_ref[...] = jnp.zeros_like(acc_ref)

    acc_ref[...] += _dot(fi_ref[...], y_ref[...])

    @pl.when(k == pl.num_programs(1) - 1)
    def _():
        nb, _, hw = x1_ref.shape
        for b in range(nb):
            conv = acc_ref[:, b * hw:(b + 1) * hw] * scale
            vv = vv_ref[b].astype(F32)
            o_ref[b] = (x1_ref[b].astype(F32) * (conv + vv * bias_ref[...])).astype(BF16)


def _hy_inv(inv, y2, x1, vv, hy_bias, layer, n, hw):
    batch = vv.shape[0]
    tl = min(SEQ_TILE_1ACC, n)
    tk = min(SEQ_KTILE, n)
    nt = n // tl
    return pl.pallas_call(
        functools.partial(_hy_inv_kernel, scale=1.0 / n),
        grid=(nt, 2 * n // tk),
        in_specs=[
            pl.BlockSpec((tl, tk), lambda i, k: (i, k)),
            pl.BlockSpec((tk, batch * hw), lambda i, k: (k, 0)),
            pl.BlockSpec((batch, tl, hw), lambda i, k: (0, i, 0)),
            pl.BlockSpec((batch, tl, hw), lambda i, k: (0, i, 0)),
            pl.BlockSpec((None, 1, hw), lambda i, k: (layer, 0, 0)),
        ],
        out_specs=pl.BlockSpec((batch, tl, hw), lambda i, k: (0, i, 0)),
        out_shape=jax.ShapeDtypeStruct((batch, n, hw), BF16),
        scratch_shapes=[pltpu.VMEM((tl, batch * hw), F32)],
        compiler_params=_cparams(("arbitrary", "arbitrary")),
        name="hyena_inv_dft",
    )(inv, y2.reshape(2 * n, batch * hw), x1, vv, hy_bias)


def _softplus(x):
    return jnp.maximum(x, 0.0) + jnp.log1p(jnp.exp(-jnp.abs(x)))


def _ssd_kernel(*refs, reverse, final, nc, lane0):
    if final:
        (xbc_ref, xp_ref, xn_ref, dt_ref, h0_ref, tri_ref, exp_ref, cw_ref, cb_ref, dtb_ref, a_ref,
         z_ref, yf_ref, d_ref, nrm_ref, out_ref, hout_ref, h_scr) = refs
    else:
        (xbc_ref, xp_ref, xn_ref, dt_ref, h0_ref, tri_ref, exp_ref, cw_ref, cb_ref, dtb_ref, a_ref,
         out_ref, hout_ref, h_scr) = refs
    step = pl.program_id(1)
    cc = (nc - 1 - step) if reverse else step

    @pl.when(step == 0)
    def _():
        h_scr[...] = h0_ref[...]

    nb, q = xbc_ref.shape[0], xbc_ref.shape[1]
    hs = xp_ref.shape[1]
    width = SSM_HEADS * SSM_HEAD_DIM
    gs = SSM_STATE
    pd = SSM_HEAD_DIM
    ri = lax.broadcasted_iota(jnp.int32, (q, q), 0)
    ci = lax.broadcasted_iota(jnp.int32, (q, q), 1)
    mask = (ri <= ci) if reverse else (ri >= ci)

    for s in range(nb):
        x = xbc_ref[s].astype(F32)
        prev_row = jnp.where(cc == 0, 0.0, xp_ref[s, hs - 1:hs, :].astype(F32))
        next_row = jnp.where(cc == nc - 1, 0.0, xn_ref[s, 0:1, :].astype(F32))
        u = _conv3(x, prev_row, next_row, cw_ref, cb_ref)
        u = u * _sigmoid(u)
        xs = u[:, :width]
        bm = u[:, width:width + SSM_GROUPS * gs]
        cm = u[:, width + SSM_GROUPS * gs:]

        dt = _softplus(dt_ref[s] + dtb_ref[...])
        acs = _dot_exact01(tri_ref[...], dt * a_ref[...])
        acs_t = acs.T
        tot = acs[0:1, :] if reverse else acs[q - 1:q, :]
        to_end = jnp.exp(tot - acs)
        frm = jnp.exp(acs)
        cdec = jnp.exp(tot)

        stacked = jnp.concatenate([dt, to_end, frm, jnp.broadcast_to(cdec, (8, LANE))], axis=0)
        s1, s2, s3 = _split3(stacked)
        ex = exp_ref[...]
        rep = _dot(s1, ex) + _dot(s2, ex) + _dot(s3, ex)
        xdt = xs * rep[0:q]
        wts = (xdt * rep[q:2 * q]).astype(BF16)
        frm_rep = rep[2 * q:3 * q]
        cdec_rep = rep[3 * q:3 * q + 1]
        xdt_b = xdt.astype(BF16)
        half = lax.broadcasted_iota(jnp.int32, (q, 2 * pd), 1) < pd

        ys = []
        gw = HEADS_PER_GROUP * pd
        for g in range(SSM_GROUPS):
            bg = bm[:, g * gs:(g + 1) * gs]
            cg = cm[:, g * gs:(g + 1) * gs].astype(BF16)
            scores = lax.dot_general(cg, bg.astype(BF16), (((1,), (1,)), ((), ())),
                                     preferred_element_type=F32)
            hg = h_scr[s, g]
            yoff = _dot(cg, hg.astype(BF16)) * frm_rep[:, g * gw:(g + 1) * gw]
            for pr in range(HEADS_PER_GROUP // 2):
                pair = []
                c0 = g * gw + pr * 2 * pd
                for j in range(2):
                    li = lane0 + g * HEADS_PER_GROUP + 2 * pr + j
                    lm = jnp.where(mask, jnp.exp(acs[:, li:li + 1] - acs_t[li:li + 1, :]), 0.0)
                    m = (scores * lm).astype(BF16)
                    pair.append(_dot(m, xdt_b[:, c0:c0 + 2 * pd]))
                ys.append(jnp.where(half, pair[0], pair[1]) + yoff[:, pr * 2 * pd:(pr + 1) * 2 * pd])
            st = _dot(bg.T.astype(BF16), wts[:, g * gw:(g + 1) * gw])
            h_scr[s, g] = hg * cdec_rep[:, g * gw:(g + 1) * gw] + st
        y = jnp.concatenate(ys, axis=1)

        if final:
            y = y + yf_ref[s].astype(F32) + xs * d_ref[...]
            z = z_ref[s].astype(F32)
            y = y * (z * _sigmoid(z))
            ms = jnp.mean(y * y, axis=-1, keepdims=True)
            out_ref[s] = ((y * lax.rsqrt(ms + EPS)) * nrm_ref[...]).astype(out_ref.dtype)
        else:
            out_ref[s] = y.astype(out_ref.dtype)

    @pl.when(step == nc - 1)
    def _():
        hout_ref[...] = h_scr[...]


def _ssd_pass(xbc3, dt3, h0, tri, conv_w, conv_b, dt_bias, a_row, layer, n, row_off, *, reverse,
              fin=None):
    batch, seq, cw = xbc3.shape
    q = SSM_CHUNK
    nc = n // q
    off = row_off // q
    width = SSM_HEADS * SSM_HEAD_DIM

    def cidx(c):
        return off + ((nc - 1 - c) if reverse else c)

    hsz = BF16_SUBLANE
    per = q // hsz
    top = seq // hsz - 1
    nb = SSD_SAMPLES if batch % SSD_SAMPLES == 0 else 1
    per_layer = lambda b, c: (layer, 0, 0)
    in_specs = [
        pl.BlockSpec((nb, q, cw), lambda b, c: (b, cidx(c), 0)),
        pl.BlockSpec((nb, hsz, cw), lambda b, c: (b, jnp.maximum(cidx(c) * per - 1, 0), 0)),
        pl.BlockSpec((nb, hsz, cw), lambda b, c: (b, jnp.minimum((cidx(c) + 1) * per, top), 0)),
        pl.BlockSpec((nb, q, LANE), lambda b, c: (b, cidx(c), 0)),
        pl.BlockSpec((nb,) + h0.shape[1:], lambda b, c: (b, 0, 0, 0)),
        pl.BlockSpec((q, q), lambda b, c: (0, 0)),
        pl.BlockSpec((LANE, width), lambda b, c: (0, 0)),
        pl.BlockSpec((None,) + conv_w.shape[1:], per_layer),
        pl.BlockSpec((None, 1, cw), per_layer),
        pl.BlockSpec((None, 1, LANE), per_layer),
        pl.BlockSpec((None, 1, LANE), per_layer),
    ]
    lane0 = SSM_HEADS if reverse else 0
    head_rep = (jnp.arange(LANE)[:, None] == lane0 + jnp.arange(width)[None, :] // SSM_HEAD_DIM).astype(BF16)
    args = [xbc3, xbc3, xbc3, dt3, h0, tri, head_rep, conv_w, conv_b, dt_bias, a_row]
    if fin is not None:
        z3, yf, d_row, nrm_row = fin
        in_specs += [
            pl.BlockSpec((nb, q, width), lambda b, c: (b, cidx(c), 0)),
            pl.BlockSpec((nb, q, width), lambda b, c: (b, cidx(c) - off, 0)),
            pl.BlockSpec((None, 1, width), per_layer),
            pl.BlockSpec((None, 1, width), per_layer),
        ]
        args += [z3, yf, d_row, nrm_row]
    out_dtype = BF16 if fin is not None else F32
    return pl.pallas_call(
        functools.partial(_ssd_kernel, reverse=reverse, final=fin is not None, nc=nc, lane0=lane0),
        grid=(batch // nb, nc),
        in_specs=in_specs,
        out_specs=[
            pl.BlockSpec((nb, q, width), lambda b, c: (b, cidx(c) - off, 0)),
            pl.BlockSpec((nb,) + h0.shape[1:], lambda b, c: (b, 0, 0, 0)),
        ],
        out_shape=[
            jax.ShapeDtypeStruct((batch, n, width), out_dtype),
            jax.ShapeDtypeStruct(h0.shape, F32),
        ],
        scratch_shapes=[pltpu.VMEM((nb,) + h0.shape[1:], F32)],
        compiler_params=_cparams(("arbitrary", "arbitrary")),
        name="ssd_bwd_final" if reverse else "ssd_fwd",
    )(*args)


def _merge_kernel(*refs, tok, splits, n_sub):
    it = iter(refs)
    take = lambda k: [next(it) for _ in range(k)]
    x_ref, = take(1)
    per_sub = [take(9) for _ in range(n_sub)]
    gn_ref, p_ref, wg_ref, wb_ref, wo_ref, o_ref = take(6)
    tm = TOKEN_TILE
    for s in range(n_sub):
        sc_ref, sh_ref, g1_ref, fl_ref, fc_ref, hl_ref, hc_ref, sl_ref, scx_ref = per_sub[s]
        rows = slice(s * tm, (s + 1) * tm)
        ctx = tok.is_ctx(pl.program_id(0) * n_sub + s)
        x = x_ref[rows, :]
        hb = _norm_mod(x, gn_ref[...], sc_ref[...], sh_ref[...]).astype(BF16)
        ys = (jnp.where(ctx, fc_ref[...], fl_ref[...]), jnp.where(ctx, hc_ref[...], hl_ref[...]),
              p_ref[rows, :], jnp.where(ctx, scx_ref[...], sl_ref[...]))
        merged = None
        lo = 0
        for k, hi in enumerate(splits):
            gate = _sigmoid(_dot(hb, wg_ref[k]))
            term = gate * _dot(ys[k], wb_ref[lo:hi, :])
            merged = term if merged is None else merged + term
            lo = hi
        out = _dot(merged.astype(BF16), wo_ref[...])
        o_ref[rows, :] = x + g1_ref[...] * out


def _merge(tok, layer, x_all, mod, norm_mix, f_lat, f_ctx, h_lat, h_ctx, pool, s_lat, s_ctx,
           w_gate, w_branch, w_out, splits):
    d = x_all.shape[-1]
    tm = TOKEN_TILE
    fw, hw, pw, sw = f_lat.shape[-1], h_lat.shape[-1], pool.shape[-1], s_lat.shape[-1]
    n_sub = MERGE_TILES if tok.n_tiles % MERGE_TILES == 0 else 1
    rows = n_sub * tm
    sub_specs, sub_args = [], []
    for s in range(n_sub):
        tile = lambda i, s=s: i * n_sub + s
        sub_specs += [
            tok.mod_spec(layer, 1, d, tile), tok.mod_spec(layer, 0, d, tile), tok.mod_spec(layer, 2, d, tile),
            tok.lat_spec(fw, tile), tok.ctx_spec(fw, tile),
            tok.lat_spec(hw, tile), tok.ctx_spec(hw, tile),
            tok.lat_spec(sw, tile), tok.ctx_spec(sw, tile),
        ]
        sub_args += [mod, mod, mod, f_lat, f_ctx, h_lat, h_ctx, s_lat, s_ctx]
    return pl.pallas_call(
        functools.partial(_merge_kernel, tok=tok, splits=splits, n_sub=n_sub),
        grid=(tok.n_tiles // n_sub,),
        in_specs=[pl.BlockSpec((rows, d), lambda i: (i, 0))] + sub_specs + [
            pl.BlockSpec((None, 1, d), lambda i: (layer, 0, 0)),
            pl.BlockSpec((rows, pw), lambda i: (i, 0)),
            pl.BlockSpec((None,) + w_gate.shape[1:], lambda i: (layer, 0, 0, 0)),
            pl.BlockSpec((None,) + w_branch.shape[1:], lambda i: (layer, 0, 0)),
            pl.BlockSpec((None,) + w_out.shape[1:], lambda i: (layer, 0, 0)),
        ],
        out_specs=pl.BlockSpec((rows, d), lambda i: (i, 0)),
        out_shape=jax.ShapeDtypeStruct(x_all.shape, F32),
        compiler_params=_cparams(("arbitrary",)),
        name="merge",
    )(x_all, *sub_args, norm_mix, pool, w_gate, w_branch, w_out)


def _router_kernel(x_ref, sc_ref, sh_ref, gn_ref, rwh_ref, rwl_ref, rb_ref, utri_ref,
                   h_ref, e_ref, r_ref, w_ref, cnt_ref):
    t = pl.program_id(0)

    @pl.when(t == 0)
    def _():
        cnt_ref[...] = jnp.zeros_like(cnt_ref)

    h = _norm_mod(x_ref[...], gn_ref[...], sc_ref[...], sh_ref[...])
    hh = h.astype(BF16)
    h_ref[...] = hh
    hl = (h - hh.astype(F32)).astype(BF16)
    tm = h.shape[0]
    n_exp = rwh_ref.shape[0]
    nt = (((1,), (1,)), ((), ()))
    dg = lambda a, b: lax.dot_general(a, b, nt, preferred_element_type=F32)
    rwh = rwh_ref[...]
    logits = dg(rwh, hh) + dg(rwh, hl) + dg(rwl_ref[...], hh) + rb_ref[:, 0:1]
    eid = lax.broadcasted_iota(jnp.int32, (n_exp, tm), 0)
    carry = cnt_ref[:, 0:1]
    tops, sels = [], []
    cur = logits
    for _ in range(TOP_K):
        m = jnp.max(cur, axis=0, keepdims=True)
        idx = jnp.min(jnp.where(cur == m, eid, n_exp), axis=0, keepdims=True)
        sel = eid == idx
        tops.append((m, idx))
        sels.append(sel)
        cur = jnp.where(sel, -jnp.inf, cur)
    cnt = jnp.zeros((n_exp, tm), F32)
    for s in sels:
        cnt = jnp.where(s, 1.0, cnt)
    before = _dot(cnt.astype(BF16), utri_ref[...]) + carry
    exps = [jnp.exp(m - tops[0][0]) for m, _ in tops]
    den = exps[0]
    for v in exps[1:]:
        den = den + v
    row8 = lax.broadcasted_iota(jnp.int32, (8, tm), 0)
    rowl = lax.broadcasted_iota(jnp.int32, (LANE, tm), 0)
    e_out = jnp.zeros((8, tm), jnp.int32)
    r_out = jnp.zeros((8, tm), jnp.int32)
    w_t = jnp.zeros((LANE, tm), F32)
    for k in range(TOP_K):
        rank = jnp.sum(jnp.where(sels[k], before, 0.0), axis=0, keepdims=True)
        e_out = jnp.where(row8 == k, tops[k][1], e_out)
        r_out = jnp.where(row8 == k, rank.astype(jnp.int32), r_out)
        w_t = jnp.where(rowl == k, exps[k] / den, w_t)
    e_ref[...] = e_out
    r_ref[...] = r_out
    w_ref[...] = w_t.T
    cnt_ref[...] += jnp.broadcast_to(jnp.sum(cnt, axis=1, keepdims=True), cnt_ref.shape)


def _router(tok, layer, x_all, mod, norm_ffn, rw_hi, rw_lo, router_b_col, utri):
    d = x_all.shape[-1]
    tm = TOKEN_TILE
    n_tok = tok.n_tok
    n_exp = rw_hi.shape[1]
    row_i = jax.ShapeDtypeStruct((8, n_tok), jnp.int32)
    return pl.pallas_call(
        _router_kernel,
        grid=(tok.n_tiles,),
        in_specs=[
            pl.BlockSpec((tm, d), lambda t: (t, 0)),
            tok.mod_spec(layer, 4, d),
            tok.mod_spec(layer, 3, d),
            pl.BlockSpec((None, 1, d), lambda t: (layer, 0, 0)),
            pl.BlockSpec((None, n_exp, d), lambda t: (layer, 0, 0)),
            pl.BlockSpec((None, n_exp, d), lambda t: (layer, 0, 0)),
            pl.BlockSpec((None, n_exp, LANE), lambda t: (layer, 0, 0)),
            pl.BlockSpec((tm, tm), lambda t: (0, 0)),
        ],
        out_specs=[
            pl.BlockSpec((tm, d), lambda t: (t, 0)),
            pl.BlockSpec((8, tm), lambda t: (0, t)),
            pl.BlockSpec((8, tm), lambda t: (0, t)),
            pl.BlockSpec((tm, LANE), lambda t: (t, 0)),
            pl.BlockSpec((n_exp, LANE), lambda t: (0, 0)),
        ],
        out_shape=[
            jax.ShapeDtypeStruct((n_tok, d), BF16),
            row_i,
            row_i,
            jax.ShapeDtypeStruct((n_tok, LANE), F32),
            jax.ShapeDtypeStruct((n_exp, LANE), F32),
        ],
        compiler_params=_cparams(("arbitrary",)),
        name="router",
    )(x_all, mod, mod, norm_ffn, rw_hi, rw_lo, router_b_col, utri)


def _moe_kernel(be_ref, na_ref, x_ref, wu_ref, bu_ref, wd_ref, bd_ref, o_ref, wu_scr, wd_scr, *, ff):
    i = pl.program_id(0)
    active = i < na_ref[0]
    fresh = jnp.logical_or(i == 0, be_ref[i] != be_ref[jnp.maximum(i - 1, 0)])

    @pl.when(jnp.logical_and(active, fresh))
    def _():
        wu_scr[...] = wu_ref[...].astype(BF16)
        wd_scr[...] = wd_ref[...].astype(BF16)

    @pl.when(active)
    def _():
        hu = _dot(x_ref[...], wu_scr[...]) + bu_ref[...]
        glu = jnp.minimum(hu[:, :ff], SWIGLU_LIMIT)
        lin = jnp.clip(hu[:, ff:], -SWIGLU_LIMIT, SWIGLU_LIMIT)
        act = glu * _sigmoid(SWIGLU_ALPHA * glu) * (lin + 1.0)
        o_ref[...] = (_dot(act.astype(BF16), wd_scr[...]) + bd_ref[...]).astype(o_ref.dtype)

    @pl.when(jnp.logical_not(active))
    def _():
        o_ref[...] = jnp.zeros_like(o_ref)


def _moe(layer, x_sorted, blk_e, n_active, w_up, b_up, w_down, b_down):
    n_rows, d = x_sorted.shape
    ff = w_down.shape[2]
    bm = MOE_ROWS
    grid_spec = pltpu.PrefetchScalarGridSpec(
        num_scalar_prefetch=2,
        grid=(n_rows // bm,),
        in_specs=[
            pl.BlockSpec((bm, d), lambda i, be, na: (i, 0)),
            pl.BlockSpec((None, None, d, 2 * ff), lambda i, be, na: (layer, be[i], 0, 0)),
            pl.BlockSpec((None, None, 1, 2 * ff), lambda i, be, na: (layer, be[i], 0, 0)),
            pl.BlockSpec((None, None, ff, d), lambda i, be, na: (layer, be[i], 0, 0)),
            pl.BlockSpec((None, None, 1, d), lambda i, be, na: (layer, be[i], 0, 0)),
        ],
        out_specs=pl.BlockSpec((bm, d), lambda i, be, na: (i, 0)),
        scratch_shapes=[pltpu.VMEM((d, 2 * ff), BF16), pltpu.VMEM((ff, d), BF16)],
    )
    return pl.pallas_call(
        functools.partial(_moe_kernel, ff=ff),
        grid_spec=grid_spec,
        out_shape=jax.ShapeDtypeStruct((n_rows, d), BF16),
        compiler_params=_cparams(("arbitrary",)),
        name="moe_experts",
    )(blk_e, n_active, x_sorted, w_up, b_up, w_down, b_down)


def _final_kernel(x_ref, g2_ref, yg_ref, w_ref, g_ref, o_ref):
    x = _moe_residual(x_ref[...], g2_ref[...], yg_ref, w_ref)
    ms = jnp.mean(x * x, axis=-1, keepdims=True)
    o_ref[...] = (x * lax.rsqrt(ms + EPS)) * g_ref[...]


def _final(tok, layer, x_all, mod, y_gath, top_w, gain):
    batch, seq, n_lat = tok.batch, tok.seq, tok.n_lat
    d = x_all.shape[-1]
    tm = TOKEN_TILE
    return pl.pallas_call(
        _final_kernel,
        grid=(batch, n_lat // tm),
        in_specs=[
            pl.BlockSpec((None, tm, d), lambda b, i: (b, i, 0)),
            pl.BlockSpec((None, None, None, 1, d), lambda b, i: (layer, tok.mod_off + b, 5, 0, 0)),
            pl.BlockSpec((TOP_K, None, tm, d), lambda b, i: (0, b, i, 0)),
            pl.BlockSpec((None, tm, LANE), lambda b, i: (b, i, 0)),
            pl.BlockSpec((1, d), lambda b, i: (0, 0)),
        ],
        out_specs=pl.BlockSpec((None, tm, d), lambda b, i: (b, i, 0)),
        out_shape=jax.ShapeDtypeStruct((batch, n_lat, d), F32),
        compiler_params=_cparams(("arbitrary", "arbitrary")),
        name="final_combine_norm",
    )(x_all.reshape(batch, seq, d), mod, y_gath.reshape(TOP_K, batch, seq, d),
      top_w.reshape(batch, seq, LANE), gain)


def _dispatch_plan(top_e, rank, counts, n_tok):
    n_exp = counts.shape[0]
    bm = MOE_ROWS
    n_pairs = n_tok * TOP_K
    n_blk = -(-(n_pairs + n_exp * (bm - 1)) // bm)
    padded = (counts + bm - 1) // bm * bm
    pad_end = jnp.cumsum(padded)
    pad_start = pad_end - padded
    dest = rank
    for e in range(n_exp):
        dest = dest + jnp.where(top_e == e, pad_start[e], 0)
    n_active = (pad_end[-1] // bm).astype(jnp.int32).reshape(1)
    blk_start = jnp.arange(n_blk, dtype=jnp.int32) * bm
    blk_e = jnp.minimum(jnp.sum(pad_end[None, :] <= blk_start[:, None], axis=1), n_exp - 1).astype(jnp.int32)
    n_rows = n_blk * bm
    shift = (n_rows - 1).bit_length()
    assert (n_exp + 1) << shift < 2 ** 31
    t_ids = jnp.arange(n_tok, dtype=jnp.int32)[None, :]
    k_ids = jnp.arange(TOP_K, dtype=jnp.int32)[:, None]
    pair_keys = (top_e << shift) | (t_ids * TOP_K + k_ids)
    n_fill = n_rows - n_pairs
    per = bm - 1
    need = jnp.repeat(padded - counts, per)
    fill_e = jnp.where(jnp.tile(jnp.arange(per, dtype=jnp.int32), n_exp) < need,
                       jnp.repeat(jnp.arange(n_exp, dtype=jnp.int32), per), n_exp)
    fill_e = jnp.concatenate([fill_e, jnp.full((n_fill - n_exp * per,), n_exp, jnp.int32)])
    fill_keys = (fill_e << shift) | (n_pairs + jnp.arange(n_fill, dtype=jnp.int32))
    pos = lax.sort(jnp.concatenate([pair_keys.reshape(-1), fill_keys])) & ((1 << shift) - 1)
    row_tok = jnp.where(pos < n_pairs, pos // TOP_K, pos % n_tok)
    return dest, row_tok, blk_e, n_active


def kernel(x, c, ctx, c_ctx, w_mod, b_mod, norm_mix, norm_ffn, w_in, hy_conv_w, hy_conv_b, hy_ffn_w1, hy_ffn_b1, hy_ffn_w2, hy_ffn_b2, hy_ffn_w3, hy_bias, pool_w, pool_scale, ssm_conv_w, ssm_conv_b, ssm_dt_bias, ssm_a_log, ssm_d, ssm_norm, w_branch, w_gate, w_out, router_w, router_b, exp_w_up, exp_b_up, exp_w_down, exp_b_down, norm_final):
    batch, n_lat, d = x.shape
    n_ctx = ctx.shape[1]
    depth = w_mod.shape[0]
    assert n_lat // GRID_W * GRID_W == n_lat and n_lat % n_ctx == 0

    hw = hy_bias.shape[1]
    pw = pool_scale.shape[1]
    sw = ssm_norm.shape[1]
    cw = ssm_conv_w.shape[2]
    fw = w_in.shape[2] - 3 * hw - pw - sw - cw - 2 * SSM_HEADS
    widths = (fw, 3 * hw, pw, sw, cw)
    splits = (fw, fw + hw, fw + hw + pw, fw + hw + pw + sw)
    n_exp = router_w.shape[2]

    w_in_p = jnp.pad(w_in, ((0, 0), (0, 0), (0, LANE - 2 * SSM_HEADS))).astype(BF16)
    w_gate_b = w_gate.astype(BF16)
    w_branch_b = w_branch.astype(BF16)
    w_out_b = w_out.astype(BF16)
    pool_w_bd = jax.vmap(_block_diag)(pool_w).astype(BF16)
    r3 = lambda a: a.reshape(a.shape[0], 1, a.shape[1])
    lane_pad = lambda a: jnp.pad(a, ((0, 0), (0, LANE - a.shape[1])))
    norm_mix3, norm_ffn3 = r3(norm_mix), r3(norm_ffn)
    hy_conv_b3, hy_bias3 = r3(hy_conv_b), r3(hy_bias)
    ssm_conv_b3 = r3(ssm_conv_b)
    pool_scale3 = r3(pool_scale)
    dt_bias3 = r3(lane_pad(ssm_dt_bias.reshape(depth, 2 * SSM_HEADS)))
    a_row3 = r3(lane_pad(-jnp.exp(ssm_a_log.astype(F32)).reshape(depth, 2 * SSM_HEADS)))
    d_row3 = r3(jnp.repeat(ssm_d, SSM_HEAD_DIM, axis=1))
    ssm_norm3 = r3(ssm_norm)
    rw_t = jnp.swapaxes(router_w, 1, 2)
    rw_hi = rw_t.astype(BF16)
    rw_lo = (rw_t - rw_hi.astype(F32)).astype(BF16)
    router_b_col = jnp.broadcast_to(router_b[:, :, None], (depth, n_exp, LANE))
    exp_b_up4 = exp_b_up.reshape(depth, n_exp, 1, -1)
    exp_b_down4 = exp_b_down.reshape(depth, n_exp, 1, -1)
    emb = hy_ffn_w1.shape[1]
    ffn = hy_ffn_w1.shape[2]
    f_w1 = jnp.pad(hy_ffn_w1, ((0, 0), (0, LANE - emb), (0, LANE - ffn)))
    f_b1 = r3(lane_pad(hy_ffn_b1))
    f_w2 = jnp.pad(hy_ffn_w2, ((0, 0), (0, LANE - ffn), (0, LANE - ffn)))
    f_b2 = r3(lane_pad(hy_ffn_b2))
    f_w3 = jnp.pad(hy_ffn_w3, ((0, 0), (0, LANE - ffn), (0, 0)))

    freqs = jnp.linspace(1e-4, HYENA_BANDS - 1, HYENA_BANDS, dtype=F32)
    freqs_row = jnp.concatenate([jnp.zeros((1,), F32), freqs, freqs,
                                 jnp.zeros((LANE - 1 - 2 * HYENA_BANDS,), F32)])[None, :]
    deltas_row = jnp.linspace(HYENA_MIN_DECAY, HYENA_MAX_DECAY, hw, dtype=F32)[None, :]
    gi = jnp.arange(FOURIER_GROUP_W)
    ang = (gi[:, None] * gi[None, :] % FOURIER_GROUP_W).astype(F32) * (2.0 * math.pi / FOURIER_GROUP_W)
    n_fg = fw // FOURIER_GROUP_W
    eye = jnp.eye(n_fg, dtype=F32)
    wc = jnp.concatenate([jnp.kron(eye, jnp.cos(ang)), jnp.kron(eye, jnp.sin(ang))], axis=1).astype(BF16)
    band, invc = _pool_tables(n_lat, n_ctx)
    qi = jnp.arange(SSM_CHUNK)
    tri_f = (qi[:, None] >= qi[None, :]).astype(BF16)
    tri_b = (qi[:, None] <= qi[None, :]).astype(BF16)
    ti = jnp.arange(TOKEN_TILE)
    utri = (ti[:, None] < ti[None, :]).astype(BF16)

    seqs = {}
    for name, n, row_off in (("ctx", n_ctx, n_lat), ("lat", n_lat, 0)):
        fwd, inv = _hyena_dft_tables(n)
        g_all, nrm = _filter_mlp(n, freqs_row, deltas_row, f_w1, f_b1, f_w2, f_b2, f_w3)
        kr, ki = _filter_spectrum(fwd, g_all, nrm, n, hw)
        cn, sn = _trig_tables(n, n)
        seqs[name] = dict(n=n, off=row_off, cn=cn.astype(BF16), sn=sn.astype(BF16), fwd=fwd, inv=inv,
                          kr=kr, ki=ki)

    cc = jnp.concatenate([c, c_ctx[None, :]], axis=0)
    rows = -(-cc.shape[0] // 8) * 8
    mod = _modulation(jnp.pad(cc, ((0, rows - cc.shape[0]), (0, 0))), w_mod, b_mod)
    mod = mod.reshape(depth, rows, 6, 1, d)

    seq = n_lat + n_ctx

    def layer_step(tok, layer, x_all, moe):
        gb, n_tok = tok.batch, tok.n_tok
        res = _inproj(tok, layer, x_all, mod, norm_mix3, w_in_p, wc, band, invc, pool_w_bd, pool_scale3,
                      widths, moe=moe)
        if moe is not None:
            x_all = res[0]
            res = res[1:]
        xcs, hy, pool, z, xbc, dtr = res
        xcs3 = xcs.reshape(gb, seq, 2 * fw)
        hy3 = hy.reshape(gb, seq, 3 * hw)
        z3 = z.reshape(gb, seq, sw)
        xbc3 = xbc.reshape(gb, seq, cw)
        dt3 = dtr.reshape(gb, seq, LANE)
        outs = {}
        zero_state = jnp.zeros((gb, SSM_GROUPS, SSM_STATE, HEADS_PER_GROUP * SSM_HEAD_DIM), F32)
        states = (zero_state, zero_state)
        for name in ("ctx", "lat"):
            s = seqs[name]
            n, off = s["n"], s["off"]
            y_f = _fourier_seq(xcs3, s["cn"], s["sn"], n, off, fw)
            vv, x1 = _hy_pre(hy3, hy_conv_w, hy_conv_b3, layer, n, off, hw)
            y2 = _hy_fwd(s["fwd"], vv, s["kr"], s["ki"], layer, n, hw)
            y_h = _hy_inv(s["inv"], y2, x1, vv, hy_bias3, layer, n, hw)
            yf, st_f = _ssd_pass(xbc3, dt3, states[0], tri_f, ssm_conv_w, ssm_conv_b3, dt_bias3, a_row3,
                                 layer, n, off, reverse=False)
            y_s, st_b = _ssd_pass(xbc3, dt3, states[1], tri_b, ssm_conv_w, ssm_conv_b3, dt_bias3, a_row3,
                                  layer, n, off, reverse=True, fin=(z3, yf, d_row3, ssm_norm3))
            states = (st_f, st_b)
            outs[name] = (y_f, y_h, y_s)
        x_all = _merge(tok, layer, x_all, mod, norm_mix3, outs["lat"][0], outs["ctx"][0], outs["lat"][1],
                       outs["ctx"][1], pool, outs["lat"][2], outs["ctx"][2], w_gate_b, w_branch_b, w_out_b,
                       splits)
        h2, top_e, rank, top_w, counts = _router(tok, layer, x_all, mod, norm_ffn3, rw_hi, rw_lo, router_b_col,
                                                 utri)
        dest, row_tok, blk_e, n_active = _dispatch_plan(top_e[:TOP_K], rank[:TOP_K],
                                                        counts[:, 0].astype(jnp.int32), n_tok)
        x_sorted = h2[row_tok]
        y_rows = _moe(layer, x_sorted, blk_e, n_active, exp_w_up, exp_b_up4, exp_w_down, exp_b_down4)
        y_gath = y_rows[dest.reshape(-1)].reshape(TOP_K, n_tok, d)
        return x_all, (y_gath, top_w)

    n_groups = SAMPLE_GROUPS if batch % SAMPLE_GROUPS == 0 else 1
    gb = batch // n_groups
    groups = []
    for g in range(n_groups):
        tok = _Tok(gb, n_lat, n_ctx, mod_off=g * gb, mod_ctx=batch)
        x_all = jnp.concatenate([x[g * gb:(g + 1) * gb], ctx[g * gb:(g + 1) * gb]], axis=1)
        groups.append([tok, x_all.reshape(tok.n_tok, d), None])
    for layer in range(depth):
        for grp in groups:
            grp[1], grp[2] = layer_step(grp[0], layer, grp[1], grp[2])
    outs = [_final(tok, depth - 1, x_all, mod, moe[0], moe[1], norm_final[None, :])
            for tok, x_all, moe in groups]
    return outs[0] if n_groups == 1 else jnp.concatenate(outs, axis=0)
```

```python
import functools
import math

import jax
import jax.numpy as jnp
from jax import lax
from jax.experimental import pallas as pl
from jax.experimental.pallas import tpu as pltpu

F32 = jnp.float32
BF16 = jnp.bfloat16

GRID_W = 64
EPS = 1e-6
FOURIER_GROUP_W = 64
HYENA_BANDS = 16
HYENA_MIN_DECAY = -math.log(1e-2) / 1.5
HYENA_MAX_DECAY = -math.log(1e-2) / 0.3
POOL_WINDOWS = (2, 4, 8, 16)
POOL_GROUP_W = 64
SSM_GROUPS = 2
HEADS_PER_GROUP = 4
SSM_HEADS = SSM_GROUPS * HEADS_PER_GROUP
SSM_HEAD_DIM = 64
SSM_STATE = 128
SSM_CHUNK = 128
TOP_K = 4
SWIGLU_LIMIT = 7.0
SWIGLU_ALPHA = 1.702

LANE = 128
BF16_SUBLANE = 16
TOKEN_TILE = 256
INPROJ_TILES = 2
MERGE_TILES = 2
SEQ_TILE = 512
SEQ_TILE_1ACC = 1024
SEQ_KTILE = 1024
MOE_ROWS = 512
SSD_SAMPLES = 8
SAMPLE_GROUPS = 1
VMEM_LIMIT = 56 * 1024 * 1024
VMEM_LIMIT_SMALL = 32 * 1024 * 1024


def _cparams(sem, vmem=VMEM_LIMIT):
    return pltpu.CompilerParams(dimension_semantics=sem, vmem_limit_bytes=vmem)


def _sigmoid(x):
    return 1.0 / (1.0 + jnp.exp(-x))


def _dot(a, b):
    return jnp.dot(a, b, preferred_element_type=F32)


def _dot_hi(a, b):
    return jnp.dot(a, b, preferred_element_type=F32, precision=lax.Precision.HIGHEST)


def _split3(x):
    x1 = x.astype(BF16)
    r1 = x - x1.astype(F32)
    x2 = r1.astype(BF16)
    x3 = (r1 - x2.astype(F32)).astype(BF16)
    return x1, x2, x3


def _dot_exact01(m01, x):
    x1, x2, x3 = _split3(x)
    return _dot(m01, x1) + _dot(m01, x2) + _dot(m01, x3)


def _mod_kernel(c_ref, w_ref, b_ref, o_ref):
    c = c_ref[...]
    s = c * _sigmoid(c)
    o_ref[...] = _dot_hi(s, w_ref[...]) + b_ref[...]


def _modulation(cc, w_mod, b_mod):
    depth, d, six_d = w_mod.shape
    rows = cc.shape[0]
    nj = six_d // d
    return pl.pallas_call(
        _mod_kernel,
        grid=(depth, nj),
        in_specs=[
            pl.BlockSpec((rows, d), lambda l, j: (0, 0)),
            pl.BlockSpec((None, d, d), lambda l, j: (l, 0, j)),
            pl.BlockSpec((None, 1, d), lambda l, j: (l, 0, j)),
        ],
        out_specs=pl.BlockSpec((None, rows, d), lambda l, j: (l, 0, j)),
        out_shape=jax.ShapeDtypeStruct((depth, rows, six_d), F32),
        compiler_params=_cparams(("arbitrary", "arbitrary"), VMEM_LIMIT_SMALL),
        name="modulation",
    )(cc, w_mod, b_mod.reshape(depth, 1, six_d))


class _Tok:
    def __init__(self, batch, n_lat, n_ctx, mod_off, mod_ctx):
        assert n_ctx == TOKEN_TILE and n_lat % TOKEN_TILE == 0
        self.batch, self.n_lat, self.n_ctx = batch, n_lat, n_ctx
        self.mod_off, self.mod_ctx = mod_off, mod_ctx
        self.seq = n_lat + n_ctx
        self.tiles_per_b = self.seq // TOKEN_TILE
        self.lat_tiles = n_lat // TOKEN_TILE
        self.n_tiles = batch * self.tiles_per_b
        self.n_tok = batch * self.seq

    def b(self, t):
        return t // self.tiles_per_b

    def w(self, t):
        return t % self.tiles_per_b

    def is_ctx(self, t):
        return (t % self.tiles_per_b) >= self.lat_tiles

    def mod_row(self, t):
        return jnp.where(self.is_ctx(t), self.mod_ctx, self.mod_off + self.b(t))

    def mod_spec(self, layer, part, d, tile=lambda i: i):
        return pl.BlockSpec((None, None, None, 1, d), lambda i: (layer, self.mod_row(tile(i)), part, 0, 0))

    def lat_spec(self, width, tile=lambda i: i):
        return pl.BlockSpec((None, TOKEN_TILE, width),
                            lambda i: (self.b(tile(i)), jnp.minimum(self.w(tile(i)), self.lat_tiles - 1), 0))

    def ctx_spec(self, width, tile=lambda i: i):
        return pl.BlockSpec((None, TOKEN_TILE, width), lambda i: (self.b(tile(i)), 0, 0))


def _norm_mod(x, gain, scale, shift):
    ms = jnp.mean(x * x, axis=-1, keepdims=True)
    return (x * lax.rsqrt(ms + EPS)) * gain * (1.0 + scale) + shift


def _moe_residual(x, g2, yg_ref, w_ref, rows=slice(None)):
    w = w_ref[rows, :]
    y = None
    for k in range(TOP_K):
        term = w[:, k:k + 1] * yg_ref[k, rows, :].astype(F32)
        y = term if y is None else y + term
    return x + g2 * y


def _inproj_kernel(*refs, widths, combine, n_sub):
    it = iter(refs)
    take = lambda k: [next(it) for _ in range(k)]
    x_ref, = take(1)
    if combine:
        g2_refs = take(n_sub)
        yg_ref, w_ref = take(2)
    sc_refs, sh_refs = take(n_sub), take(n_sub)
    g_ref, win_ref, wc_ref = take(3)
    band_refs, invc_refs = take(n_sub), take(n_sub)
    pw_ref, ps_ref = take(2)
    if combine:
        xn_ref, = take(1)
    xcs_ref, hy_ref, pool_ref, z_ref, xbc_ref, dt_ref = take(6)
    fw, hw, pw, zw, cw = widths
    tm = TOKEN_TILE
    for s in range(n_sub):
        rows = slice(s * tm, (s + 1) * tm)
        x = x_ref[rows, :]
        if combine:
            x = _moe_residual(x, g2_refs[s][...], yg_ref, w_ref, rows)
            xn_ref[rows, :] = x
        hb = _norm_mod(x, g_ref[...], sc_refs[s][...], sh_refs[s][...]).astype(BF16)
        o = 0
        f_in = _dot(hb, win_ref[:, o:o + fw])
        xcs_ref[rows, :] = _dot(f_in.astype(BF16), wc_ref[...]).astype(BF16)
        o += fw
        hy_ref[rows, :] = _dot(hb, win_ref[:, o:o + hw]).astype(BF16)
        o += hw
        u = _dot(hb, win_ref[:, o:o + pw])
        o += pw
        z_ref[rows, :] = _dot(hb, win_ref[:, o:o + zw]).astype(BF16)
        o += zw
        xbc_ref[rows, :] = _dot(hb, win_ref[:, o:o + cw]).astype(BF16)
        o += cw
        dt_ref[rows, :] = _dot(hb, win_ref[:, o:o + LANE])
        gw = POOL_GROUP_W
        parts = []
        for gi in range(len(POOL_WINDOWS)):
            ug = u[:, gi * gw:(gi + 1) * gw]
            parts.append(_dot_exact01(band_refs[s][gi], ug))
        pooled = jnp.concatenate(parts, axis=1) * invc_refs[s][...] - u
        pool_ref[rows, :] = (_dot(pooled.astype(BF16), pw_ref[...]) * ps_ref[...]).astype(BF16)


def _pool_tables(n_lat, n_ctx):
    tm = TOKEN_TILE
    p = jnp.arange(tm)
    bands, invs = [], []
    for row_len in (GRID_W, n_ctx):
        assert tm % row_len == 0
        pr, rr = p % row_len, p // row_len
        bg, ig = [], []
        for win in POOL_WINDOWS:
            lo = jnp.clip(pr - win // 2, 0, row_len)
            hi = jnp.clip(pr + win // 2, 0, row_len)
            q = pr[None, :]
            m = (rr[:, None] == rr[None, :]) & (q >= lo[:, None]) & (q < hi[:, None])
            bg.append(m.astype(BF16))
            ig.append(jnp.broadcast_to((1.0 / (hi - lo).astype(F32))[:, None], (tm, POOL_GROUP_W)))
        bands.append(jnp.stack(bg))
        invs.append(jnp.concatenate(ig, axis=1))
    return jnp.stack(bands), jnp.stack(invs)


def _block_diag(w):
    g, a, b = w.shape
    out = jnp.zeros((g * a, g * b), w.dtype)
    for i in range(g):
        out = out.at[i * a:(i + 1) * a, i * b:(i + 1) * b].set(w[i])
    return out


def _inproj(tok, layer, x_all, mod, norm_mix, w_in_p, wc, band, invc, pool_w_bd, pool_scale, widths,
            moe=None):
    d = x_all.shape[-1]
    fw, hw, pw, zw, cw = widths
    tm = TOKEN_TILE
    n_tok = tok.n_tok
    n_sub = INPROJ_TILES if tok.n_tiles % INPROJ_TILES == 0 else 1
    rows = n_sub * tm
    subs = range(n_sub)
    tile = lambda i, s: i * n_sub + s
    kind = lambda t: jnp.where(tok.is_ctx(t), 1, 0)
    const2 = lambda i: (0, 0)
    mod_specs = lambda lyr, part: [
        pl.BlockSpec((None, None, None, 1, d), lambda i, s=s: (lyr, tok.mod_row(tile(i, s)), part, 0, 0))
        for s in subs]
    pre_specs, pre_args, pre_outs = [], [], []
    if moe is not None:
        pre_specs = mod_specs(layer - 1, 5) + [
            pl.BlockSpec((TOP_K, rows, d), lambda i: (0, i, 0)),
            pl.BlockSpec((rows, LANE), lambda i: (i, 0))]
        pre_args = [mod] * n_sub + [moe[0], moe[1]]
        pre_outs = [jax.ShapeDtypeStruct((n_tok, d), F32)]
    outs = pre_outs + [
        jax.ShapeDtypeStruct((n_tok, 2 * fw), BF16),
        jax.ShapeDtypeStruct((n_tok, hw), BF16),
        jax.ShapeDtypeStruct((n_tok, pw), BF16),
        jax.ShapeDtypeStruct((n_tok, zw), BF16),
        jax.ShapeDtypeStruct((n_tok, cw), BF16),
        jax.ShapeDtypeStruct((n_tok, LANE), F32),
    ]
    return pl.pallas_call(
        functools.partial(_inproj_kernel, widths=widths, combine=moe is not None, n_sub=n_sub),
        grid=(tok.n_tiles // n_sub,),
        in_specs=[pl.BlockSpec((rows, d), lambda i: (i, 0))] + pre_specs
        + mod_specs(layer, 1) + mod_specs(layer, 0) + [
            pl.BlockSpec((None, 1, d), lambda i: (layer, 0, 0)),
            pl.BlockSpec((None,) + w_in_p.shape[1:], lambda i: (layer, 0, 0)),
            pl.BlockSpec(wc.shape, const2),
        ]
        + [pl.BlockSpec((None,) + band.shape[1:], lambda i, s=s: (kind(tile(i, s)), 0, 0, 0)) for s in subs]
        + [pl.BlockSpec((None,) + invc.shape[1:], lambda i, s=s: (kind(tile(i, s)), 0, 0)) for s in subs]
        + [
            pl.BlockSpec((None,) + pool_w_bd.shape[1:], lambda i: (layer, 0, 0)),
            pl.BlockSpec((None, 1, pw), lambda i: (layer, 0, 0)),
        ],
        out_specs=[pl.BlockSpec((rows, o.shape[1]), lambda i: (i, 0)) for o in outs],
        out_shape=outs,
        compiler_params=_cparams(("arbitrary",)),
        name="inproj",
    )(x_all, *pre_args, *([mod] * (2 * n_sub)), norm_mix, w_in_p, wc, *([band] * n_sub), *([invc] * n_sub),
      pool_w_bd, pool_scale)


def _trig_tables(n, period):
    k = jnp.arange(n, dtype=jnp.int32)

    def cs(j):
        m = (j[:, None] * k[None, :]) % period
        ang = m.astype(F32) * (2.0 * math.pi / period)
        return jnp.cos(ang), jnp.sin(ang)

    split = 64
    if n % split or n <= split:
        return cs(k)
    c1, s1 = cs(jnp.arange(n // split, dtype=jnp.int32) * split)
    c2, s2 = cs(jnp.arange(split, dtype=jnp.int32))
    c = c1[:, None, :] * c2[None, :, :] - s1[:, None, :] * s2[None, :, :]
    s = s1[:, None, :] * c2[None, :, :] + c1[:, None, :] * s2[None, :, :]
    return c.reshape(n, n), s.reshape(n, n)


def _hyena_dft_tables(n):
    c, s = _trig_tables(n, 2 * n)
    nyq = jnp.where(jnp.arange(n) % 2 == 0, 1.0, -1.0).astype(F32)
    fwd = jnp.concatenate([c, (-s).at[0].set(nyq)], axis=0).astype(BF16)
    inv = jnp.concatenate([c, (-s).at[:, 0].set(nyq)], axis=1).astype(BF16)
    return fwd, inv


def _fourier_kernel(c_ref, s_ref, xc_ref, xs_ref, o_ref, acc_ref, *, scale):
    k = pl.program_id(1)

    @pl.when(k == 0)
    def _():
        acc_ref[...] = jnp.zeros_like(acc_ref)

    nb = xc_ref.shape[0]
    rc = jnp.concatenate([xc_ref[b] for b in range(nb)], axis=1)
    rs = jnp.concatenate([xs_ref[b] for b in range(nb)], axis=1)
    acc_ref[...] += _dot(c_ref[...], rc) - _dot(s_ref[...], rs)

    @pl.when(k == pl.num_programs(1) - 1)
    def _():
        w = o_ref.shape[2]
        for b in range(nb):
            o_ref[b] = (acc_ref[:, b * w:(b + 1) * w] * scale).astype(o_ref.dtype)


def _fourier_seq(xcs3, cn, sn, n, row_off, fw):
    batch = xcs3.shape[0]
    tl = min(SEQ_TILE_1ACC, n)
    tk = min(SEQ_KTILE, n)
    nt = n // tl
    off = row_off // tk
    scale = 1.0 / math.sqrt(n * FOURIER_GROUP_W)
    return pl.pallas_call(
        functools.partial(_fourier_kernel, scale=scale),
        grid=(nt, n // tk),
        in_specs=[
            pl.BlockSpec((tl, tk), lambda i, k: (i, k)),
            pl.BlockSpec((tl, tk), lambda i, k: (i, k)),
            pl.BlockSpec((batch, tk, fw), lambda i, k: (0, off + k, 0)),
            pl.BlockSpec((batch, tk, fw), lambda i, k: (0, off + k, 1)),
        ],
        out_specs=pl.BlockSpec((batch, tl, fw), lambda i, k: (0, i, 0)),
        out_shape=jax.ShapeDtypeStruct((batch, n, fw), BF16),
        scratch_shapes=[pltpu.VMEM((tl, batch * fw), F32)],
        compiler_params=_cparams(("arbitrary", "arbitrary")),
        name="fourier_seq",
    )(cn, sn, xcs3, xcs3)


def _shift_rows(x, prev_row, next_row):
    n = x.shape[0]
    rows = lax.broadcasted_iota(jnp.int32, x.shape, 0)
    xm = jnp.where(rows == 0, prev_row, pltpu.roll(x, 1, 0))
    xp = jnp.where(rows == n - 1, next_row, pltpu.roll(x, n - 1, 0))
    return xm, xp


def _conv3(x, prev_row, next_row, w_ref, b_ref):
    xm, xp = _shift_rows(x, prev_row, next_row)
    return w_ref[0:1, :] * xm + w_ref[1:2, :] * x + w_ref[2:3, :] * xp + b_ref[...]


def _hy_pre_kernel(x_ref, xp_ref, xn_ref, w_ref, b_ref, vv_ref, x1_ref, *, hw):
    i = pl.program_id(1)
    last = pl.num_programs(1) - 1
    x = x_ref[...].astype(F32)
    hs = xp_ref.shape[0]
    prev_row = jnp.where(i == 0, 0.0, xp_ref[hs - 1:hs, :].astype(F32))
    next_row = jnp.where(i == last, 0.0, xn_ref[0:1, :].astype(F32))
    u = _conv3(x, prev_row, next_row, w_ref, b_ref)
    x1_ref[...] = u[:, :hw].astype(BF16)
    vv_ref[...] = (u[:, hw:2 * hw] * u[:, 2 * hw:]).astype(BF16)


def _halo_specs(width, tile, row_off, seq_rows):
    hs = BF16_SUBLANE
    per = tile // hs
    base = row_off // hs
    top = seq_rows // hs - 1
    prev = pl.BlockSpec((None, hs, width), lambda b, i: (b, jnp.maximum(base + i * per - 1, 0), 0))
    nxt = pl.BlockSpec((None, hs, width), lambda b, i: (b, jnp.minimum(base + (i + 1) * per, top), 0))
    return prev, nxt


def _hy_pre(hy3, conv_w, conv_b, layer, n, row_off, hw):
    batch, seq, w3 = hy3.shape
    tl = min(SEQ_TILE, n)
    off = row_off // tl
    prev, nxt = _halo_specs(w3, tl, row_off, seq)
    out = jax.ShapeDtypeStruct((batch, n, hw), BF16)
    return pl.pallas_call(
        functools.partial(_hy_pre_kernel, hw=hw),
        grid=(batch, n // tl),
        in_specs=[
            pl.BlockSpec((None, tl, w3), lambda b, i: (b, off + i, 0)),
            prev, nxt,
            pl.BlockSpec((None,) + conv_w.shape[1:], lambda b, i: (layer, 0, 0)),
            pl.BlockSpec((None, 1, w3), lambda b, i: (layer, 0, 0)),
        ],
        out_specs=[pl.BlockSpec((None, tl, hw), lambda b, i: (b, i, 0))] * 2,
        out_shape=[out, out],
        compiler_params=_cparams(("arbitrary", "arbitrary"), VMEM_LIMIT_SMALL),
        name="hyena_pre",
    )(hy3, hy3, hy3, conv_w, conv_b)


def _filter_mlp_kernel(fr_ref, dl_ref, w1_ref, b1_ref, w2_ref, b2_ref, w3_ref, g_ref, nrm_ref, *, n, hw):
    j = pl.program_id(1)
    rt = g_ref.shape[0]
    lane = lax.broadcasted_iota(jnp.int32, (rt, LANE), 1)
    pos = (lax.broadcasted_iota(jnp.int32, (rt, LANE), 0) + j * rt).astype(F32)
    t = pos / (n - 1)
    ang = 2.0 * math.pi * pos / n
    arg = ang * fr_ref[...]
    feats = jnp.where(lane == 0, t,
                      jnp.where(lane <= HYENA_BANDS, jnp.cos(arg),
                                jnp.where(lane <= 2 * HYENA_BANDS, -jnp.sin(arg), 0.0)))
    h1 = jnp.sin(_dot_hi(feats, w1_ref[...]) + b1_ref[...])
    h2 = jnp.sin(_dot_hi(h1, w2_ref[...]) + b2_ref[...])
    k = _dot_hi(h2, w3_ref[...])
    decay = jnp.exp(-t[:, 0:1] * dl_ref[...])
    kf = k[:, :hw] * decay
    kb = jnp.where(pos[:, 0:1] == 0.0, 0.0, k[:, hw:] * decay)
    g_ref[:, :hw] = (kf + kb).astype(BF16)
    g_ref[:, hw:] = (kf - kb).astype(BF16)
    part = jnp.sum(jnp.abs(kf) + jnp.abs(kb), axis=0, keepdims=True)

    @pl.when(j == 0)
    def _():
        nrm_ref[...] = part

    @pl.when(j > 0)
    def _():
        nrm_ref[...] += part


def _filter_mlp(n, freqs_row, deltas_row, w1, b1, w2, b2, w3):
    depth = w1.shape[0]
    hw = deltas_row.shape[1]
    rt = min(SEQ_TILE, n)
    per_layer = lambda l, j: (l, 0, 0)
    return pl.pallas_call(
        functools.partial(_filter_mlp_kernel, n=n, hw=hw),
        grid=(depth, n // rt),
        in_specs=[
            pl.BlockSpec((1, LANE), lambda l, j: (0, 0)),
            pl.BlockSpec((1, hw), lambda l, j: (0, 0)),
            pl.BlockSpec((None,) + w1.shape[1:], per_layer),
            pl.BlockSpec((None,) + b1.shape[1:], per_layer),
            pl.BlockSpec((None,) + w2.shape[1:], per_layer),
            pl.BlockSpec((None,) + b2.shape[1:], per_layer),
            pl.BlockSpec((None,) + w3.shape[1:], per_layer),
        ],
        out_specs=[
            pl.BlockSpec((rt, 2 * hw), lambda l, j: (j, l)),
            pl.BlockSpec((None, 1, hw), lambda l, j: (l, 0, 0)),
        ],
        out_shape=[
            jax.ShapeDtypeStruct((n, depth * 2 * hw), BF16),
            jax.ShapeDtypeStruct((depth, 1, hw), F32),
        ],
        compiler_params=_cparams(("arbitrary", "arbitrary"), VMEM_LIMIT_SMALL),
        name="hyena_filter_mlp",
    )(freqs_row, deltas_row, w1, b1, w2, b2, w3)


def _rdft_accumulate(fc_ref, fs_ref, rhs, accc_ref, accs_ref):
    k = pl.program_id(1)

    @pl.when(k == 0)
    def _():
        accc_ref[...] = jnp.zeros_like(accc_ref)
        accs_ref[...] = jnp.zeros_like(accs_ref)

    accc_ref[...] += _dot(fc_ref[...], rhs)
    accs_ref[...] += _dot(fs_ref[...], rhs)


def _filter_spec_kernel(fc_ref, fs_ref, g_ref, nrm_ref, kr_ref, ki_ref, accc_ref, accs_ref, *, hw):
    i = pl.program_id(0)
    _rdft_accumulate(fc_ref, fs_ref, g_ref[...], accc_ref, accs_ref)

    @pl.when(pl.program_id(1) == pl.num_programs(1) - 1)
    def _():
        tm = accc_ref.shape[0]
        row0 = (lax.broadcasted_iota(jnp.int32, (tm, hw), 0) + i * tm) == 0
        for l in range(kr_ref.shape[0]):
            inv = 1.0 / nrm_ref[l]
            o = l * 2 * hw
            kr_ref[l] = accc_ref[:, o:o + hw] * inv
            ki_ref[l] = jnp.where(row0, accs_ref[:, o:o + hw], accs_ref[:, o + hw:o + 2 * hw]) * inv


def _filter_spectrum(fwd, g_all, nrm, n, hw):
    depth = nrm.shape[0]
    tl = min(SEQ_TILE, n)
    nt = n // tl
    width = g_all.shape[1]
    out = jax.ShapeDtypeStruct((depth, n, hw), F32)
    return pl.pallas_call(
        functools.partial(_filter_spec_kernel, hw=hw),
        grid=(nt, nt),
        in_specs=[
            pl.BlockSpec((tl, tl), lambda i, k: (i, k)),
            pl.BlockSpec((tl, tl), lambda i, k: (nt + i, k)),
            pl.BlockSpec((tl, width), lambda i, k: (k, 0)),
            pl.BlockSpec(nrm.shape, lambda i, k: (0, 0, 0)),
        ],
        out_specs=[pl.BlockSpec((depth, tl, hw), lambda i, k: (0, i, 0))] * 2,
        out_shape=[out, out],
        scratch_shapes=[pltpu.VMEM((tl, width), F32)] * 2,
        compiler_params=_cparams(("arbitrary", "arbitrary")),
        name="hyena_filter_spectrum",
    )(fwd, fwd, g_all, nrm)


def _hy_fwd_kernel(fc_ref, fs_ref, vv_ref, kr_ref, ki_ref, y_ref, accc_ref, accs_ref):
    i = pl.program_id(0)
    nb = vv_ref.shape[0]
    rhs = jnp.concatenate([vv_ref[b] for b in range(nb)], axis=1)
    _rdft_accumulate(fc_ref, fs_ref, rhs, accc_ref, accs_ref)

    @pl.when(pl.program_id(1) == pl.num_programs(1) - 1)
    def _():
        tm, hw = kr_ref.shape
        row0 = (lax.broadcasted_iota(jnp.int32, (tm, hw), 0) + i * tm) == 0
        kr, ki = kr_ref[...], ki_ref[...]
        for b in range(nb):
            vr = accc_ref[:, b * hw:(b + 1) * hw]
            vi = accs_ref[:, b * hw:(b + 1) * hw]
            yr = jnp.where(row0, 0.5 * vr * kr, vr * kr - vi * ki)
            yi = jnp.where(row0, 0.5 * vi * ki, vr * ki + vi * kr)
            y_ref[0, :, b * hw:(b + 1) * hw] = yr.astype(BF16)
            y_ref[1, :, b * hw:(b + 1) * hw] = yi.astype(BF16)


def _hy_fwd(fwd, vv, kr, ki, layer, n, hw):
    batch = vv.shape[0]
    tl = min(SEQ_TILE, n)
    tk = min(SEQ_KTILE, n)
    nt = n // tl
    return pl.pallas_call(
        _hy_fwd_kernel,
        grid=(nt, n // tk),
        in_specs=[
            pl.BlockSpec((tl, tk), lambda i, k: (i, k)),
            pl.BlockSpec((tl, tk), lambda i, k: (nt + i, k)),
            pl.BlockSpec((batch, tk, hw), lambda i, k: (0, k, 0)),
            pl.BlockSpec((None, tl, hw), lambda i, k: (layer, i, 0)),
            pl.BlockSpec((None, tl, hw), lambda i, k: (layer, i, 0)),
        ],
        out_specs=pl.BlockSpec((2, tl, batch * hw), lambda i, k: (0, i, 0)),
        out_shape=jax.ShapeDtypeStruct((2, n, batch * hw), BF16),
        scratch_shapes=[pltpu.VMEM((tl, batch * hw), F32)] * 2,
        compiler_params=_cparams(("arbitrary", "arbitrary")),
        name="hyena_fwd_dft",
    )(fwd, fwd, vv, kr, ki)


def _hy_inv_kernel(fi_ref, y_ref, x1_ref, vv_ref, bias_ref, o_ref, acc_ref, *, scale):
    k = pl.program_id(1)

    @pl.when(k == 0)
    def _():
        acc_ref[...] = jnp.zeros_like(acc_ref)

    acc_ref[...] += _dot(fi_ref[...], y_ref[...])

    @pl.when(k == pl.num_programs(1) - 1)
    def _():
        nb, _, hw = x1_ref.shape
        for b in range(nb):
            conv = acc_ref[:, b * hw:(b + 1) * hw] * scale
            vv = vv_ref[b].astype(F32)
            o_ref[b] = (x1_ref[b].astype(F32) * (conv + vv * bias_ref[...])).astype(BF16)


def _hy_inv(inv, y2, x1, vv, hy_bias, layer, n, hw):
    batch = vv.shape[0]
    tl = min(SEQ_TILE_1ACC, n)
    tk = min(SEQ_KTILE, n)
    nt = n // tl
    return pl.pallas_call(
        functools.partial(_hy_inv_kernel, scale=1.0 / n),
        grid=(nt, 2 * n // tk),
        in_specs=[
            pl.BlockSpec((tl, tk), lambda i, k: (i, k)),
            pl.BlockSpec((tk, batch * hw), lambda i, k: (k, 0)),
            pl.BlockSpec((batch, tl, hw), lambda i, k: (0, i, 0)),
            pl.BlockSpec((batch, tl, hw), lambda i, k: (0, i, 0)),
            pl.BlockSpec((None, 1, hw), lambda i, k: (layer, 0, 0)),
        ],
        out_specs=pl.BlockSpec((batch, tl, hw), lambda i, k: (0, i, 0)),
        out_shape=jax.ShapeDtypeStruct((batch, n, hw), BF16),
        scratch_shapes=[pltpu.VMEM((tl, batch * hw), F32)],
        compiler_params=_cparams(("arbitrary", "arbitrary")),
        name="hyena_inv_dft",
    )(inv, y2.reshape(2 * n, batch * hw), x1, vv, hy_bias)


def _softplus(x):
    return jnp.maximum(x, 0.0) + jnp.log1p(jnp.exp(-jnp.abs(x)))


def _ssd_kernel(*refs, reverse, final, nc, lane0):
    if final:
        (xbc_ref, xp_ref, xn_ref, dt_ref, h0_ref, tri_ref, exp_ref, cw_ref, cb_ref, dtb_ref, a_ref,
         z_ref, yf_ref, d_ref, nrm_ref, out_ref, hout_ref, h_scr) = refs
    else:
        (xbc_ref, xp_ref, xn_ref, dt_ref, h0_ref, tri_ref, exp_ref, cw_ref, cb_ref, dtb_ref, a_ref,
         out_ref, hout_ref, h_scr) = refs
    step = pl.program_id(1)
    cc = (nc - 1 - step) if reverse else step

    @pl.when(step == 0)
    def _():
        h_scr[...] = h0_ref[...]

    nb, q = xbc_ref.shape[0], xbc_ref.shape[1]
    hs = xp_ref.shape[1]
    width = SSM_HEADS * SSM_HEAD_DIM
    gs = SSM_STATE
    pd = SSM_HEAD_DIM
    ri = lax.broadcasted_iota(jnp.int32, (q, q), 0)
    ci = lax.broadcasted_iota(jnp.int32, (q, q), 1)
    mask = (ri <= ci) if reverse else (ri >= ci)

    for s in range(nb):
        x = xbc_ref[s].astype(F32)
        prev_row = jnp.where(cc == 0, 0.0, xp_ref[s, hs - 1:hs, :].astype(F32))
        next_row = jnp.where(cc == nc - 1, 0.0, xn_ref[s, 0:1, :].astype(F32))
        u = _conv3(x, prev_row, next_row, cw_ref, cb_ref)
        u = u * _sigmoid(u)
        xs = u[:, :width]
        bm = u[:, width:width + SSM_GROUPS * gs]
        cm = u[:, width + SSM_GROUPS * gs:]

        dt = _softplus(dt_ref[s] + dtb_ref[...])
        acs = _dot_exact01(tri_ref[...], dt * a_ref[...])
        acs_t = acs.T
        tot = acs[0:1, :] if reverse else acs[q - 1:q, :]
        to_end = jnp.exp(tot - acs)
        frm = jnp.exp(acs)
        cdec = jnp.exp(tot)

        stacked = jnp.concatenate([dt, to_end, frm, jnp.broadcast_to(cdec, (8, LANE))], axis=0)
        s1, s2, s3 = _split3(stacked)
        ex = exp_ref[...]
        rep = _dot(s1, ex) + _dot(s2, ex) + _dot(s3, ex)
        xdt = xs * rep[0:q]
        wts = (xdt * rep[q:2 * q]).astype(BF16)
        frm_rep = rep[2 * q:3 * q]
        cdec_rep = rep[3 * q:3 * q + 1]
        xdt_b = xdt.astype(BF16)
        half = lax.broadcasted_iota(jnp.int32, (q, 2 * pd), 1) < pd

        ys = []
        gw = HEADS_PER_GROUP * pd
        for g in range(SSM_GROUPS):
            bg = bm[:, g * gs:(g + 1) * gs]
            cg = cm[:, g * gs:(g + 1) * gs].astype(BF16)
            scores = lax.dot_general(cg, bg.astype(BF16), (((1,), (1,)), ((), ())),
                                     preferred_element_type=F32)
            hg = h_scr[s, g]
            yoff = _dot(cg, hg.astype(BF16)) * frm_rep[:, g * gw:(g + 1) * gw]
            for pr in range(HEADS_PER_GROUP // 2):
                pair = []
                c0 = g * gw + pr * 2 * pd
                for j in range(2):
                    li = lane0 + g * HEADS_PER_GROUP + 2 * pr + j
                    lm = jnp.where(mask, jnp.exp(acs[:, li:li + 1] - acs_t[li:li + 1, :]), 0.0)
                    m = (scores * lm).astype(BF16)
                    pair.append(_dot(m, xdt_b[:, c0:c0 + 2 * pd]))
                ys.append(jnp.where(half, pair[0], pair[1]) + yoff[:, pr * 2 * pd:(pr + 1) * 2 * pd])
            st = _dot(bg.T.astype(BF16), wts[:, g * gw:(g + 1) * gw])
            h_scr[s, g] = hg * cdec_rep[:, g * gw:(g + 1) * gw] + st
        y = jnp.concatenate(ys, axis=1)

        if final:
            y = y + yf_ref[s].astype(F32) + xs * d_ref[...]
            z = z_ref[s].astype(F32)
            y = y * (z * _sigmoid(z))
            ms = jnp.mean(y * y, axis=-1, keepdims=True)
            out_ref[s] = ((y * lax.rsqrt(ms + EPS)) * nrm_ref[...]).astype(out_ref.dtype)
        else:
            out_ref[s] = y.astype(out_ref.dtype)

    @pl.when(step == nc - 1)
    def _():
        hout_ref[...] = h_scr[...]


def _ssd_pass(xbc3, dt3, h0, tri, conv_w, conv_b, dt_bias, a_row, layer, n, row_off, *, reverse,
              fin=None):
    batch, seq, cw = xbc3.shape
    q = SSM_CHUNK
    nc = n // q
    off = row_off // q
    width = SSM_HEADS * SSM_HEAD_DIM

    def cidx(c):
        return off + ((nc - 1 - c) if reverse else c)

    hsz = BF16_SUBLANE
    per = q // hsz
    top = seq // hsz - 1
    nb = SSD_SAMPLES if batch % SSD_SAMPLES == 0 else 1
    per_layer = lambda b, c: (layer, 0, 0)
    in_specs = [
        pl.BlockSpec((nb, q, cw), lambda b, c: (b, cidx(c), 0)),
        pl.BlockSpec((nb, hsz, cw), lambda b, c: (b, jnp.maximum(cidx(c) * per - 1, 0), 0)),
        pl.BlockSpec((nb, hsz, cw), lambda b, c: (b, jnp.minimum((cidx(c) + 1) * per, top), 0)),
        pl.BlockSpec((nb, q, LANE), lambda b, c: (b, cidx(c), 0)),
        pl.BlockSpec((nb,) + h0.shape[1:], lambda b, c: (b, 0, 0, 0)),
        pl.BlockSpec((q, q), lambda b, c: (0, 0)),
        pl.BlockSpec((LANE, width), lambda b, c: (0, 0)),
        pl.BlockSpec((None,) + conv_w.shape[1:], per_layer),
        pl.BlockSpec((None, 1, cw), per_layer),
        pl.BlockSpec((None, 1, LANE), per_layer),
        pl.BlockSpec((None, 1, LANE), per_layer),
    ]
    lane0 = SSM_HEADS if reverse else 0
    head_rep = (jnp.arange(LANE)[:, None] == lane0 + jnp.arange(width)[None, :] // SSM_HEAD_DIM).astype(BF16)
    args = [xbc3, xbc3, xbc3, dt3, h0, tri, head_rep, conv_w, conv_b, dt_bias, a_row]
    if fin is not None:
        z3, yf, d_row, nrm_row = fin
        in_specs += [
            pl.BlockSpec((nb, q, width), lambda b, c: (b, cidx(c), 0)),
            pl.BlockSpec((nb, q, width), lambda b, c: (b, cidx(c) - off, 0)),
            pl.BlockSpec((None, 1, width), per_layer),
            pl.BlockSpec((None, 1, width), per_layer),
        ]
        args += [z3, yf, d_row, nrm_row]
    out_dtype = BF16 if fin is not None else F32
    return pl.pallas_call(
        functools.partial(_ssd_kernel, reverse=reverse, final=fin is not None, nc=nc, lane0=lane0),
        grid=(batch // nb, nc),
        in_specs=in_specs,
        out_specs=[
            pl.BlockSpec((nb, q, width), lambda b, c: (b, cidx(c) - off, 0)),
            pl.BlockSpec((nb,) + h0.shape[1:], lambda b, c: (b, 0, 0, 0)),
        ],
        out_shape=[
            jax.ShapeDtypeStruct((batch, n, width), out_dtype),
            jax.ShapeDtypeStruct(h0.shape, F32),
        ],
        scratch_shapes=[pltpu.VMEM((nb,) + h0.shape[1:], F32)],
        compiler_params=_cparams(("arbitrary", "arbitrary"), VMEM_LIMIT_SMALL),
        name="ssd_bwd_final" if reverse else "ssd_fwd",
    )(*args)


def _merge_kernel(*refs, tok, splits, n_sub):
    it = iter(refs)
    take = lambda k: [next(it) for _ in range(k)]
    x_ref, = take(1)
    per_sub = [take(9) for _ in range(n_sub)]
    gn_ref, p_ref, wg_ref, wb_ref, wo_ref, o_ref = take(6)
    tm = TOKEN_TILE
    for s in range(n_sub):
        sc_ref, sh_ref, g1_ref, fl_ref, fc_ref, hl_ref, hc_ref, sl_ref, scx_ref = per_sub[s]
        rows = slice(s * tm, (s + 1) * tm)
        ctx = tok.is_ctx(pl.program_id(0) * n_sub + s)
        x = x_ref[rows, :]
        hb = _norm_mod(x, gn_ref[...], sc_ref[...], sh_ref[...]).astype(BF16)
        ys = (jnp.where(ctx, fc_ref[...], fl_ref[...]), jnp.where(ctx, hc_ref[...], hl_ref[...]),
              p_ref[rows, :], jnp.where(ctx, scx_ref[...], sl_ref[...]))
        merged = None
        lo = 0
        for k, hi in enumerate(splits):
            gate = _sigmoid(_dot(hb, wg_ref[k]))
            term = gate * _dot(ys[k], wb_ref[lo:hi, :])
            merged = term if merged is None else merged + term
            lo = hi
        out = _dot(merged.astype(BF16), wo_ref[...])
        o_ref[rows, :] = x + g1_ref[...] * out


def _merge(tok, layer, x_all, mod, norm_mix, f_lat, f_ctx, h_lat, h_ctx, pool, s_lat, s_ctx,
           w_gate, w_branch, w_out, splits):
    d = x_all.shape[-1]
    tm = TOKEN_TILE
    fw, hw, pw, sw = f_lat.shape[-1], h_lat.shape[-1], pool.shape[-1], s_lat.shape[-1]
    n_sub = MERGE_TILES if tok.n_tiles % MERGE_TILES == 0 else 1
    rows = n_sub * tm
    sub_specs, sub_args = [], []
    for s in range(n_sub):
        tile = lambda i, s=s: i * n_sub + s
        sub_specs += [
            tok.mod_spec(layer, 1, d, tile), tok.mod_spec(layer, 0, d, tile), tok.mod_spec(layer, 2, d, tile),
            tok.lat_spec(fw, tile), tok.ctx_spec(fw, tile),
            tok.lat_spec(hw, tile), tok.ctx_spec(hw, tile),
            tok.lat_spec(sw, tile), tok.ctx_spec(sw, tile),
        ]
        sub_args += [mod, mod, mod, f_lat, f_ctx, h_lat, h_ctx, s_lat, s_ctx]
    return pl.pallas_call(
        functools.partial(_merge_kernel, tok=tok, splits=splits, n_sub=n_sub),
        grid=(tok.n_tiles // n_sub,),
        in_specs=[pl.BlockSpec((rows, d), lambda i: (i, 0))] + sub_specs + [
            pl.BlockSpec((None, 1, d), lambda i: (layer, 0, 0)),
            pl.BlockSpec((rows, pw), lambda i: (i, 0)),
            pl.BlockSpec((None,) + w_gate.shape[1:], lambda i: (layer, 0, 0, 0)),
            pl.BlockSpec((None,) + w_branch.shape[1:], lambda i: (layer, 0, 0)),
            pl.BlockSpec((None,) + w_out.shape[1:], lambda i: (layer, 0, 0)),
        ],
        out_specs=pl.BlockSpec((rows, d), lambda i: (i, 0)),
        out_shape=jax.ShapeDtypeStruct(x_all.shape, F32),
        compiler_params=_cparams(("arbitrary",)),
        name="merge",
    )(x_all, *sub_args, norm_mix, pool, w_gate, w_branch, w_out)


def _router_kernel(x_ref, sc_ref, sh_ref, gn_ref, rwh_ref, rwl_ref, rb_ref, utri_ref,
                   h_ref, e_ref, r_ref, w_ref, cnt_ref):
    t = pl.program_id(0)

    @pl.when(t == 0)
    def _():
        cnt_ref[...] = jnp.zeros_like(cnt_ref)

    h = _norm_mod(x_ref[...], gn_ref[...], sc_ref[...], sh_ref[...])
    hh = h.astype(BF16)
    h_ref[...] = hh
    hl = (h - hh.astype(F32)).astype(BF16)
    tm = h.shape[0]
    n_exp = rwh_ref.shape[0]
    nt = (((1,), (1,)), ((), ()))
    dg = lambda a, b: lax.dot_general(a, b, nt, preferred_element_type=F32)
    rwh = rwh_ref[...]
    logits = dg(rwh, hh) + dg(rwh, hl) + dg(rwl_ref[...], hh) + rb_ref[:, 0:1]
    eid = lax.broadcasted_iota(jnp.int32, (n_exp, tm), 0)
    carry = cnt_ref[:, 0:1]
    tops, sels = [], []
    cur = logits
    for _ in range(TOP_K):
        m = jnp.max(cur, axis=0, keepdims=True)
        idx = jnp.min(jnp.where(cur == m, eid, n_exp), axis=0, keepdims=True)
        sel = eid == idx
        tops.append((m, idx))
        sels.append(sel)
        cur = jnp.where(sel, -jnp.inf, cur)
    cnt = jnp.zeros((n_exp, tm), F32)
    for s in sels:
        cnt = jnp.where(s, 1.0, cnt)
    before = _dot(cnt.astype(BF16), utri_ref[...]) + carry
    exps = [jnp.exp(m - tops[0][0]) for m, _ in tops]
    den = exps[0]
    for v in exps[1:]:
        den = den + v
    row8 = lax.broadcasted_iota(jnp.int32, (8, tm), 0)
    rowl = lax.broadcasted_iota(jnp.int32, (LANE, tm), 0)
    e_out = jnp.zeros((8, tm), jnp.int32)
    r_out = jnp.zeros((8, tm), jnp.int32)
    w_t = jnp.zeros((LANE, tm), F32)
    for k in range(TOP_K):
        rank = jnp.sum(jnp.where(sels[k], before, 0.0), axis=0, keepdims=True)
        e_out = jnp.where(row8 == k, tops[k][1], e_out)
        r_out = jnp.where(row8 == k, rank.astype(jnp.int32), r_out)
        w_t = jnp.where(rowl == k, exps[k] / den, w_t)
    e_ref[...] = e_out
    r_ref[...] = r_out
    w_ref[...] = w_t.T
    cnt_ref[...] += jnp.broadcast_to(jnp.sum(cnt, axis=1, keepdims=True), cnt_ref.shape)


def _router(tok, layer, x_all, mod, norm_ffn, rw_hi, rw_lo, router_b_col, utri):
    d = x_all.shape[-1]
    tm = TOKEN_TILE
    n_tok = tok.n_tok
    n_exp = rw_hi.shape[1]
    row_i = jax.ShapeDtypeStruct((8, n_tok), jnp.int32)
    return pl.pallas_call(
        _router_kernel,
        grid=(tok.n_tiles,),
        in_specs=[
            pl.BlockSpec((tm, d), lambda t: (t, 0)),
            tok.mod_spec(layer, 4, d),
            tok.mod_spec(layer, 3, d),
            pl.BlockSpec((None, 1, d), lambda t: (layer, 0, 0)),
            pl.BlockSpec((None, n_exp, d), lambda t: (layer, 0, 0)),
            pl.BlockSpec((None, n_exp, d), lambda t: (layer, 0, 0)),
            pl.BlockSpec((None, n_exp, LANE), lambda t: (layer, 0, 0)),
            pl.BlockSpec((tm, tm), lambda t: (0, 0)),
        ],
        out_specs=[
            pl.BlockSpec((tm, d), lambda t: (t, 0)),
            pl.BlockSpec((8, tm), lambda t: (0, t)),
            pl.BlockSpec((8, tm), lambda t: (0, t)),
            pl.BlockSpec((tm, LANE), lambda t: (t, 0)),
            pl.BlockSpec((n_exp, LANE), lambda t: (0, 0)),
        ],
        out_shape=[
            jax.ShapeDtypeStruct((n_tok, d), BF16),
            row_i,
            row_i,
            jax.ShapeDtypeStruct((n_tok, LANE), F32),
            jax.ShapeDtypeStruct((n_exp, LANE), F32),
        ],
        compiler_params=_cparams(("arbitrary",), VMEM_LIMIT_SMALL),
        name="router",
    )(x_all, mod, mod, norm_ffn, rw_hi, rw_lo, router_b_col, utri)


def _moe_kernel(be_ref, na_ref, x_ref, wu_ref, bu_ref, wd_ref, bd_ref, o_ref, wu_scr, wd_scr, *, ff):
    i = pl.program_id(0)
    active = i < na_ref[0]
    fresh = jnp.logical_or(i == 0, be_ref[i] != be_ref[jnp.maximum(i - 1, 0)])

    @pl.when(jnp.logical_and(active, fresh))
    def _():
        wu_scr[...] = wu_ref[...].astype(BF16)
        wd_scr[...] = wd_ref[...].astype(BF16)

    @pl.when(active)
    def _():
        hu = _dot(x_ref[...], wu_scr[...]) + bu_ref[...]
        glu = jnp.minimum(hu[:, :ff], SWIGLU_LIMIT)
        lin = jnp.clip(hu[:, ff:], -SWIGLU_LIMIT, SWIGLU_LIMIT)
        act = glu * _sigmoid(SWIGLU_ALPHA * glu) * (lin + 1.0)
        o_ref[...] = (_dot(act.astype(BF16), wd_scr[...]) + bd_ref[...]).astype(o_ref.dtype)

    @pl.when(jnp.logical_not(active))
    def _():
        o_ref[...] = jnp.zeros_like(o_ref)


def _moe(layer, x_sorted, blk_e, n_active, w_up, b_up, w_down, b_down):
    n_rows, d = x_sorted.shape
    ff = w_down.shape[2]
    bm = MOE_ROWS
    grid_spec = pltpu.PrefetchScalarGridSpec(
        num_scalar_prefetch=2,
        grid=(n_rows // bm,),
        in_specs=[
            pl.BlockSpec((bm, d), lambda i, be, na: (i, 0)),
            pl.BlockSpec((None, None, d, 2 * ff), lambda i, be, na: (layer, be[i], 0, 0)),
            pl.BlockSpec((None, None, 1, 2 * ff), lambda i, be, na: (layer, be[i], 0, 0)),
            pl.BlockSpec((None, None, ff, d), lambda i, be, na: (layer, be[i], 0, 0)),
            pl.BlockSpec((None, None, 1, d), lambda i, be, na: (layer, be[i], 0, 0)),
        ],
        out_specs=pl.BlockSpec((bm, d), lambda i, be, na: (i, 0)),
        scratch_shapes=[pltpu.VMEM((d, 2 * ff), BF16), pltpu.VMEM((ff, d), BF16)],
    )
    return pl.pallas_call(
        functools.partial(_moe_kernel, ff=ff),
        grid_spec=grid_spec,
        out_shape=jax.ShapeDtypeStruct((n_rows, d), BF16),
        compiler_params=_cparams(("arbitrary",)),
        name="moe_experts",
    )(blk_e, n_active, x_sorted, w_up, b_up, w_down, b_down)


def _final_kernel(x_ref, g2_ref, yg_ref, w_ref, g_ref, o_ref):
    x = _moe_residual(x_ref[...], g2_ref[...], yg_ref, w_ref)
    ms = jnp.mean(x * x, axis=-1, keepdims=True)
    o_ref[...] = (x * lax.rsqrt(ms + EPS)) * g_ref[...]


def _final(tok, layer, x_all, mod, y_gath, top_w, gain):
    batch, seq, n_lat = tok.batch, tok.seq, tok.n_lat
    d = x_all.shape[-1]
    tm = TOKEN_TILE
    return pl.pallas_call(
        _final_kernel,
        grid=(batch, n_lat // tm),
        in_specs=[
            pl.BlockSpec((None, tm, d), lambda b, i: (b, i, 0)),
            pl.BlockSpec((None, None, None, 1, d), lambda b, i: (layer, tok.mod_off + b, 5, 0, 0)),
            pl.BlockSpec((TOP_K, None, tm, d), lambda b, i: (0, b, i, 0)),
            pl.BlockSpec((None, tm, LANE), lambda b, i: (b, i, 0)),
            pl.BlockSpec((1, d), lambda b, i: (0, 0)),
        ],
        out_specs=pl.BlockSpec((None, tm, d), lambda b, i: (b, i, 0)),
        out_shape=jax.ShapeDtypeStruct((batch, n_lat, d), F32),
        compiler_params=_cparams(("arbitrary", "arbitrary"), VMEM_LIMIT_SMALL),
        name="final_combine_norm",
    )(x_all.reshape(batch, seq, d), mod, y_gath.reshape(TOP_K, batch, seq, d),
      top_w.reshape(batch, seq, LANE), gain)


def _dispatch_plan(top_e, rank, counts, n_tok):
    n_exp = counts.shape[0]
    bm = MOE_ROWS
    n_pairs = n_tok * TOP_K
    n_blk = -(-(n_pairs + n_exp * (bm - 1)) // bm)
    padded = (counts + bm - 1) // bm * bm
    pad_end = jnp.cumsum(padded)
    pad_start = pad_end - padded
    dest = rank
    for e in range(n_exp):
        dest = dest + jnp.where(top_e == e, pad_start[e], 0)
    n_active = (pad_end[-1] // bm).astype(jnp.int32).reshape(1)
    blk_start = jnp.arange(n_blk, dtype=jnp.int32) * bm
    blk_e = jnp.minimum(jnp.sum(pad_end[None, :] <= blk_start[:, None], axis=1), n_exp - 1).astype(jnp.int32)
    n_rows = n_blk * bm
    shift = (n_rows - 1).bit_length()
    assert (n_exp + 1) << shift < 2 ** 31
    t_ids = jnp.arange(n_tok, dtype=jnp.int32)[None, :]
    k_ids = jnp.arange(TOP_K, dtype=jnp.int32)[:, None]
    pair_keys = (top_e << shift) | (t_ids * TOP_K + k_ids)
    n_fill = n_rows - n_pairs
    per = bm - 1
    need = jnp.repeat(padded - counts, per)
    fill_e = jnp.where(jnp.tile(jnp.arange(per, dtype=jnp.int32), n_exp) < need,
                       jnp.repeat(jnp.arange(n_exp, dtype=jnp.int32), per), n_exp)
    fill_e = jnp.concatenate([fill_e, jnp.full((n_fill - n_exp * per,), n_exp, jnp.int32)])
    fill_keys = (fill_e << shift) | (n_pairs + jnp.arange(n_fill, dtype=jnp.int32))
    pos = lax.sort(jnp.concatenate([pair_keys.reshape(-1), fill_keys])) & ((1 << shift) - 1)
    row_tok = jnp.where(pos < n_pairs, pos // TOP_K, pos % n_tok)
    return dest, row_tok, blk_e, n_active


def kernel(x, c, ctx, c_ctx, w_mod, b_mod, norm_mix, norm_ffn, w_in, hy_conv_w, hy_conv_b, hy_ffn_w1, hy_ffn_b1, hy_ffn_w2, hy_ffn_b2, hy_ffn_w3, hy_bias, pool_w, pool_scale, ssm_conv_w, ssm_conv_b, ssm_dt_bias, ssm_a_log, ssm_d, ssm_norm, w_branch, w_gate, w_out, router_w, router_b, exp_w_up, exp_b_up, exp_w_down, exp_b_down, norm_final):
    batch, n_lat, d = x.shape
    n_ctx = ctx.shape[1]
    depth = w_mod.shape[0]
    assert n_lat // GRID_W * GRID_W == n_lat and n_lat % n_ctx == 0

    hw = hy_bias.shape[1]
    pw = pool_scale.shape[1]
    sw = ssm_norm.shape[1]
    cw = ssm_conv_w.shape[2]
    fw = w_in.shape[2] - 3 * hw - pw - sw - cw - 2 * SSM_HEADS
    widths = (fw, 3 * hw, pw, sw, cw)
    splits = (fw, fw + hw, fw + hw + pw, fw + hw + pw + sw)
    n_exp = router_w.shape[2]

    w_in_p = jnp.pad(w_in, ((0, 0), (0, 0), (0, LANE - 2 * SSM_HEADS))).astype(BF16)
    w_gate_b = w_gate.astype(BF16)
    w_branch_b = w_branch.astype(BF16)
    w_out_b = w_out.astype(BF16)
    pool_w_bd = jax.vmap(_block_diag)(pool_w).astype(BF16)
    r3 = lambda a: a.reshape(a.shape[0], 1, a.shape[1])
    lane_pad = lambda a: jnp.pad(a, ((0, 0), (0, LANE - a.shape[1])))
    norm_mix3, norm_ffn3 = r3(norm_mix), r3(norm_ffn)
    hy_conv_b3, hy_bias3 = r3(hy_conv_b), r3(hy_bias)
    ssm_conv_b3 = r3(ssm_conv_b)
    pool_scale3 = r3(pool_scale)
    dt_bias3 = r3(lane_pad(ssm_dt_bias.reshape(depth, 2 * SSM_HEADS)))
    a_row3 = r3(lane_pad(-jnp.exp(ssm_a_log.astype(F32)).reshape(depth, 2 * SSM_HEADS)))
    d_row3 = r3(jnp.repeat(ssm_d, SSM_HEAD_DIM, axis=1))
    ssm_norm3 = r3(ssm_norm)
    rw_t = jnp.swapaxes(router_w, 1, 2)
    rw_hi = rw_t.astype(BF16)
    rw_lo = (rw_t - rw_hi.astype(F32)).astype(BF16)
    router_b_col = jnp.broadcast_to(router_b[:, :, None], (depth, n_exp, LANE))
    exp_b_up4 = exp_b_up.reshape(depth, n_exp, 1, -1)
    exp_b_down4 = exp_b_down.reshape(depth, n_exp, 1, -1)
    emb = hy_ffn_w1.shape[1]
    ffn = hy_ffn_w1.shape[2]
    f_w1 = jnp.pad(hy_ffn_w1, ((0, 0), (0, LANE - emb), (0, LANE - ffn)))
    f_b1 = r3(lane_pad(hy_ffn_b1))
    f_w2 = jnp.pad(hy_ffn_w2, ((0, 0), (0, LANE - ffn), (0, LANE - ffn)))
    f_b2 = r3(lane_pad(hy_ffn_b2))
    f_w3 = jnp.pad(hy_ffn_w3, ((0, 0), (0, LANE - ffn), (0, 0)))

    freqs = jnp.linspace(1e-4, HYENA_BANDS - 1, HYENA_BANDS, dtype=F32)
    freqs_row = jnp.concatenate([jnp.zeros((1,), F32), freqs, freqs,
                                 jnp.zeros((LANE - 1 - 2 * HYENA_BANDS,), F32)])[None, :]
    deltas_row = jnp.linspace(HYENA_MIN_DECAY, HYENA_MAX_DECAY, hw, dtype=F32)[None, :]
    gi = jnp.arange(FOURIER_GROUP_W)
    ang = (gi[:, None] * gi[None, :] % FOURIER_GROUP_W).astype(F32) * (2.0 * math.pi / FOURIER_GROUP_W)
    n_fg = fw // FOURIER_GROUP_W
    eye = jnp.eye(n_fg, dtype=F32)
    wc = jnp.concatenate([jnp.kron(eye, jnp.cos(ang)), jnp.kron(eye, jnp.sin(ang))], axis=1).astype(BF16)
    band, invc = _pool_tables(n_lat, n_ctx)
    qi = jnp.arange(SSM_CHUNK)
    tri_f = (qi[:, None] >= qi[None, :]).astype(BF16)
    tri_b = (qi[:, None] <= qi[None, :]).astype(BF16)
    ti = jnp.arange(TOKEN_TILE)
    utri = (ti[:, None] < ti[None, :]).astype(BF16)

    seqs = {}
    for name, n, row_off in (("ctx", n_ctx, n_lat), ("lat", n_lat, 0)):
        fwd, inv = _hyena_dft_tables(n)
        g_all, nrm = _filter_mlp(n, freqs_row, deltas_row, f_w1, f_b1, f_w2, f_b2, f_w3)
        kr, ki = _filter_spectrum(fwd, g_all, nrm, n, hw)
        cn, sn = _trig_tables(n, n)
        seqs[name] = dict(n=n, off=row_off, cn=cn.astype(BF16), sn=sn.astype(BF16), fwd=fwd, inv=inv,
                          kr=kr, ki=ki)

    cc = jnp.concatenate([c, c_ctx[None, :]], axis=0)
    rows = -(-cc.shape[0] // 8) * 8
    mod = _modulation(jnp.pad(cc, ((0, rows - cc.shape[0]), (0, 0))), w_mod, b_mod)
    mod = mod.reshape(depth, rows, 6, 1, d)

    seq = n_lat + n_ctx

    def layer_step(tok, layer, x_all, moe):
        gb, n_tok = tok.batch, tok.n_tok
        res = _inproj(tok, layer, x_all, mod, norm_mix3, w_in_p, wc, band, invc, pool_w_bd, pool_scale3,
                      widths, moe=moe)
        if moe is not None:
            x_all = res[0]
            res = res[1:]
        xcs, hy, pool, z, xbc, dtr = res
        xcs3 = xcs.reshape(gb, seq, 2 * fw)
        hy3 = hy.reshape(gb, seq, 3 * hw)
        z3 = z.reshape(gb, seq, sw)
        xbc3 = xbc.reshape(gb, seq, cw)
        dt3 = dtr.reshape(gb, seq, LANE)
        outs = {}
        zero_state = jnp.zeros((gb, SSM_GROUPS, SSM_STATE, HEADS_PER_GROUP * SSM_HEAD_DIM), F32)
        states = (zero_state, zero_state)
        for name in ("ctx", "lat"):
            s = seqs[name]
            n, off = s["n"], s["off"]
            y_f = _fourier_seq(xcs3, s["cn"], s["sn"], n, off, fw)
            vv, x1 = _hy_pre(hy3, hy_conv_w, hy_conv_b3, layer, n, off, hw)
            y2 = _hy_fwd(s["fwd"], vv, s["kr"], s["ki"], layer, n, hw)
            y_h = _hy_inv(s["inv"], y2, x1, vv, hy_bias3, layer, n, hw)
            yf, st_f = _ssd_pass(xbc3, dt3, states[0], tri_f, ssm_conv_w, ssm_conv_b3, dt_bias3, a_row3,
                                 layer, n, off, reverse=False)
            y_s, st_b = _ssd_pass(xbc3, dt3, states[1], tri_b, ssm_conv_w, ssm_conv_b3, dt_bias3, a_row3,
                                  layer, n, off, reverse=True, fin=(z3, yf, d_row3, ssm_norm3))
            states = (st_f, st_b)
            outs[name] = (y_f, y_h, y_s)
        x_all = _merge(tok, layer, x_all, mod, norm_mix3, outs["lat"][0], outs["ctx"][0], outs["lat"][1],
                       outs["ctx"][1], pool, outs["lat"][2], outs["ctx"][2], w_gate_b, w_branch_b, w_out_b,
                       splits)
        h2, top_e, rank, top_w, counts = _router(tok, layer, x_all, mod, norm_ffn3, rw_hi, rw_lo, router_b_col,
                                                 utri)
        dest, row_tok, blk_e, n_active = _dispatch_plan(top_e[:TOP_K], rank[:TOP_K],
                                                        counts[:, 0].astype(jnp.int32), n_tok)
        x_sorted = h2[row_tok]
        y_rows = _moe(layer, x_sorted, blk_e, n_active, exp_w_up, exp_b_up4, exp_w_down, exp_b_down4)
        y_gath = y_rows[dest.reshape(-1)].reshape(TOP_K, n_tok, d)
        return x_all, (y_gath, top_w)

    n_groups = SAMPLE_GROUPS if batch % SAMPLE_GROUPS == 0 else 1
    gb = batch // n_groups
    groups = []
    for g in range(n_groups):
        tok = _Tok(gb, n_lat, n_ctx, mod_off=g * gb, mod_ctx=batch)
        x_all = jnp.concatenate([x[g * gb:(g + 1) * gb], ctx[g * gb:(g + 1) * gb]], axis=1)
        groups.append([tok, x_all.reshape(tok.n_tok, d), None])
    for layer in range(depth):
        for grp in groups:
            grp[1], grp[2] = layer_step(grp[0], layer, grp[1], grp[2])
    outs = [_final(tok, depth - 1, x_all, mod, moe[0], moe[1], norm_final[None, :])
            for tok, x_all, moe in groups]
    return outs[0] if n_groups == 1 else jnp.concatenate(outs, axis=0)
```

```python
import functools
import math

import jax
import jax.numpy as jnp
from jax import lax
from jax.experimental import pallas as pl
from jax.experimental.pallas import tpu as pltpu

F32 = jnp.float32
BF16 = jnp.bfloat16

GRID_W = 64
EPS = 1e-6
FOURIER_GROUP_W = 64
HYENA_BANDS = 16
HYENA_MIN_DECAY = -math.log(1e-2) / 1.5
HYENA_MAX_DECAY = -math.log(1e-2) / 0.3
POOL_WINDOWS = (2, 4, 8, 16)
POOL_GROUP_W = 64
SSM_GROUPS = 2
HEADS_PER_GROUP = 4
SSM_HEADS = SSM_GROUPS * HEADS_PER_GROUP
SSM_HEAD_DIM = 64
SSM_STATE = 128
SSM_CHUNK = 128
TOP_K = 4
SWIGLU_LIMIT = 7.0
SWIGLU_ALPHA = 1.702

LANE = 128
BF16_SUBLANE = 16
TOKEN_TILE = 256
INPROJ_TILES = 2
MERGE_TILES = 4
SEQ_TILE = 512
SEQ_TILE_1ACC = 1024
SEQ_KTILE = 1024
MOE_ROWS = 512
SSD_SAMPLES = 8
SAMPLE_GROUPS = 1
VMEM_LIMIT = 56 * 1024 * 1024


def _cparams(sem):
    return pltpu.CompilerParams(dimension_semantics=sem, vmem_limit_bytes=VMEM_LIMIT)


def _sigmoid(x):
    return 1.0 / (1.0 + jnp.exp(-x))


def _dot(a, b):
    return jnp.dot(a, b, preferred_element_type=F32)


def _dot_hi(a, b):
    return jnp.dot(a, b, preferred_element_type=F32, precision=lax.Precision.HIGHEST)


def _split3(x):
    x1 = x.astype(BF16)
    r1 = x - x1.astype(F32)
    x2 = r1.astype(BF16)
    x3 = (r1 - x2.astype(F32)).astype(BF16)
    return x1, x2, x3


def _dot_exact01(m01, x):
    x1, x2, x3 = _split3(x)
    return _dot(m01, x1) + _dot(m01, x2) + _dot(m01, x3)


def _mod_kernel(c_ref, w_ref, b_ref, o_ref):
    c = c_ref[...]
    s = c * _sigmoid(c)
    o_ref[...] = _dot_hi(s, w_ref[...]) + b_ref[...]


def _modulation(cc, w_mod, b_mod):
    depth, d, six_d = w_mod.shape
    rows = cc.shape[0]
    nj = six_d // d
    return pl.pallas_call(
        _mod_kernel,
        grid=(depth, nj),
        in_specs=[
            pl.BlockSpec((rows, d), lambda l, j: (0, 0)),
            pl.BlockSpec((None, d, d), lambda l, j: (l, 0, j)),
            pl.BlockSpec((None, 1, d), lambda l, j: (l, 0, j)),
        ],
        out_specs=pl.BlockSpec((None, rows, d), lambda l, j: (l, 0, j)),
        out_shape=jax.ShapeDtypeStruct((depth, rows, six_d), F32),
        compiler_params=_cparams(("arbitrary", "arbitrary")),
        name="modulation",
    )(cc, w_mod, b_mod.reshape(depth, 1, six_d))


class _Tok:
    def __init__(self, batch, n_lat, n_ctx, mod_off, mod_ctx):
        assert n_ctx == TOKEN_TILE and n_lat % TOKEN_TILE == 0
        self.batch, self.n_lat, self.n_ctx = batch, n_lat, n_ctx
        self.mod_off, self.mod_ctx = mod_off, mod_ctx
        self.seq = n_lat + n_ctx
        self.tiles_per_b = self.seq // TOKEN_TILE
        self.lat_tiles = n_lat // TOKEN_TILE
        self.n_tiles = batch * self.tiles_per_b
        self.n_tok = batch * self.seq

    def b(self, t):
        return t // self.tiles_per_b

    def w(self, t):
        return t % self.tiles_per_b

    def is_ctx(self, t):
        return (t % self.tiles_per_b) >= self.lat_tiles

    def mod_row(self, t):
        return jnp.where(self.is_ctx(t), self.mod_ctx, self.mod_off + self.b(t))

    def mod_spec(self, layer, part, d, tile=lambda i: i):
        return pl.BlockSpec((None, None, None, 1, d), lambda i: (layer, self.mod_row(tile(i)), part, 0, 0))

    def lat_spec(self, width, tile=lambda i: i):
        return pl.BlockSpec((None, TOKEN_TILE, width),
                            lambda i: (self.b(tile(i)), jnp.minimum(self.w(tile(i)), self.lat_tiles - 1), 0))

    def ctx_spec(self, width, tile=lambda i: i):
        return pl.BlockSpec((None, TOKEN_TILE, width), lambda i: (self.b(tile(i)), 0, 0))


def _norm_mod(x, gain, scale, shift):
    ms = jnp.mean(x * x, axis=-1, keepdims=True)
    return (x * lax.rsqrt(ms + EPS)) * gain * (1.0 + scale) + shift


def _moe_residual(x, g2, yg_ref, w_ref, rows=slice(None)):
    w = w_ref[rows, :]
    y = None
    for k in range(TOP_K):
        term = w[:, k:k + 1] * yg_ref[k, rows, :].astype(F32)
        y = term if y is None else y + term
    return x + g2 * y


def _inproj_kernel(*refs, widths, combine, n_sub):
    it = iter(refs)
    take = lambda k: [next(it) for _ in range(k)]
    x_ref, = take(1)
    if combine:
        g2_refs = take(n_sub)
        yg_ref, w_ref = take(2)
    sc_refs, sh_refs = take(n_sub), take(n_sub)
    g_ref, win_ref, wc_ref = take(3)
    band_refs, invc_refs = take(n_sub), take(n_sub)
    pw_ref, ps_ref = take(2)
    if combine:
        xn_ref, = take(1)
    xcs_ref, hy_ref, pool_ref, z_ref, xbc_ref, dt_ref = take(6)
    fw, hw, pw, zw, cw = widths
    tm = TOKEN_TILE
    for s in range(n_sub):
        rows = slice(s * tm, (s + 1) * tm)
        x = x_ref[rows, :]
        if combine:
            x = _moe_residual(x, g2_refs[s][...], yg_ref, w_ref, rows)
            xn_ref[rows, :] = x
        hb = _norm_mod(x, g_ref[...], sc_refs[s][...], sh_refs[s][...]).astype(BF16)
        o = 0
        f_in = _dot(hb, win_ref[:, o:o + fw])
        xcs_ref[rows, :] = _dot(f_in.astype(BF16), wc_ref[...]).astype(BF16)
        o += fw
        hy_ref[rows, :] = _dot(hb, win_ref[:, o:o + hw]).astype(BF16)
        o += hw
        u = _dot(hb, win_ref[:, o:o + pw])
        o += pw
        z_ref[rows, :] = _dot(hb, win_ref[:, o:o + zw]).astype(BF16)
        o += zw
        xbc_ref[rows, :] = _dot(hb, win_ref[:, o:o + cw]).astype(BF16)
        o += cw
        dt_ref[rows, :] = _dot(hb, win_ref[:, o:o + LANE])
        gw = POOL_GROUP_W
        parts = []
        for gi in range(len(POOL_WINDOWS)):
            ug = u[:, gi * gw:(gi + 1) * gw]
            parts.append(_dot_exact01(band_refs[s][gi], ug))
        pooled = jnp.concatenate(parts, axis=1) * invc_refs[s][...] - u
        pool_ref[rows, :] = (_dot(pooled.astype(BF16), pw_ref[...]) * ps_ref[...]).astype(BF16)


def _pool_tables(n_lat, n_ctx):
    tm = TOKEN_TILE
    p = jnp.arange(tm)
    bands, invs = [], []
    for row_len in (GRID_W, n_ctx):
        assert tm % row_len == 0
        pr, rr = p % row_len, p // row_len
        bg, ig = [], []
        for win in POOL_WINDOWS:
            lo = jnp.clip(pr - win // 2, 0, row_len)
            hi = jnp.clip(pr + win // 2, 0, row_len)
            q = pr[None, :]
            m = (rr[:, None] == rr[None, :]) & (q >= lo[:, None]) & (q < hi[:, None])
            bg.append(m.astype(BF16))
            ig.append(jnp.broadcast_to((1.0 / (hi - lo).astype(F32))[:, None], (tm, POOL_GROUP_W)))
        bands.append(jnp.stack(bg))
        invs.append(jnp.concatenate(ig, axis=1))
    return jnp.stack(bands), jnp.stack(invs)


def _block_diag(w):
    g, a, b = w.shape
    out = jnp.zeros((g * a, g * b), w.dtype)
    for i in range(g):
        out = out.at[i * a:(i + 1) * a, i * b:(i + 1) * b].set(w[i])
    return out


def _inproj(tok, layer, x_all, mod, norm_mix, w_in_p, wc, band, invc, pool_w_bd, pool_scale, widths,
            moe=None):
    d = x_all.shape[-1]
    fw, hw, pw, zw, cw = widths
    tm = TOKEN_TILE
    n_tok = tok.n_tok
    n_sub = INPROJ_TILES if tok.n_tiles % INPROJ_TILES == 0 else 1
    rows = n_sub * tm
    subs = range(n_sub)
    tile = lambda i, s: i * n_sub + s
    kind = lambda t: jnp.where(tok.is_ctx(t), 1, 0)
    const2 = lambda i: (0, 0)
    mod_specs = lambda lyr, part: [
        pl.BlockSpec((None, None, None, 1, d), lambda i, s=s: (lyr, tok.mod_row(tile(i, s)), part, 0, 0))
        for s in subs]
    pre_specs, pre_args, pre_outs = [], [], []
    if moe is not None:
        pre_specs = mod_specs(layer - 1, 5) + [
            pl.BlockSpec((TOP_K, rows, d), lambda i: (0, i, 0)),
            pl.BlockSpec((rows, LANE), lambda i: (i, 0))]
        pre_args = [mod] * n_sub + [moe[0], moe[1]]
        pre_outs = [jax.ShapeDtypeStruct((n_tok, d), F32)]
    outs = pre_outs + [
        jax.ShapeDtypeStruct((n_tok, 2 * fw), BF16),
        jax.ShapeDtypeStruct((n_tok, hw), BF16),
        jax.ShapeDtypeStruct((n_tok, pw), BF16),
        jax.ShapeDtypeStruct((n_tok, zw), BF16),
        jax.ShapeDtypeStruct((n_tok, cw), BF16),
        jax.ShapeDtypeStruct((n_tok, LANE), F32),
    ]
    return pl.pallas_call(
        functools.partial(_inproj_kernel, widths=widths, combine=moe is not None, n_sub=n_sub),
        grid=(tok.n_tiles // n_sub,),
        in_specs=[pl.BlockSpec((rows, d), lambda i: (i, 0))] + pre_specs
        + mod_specs(layer, 1) + mod_specs(layer, 0) + [
            pl.BlockSpec((None, 1, d), lambda i: (layer, 0, 0)),
            pl.BlockSpec((None,) + w_in_p.shape[1:], lambda i: (layer, 0, 0)),
            pl.BlockSpec(wc.shape, const2),
        ]
        + [pl.BlockSpec((None,) + band.shape[1:], lambda i, s=s: (kind(tile(i, s)), 0, 0, 0)) for s in subs]
        + [pl.BlockSpec((None,) + invc.shape[1:], lambda i, s=s: (kind(tile(i, s)), 0, 0)) for s in subs]
        + [
            pl.BlockSpec((None,) + pool_w_bd.shape[1:], lambda i: (layer, 0, 0)),
            pl.BlockSpec((None, 1, pw), lambda i: (layer, 0, 0)),
        ],
        out_specs=[pl.BlockSpec((rows, o.shape[1]), lambda i: (i, 0)) for o in outs],
        out_shape=outs,
        compiler_params=_cparams(("arbitrary",)),
        name="inproj",
    )(x_all, *pre_args, *([mod] * (2 * n_sub)), norm_mix, w_in_p, wc, *([band] * n_sub), *([invc] * n_sub),
      pool_w_bd, pool_scale)


def _trig_tables(n, period):
    k = jnp.arange(n, dtype=jnp.int32)

    def cs(j):
        m = (j[:, None] * k[None, :]) % period
        ang = m.astype(F32) * (2.0 * math.pi / period)
        return jnp.cos(ang), jnp.sin(ang)

    split = 64
    if n % split or n <= split:
        return cs(k)
    c1, s1 = cs(jnp.arange(n // split, dtype=jnp.int32) * split)
    c2, s2 = cs(jnp.arange(split, dtype=jnp.int32))
    c = c1[:, None, :] * c2[None, :, :] - s1[:, None, :] * s2[None, :, :]
    s = s1[:, None, :] * c2[None, :, :] + c1[:, None, :] * s2[None, :, :]
    return c.reshape(n, n), s.reshape(n, n)


def _hyena_dft_tables(n):
    c, s = _trig_tables(n, 2 * n)
    nyq = jnp.where(jnp.arange(n) % 2 == 0, 1.0, -1.0).astype(F32)
    fwd = jnp.concatenate([c, (-s).at[0].set(nyq)], axis=0).astype(BF16)
    inv = jnp.concatenate([c, (-s).at[:, 0].set(nyq)], axis=1).astype(BF16)
    return fwd, inv


def _fourier_kernel(c_ref, s_ref, xc_ref, xs_ref, o_ref, acc_ref, *, scale):
    k = pl.program_id(1)

    @pl.when(k == 0)
    def _():
        acc_ref[...] = jnp.zeros_like(acc_ref)

    nb = xc_ref.shape[0]
    rc = jnp.concatenate([xc_ref[b] for b in range(nb)], axis=1)
    rs = jnp.concatenate([xs_ref[b] for b in range(nb)], axis=1)
    acc_ref[...] += _dot(c_ref[...], rc) - _dot(s_ref[...], rs)

    @pl.when(k == pl.num_programs(1) - 1)
    def _():
        w = o_ref.shape[2]
        for b in range(nb):
            o_ref[b] = (acc_ref[:, b * w:(b + 1) * w] * scale).astype(o_ref.dtype)


def _fourier_seq(xcs3, cn, sn, n, row_off, fw):
    batch = xcs3.shape[0]
    tl = min(SEQ_TILE_1ACC, n)
    tk = min(SEQ_KTILE, n)
    nt = n // tl
    off = row_off // tk
    scale = 1.0 / math.sqrt(n * FOURIER_GROUP_W)
    return pl.pallas_call(
        functools.partial(_fourier_kernel, scale=scale),
        grid=(nt, n // tk),
        in_specs=[
            pl.BlockSpec((tl, tk), lambda i, k: (i, k)),
            pl.BlockSpec((tl, tk), lambda i, k: (i, k)),
            pl.BlockSpec((batch, tk, fw), lambda i, k: (0, off + k, 0)),
            pl.BlockSpec((batch, tk, fw), lambda i, k: (0, off + k, 1)),
        ],
        out_specs=pl.BlockSpec((batch, tl, fw), lambda i, k: (0, i, 0)),
        out_shape=jax.ShapeDtypeStruct((batch, n, fw), BF16),
        scratch_shapes=[pltpu.VMEM((tl, batch * fw), F32)],
        compiler_params=_cparams(("arbitrary", "arbitrary")),
        name="fourier_seq",
    )(cn, sn, xcs3, xcs3)


def _shift_rows(x, prev_row, next_row):
    n = x.shape[0]
    rows = lax.broadcasted_iota(jnp.int32, x.shape, 0)
    xm = jnp.where(rows == 0, prev_row, pltpu.roll(x, 1, 0))
    xp = jnp.where(rows == n - 1, next_row, pltpu.roll(x, n - 1, 0))
    return xm, xp


def _conv3(x, prev_row, next_row, w_ref, b_ref):
    xm, xp = _shift_rows(x, prev_row, next_row)
    return w_ref[0:1, :] * xm + w_ref[1:2, :] * x + w_ref[2:3, :] * xp + b_ref[...]


def _hy_pre_kernel(x_ref, xp_ref, xn_ref, w_ref, b_ref, vv_ref, x1_ref, *, hw):
    i = pl.program_id(1)
    last = pl.num_programs(1) - 1
    x = x_ref[...].astype(F32)
    hs = xp_ref.shape[0]
    prev_row = jnp.where(i == 0, 0.0, xp_ref[hs - 1:hs, :].astype(F32))
    next_row = jnp.where(i == last, 0.0, xn_ref[0:1, :].astype(F32))
    u = _conv3(x, prev_row, next_row, w_ref, b_ref)
    x1_ref[...] = u[:, :hw].astype(BF16)
    vv_ref[...] = (u[:, hw:2 * hw] * u[:, 2 * hw:]).astype(BF16)


def _halo_specs(width, tile, row_off, seq_rows):
    hs = BF16_SUBLANE
    per = tile // hs
    base = row_off // hs
    top = seq_rows // hs - 1
    prev = pl.BlockSpec((None, hs, width), lambda b, i: (b, jnp.maximum(base + i * per - 1, 0), 0))
    nxt = pl.BlockSpec((None, hs, width), lambda b, i: (b, jnp.minimum(base + (i + 1) * per, top), 0))
    return prev, nxt


def _hy_pre(hy3, conv_w, conv_b, layer, n, row_off, hw):
    batch, seq, w3 = hy3.shape
    tl = min(SEQ_TILE, n)
    off = row_off // tl
    prev, nxt = _halo_specs(w3, tl, row_off, seq)
    out = jax.ShapeDtypeStruct((batch, n, hw), BF16)
    return pl.pallas_call(
        functools.partial(_hy_pre_kernel, hw=hw),
        grid=(batch, n // tl),
        in_specs=[
            pl.BlockSpec((None, tl, w3), lambda b, i: (b, off + i, 0)),
            prev, nxt,
            pl.BlockSpec((None,) + conv_w.shape[1:], lambda b, i: (layer, 0, 0)),
            pl.BlockSpec((None, 1, w3), lambda b, i: (layer, 0, 0)),
        ],
        out_specs=[pl.BlockSpec((None, tl, hw), lambda b, i: (b, i, 0))] * 2,
        out_shape=[out, out],
        compiler_params=_cparams(("arbitrary", "arbitrary")),
        name="hyena_pre",
    )(hy3, hy3, hy3, conv_w, conv_b)


def _filter_mlp_kernel(fr_ref, dl_ref, w1_ref, b1_ref, w2_ref, b2_ref, w3_ref, g_ref, nrm_ref, *, n, hw):
    j = pl.program_id(1)
    rt = g_ref.shape[0]
    lane = lax.broadcasted_iota(jnp.int32, (rt, LANE), 1)
    pos = (lax.broadcasted_iota(jnp.int32, (rt, LANE), 0) + j * rt).astype(F32)
    t = pos / (n - 1)
    ang = 2.0 * math.pi * pos / n
    arg = ang * fr_ref[...]
    feats = jnp.where(lane == 0, t,
                      jnp.where(lane <= HYENA_BANDS, jnp.cos(arg),
                                jnp.where(lane <= 2 * HYENA_BANDS, -jnp.sin(arg), 0.0)))
    h1 = jnp.sin(_dot_hi(feats, w1_ref[...]) + b1_ref[...])
    h2 = jnp.sin(_dot_hi(h1, w2_ref[...]) + b2_ref[...])
    k = _dot_hi(h2, w3_ref[...])
    decay = jnp.exp(-t[:, 0:1] * dl_ref[...])
    kf = k[:, :hw] * decay
    kb = jnp.where(pos[:, 0:1] == 0.0, 0.0, k[:, hw:] * decay)
    g_ref[:, :hw] = (kf + kb).astype(BF16)
    g_ref[:, hw:] = (kf - kb).astype(BF16)
    part = jnp.sum(jnp.abs(kf) + jnp.abs(kb), axis=0, keepdims=True)

    @pl.when(j == 0)
    def _():
        nrm_ref[...] = part

    @pl.when(j > 0)
    def _():
        nrm_ref[...] += part


def _filter_mlp(n, freqs_row, deltas_row, w1, b1, w2, b2, w3):
    depth = w1.shape[0]
    hw = deltas_row.shape[1]
    rt = min(SEQ_TILE, n)
    per_layer = lambda l, j: (l, 0, 0)
    return pl.pallas_call(
        functools.partial(_filter_mlp_kernel, n=n, hw=hw),
        grid=(depth, n // rt),
        in_specs=[
            pl.BlockSpec((1, LANE), lambda l, j: (0, 0)),
            pl.BlockSpec((1, hw), lambda l, j: (0, 0)),
            pl.BlockSpec((None,) + w1.shape[1:], per_layer),
            pl.BlockSpec((None,) + b1.shape[1:], per_layer),
            pl.BlockSpec((None,) + w2.shape[1:], per_layer),
            pl.BlockSpec((None,) + b2.shape[1:], per_layer),
            pl.BlockSpec((None,) + w3.shape[1:], per_layer),
        ],
        out_specs=[
            pl.BlockSpec((rt, 2 * hw), lambda l, j: (j, l)),
            pl.BlockSpec((None, 1, hw), lambda l, j: (l, 0, 0)),
        ],
        out_shape=[
            jax.ShapeDtypeStruct((n, depth * 2 * hw), BF16),
            jax.ShapeDtypeStruct((depth, 1, hw), F32),
        ],
        compiler_params=_cparams(("arbitrary", "arbitrary")),
        name="hyena_filter_mlp",
    )(freqs_row, deltas_row, w1, b1, w2, b2, w3)


def _rdft_accumulate(fc_ref, fs_ref, rhs, accc_ref, accs_ref):
    k = pl.program_id(1)

    @pl.when(k == 0)
    def _():
        accc_ref[...] = jnp.zeros_like(accc_ref)
        accs_ref[...] = jnp.zeros_like(accs_ref)

    accc_ref[...] += _dot(fc_ref[...], rhs)
    accs_ref[...] += _dot(fs_ref[...], rhs)


def _filter_spec_kernel(fc_ref, fs_ref, g_ref, nrm_ref, kr_ref, ki_ref, accc_ref, accs_ref, *, hw):
    i = pl.program_id(0)
    _rdft_accumulate(fc_ref, fs_ref, g_ref[...], accc_ref, accs_ref)

    @pl.when(pl.program_id(1) == pl.num_programs(1) - 1)
    def _():
        tm = accc_ref.shape[0]
        row0 = (lax.broadcasted_iota(jnp.int32, (tm, hw), 0) + i * tm) == 0
        for l in range(kr_ref.shape[0]):
            inv = 1.0 / nrm_ref[l]
            o = l * 2 * hw
            kr_ref[l] = accc_ref[:, o:o + hw] * inv
            ki_ref[l] = jnp.where(row0, accs_ref[:, o:o + hw], accs_ref[:, o + hw:o + 2 * hw]) * inv


def _filter_spectrum(fwd, g_all, nrm, n, hw):
    depth = nrm.shape[0]
    tl = min(SEQ_TILE, n)
    nt = n // tl
    width = g_all.shape[1]
    out = jax.ShapeDtypeStruct((depth, n, hw), F32)
    return pl.pallas_call(
        functools.partial(_filter_spec_kernel, hw=hw),
        grid=(nt, nt),
        in_specs=[
            pl.BlockSpec((tl, tl), lambda i, k: (i, k)),
            pl.BlockSpec((tl, tl), lambda i, k: (nt + i, k)),
            pl.BlockSpec((tl, width), lambda i, k: (k, 0)),
            pl.BlockSpec(nrm.shape, lambda i, k: (0, 0, 0)),
        ],
        out_specs=[pl.BlockSpec((depth, tl, hw), lambda i, k: (0, i, 0))] * 2,
        out_shape=[out, out],
        scratch_shapes=[pltpu.VMEM((tl, width), F32)] * 2,
        compiler_params=_cparams(("arbitrary", "arbitrary")),
        name="hyena_filter_spectrum",
    )(fwd, fwd, g_all, nrm)


def _hy_fwd_kernel(fc_ref, fs_ref, vv_ref, kr_ref, ki_ref, y_ref, accc_ref, accs_ref):
    i = pl.program_id(0)
    nb = vv_ref.shape[0]
    rhs = jnp.concatenate([vv_ref[b] for b in range(nb)], axis=1)
    _rdft_accumulate(fc_ref, fs_ref, rhs, accc_ref, accs_ref)

    @pl.when(pl.program_id(1) == pl.num_programs(1) - 1)
    def _():
        tm, hw = kr_ref.shape
        row0 = (lax.broadcasted_iota(jnp.int32, (tm, hw), 0) + i * tm) == 0
        kr, ki = kr_ref[...], ki_ref[...]
        for b in range(nb):
            vr = accc_ref[:, b * hw:(b + 1) * hw]
            vi = accs_ref[:, b * hw:(b + 1) * hw]
            yr = jnp.where(row0, 0.5 * vr * kr, vr * kr - vi * ki)
            yi = jnp.where(row0, 0.5 * vi * ki, vr * ki + vi * kr)
            y_ref[0, :, b * hw:(b + 1) * hw] = yr.astype(BF16)
            y_ref[1, :, b * hw:(b + 1) * hw] = yi.astype(BF16)


def _hy_fwd(fwd, vv, kr, ki, layer, n, hw):
    batch = vv.shape[0]
    tl = min(SEQ_TILE, n)
    tk = min(SEQ_KTILE, n)
    nt = n // tl
    return pl.pallas_call(
        _hy_fwd_kernel,
        grid=(nt, n // tk),
        in_specs=[
            pl.BlockSpec((tl, tk), lambda i, k: (i, k)),
            pl.BlockSpec((tl, tk), lambda i, k: (nt + i, k)),
            pl.BlockSpec((batch, tk, hw), lambda i, k: (0, k, 0)),
            pl.BlockSpec((None, tl, hw), lambda i, k: (layer, i, 0)),
            pl.BlockSpec((None, tl, hw), lambda i, k: (layer, i, 0)),
        ],
        out_specs=pl.BlockSpec((2, tl, batch * hw), lambda i, k: (0, i, 0)),
        out_shape=jax.ShapeDtypeStruct((2, n, batch * hw), BF16),
        scratch_shapes=[pltpu.VMEM((tl, batch * hw), F32)] * 2,
        compiler_params=_cparams(("arbitrary", "arbitrary")),
        name="hyena_fwd_dft",
    )(fwd, fwd, vv, kr, ki)


def _hy_inv_kernel(fi_ref, y_ref, x1_ref, vv_ref, bias_ref, o_ref, acc_ref, *, scale):
    k = pl.program_id(1)

    @pl.when(k == 0)
    def _():
        acc_ref[...] = jnp.zeros_like(acc_ref)

    acc_ref[...] += _dot(fi_ref[...], y_ref[...])

    @pl.when(k == pl.num_programs(1) - 1)
    def _():
        nb, _, hw = x1_ref.shape
        for b in range(nb):
            conv = acc_ref[:, b * hw:(b + 1) * hw] * scale
            vv = vv_ref[b].astype(F32)
            o_ref[b] = (x1_ref[b].astype(F32) * (conv + vv * bias_ref[...])).astype(BF16)


def _hy_inv(inv, y2, x1, vv, hy_bias, layer, n, hw):
    batch = vv.shape[0]
    tl = min(SEQ_TILE_1ACC, n)
    tk = min(SEQ_KTILE, n)
    nt = n // tl
    return pl.pallas_call(
        functools.partial(_hy_inv_kernel, scale=1.0 / n),
        grid=(nt, 2 * n // tk),
        in_specs=[
            pl.BlockSpec((tl, tk), lambda i, k: (i, k)),
            pl.BlockSpec((tk, batch * hw), lambda i, k: (k, 0)),
            pl.BlockSpec((batch, tl, hw), lambda i, k: (0, i, 0)),
            pl.BlockSpec((batch, tl, hw), lambda i, k: (0, i, 0)),
            pl.BlockSpec((None, 1, hw), lambda i, k: (layer, 0, 0)),
        ],
        out_specs=pl.BlockSpec((batch, tl, hw), lambda i, k: (0, i, 0)),
        out_shape=jax.ShapeDtypeStruct((batch, n, hw), BF16),
        scratch_shapes=[pltpu.VMEM((tl, batch * hw), F32)],
        compiler_params=_cparams(("arbitrary", "arbitrary")),
        name="hyena_inv_dft",
    )(inv, y2.reshape(2 * n, batch * hw), x1, vv, hy_bias)


def _softplus(x):
    return jnp.maximum(x, 0.0) + jnp.log1p(jnp.exp(-jnp.abs(x)))


def _ssd_kernel(*refs, reverse, final, nc, lane0):
    if final:
        (xbc_ref, xp_ref, xn_ref, dt_ref, h0_ref, tri_ref, exp_ref, cw_ref, cb_ref, dtb_ref, a_ref,
         z_ref, yf_ref, d_ref, nrm_ref, out_ref, hout_ref, h_scr) = refs
    else:
        (xbc_ref, xp_ref, xn_ref, dt_ref, h0_ref, tri_ref, exp_ref, cw_ref, cb_ref, dtb_ref, a_ref,
         out_ref, hout_ref, h_scr) = refs
    step = pl.program_id(1)
    cc = (nc - 1 - step) if reverse else step

    @pl.when(step == 0)
    def _():
        h_scr[...] = h0_ref[...]

    nb, q = xbc_ref.shape[0], xbc_ref.shape[1]
    hs = xp_ref.shape[1]
    width = SSM_HEADS * SSM_HEAD_DIM
    gs = SSM_STATE
    pd = SSM_HEAD_DIM
    ri = lax.broadcasted_iota(jnp.int32, (q, q), 0)
    ci = lax.broadcasted_iota(jnp.int32, (q, q), 1)
    mask = (ri <= ci) if reverse else (ri >= ci)

    for s in range(nb):
        x = xbc_ref[s].astype(F32)
        prev_row = jnp.where(cc == 0, 0.0, xp_ref[s, hs - 1:hs, :].astype(F32))
        next_row = jnp.where(cc == nc - 1, 0.0, xn_ref[s, 0:1, :].astype(F32))
        u = _conv3(x, prev_row, next_row, cw_ref, cb_ref)
        u = u * _sigmoid(u)
        xs = u[:, :width]
        bm = u[:, width:width + SSM_GROUPS * gs]
        cm = u[:, width + SSM_GROUPS * gs:]

        dt = _softplus(dt_ref[s] + dtb_ref[...])
        acs = _dot_exact01(tri_ref[...], dt * a_ref[...])
        acs_t = acs.T
        tot = acs[0:1, :] if reverse else acs[q - 1:q, :]
        to_end = jnp.exp(tot - acs)
        frm = jnp.exp(acs)
        cdec = jnp.exp(tot)

        stacked = jnp.concatenate([dt, to_end, frm, jnp.broadcast_to(cdec, (8, LANE))], axis=0)
        s1, s2, s3 = _split3(stacked)
        ex = exp_ref[...]
        rep = _dot(s1, ex) + _dot(s2, ex) + _dot(s3, ex)
        xdt = xs * rep[0:q]
        wts = (xdt * rep[q:2 * q]).astype(BF16)
        frm_rep = rep[2 * q:3 * q]
        cdec_rep = rep[3 * q:3 * q + 1]
        xdt_b = xdt.astype(BF16)
        half = lax.broadcasted_iota(jnp.int32, (q, 2 * pd), 1) < pd

        ys = []
        gw = HEADS_PER_GROUP * pd
        for g in range(SSM_GROUPS):
            bg = bm[:, g * gs:(g + 1) * gs]
            cg = cm[:, g * gs:(g + 1) * gs].astype(BF16)
            scores = lax.dot_general(cg, bg.astype(BF16), (((1,), (1,)), ((), ())),
                                     preferred_element_type=F32)
            hg = h_scr[s, g]
            yoff = _dot(cg, hg.astype(BF16)) * frm_rep[:, g * gw:(g + 1) * gw]
            for pr in range(HEADS_PER_GROUP // 2):
                pair = []
                c0 = g * gw + pr * 2 * pd
                for j in range(2):
                    li = lane0 + g * HEADS_PER_GROUP + 2 * pr + j
                    lm = jnp.where(mask, jnp.exp(acs[:, li:li + 1] - acs_t[li:li + 1, :]), 0.0)
                    m = (scores * lm).astype(BF16)
                    pair.append(_dot(m, xdt_b[:, c0:c0 + 2 * pd]))
                ys.append(jnp.where(half, pair[0], pair[1]) + yoff[:, pr * 2 * pd:(pr + 1) * 2 * pd])
            st = _dot(bg.T.astype(BF16), wts[:, g * gw:(g + 1) * gw])
            h_scr[s, g] = hg * cdec_rep[:, g * gw:(g + 1) * gw] + st
        y = jnp.concatenate(ys, axis=1)

        if final:
            y = y + yf_ref[s].astype(F32) + xs * d_ref[...]
            z = z_ref[s].astype(F32)
            y = y * (z * _sigmoid(z))
            ms = jnp.mean(y * y, axis=-1, keepdims=True)
            out_ref[s] = ((y * lax.rsqrt(ms + EPS)) * nrm_ref[...]).astype(out_ref.dtype)
        else:
            out_ref[s] = y.astype(out_ref.dtype)

    @pl.when(step == nc - 1)
    def _():
        hout_ref[...] = h_scr[...]


def _ssd_pass(xbc3, dt3, h0, tri, conv_w, conv_b, dt_bias, a_row, layer, n, row_off, *, reverse,
              fin=None):
    batch, seq, cw = xbc3.shape
    q = SSM_CHUNK
    nc = n // q
    off = row_off // q
    width = SSM_HEADS * SSM_HEAD_DIM

    def cidx(c):
        return off + ((nc - 1 - c) if reverse else c)

    hsz = BF16_SUBLANE
    per = q // hsz
    top = seq // hsz - 1
    nb = SSD_SAMPLES if batch % SSD_SAMPLES == 0 else 1
    per_layer = lambda b, c: (layer, 0, 0)
    in_specs = [
        pl.BlockSpec((nb, q, cw), lambda b, c: (b, cidx(c), 0)),
        pl.BlockSpec((nb, hsz, cw), lambda b, c: (b, jnp.maximum(cidx(c) * per - 1, 0), 0)),
        pl.BlockSpec((nb, hsz, cw), lambda b, c: (b, jnp.minimum((cidx(c) + 1) * per, top), 0)),
        pl.BlockSpec((nb, q, LANE), lambda b, c: (b, cidx(c), 0)),
        pl.BlockSpec((nb,) + h0.shape[1:], lambda b, c: (b, 0, 0, 0)),
        pl.BlockSpec((q, q), lambda b, c: (0, 0)),
        pl.BlockSpec((LANE, width), lambda b, c: (0, 0)),
        pl.BlockSpec((None,) + conv_w.shape[1:], per_layer),
        pl.BlockSpec((None, 1, cw), per_layer),
        pl.BlockSpec((None, 1, LANE), per_layer),
        pl.BlockSpec((None, 1, LANE), per_layer),
    ]
    lane0 = SSM_HEADS if reverse else 0
    head_rep = (jnp.arange(LANE)[:, None] == lane0 + jnp.arange(width)[None, :] // SSM_HEAD_DIM).astype(BF16)
    args = [xbc3, xbc3, xbc3, dt3, h0, tri, head_rep, conv_w, conv_b, dt_bias, a_row]
    if fin is not None:
        z3, yf, d_row, nrm_row = fin
        in_specs += [
            pl.BlockSpec((nb, q, width), lambda b, c: (b, cidx(c), 0)),
            pl.BlockSpec((nb, q, width), lambda b, c: (b, cidx(c) - off, 0)),
            pl.BlockSpec((None, 1, width), per_layer),
            pl.BlockSpec((None, 1, width), per_layer),
        ]
        args += [z3, yf, d_row, nrm_row]
    out_dtype = BF16 if fin is not None else F32
    return pl.pallas_call(
        functools.partial(_ssd_kernel, reverse=reverse, final=fin is not None, nc=nc, lane0=lane0),
        grid=(batch // nb, nc),
        in_specs=in_specs,
        out_specs=[
            pl.BlockSpec((nb, q, width), lambda b, c: (b, cidx(c) - off, 0)),
            pl.BlockSpec((nb,) + h0.shape[1:], lambda b, c: (b, 0, 0, 0)),
        ],
        out_shape=[
            jax.ShapeDtypeStruct((batch, n, width), out_dtype),
            jax.ShapeDtypeStruct(h0.shape, F32),
        ],
        scratch_shapes=[pltpu.VMEM((nb,) + h0.shape[1:], F32)],
        compiler_params=_cparams(("arbitrary", "arbitrary")),
        name="ssd_bwd_final" if reverse else "ssd_fwd",
    )(*args)


def _merge_kernel(*refs, tok, splits, n_sub):
    it = iter(refs)
    take = lambda k: [next(it) for _ in range(k)]
    x_ref, = take(1)
    per_sub = [take(9) for _ in range(n_sub)]
    gn_ref, p_ref, wg_ref, wb_ref, wo_ref, o_ref = take(6)
    tm = TOKEN_TILE
    for s in range(n_sub):
        sc_ref, sh_ref, g1_ref, fl_ref, fc_ref, hl_ref, hc_ref, sl_ref, scx_ref = per_sub[s]
        rows = slice(s * tm, (s + 1) * tm)
        ctx = tok.is_ctx(pl.program_id(0) * n_sub + s)
        x = x_ref[rows, :]
        hb = _norm_mod(x, gn_ref[...], sc_ref[...], sh_ref[...]).astype(BF16)
        ys = (jnp.where(ctx, fc_ref[...], fl_ref[...]), jnp.where(ctx, hc_ref[...], hl_ref[...]),
              p_ref[rows, :], jnp.where(ctx, scx_ref[...], sl_ref[...]))
        merged = None
        lo = 0
        for k, hi in enumerate(splits):
            gate = _sigmoid(_dot(hb, wg_ref[k]))
            term = gate * _dot(ys[k], wb_ref[lo:hi, :])
            merged = term if merged is None else merged + term
            lo = hi
        out = _dot(merged.astype(BF16), wo_ref[...])
        o_ref[rows, :] = x + g1_ref[...] * out


def _merge(tok, layer, x_all, mod, norm_mix, f_lat, f_ctx, h_lat, h_ctx, pool, s_lat, s_ctx,
           w_gate, w_branch, w_out, splits):
    d = x_all.shape[-1]
    tm = TOKEN_TILE
    fw, hw, pw, sw = f_lat.shape[-1], h_lat.shape[-1], pool.shape[-1], s_lat.shape[-1]
    n_sub = MERGE_TILES if tok.n_tiles % MERGE_TILES == 0 else 1
    rows = n_sub * tm
    sub_specs, sub_args = [], []
    for s in range(n_sub):
        tile = lambda i, s=s: i * n_sub + s
        sub_specs += [
            tok.mod_spec(layer, 1, d, tile), tok.mod_spec(layer, 0, d, tile), tok.mod_spec(layer, 2, d, tile),
            tok.lat_spec(fw, tile), tok.ctx_spec(fw, tile),
            tok.lat_spec(hw, tile), tok.ctx_spec(hw, tile),
            tok.lat_spec(sw, tile), tok.ctx_spec(sw, tile),
        ]
        sub_args += [mod, mod, mod, f_lat, f_ctx, h_lat, h_ctx, s_lat, s_ctx]
    return pl.pallas_call(
        functools.partial(_merge_kernel, tok=tok, splits=splits, n_sub=n_sub),
        grid=(tok.n_tiles // n_sub,),
        in_specs=[pl.BlockSpec((rows, d), lambda i: (i, 0))] + sub_specs + [
            pl.BlockSpec((None, 1, d), lambda i: (layer, 0, 0)),
            pl.BlockSpec((rows, pw), lambda i: (i, 0)),
            pl.BlockSpec((None,) + w_gate.shape[1:], lambda i: (layer, 0, 0, 0), pipeline_mode=pl.Buffered(1)),
            pl.BlockSpec((None,) + w_branch.shape[1:], lambda i: (layer, 0, 0), pipeline_mode=pl.Buffered(1)),
            pl.BlockSpec((None,) + w_out.shape[1:], lambda i: (layer, 0, 0), pipeline_mode=pl.Buffered(1)),
        ],
        out_specs=pl.BlockSpec((rows, d), lambda i: (i, 0)),
        out_shape=jax.ShapeDtypeStruct(x_all.shape, F32),
        compiler_params=_cparams(("arbitrary",)),
        name="merge",
    )(x_all, *sub_args, norm_mix, pool, w_gate, w_branch, w_out)


def _router_kernel(x_ref, sc_ref, sh_ref, gn_ref, rwh_ref, rwl_ref, rb_ref, utri_ref,
                   h_ref, e_ref, r_ref, w_ref, cnt_ref):
    t = pl.program_id(0)

    @pl.when(t == 0)
    def _():
        cnt_ref[...] = jnp.zeros_like(cnt_ref)

    h = _norm_mod(x_ref[...], gn_ref[...], sc_ref[...], sh_ref[...])
    hh = h.astype(BF16)
    h_ref[...] = hh
    hl = (h - hh.astype(F32)).astype(BF16)
    tm = h.shape[0]
    n_exp = rwh_ref.shape[0]
    nt = (((1,), (1,)), ((), ()))
    dg = lambda a, b: lax.dot_general(a, b, nt, preferred_element_type=F32)
    rwh = rwh_ref[...]
    logits = dg(rwh, hh) + dg(rwh, hl) + dg(rwl_ref[...], hh) + rb_ref[:, 0:1]
    eid = lax.broadcasted_iota(jnp.int32, (n_exp, tm), 0)
    carry = cnt_ref[:, 0:1]
    tops, sels = [], []
    cur = logits
    for _ in range(TOP_K):
        m = jnp.max(cur, axis=0, keepdims=True)
        idx = jnp.min(jnp.where(cur == m, eid, n_exp), axis=0, keepdims=True)
        sel = eid == idx
        tops.append((m, idx))
        sels.append(sel)
        cur = jnp.where(sel, -jnp.inf, cur)
    cnt = jnp.zeros((n_exp, tm), F32)
    for s in sels:
        cnt = jnp.where(s, 1.0, cnt)
    before = _dot(cnt.astype(BF16), utri_ref[...]) + carry
    exps = [jnp.exp(m - tops[0][0]) for m, _ in tops]
    den = exps[0]
    for v in exps[1:]:
        den = den + v
    row8 = lax.broadcasted_iota(jnp.int32, (8, tm), 0)
    rowl = lax.broadcasted_iota(jnp.int32, (LANE, tm), 0)
    e_out = jnp.zeros((8, tm), jnp.int32)
    r_out = jnp.zeros((8, tm), jnp.int32)
    w_t = jnp.zeros((LANE, tm), F32)
    for k in range(TOP_K):
        rank = jnp.sum(jnp.where(sels[k], before, 0.0), axis=0, keepdims=True)
        e_out = jnp.where(row8 == k, tops[k][1], e_out)
        r_out = jnp.where(row8 == k, rank.astype(jnp.int32), r_out)
        w_t = jnp.where(rowl == k, exps[k] / den, w_t)
    e_ref[...] = e_out
    r_ref[...] = r_out
    w_ref[...] = w_t.T
    cnt_ref[...] += jnp.broadcast_to(jnp.sum(cnt, axis=1, keepdims=True), cnt_ref.shape)


def _router(tok, layer, x_all, mod, norm_ffn, rw_hi, rw_lo, router_b_col, utri):
    d = x_all.shape[-1]
    tm = TOKEN_TILE
    n_tok = tok.n_tok
    n_exp = rw_hi.shape[1]
    row_i = jax.ShapeDtypeStruct((8, n_tok), jnp.int32)
    return pl.pallas_call(
        _router_kernel,
        grid=(tok.n_tiles,),
        in_specs=[
            pl.BlockSpec((tm, d), lambda t: (t, 0)),
            tok.mod_spec(layer, 4, d),
            tok.mod_spec(layer, 3, d),
            pl.BlockSpec((None, 1, d), lambda t: (layer, 0, 0)),
            pl.BlockSpec((None, n_exp, d), lambda t: (layer, 0, 0)),
            pl.BlockSpec((None, n_exp, d), lambda t: (layer, 0, 0)),
            pl.BlockSpec((None, n_exp, LANE), lambda t: (layer, 0, 0)),
            pl.BlockSpec((tm, tm), lambda t: (0, 0)),
        ],
        out_specs=[
            pl.BlockSpec((tm, d), lambda t: (t, 0)),
            pl.BlockSpec((8, tm), lambda t: (0, t)),
            pl.BlockSpec((8, tm), lambda t: (0, t)),
            pl.BlockSpec((tm, LANE), lambda t: (t, 0)),
            pl.BlockSpec((n_exp, LANE), lambda t: (0, 0)),
        ],
        out_shape=[
            jax.ShapeDtypeStruct((n_tok, d), BF16),
            row_i,
            row_i,
            jax.ShapeDtypeStruct((n_tok, LANE), F32),
            jax.ShapeDtypeStruct((n_exp, LANE), F32),
        ],
        compiler_params=_cparams(("arbitrary",)),
        name="router",
    )(x_all, mod, mod, norm_ffn, rw_hi, rw_lo, router_b_col, utri)


def _moe_kernel(be_ref, na_ref, x_ref, wu_ref, bu_ref, wd_ref, bd_ref, o_ref, wu_scr, wd_scr, *, ff):
    i = pl.program_id(0)
    active = i < na_ref[0]
    fresh = jnp.logical_or(i == 0, be_ref[i] != be_ref[jnp.maximum(i - 1, 0)])

    @pl.when(jnp.logical_and(active, fresh))
    def _():
        wu_scr[...] = wu_ref[...].astype(BF16)
        wd_scr[...] = wd_ref[...].astype(BF16)

    @pl.when(active)
    def _():
        hu = _dot(x_ref[...], wu_scr[...]) + bu_ref[...]
        glu = jnp.minimum(hu[:, :ff], SWIGLU_LIMIT)
        lin = jnp.clip(hu[:, ff:], -SWIGLU_LIMIT, SWIGLU_LIMIT)
        act = glu * _sigmoid(SWIGLU_ALPHA * glu) * (lin + 1.0)
        o_ref[...] = (_dot(act.astype(BF16), wd_scr[...]) + bd_ref[...]).astype(o_ref.dtype)

    @pl.when(jnp.logical_not(active))
    def _():
        o_ref[...] = jnp.zeros_like(o_ref)


def _moe(layer, x_sorted, blk_e, n_active, w_up, b_up, w_down, b_down):
    n_rows, d = x_sorted.shape
    ff = w_down.shape[2]
    bm = MOE_ROWS
    grid_spec = pltpu.PrefetchScalarGridSpec(
        num_scalar_prefetch=2,
        grid=(n_rows // bm,),
        in_specs=[
            pl.BlockSpec((bm, d), lambda i, be, na: (i, 0)),
            pl.BlockSpec((None, None, d, 2 * ff), lambda i, be, na: (layer, be[i], 0, 0)),
            pl.BlockSpec((None, None, 1, 2 * ff), lambda i, be, na: (layer, be[i], 0, 0)),
            pl.BlockSpec((None, None, ff, d), lambda i, be, na: (layer, be[i], 0, 0)),
            pl.BlockSpec((None, None, 1, d), lambda i, be, na: (layer, be[i], 0, 0)),
        ],
        out_specs=pl.BlockSpec((bm, d), lambda i, be, na: (i, 0)),
        scratch_shapes=[pltpu.VMEM((d, 2 * ff), BF16), pltpu.VMEM((ff, d), BF16)],
    )
    return pl.pallas_call(
        functools.partial(_moe_kernel, ff=ff),
        grid_spec=grid_spec,
        out_shape=jax.ShapeDtypeStruct((n_rows, d), BF16),
        compiler_params=_cparams(("arbitrary",)),
        name="moe_experts",
    )(blk_e, n_active, x_sorted, w_up, b_up, w_down, b_down)


def _final_kernel(x_ref, g2_ref, yg_ref, w_ref, g_ref, o_ref):
    x = _moe_residual(x_ref[...], g2_ref[...], yg_ref, w_ref)
    ms = jnp.mean(x * x, axis=-1, keepdims=True)
    o_ref[...] = (x * lax.rsqrt(ms + EPS)) * g_ref[...]


def _final(tok, layer, x_all, mod, y_gath, top_w, gain):
    batch, seq, n_lat = tok.batch, tok.seq, tok.n_lat
    d = x_all.shape[-1]
    tm = TOKEN_TILE
    return pl.pallas_call(
        _final_kernel,
        grid=(batch, n_lat // tm),
        in_specs=[
            pl.BlockSpec((None, tm, d), lambda b, i: (b, i, 0)),
            pl.BlockSpec((None, None, None, 1, d), lambda b, i: (layer, tok.mod_off + b, 5, 0, 0)),
            pl.BlockSpec((TOP_K, None, tm, d), lambda b, i: (0, b, i, 0)),
            pl.BlockSpec((None, tm, LANE), lambda b, i: (b, i, 0)),
            pl.BlockSpec((1, d), lambda b, i: (0, 0)),
        ],
        out_specs=pl.BlockSpec((None, tm, d), lambda b, i: (b, i, 0)),
        out_shape=jax.ShapeDtypeStruct((batch, n_lat, d), F32),
        compiler_params=_cparams(("arbitrary", "arbitrary")),
        name="final_combine_norm",
    )(x_all.reshape(batch, seq, d), mod, y_gath.reshape(TOP_K, batch, seq, d),
      top_w.reshape(batch, seq, LANE), gain)


def _dispatch_plan(top_e, rank, counts, n_tok):
    n_exp = counts.shape[0]
    bm = MOE_ROWS
    n_pairs = n_tok * TOP_K
    n_blk = -(-(n_pairs + n_exp * (bm - 1)) // bm)
    padded = (counts + bm - 1) // bm * bm
    pad_end = jnp.cumsum(padded)
    pad_start = pad_end - padded
    dest = rank
    for e in range(n_exp):
        dest = dest + jnp.where(top_e == e, pad_start[e], 0)
    n_active = (pad_end[-1] // bm).astype(jnp.int32).reshape(1)
    blk_start = jnp.arange(n_blk, dtype=jnp.int32) * bm
    blk_e = jnp.minimum(jnp.sum(pad_end[None, :] <= blk_start[:, None], axis=1), n_exp - 1).astype(jnp.int32)
    n_rows = n_blk * bm
    shift = (n_rows - 1).bit_length()
    assert (n_exp + 1) << shift < 2 ** 31
    t_ids = jnp.arange(n_tok, dtype=jnp.int32)[None, :]
    k_ids = jnp.arange(TOP_K, dtype=jnp.int32)[:, None]
    pair_keys = (top_e << shift) | (t_ids * TOP_K + k_ids)
    n_fill = n_rows - n_pairs
    per = bm - 1
    need = jnp.repeat(padded - counts, per)
    fill_e = jnp.where(jnp.tile(jnp.arange(per, dtype=jnp.int32), n_exp) < need,
                       jnp.repeat(jnp.arange(n_exp, dtype=jnp.int32), per), n_exp)
    fill_e = jnp.concatenate([fill_e, jnp.full((n_fill - n_exp * per,), n_exp, jnp.int32)])
    fill_keys = (fill_e << shift) | (n_pairs + jnp.arange(n_fill, dtype=jnp.int32))
    pos = lax.sort(jnp.concatenate([pair_keys.reshape(-1), fill_keys])) & ((1 << shift) - 1)
    row_tok = jnp.where(pos < n_pairs, pos // TOP_K, pos % n_tok)
    return dest, row_tok, blk_e, n_active


def kernel(x, c, ctx, c_ctx, w_mod, b_mod, norm_mix, norm_ffn, w_in, hy_conv_w, hy_conv_b, hy_ffn_w1, hy_ffn_b1, hy_ffn_w2, hy_ffn_b2, hy_ffn_w3, hy_bias, pool_w, pool_scale, ssm_conv_w, ssm_conv_b, ssm_dt_bias, ssm_a_log, ssm_d, ssm_norm, w_branch, w_gate, w_out, router_w, router_b, exp_w_up, exp_b_up, exp_w_down, exp_b_down, norm_final):
    batch, n_lat, d = x.shape
    n_ctx = ctx.shape[1]
    depth = w_mod.shape[0]
    assert n_lat // GRID_W * GRID_W == n_lat and n_lat % n_ctx == 0

    hw = hy_bias.shape[1]
    pw = pool_scale.shape[1]
    sw = ssm_norm.shape[1]
    cw = ssm_conv_w.shape[2]
    fw = w_in.shape[2] - 3 * hw - pw - sw - cw - 2 * SSM_HEADS
    widths = (fw, 3 * hw, pw, sw, cw)
    splits = (fw, fw + hw, fw + hw + pw, fw + hw + pw + sw)
    n_exp = router_w.shape[2]

    w_in_p = jnp.pad(w_in, ((0, 0), (0, 0), (0, LANE - 2 * SSM_HEADS))).astype(BF16)
    w_gate_b = w_gate.astype(BF16)
    w_branch_b = w_branch.astype(BF16)
    w_out_b = w_out.astype(BF16)
    pool_w_bd = jax.vmap(_block_diag)(pool_w).astype(BF16)
    r3 = lambda a: a.reshape(a.shape[0], 1, a.shape[1])
    lane_pad = lambda a: jnp.pad(a, ((0, 0), (0, LANE - a.shape[1])))
    norm_mix3, norm_ffn3 = r3(norm_mix), r3(norm_ffn)
    hy_conv_b3, hy_bias3 = r3(hy_conv_b), r3(hy_bias)
    ssm_conv_b3 = r3(ssm_conv_b)
    pool_scale3 = r3(pool_scale)
    dt_bias3 = r3(lane_pad(ssm_dt_bias.reshape(depth, 2 * SSM_HEADS)))
    a_row3 = r3(lane_pad(-jnp.exp(ssm_a_log.astype(F32)).reshape(depth, 2 * SSM_HEADS)))
    d_row3 = r3(jnp.repeat(ssm_d, SSM_HEAD_DIM, axis=1))
    ssm_norm3 = r3(ssm_norm)
    rw_t = jnp.swapaxes(router_w, 1, 2)
    rw_hi = rw_t.astype(BF16)
    rw_lo = (rw_t - rw_hi.astype(F32)).astype(BF16)
    router_b_col = jnp.broadcast_to(router_b[:, :, None], (depth, n_exp, LANE))
    exp_b_up4 = exp_b_up.reshape(depth, n_exp, 1, -1)
    exp_b_down4 = exp_b_down.reshape(depth, n_exp, 1, -1)
    emb = hy_ffn_w1.shape[1]
    ffn = hy_ffn_w1.shape[2]
    f_w1 = jnp.pad(hy_ffn_w1, ((0, 0), (0, LANE - emb), (0, LANE - ffn)))
    f_b1 = r3(lane_pad(hy_ffn_b1))
    f_w2 = jnp.pad(hy_ffn_w2, ((0, 0), (0, LANE - ffn), (0, LANE - ffn)))
    f_b2 = r3(lane_pad(hy_ffn_b2))
    f_w3 = jnp.pad(hy_ffn_w3, ((0, 0), (0, LANE - ffn), (0, 0)))

    freqs = jnp.linspace(1e-4, HYENA_BANDS - 1, HYENA_BANDS, dtype=F32)
    freqs_row = jnp.concatenate([jnp.zeros((1,), F32), freqs, freqs,
                                 jnp.zeros((LANE - 1 - 2 * HYENA_BANDS,), F32)])[None, :]
    deltas_row = jnp.linspace(HYENA_MIN_DECAY, HYENA_MAX_DECAY, hw, dtype=F32)[None, :]
    gi = jnp.arange(FOURIER_GROUP_W)
    ang = (gi[:, None] * gi[None, :] % FOURIER_GROUP_W).astype(F32) * (2.0 * math.pi / FOURIER_GROUP_W)
    n_fg = fw // FOURIER_GROUP_W
    eye = jnp.eye(n_fg, dtype=F32)
    wc = jnp.concatenate([jnp.kron(eye, jnp.cos(ang)), jnp.kron(eye, jnp.sin(ang))], axis=1).astype(BF16)
    band, invc = _pool_tables(n_lat, n_ctx)
    qi = jnp.arange(SSM_CHUNK)
    tri_f = (qi[:, None] >= qi[None, :]).astype(BF16)
    tri_b = (qi[:, None] <= qi[None, :]).astype(BF16)
    ti = jnp.arange(TOKEN_TILE)
    utri = (ti[:, None] < ti[None, :]).astype(BF16)

    seqs = {}
    for name, n, row_off in (("ctx", n_ctx, n_lat), ("lat", n_lat, 0)):
        fwd, inv = _hyena_dft_tables(n)
        g_all, nrm = _filter_mlp(n, freqs_row, deltas_row, f_w1, f_b1, f_w2, f_b2, f_w3)
        kr, ki = _filter_spectrum(fwd, g_all, nrm, n, hw)
        cn, sn = _trig_tables(n, n)
        seqs[name] = dict(n=n, off=row_off, cn=cn.astype(BF16), sn=sn.astype(BF16), fwd=fwd, inv=inv,
                          kr=kr, ki=ki)

    cc = jnp.concatenate([c, c_ctx[None, :]], axis=0)
    rows = -(-cc.shape[0] // 8) * 8
    mod = _modulation(jnp.pad(cc, ((0, rows - cc.shape[0]), (0, 0))), w_mod, b_mod)
    mod = mod.reshape(depth, rows, 6, 1, d)

    seq = n_lat + n_ctx

    def layer_step(tok, layer, x_all, moe):
        gb, n_tok = tok.batch, tok.n_tok
        res = _inproj(tok, layer, x_all, mod, norm_mix3, w_in_p, wc, band, invc, pool_w_bd, pool_scale3,
                      widths, moe=moe)
        if moe is not None:
            x_all = res[0]
            res = res[1:]
        xcs, hy, pool, z, xbc, dtr = res
        xcs3 = xcs.reshape(gb, seq, 2 * fw)
        hy3 = hy.reshape(gb, seq, 3 * hw)
        z3 = z.reshape(gb, seq, sw)
        xbc3 = xbc.reshape(gb, seq, cw)
        dt3 = dtr.reshape(gb, seq, LANE)
        outs = {}
        zero_state = jnp.zeros((gb, SSM_GROUPS, SSM_STATE, HEADS_PER_GROUP * SSM_HEAD_DIM), F32)
        states = (zero_state, zero_state)
        for name in ("ctx", "lat"):
            s = seqs[name]
            n, off = s["n"], s["off"]
            y_f = _fourier_seq(xcs3, s["cn"], s["sn"], n, off, fw)
            vv, x1 = _hy_pre(hy3, hy_conv_w, hy_conv_b3, layer, n, off, hw)
            y2 = _hy_fwd(s["fwd"], vv, s["kr"], s["ki"], layer, n, hw)
            y_h = _hy_inv(s["inv"], y2, x1, vv, hy_bias3, layer, n, hw)
            yf, st_f = _ssd_pass(xbc3, dt3, states[0], tri_f, ssm_conv_w, ssm_conv_b3, dt_bias3, a_row3,
                                 layer, n, off, reverse=False)
            y_s, st_b = _ssd_pass(xbc3, dt3, states[1], tri_b, ssm_conv_w, ssm_conv_b3, dt_bias3, a_row3,
                                  layer, n, off, reverse=True, fin=(z3, yf, d_row3, ssm_norm3))
            states = (st_f, st_b)
            outs[name] = (y_f, y_h, y_s)
        x_all = _merge(tok, layer, x_all, mod, norm_mix3, outs["lat"][0], outs["ctx"][0], outs["lat"][1],
                       outs["ctx"][1], pool, outs["lat"][2], outs["ctx"][2], w_gate_b, w_branch_b, w_out_b,
                       splits)
        h2, top_e, rank, top_w, counts = _router(tok, layer, x_all, mod, norm_ffn3, rw_hi, rw_lo, router_b_col,
                                                 utri)
        dest, row_tok, blk_e, n_active = _dispatch_plan(top_e[:TOP_K], rank[:TOP_K],
                                                        counts[:, 0].astype(jnp.int32), n_tok)
        x_sorted = h2[row_tok]
        y_rows = _moe(layer, x_sorted, blk_e, n_active, exp_w_up, exp_b_up4, exp_w_down, exp_b_down4)
        y_gath = y_rows[dest.reshape(-1)].reshape(TOP_K, n_tok, d)
        return x_all, (y_gath, top_w)

    n_groups = SAMPLE_GROUPS if batch % SAMPLE_GROUPS == 0 else 1
    gb = batch // n_groups
    groups = []
    for g in range(n_groups):
        tok = _Tok(gb, n_lat, n_ctx, mod_off=g * gb, mod_ctx=batch)
        x_all = jnp.concatenate([x[g * gb:(g + 1) * gb], ctx[g * gb:(g + 1) * gb]], axis=1)
        groups.append([tok, x_all.reshape(tok.n_tok, d), None])
    for layer in range(depth):
        for grp in groups:
            grp[1], grp[2] = layer_step(grp[0], layer, grp[1], grp[2])
    outs = [_final(tok, depth - 1, x_all, mod, moe[0], moe[1], norm_final[None, :])
            for tok, x_all, moe in groups]
    return outs[0] if n_groups == 1 else jnp.concatenate(outs, axis=0)
```
